```python
import math
import jax, jax.numpy as jnp
from jax import lax
import numpy as np

D_MODEL = 1024
BATCH = 1
SEQ = 16384
DEPTH = 2

N_MIXERS = 2
N_GLA_LAYERS = (DEPTH + 1) // 2
N_MLA_LAYERS = DEPTH // 2
RMS_EPS = 1e-6

GLA_HEADS = 4
GLA_DK = D_MODEL // 2
GLA_DV = D_MODEL
GLA_HEAD_K = GLA_DK // GLA_HEADS
GLA_HEAD_V = GLA_DV // GLA_HEADS
GLA_GATE_RANK = 16
GLA_TAU = 16.0
GLA_CHUNK = 64
GLA_IN = 2 * GLA_DK + 2 * GLA_DV + 2 * GLA_GATE_RANK

MLA_HEADS = 16
MLA_Q_RANK = 256
MLA_KV_RANK = 128
MLA_NOPE = 128
MLA_ROPE = 64
MLA_V = 128
MLA_IN = MLA_Q_RANK + MLA_KV_RANK + MLA_ROPE
ROPE_BASE = 10000.0
Q_BLOCK = 128

N_EXPERTS = 16
EXPERT_FF = 2048
EC_CAPACITY_FACTOR = 2

kernel_name = 'bidir_gla_mla_ec_moe_hybrid'


def rms_norm(x, g):
    xf = x.astype(jnp.float32)
    y = xf * lax.rsqrt(jnp.mean(xf * xf, axis=-1, keepdims=True) + RMS_EPS)
    return (y * g.astype(jnp.float32)).astype(x.dtype)


def gla_chunked(q, k, v, g, strict):
    B, H, T, dk = q.shape
    dv = v.shape[-1]
    n = T // GLA_CHUNK

    def to_chunks(a):
        return jnp.moveaxis(a.reshape(B, H, n, GLA_CHUNK, a.shape[-1]), 2, 0)

    qc, kc, vc, gc = to_chunks(q), to_chunks(k), to_chunks(v), to_chunks(g)
    idx = jnp.arange(GLA_CHUNK)
    mask = (idx[:, None] > idx[None, :]) if strict else (idx[:, None] >= idx[None, :])

    def step(S, inp):
        qi, ki, vi, gi = inp
        qf = qi.astype(jnp.float32)
        kf = ki.astype(jnp.float32)
        vf = vi.astype(jnp.float32)
        G = jnp.cumsum(gi.astype(jnp.float32), axis=-2)
        o_inter = jnp.einsum('bhtk,bhkv->bhtv', qf * jnp.exp(G), S)
        diff = G[:, :, :, None, :] - G[:, :, None, :, :]
        decay = jnp.exp(jnp.where(mask[None, None, :, :, None], diff, -jnp.inf))
        attn = jnp.einsum('bhtk,bhsk,bhtsk->bhts', qf, kf, decay)
        o_intra = jnp.einsum('bhts,bhsv->bhtv', attn, vf)
        G_last = G[:, :, -1, :]
        S_new = jnp.exp(G_last)[..., None] * S + jnp.einsum(
            'bhsk,bhsv->bhkv', kf * jnp.exp(G_last[:, :, None, :] - G), vf)
        return S_new, o_inter + o_intra

    S0 = jnp.zeros((B, H, dk, dv), jnp.float32)
    _, o = lax.scan(step, S0, (qc, kc, vc, gc))
    return jnp.moveaxis(o, 0, 2).reshape(B, H, T, dv).astype(v.dtype)


def gla_mixer(h, w_in, w_gate_up_f, b_gate_f, w_gate_up_b, b_gate_b, head_norm, w_out):
    B, T, _ = h.shape
    proj = h @ w_in
    s1 = GLA_DK
    s2 = 2 * GLA_DK
    s3 = s2 + GLA_DV
    s4 = s3 + GLA_DV
    s5 = s4 + GLA_GATE_RANK
    q, k, v, r, gd_f, gd_b = jnp.split(proj, [s1, s2, s3, s4, s5], axis=-1)

    def heads(a, d):
        return a.reshape(B, T, GLA_HEADS, d).transpose(0, 2, 1, 3)

    log_a_f = jax.nn.log_sigmoid((gd_f @ w_gate_up_f + b_gate_f).astype(jnp.float32)) / GLA_TAU
    log_a_b = jax.nn.log_sigmoid((gd_b @ w_gate_up_b + b_gate_b).astype(jnp.float32)) / GLA_TAU
    qh = heads(q, GLA_HEAD_K) * (GLA_HEAD_K ** -0.5)
    kh = heads(k, GLA_HEAD_K)
    vh = heads(v, GLA_HEAD_V)
    gf = heads(log_a_f, GLA_HEAD_K)
    gb = heads(log_a_b, GLA_HEAD_K)
    o_fwd = gla_chunked(qh, kh, vh, gf, False)
    o_bwd = jnp.flip(gla_chunked(jnp.flip(qh, 2), jnp.flip(kh, 2), jnp.flip(vh, 2), jnp.flip(gb, 2), True), 2)
    o = (o_fwd + o_bwd).transpose(0, 2, 1, 3)
    o = rms_norm(o, head_norm).reshape(B, T, GLA_DV)
    return (o * jax.nn.silu(r)) @ w_out


def rope(x, cos, sin):
    half = x.shape[-1] // 2
    x1 = x[..., :half].astype(jnp.float32)
    x2 = x[..., half:].astype(jnp.float32)
    return jnp.concatenate([x1 * cos - x2 * sin, x1 * sin + x2 * cos], axis=-1).astype(x.dtype)


def mla_mixer(h, positions, w_in, q_norm, w_uq, kv_norm, w_ukv, w_out):
    B, T, _ = h.shape
    c_q, c_kv, k_r = jnp.split(h @ w_in, [MLA_Q_RANK, MLA_Q_RANK + MLA_KV_RANK], axis=-1)
    q = (rms_norm(c_q, q_norm) @ w_uq).reshape(B, T, MLA_HEADS, MLA_NOPE + MLA_ROPE)
    kv = (rms_norm(c_kv, kv_norm) @ w_ukv).reshape(B, T, MLA_HEADS, MLA_NOPE + MLA_V)
    q_nope, q_rope = q[..., :MLA_NOPE], q[..., MLA_NOPE:]
    k_nope, v = kv[..., :MLA_NOPE], kv[..., MLA_NOPE:]

    half = MLA_ROPE // 2
    inv_freq = ROPE_BASE ** (-jnp.arange(half, dtype=jnp.float32) / half)
    ang = positions.astype(jnp.float32)[..., None] * inv_freq
    cos, sin = jnp.cos(ang), jnp.sin(ang)
    q_rope = rope(q_rope, cos[:, :, None, :], sin[:, :, None, :])
    k_rope = rope(k_r, cos, sin)

    scale = (MLA_NOPE + MLA_ROPE) ** -0.5
    nb = T // Q_BLOCK
    qn_blocks = q_nope.reshape(B, nb, Q_BLOCK, MLA_HEADS, MLA_NOPE).transpose(1, 0, 3, 2, 4)
    qr_blocks = q_rope.reshape(B, nb, Q_BLOCK, MLA_HEADS, MLA_ROPE).transpose(1, 0, 3, 2, 4)
    k_nope_t = k_nope.transpose(0, 2, 1, 3)
    v_t = v.transpose(0, 2, 1, 3)

    def attend(blk):
        qn, qr = blk
        s = jnp.einsum('bhqd,bhkd->bhqk', qn, k_nope_t) + jnp.einsum('bhqr,bkr->bhqk', qr, k_rope)
        p = jax.nn.softmax(s.astype(jnp.float32) * scale, axis=-1)
        return jnp.einsum('bhqk,bhkd->bqhd', p.astype(v_t.dtype), v_t)

    o = lax.map(attend, (qn_blocks, qr_blocks))
    o = o.transpose(1, 0, 2, 3, 4).reshape(B, T, MLA_HEADS * MLA_V)
    return o @ w_out


def ec_moe(h, w_router, w_gate, w_up, w_down):
    B, T, D = h.shape
    cap = max(1, EC_CAPACITY_FACTOR * T // N_EXPERTS)
    aff = jax.nn.softmax((h @ w_router).astype(jnp.float32), axis=-1)
    top_aff, top_idx = lax.top_k(jnp.swapaxes(aff, 1, 2), cap)
    xs = jax.vmap(lambda hb, ib: hb[ib])(h, top_idx)
    a = jnp.einsum('becd,edf->becf', xs, w_gate)
    u = jnp.einsum('becd,edf->becf', xs, w_up)
    y = jnp.einsum('becf,efd->becd', jax.nn.silu(a) * u, w_down)
    y = y * top_aff[..., None].astype(y.dtype)
    return jax.vmap(lambda yb, ib: jnp.zeros((T, D), yb.dtype).at[ib.reshape(-1)].add(yb.reshape(-1, D)))(y, top_idx)


def setup_inputs(seed: int = 0) -> dict:
    key = jax.random.key(seed)
    ks = jax.random.split(key, 24)

    def nrm(k, shape, fan_in):
        return jax.random.normal(k, shape, jnp.float32) * (fan_in ** -0.5)

    def gain(k, shape):
        return 1.0 + 0.02 * jax.random.normal(k, shape, jnp.float32)

    return {
        'x': jax.random.normal(ks[0], (BATCH, SEQ, D_MODEL), jnp.float32),
        'positions': jnp.broadcast_to(jnp.arange(SEQ, dtype=jnp.int32)[None, :], (BATCH, SEQ)),
        'mix_norm': gain(ks[1], (DEPTH, D_MODEL)),
        'ffn_norm': gain(ks[2], (DEPTH, D_MODEL)),
        'final_norm': gain(ks[3], (D_MODEL,)),
        'gla_w_in': nrm(ks[4], (N_GLA_LAYERS, D_MODEL, GLA_IN), D_MODEL),
        'gla_w_gate_up_f': nrm(ks[5], (N_GLA_LAYERS, GLA_GATE_RANK, GLA_DK), GLA_GATE_RANK),
        'gla_b_gate_f': 0.1 * jax.random.normal(ks[6], (N_GLA_LAYERS, GLA_DK), jnp.float32),
        'gla_w_gate_up_b': nrm(ks[7], (N_GLA_LAYERS, GLA_GATE_RANK, GLA_DK), GLA_GATE_RANK),
        'gla_b_gate_b': 0.1 * jax.random.normal(ks[8], (N_GLA_LAYERS, GLA_DK), jnp.float32),
        'gla_head_norm': gain(ks[9], (N_GLA_LAYERS, GLA_HEAD_V)),
        'gla_w_out': nrm(ks[10], (N_GLA_LAYERS, GLA_DV, D_MODEL), GLA_DV),
        'mla_w_in': nrm(ks[11], (N_MLA_LAYERS, D_MODEL, MLA_IN), D_MODEL),
        'mla_q_norm': gain(ks[12], (N_MLA_LAYERS, MLA_Q_RANK)),
        'mla_w_uq': nrm(ks[13], (N_MLA_LAYERS, MLA_Q_RANK, MLA_HEADS * (MLA_NOPE + MLA_ROPE)), MLA_Q_RANK),
        'mla_kv_norm': gain(ks[14], (N_MLA_LAYERS, MLA_KV_RANK)),
        'mla_w_ukv': nrm(ks[15], (N_MLA_LAYERS, MLA_KV_RANK, MLA_HEADS * (MLA_NOPE + MLA_V)), MLA_KV_RANK),
        'mla_w_out': nrm(ks[16], (N_MLA_LAYERS, MLA_HEADS * MLA_V, D_MODEL), MLA_HEADS * MLA_V),
        'moe_w_router': nrm(ks[17], (DEPTH, D_MODEL, N_EXPERTS), D_MODEL),
        'moe_w_gate': nrm(ks[18], (DEPTH, N_EXPERTS, D_MODEL, EXPERT_FF), D_MODEL),
        'moe_w_up': nrm(ks[19], (DEPTH, N_EXPERTS, D_MODEL, EXPERT_FF), D_MODEL),
        'moe_w_down': nrm(ks[20], (DEPTH, N_EXPERTS, EXPERT_FF, D_MODEL), EXPERT_FF),
    }


def reference(x, positions, mix_norm, ffn_norm, final_norm,
              gla_w_in, gla_w_gate_up_f, gla_b_gate_f, gla_w_gate_up_b, gla_b_gate_b, gla_head_norm, gla_w_out,
              mla_w_in, mla_q_norm, mla_w_uq, mla_kv_norm, mla_w_ukv, mla_w_out,
              moe_w_router, moe_w_gate, moe_w_up, moe_w_down):
    h = x
    for i in range(DEPTH):
        j = i // N_MIXERS
        hn = rms_norm(h, mix_norm[i])
        if i % N_MIXERS == 0:
            h = h + gla_mixer(hn, gla_w_in[j], gla_w_gate_up_f[j], gla_b_gate_f[j], gla_w_gate_up_b[j],
                              gla_b_gate_b[j], gla_head_norm[j], gla_w_out[j])
        else:
            h = h + mla_mixer(hn, positions, mla_w_in[j], mla_q_norm[j], mla_w_uq[j], mla_kv_norm[j],
                              mla_w_ukv[j], mla_w_out[j])
        h = h + ec_moe(rms_norm(h, ffn_norm[i]), moe_w_router[i], moe_w_gate[i], moe_w_up[i], moe_w_down[i])
    return rms_norm(h, final_norm)
```

```python
import functools
import math

import numpy as np
import jax
import jax.numpy as jnp
from jax import lax
from jax.experimental import pallas as pl
from jax.experimental.pallas import tpu as pltpu

F32 = jnp.float32
BF16 = jnp.bfloat16

D_MODEL = 1024
RMS_EPS = 1e-6

GLA_HEADS = 4
GLA_DK = 512
GLA_DV = 1024
GLA_HEAD_K = GLA_DK // GLA_HEADS
GLA_HEAD_V = GLA_DV // GLA_HEADS
GLA_GATE_RANK = 16
GLA_TAU = 16.0
GLA_CHUNK = 64
GLA_TILE = 256
GLA_LEVELS = 6

MLA_HEADS = 16
MLA_Q_RANK = 256
MLA_KV_RANK = 128
MLA_NOPE = 128
MLA_ROPE = 64
MLA_V = 128
MLA_QK = MLA_NOPE + MLA_ROPE
ROPE_BASE = 10000.0

N_EXPERTS = 16
EXPERT_FF = 2048
EC_CAPACITY_FACTOR = 2
MOE_TILE = 256
BF16_SUBLANES = 16
F32_SUBLANES = 8
LANES = 128

VMEM_LIMIT = 56 * 1024 * 1024


def _cparams(sem):
    return pltpu.CompilerParams(dimension_semantics=sem, vmem_limit_bytes=VMEM_LIMIT)


def _rms(x, g):
    return x * lax.rsqrt(jnp.mean(x * x, axis=-1, keepdims=True) + RMS_EPS) * g


def _split_bf16(x):
    hi = x.astype(BF16)
    lo = (x - hi.astype(F32)).astype(BF16)
    return hi, lo


def _dot(a, b):
    return jnp.dot(a, b, preferred_element_type=F32)


def _dot_nt(a, b):
    return lax.dot_general(a, b, (((1,), (1,)), ((), ())), preferred_element_type=F32)


def _dot_tn(a, b):
    return lax.dot_general(a, b, (((0,), (0,)), ((), ())), preferred_element_type=F32)


def _dot_split(a, b):
    ah, al = _split_bf16(a)
    bh, bl = _split_bf16(b)
    return _dot(ah, bh) + _dot(ah, bl) + _dot(al, bh)


def _gla_in_kernel(x_ref, g_ref, wqk_ref, wv_ref, wr_ref, wgd_ref, qk_ref, v_ref, r_ref, gd_ref):
    hn = _rms(x_ref[...], g_ref[...]).astype(BF16)
    qk_ref[...] = _dot(hn, wqk_ref[...])
    v_ref[...] = _dot(hn, wv_ref[...])
    r_ref[...] = _dot(hn, wr_ref[...])
    gd_ref[...] = _dot(hn, wgd_ref[...])


def _gla_in(x, gain, w_in, tm=512):
    T = x.shape[0]
    wqk = w_in[:, :2 * GLA_DK].astype(BF16)
    wv = w_in[:, 2 * GLA_DK:2 * GLA_DK + GLA_DV].astype(BF16)
    wr = w_in[:, 2 * GLA_DK + GLA_DV:2 * GLA_DK + 2 * GLA_DV].astype(BF16)
    wgd = w_in[:, 2 * GLA_DK + 2 * GLA_DV:].astype(BF16)
    ngd = 2 * GLA_GATE_RANK
    row = lambda n: pl.BlockSpec((tm, n), lambda i: (i, 0))
    full = lambda a: pl.BlockSpec(a.shape, lambda i: (0, 0))
    gain2 = gain.reshape(1, D_MODEL)
    return pl.pallas_call(
        _gla_in_kernel,
        grid=(T // tm,),
        in_specs=[row(D_MODEL), full(gain2), full(wqk), full(wv), full(wr), full(wgd)],
        out_specs=[row(2 * GLA_DK), row(GLA_DV), row(GLA_DV), row(ngd)],
        out_shape=[jax.ShapeDtypeStruct((T, 2 * GLA_DK), F32), jax.ShapeDtypeStruct((T, GLA_DV), F32),
                   jax.ShapeDtypeStruct((T, GLA_DV), F32), jax.ShapeDtypeStruct((T, ngd), F32)],
        compiler_params=_cparams(("arbitrary",)),
        name="gla_in",
    )(x, gain2, wqk, wv, wr, wgd)


def _gla_tables(reverse):
    n, c = GLA_TILE, GLA_CHUNK
    W = np.zeros((GLA_LEVELS + 3, n, n), np.float32)
    L = np.full((n, n), -1, np.int32)
    for t in range(n):
        c0 = (t // c) * c
        tt = t - c0
        for l in range(GLA_LEVELS):
            b = (c // 2) >> l
            p0 = c0 + (tt // (2 * b)) * 2 * b
            mid = p0 + b
            second = t >= mid
            if not reverse:
                if second:
                    W[l, t, mid:t + 1] = 1
                    L[t, p0:mid] = l
                else:
                    W[l, t, t + 1:mid] = 1
            else:
                if second:
                    W[l, t, mid:t] = 1
                else:
                    W[l, t, t:mid] = 1
                    L[t, mid:p0 + 2 * b] = l
        if not reverse:
            W[GLA_LEVELS, t, c0:t + 1] = 1
            W[GLA_LEVELS + 1, t, t + 1:c0 + c] = 1
            L[t, t] = GLA_LEVELS
        else:
            W[GLA_LEVELS, t, t:c0 + c] = 1
            W[GLA_LEVELS + 1, t, c0:t] = 1
        W[GLA_LEVELS + 2, t, c0:c0 + c] = 1
    return W.reshape(-1, n), L


def _gla_scan_kernel(q_ref, k_ref, v_ref, gd_ref, wup_ref, b_ref, w_ref, l_ref, o_ref, s_ref, *, reverse):
    n, c = GLA_TILE, GLA_CHUNK

    @pl.when(pl.program_id(1) == 0)
    def _():
        s_ref[...] = jnp.zeros_like(s_ref)

    q = q_ref[...] * (GLA_HEAD_K ** -0.5)
    k = k_ref[...]
    vb = v_ref[...].astype(BF16)
    z = _dot_split(gd_ref[...], wup_ref[...]) + b_ref[...]
    g = (jnp.minimum(z, 0.0) - jnp.log1p(jnp.exp(-jnp.abs(z)))) * (1.0 / GLA_TAU)
    ghi, glo = _split_bf16(g)
    w = w_ref[...]
    f = jnp.exp(_dot(w, ghi) + _dot(w, glo))
    lvl = l_ref[...]

    attn = jnp.zeros((n, n), F32)
    for l in range(GLA_LEVELS):
        fl = f[l * n:(l + 1) * n]
        p = _dot_nt((q * fl).astype(BF16), (k * fl).astype(BF16))
        attn = jnp.where(lvl == l, p, attn)
    if not reverse:
        p = _dot_nt(q.astype(BF16), k.astype(BF16))
        attn = jnp.where(lvl == GLA_LEVELS, p, attn)
    o_intra = _dot(attn.astype(BF16), vb)

    fq = f[GLA_LEVELS * n:(GLA_LEVELS + 1) * n]
    fk = f[(GLA_LEVELS + 1) * n:(GLA_LEVELS + 2) * n]
    ftot = f[(GLA_LEVELS + 2) * n:(GLA_LEVELS + 3) * n]
    qh = (q * fq).astype(BF16)
    kh = (k * fk).astype(BF16)
    st = s_ref[...]
    chunks = range(n // c)
    for j in (reversed(chunks) if reverse else chunks):
        rows = slice(j * c, (j + 1) * c)
        o_ref[rows, :] = o_intra[rows] + _dot_nt(qh[rows], st.astype(BF16))
        st = st * ftot[j * c:j * c + 1, :] + _dot_tn(vb[rows], kh[rows])
    s_ref[...] = st


def _gla_scan(qk, v, gd, w_up, b, reverse):
    T = qk.shape[0]
    n = GLA_TILE
    nt = T // n
    W, L = _gla_tables(reverse)
    W = jnp.asarray(W, BF16)
    L = jnp.asarray(L)
    tidx = (lambda i: nt - 1 - i) if reverse else (lambda i: i)
    r = GLA_GATE_RANK
    gdd = gd[:, r:2 * r] if reverse else gd[:, :r]
    b2 = b.reshape(1, GLA_DK)
    return pl.pallas_call(
        functools.partial(_gla_scan_kernel, reverse=reverse),
        grid=(GLA_HEADS, nt),
        in_specs=[
            pl.BlockSpec((n, GLA_HEAD_K), lambda h, i: (tidx(i), h)),
            pl.BlockSpec((n, GLA_HEAD_K), lambda h, i: (tidx(i), GLA_HEADS + h)),
            pl.BlockSpec((n, GLA_HEAD_V), lambda h, i: (tidx(i), h)),
            pl.BlockSpec((n, r), lambda h, i: (tidx(i), 0)),
            pl.BlockSpec((r, GLA_HEAD_K), lambda h, i: (0, h)),
            pl.BlockSpec((1, GLA_HEAD_K), lambda h, i: (0, h)),
            pl.BlockSpec(W.shape, lambda h, i: (0, 0)),
            pl.BlockSpec(L.shape, lambda h, i: (0, 0)),
        ],
        out_specs=pl.BlockSpec((n, GLA_HEAD_V), lambda h, i: (tidx(i), h)),
        out_shape=jax.ShapeDtypeStruct((T, GLA_DV), F32),
        scratch_shapes=[pltpu.VMEM((GLA_HEAD_V, GLA_HEAD_K), F32)],
        compiler_params=_cparams(("arbitrary", "arbitrary")),
        name="gla_scan_bwd" if reverse else "gla_scan_fwd",
    )(qk, qk, v, gdd, w_up, b2, W, L)


def _gla_out_kernel(of_ref, ob_ref, r_ref, x_ref, hn_ref, w_ref, o_ref):
    acc = x_ref[...]
    for h in range(GLA_HEADS):
        cols = slice(h * GLA_HEAD_V, (h + 1) * GLA_HEAD_V)
        o = _rms(of_ref[:, cols] + ob_ref[:, cols], hn_ref[...])
        r = r_ref[:, cols]
        gated = o * (r * (1.0 / (1.0 + jnp.exp(-r))))
        acc = acc + _dot(gated.astype(BF16), w_ref[cols, :])
    o_ref[...] = acc


def _gla_out(of, ob, r, x, head_norm, w_out, tm=512):
    T = x.shape[0]
    w = w_out.astype(BF16)
    hn = head_norm.reshape(1, GLA_HEAD_V)
    row = lambda n: pl.BlockSpec((tm, n), lambda i: (i, 0))
    full = lambda a: pl.BlockSpec(a.shape, lambda i: (0, 0))
    return pl.pallas_call(
        _gla_out_kernel,
        grid=(T // tm,),
        in_specs=[row(GLA_DV), row(GLA_DV), row(GLA_DV), row(D_MODEL), full(hn), full(w)],
        out_specs=row(D_MODEL),
        out_shape=jax.ShapeDtypeStruct((T, D_MODEL), F32),
        compiler_params=_cparams(("arbitrary",)),
        name="gla_out",
    )(of, ob, r, x, hn, w)


def _gla_mixer(x, gain, w_in, w_up_f, b_f, w_up_b, b_b, head_norm, w_out):
    qk, v, r, gd = _gla_in(x, gain, w_in)
    of = _gla_scan(qk, v, gd, w_up_f, b_f, reverse=False)
    ob = _gla_scan(qk, v, gd, w_up_b, b_b, reverse=True)
    return _gla_out(of, ob, r, x, head_norm, w_out)


def _router_kernel(h_ref, g_ref, w_ref, aff_ref, hn_ref):
    hn = _rms(h_ref[...], g_ref[...])
    hn_ref[...] = hn.astype(BF16)
    logits = _dot_split(hn, w_ref[...])
    e = jnp.exp(logits - jnp.max(logits, axis=-1, keepdims=True))
    aff_ref[...] = e / jnp.sum(e, axis=-1, keepdims=True)


def _router(h, gain, w_router, tm=512):
    T = h.shape[0]
    gain2 = gain.reshape(1, D_MODEL)
    row = lambda n: pl.BlockSpec((tm, n), lambda i: (i, 0))
    full = lambda a: pl.BlockSpec(a.shape, lambda i: (0, 0))
    return pl.pallas_call(
        _router_kernel,
        grid=(T // tm,),
        in_specs=[row(D_MODEL), full(gain2), full(w_router)],
        out_specs=[row(N_EXPERTS), row(D_MODEL)],
        out_shape=[jax.ShapeDtypeStruct((T, N_EXPERTS), F32), jax.ShapeDtypeStruct((T, D_MODEL), BF16)],
        compiler_params=_cparams(("arbitrary",)),
        name="moe_router",
    )(h, gain2, w_router)


def _select_kernel(aff_ref, tri_ref, posm_ref, before_ref, *, cap):
    T = aff_ref.shape[1]
    bits = pltpu.bitcast(aff_ref[...], jnp.int32)

    def search(it, thr):
        cand = thr | jnp.left_shift(jnp.int32(1), 30 - it)
        cnt = jnp.sum(jnp.where(bits >= cand, 1.0, 0.0), axis=1, keepdims=True)
        return jnp.where(cnt >= cap, cand, thr)

    thr = lax.fori_loop(0, 31, search, jnp.zeros((N_EXPERTS, 1), jnp.int32))
    n_gt = jnp.sum(jnp.where(bits > thr, 1.0, 0.0), axis=1, keepdims=True)
    need = cap - n_gt
    tri = tri_ref[...]

    def scan(j, carry):
        c_eq, c_sel = carry
        cols = pl.ds(pl.multiple_of(j * LANES, LANES), LANES)
        blk = pltpu.bitcast(aff_ref[:, cols], jnp.int32)
        eq = jnp.where(blk == thr, 1.0, 0.0)
        rank = _dot(eq.astype(BF16), tri) + c_eq - eq
        sel = jnp.where((blk > thr) | ((eq > 0.0) & (rank < need)), 1.0, 0.0)
        before = _dot(sel.astype(BF16), tri) + c_sel - sel
        before_ref[:, cols] = before.astype(jnp.int32)
        posm_ref[:, cols] = jnp.where(sel > 0.0, before, -1.0).astype(jnp.int32)
        return (c_eq + jnp.sum(eq, axis=1, keepdims=True), c_sel + jnp.sum(sel, axis=1, keepdims=True))

    zero = jnp.zeros((N_EXPERTS, 1), F32)
    lax.fori_loop(0, T // LANES, scan, (zero, zero))


def _select(aff_t, cap):
    T = aff_t.shape[1]
    tri = jnp.asarray(np.triu(np.ones((LANES, LANES), np.float32)), BF16)
    full = lambda a: pl.BlockSpec(a.shape, lambda: (0,) * a.ndim)
    out = jax.ShapeDtypeStruct((N_EXPERTS, T), jnp.int32)
    return pl.pallas_call(
        functools.partial(_select_kernel, cap=cap),
        in_specs=[full(aff_t), full(tri)],
        out_specs=[pl.BlockSpec((N_EXPERTS, T), lambda: (0, 0))] * 2,
        out_shape=[out, out],
        compiler_params=pltpu.CompilerParams(vmem_limit_bytes=VMEM_LIMIT),
        name="moe_select",
    )(aff_t, tri)


def _dispatch_window():
    return MOE_TILE + F32_SUBLANES


def _dispatch_kernel(start_ref, hn_ref, posm_ref, x_ref, acc_ref, *, cap, nt):
    e, t = pl.program_id(0), pl.program_id(1)
    win = _dispatch_window()

    @pl.when(t == 0)
    def _():
        acc_ref[...] = jnp.zeros_like(acc_ref)

    base = pl.multiple_of((start_ref[e * nt + t] // F32_SUBLANES) * F32_SUBLANES, F32_SUBLANES)
    slot = base + lax.broadcasted_iota(jnp.int32, (win, MOE_TILE), 0)
    onehot = jnp.where(posm_ref[0] == slot, 1.0, 0.0).astype(BF16)
    acc_ref[pl.ds(base, win), :] += _dot(onehot, hn_ref[...])

    @pl.when(t == nt - 1)
    def _():
        x_ref[0] = acc_ref[:cap, :].astype(BF16)


def _dispatch(hn, posm, start, cap):
    T = hn.shape[0]
    nt = T // MOE_TILE
    posm3 = posm.reshape(N_EXPERTS, 1, T)
    grid_spec = pltpu.PrefetchScalarGridSpec(
        num_scalar_prefetch=1,
        grid=(N_EXPERTS, nt),
        in_specs=[
            pl.BlockSpec((MOE_TILE, D_MODEL), lambda e, t, s: (t, 0)),
            pl.BlockSpec((1, 1, MOE_TILE), lambda e, t, s: (e, 0, t)),
        ],
        out_specs=pl.BlockSpec((1, cap, D_MODEL), lambda e, t, s: (e, 0, 0)),
        scratch_shapes=[pltpu.VMEM((cap + _dispatch_window(), D_MODEL), F32)],
    )
    return pl.pallas_call(
        functools.partial(_dispatch_kernel, cap=cap, nt=nt),
        grid_spec=grid_spec,
        out_shape=jax.ShapeDtypeStruct((N_EXPERTS, cap, D_MODEL), BF16),
        compiler_params=_cparams(("arbitrary", "arbitrary")),
        name="moe_dispatch",
    )(start.reshape(-1), hn, posm3)


def _ffn_kernel(x_ref, wg_ref, wu_ref, wd_ref, y_ref, acc_ref):
    f = pl.program_id(1)

    @pl.when(f == 0)
    def _():
        acc_ref[...] = jnp.zeros_like(acc_ref)

    x = x_ref[0]
    a = _dot(x, wg_ref[0, 0].astype(BF16))
    u = _dot(x, wu_ref[0, 0].astype(BF16))
    mid = (a * (1.0 / (1.0 + jnp.exp(-a))) * u).astype(BF16)
    acc_ref[...] += _dot(mid, wd_ref[0, 0].astype(BF16))

    @pl.when(f == pl.num_programs(1) - 1)
    def _():
        y_ref[0] = acc_ref[...].astype(BF16)


def _ffn(x, w_gate, w_up, w_down, layer, tf=512):
    cap = x.shape[1]
    return pl.pallas_call(
        _ffn_kernel,
        grid=(N_EXPERTS, EXPERT_FF // tf),
        in_specs=[
            pl.BlockSpec((1, cap, D_MODEL), lambda e, f: (e, 0, 0)),
            pl.BlockSpec((1, 1, D_MODEL, tf), lambda e, f: (layer, e, 0, f)),
            pl.BlockSpec((1, 1, D_MODEL, tf), lambda e, f: (layer, e, 0, f)),
            pl.BlockSpec((1, 1, tf, D_MODEL), lambda e, f: (layer, e, f, 0)),
        ],
        out_specs=pl.BlockSpec((1, cap, D_MODEL), lambda e, f: (e, 0, 0)),
        out_shape=jax.ShapeDtypeStruct((N_EXPERTS, cap, D_MODEL), BF16),
        scratch_shapes=[pltpu.VMEM((cap, D_MODEL), F32)],
        compiler_params=_cparams(("arbitrary", "arbitrary")),
        name="moe_ffn",
    )(x, w_gate, w_up, w_down)


def _combine_window():
    return MOE_TILE + BF16_SUBLANES


def _combine_kernel(start_ref, h_ref, aff_ref, posm_ref, g_ref, y_hbm, o_ref, buf_ref, sem_ref, *,
                    cap, nt, final_norm):
    t = pl.program_id(0)
    win = _combine_window()

    def base_of(e):
        b = (start_ref[e * nt + t] // BF16_SUBLANES) * BF16_SUBLANES
        return pl.multiple_of(jnp.minimum(b, cap - win), BF16_SUBLANES)

    def window_copy(e, slot):
        return pltpu.make_async_copy(y_hbm.at[e, pl.ds(base_of(e), win), :], buf_ref.at[slot], sem_ref.at[slot])

    window_copy(0, 0).start()
    acc = h_ref[...]
    posm = posm_ref[...]
    aff = aff_ref[...]
    for e in range(N_EXPERTS):
        slot = e % 2
        if e + 1 < N_EXPERTS:
            window_copy(e + 1, 1 - slot).start()
        window_copy(e, slot).wait()
        pos_e = posm[:, e:e + 1]
        base = base_of(e)
        hit_main = pos_e == base + lax.broadcasted_iota(jnp.int32, (MOE_TILE, MOE_TILE), 1)
        hit_tail = pos_e == base + MOE_TILE + lax.broadcasted_iota(jnp.int32, (MOE_TILE, BF16_SUBLANES), 1)
        y = _dot(jnp.where(hit_main, 1.0, 0.0).astype(BF16), buf_ref[slot, :MOE_TILE, :])
        y = y + _dot(jnp.where(hit_tail, 1.0, 0.0).astype(BF16), buf_ref[slot, MOE_TILE:, :])
        acc = acc + aff[:, e:e + 1] * y
    if final_norm:
        acc = _rms(acc, g_ref[...])
    o_ref[...] = acc


def _combine(h, aff, posm_t, start, y, cap, final_gain):
    T = h.shape[0]
    nt = T // MOE_TILE
    win = _combine_window()
    assert cap >= win and (cap - win) % BF16_SUBLANES == 0
    final_norm = final_gain is not None
    gain = (final_gain if final_norm else jnp.ones((D_MODEL,), F32)).reshape(1, D_MODEL)
    grid_spec = pltpu.PrefetchScalarGridSpec(
        num_scalar_prefetch=1,
        grid=(nt,),
        in_specs=[
            pl.BlockSpec((MOE_TILE, D_MODEL), lambda t, s: (t, 0)),
            pl.BlockSpec((MOE_TILE, N_EXPERTS), lambda t, s: (t, 0)),
            pl.BlockSpec((MOE_TILE, N_EXPERTS), lambda t, s: (t, 0)),
            pl.BlockSpec((1, D_MODEL), lambda t, s: (0, 0)),
            pl.BlockSpec(memory_space=pl.ANY),
        ],
        out_specs=pl.BlockSpec((MOE_TILE, D_MODEL), lambda t, s: (t, 0)),
        scratch_shapes=[pltpu.VMEM((2, win, D_MODEL), BF16), pltpu.SemaphoreType.DMA((2,))],
    )
    return pl.pallas_call(
        functools.partial(_combine_kernel, cap=cap, nt=nt, final_norm=final_norm),
        grid_spec=grid_spec,
        out_shape=jax.ShapeDtypeStruct((T, D_MODEL), F32),
        compiler_params=_cparams(("arbitrary",)),
        name="moe_combine",
    )(start.reshape(-1), h, aff, posm_t, gain, y)


def _ec_moe(h, gain, w_router, w_gate, w_up, w_down, layer, final_gain=None):
    T = h.shape[0]
    cap = max(1, EC_CAPACITY_FACTOR * T // N_EXPERTS)
    aff, hn = _router(h, gain, w_router)
    posm, before = _select(aff.T, cap)
    start = before[:, ::MOE_TILE]
    x = _dispatch(hn, posm, start, cap)
    y = _ffn(x, w_gate, w_up, w_down, layer)
    return _combine(h, aff, posm.T, start, y, cap, final_gain)


def _mla_in_kernel(h_ref, pos_ref, g_ref, win_ref, qn_ref, wqn_ref, wq1_ref, wq2_ref, kvn_ref, wkn_ref,
                   wv_ref, freq_ref, qn_out, q1_out, q2_out, kn_out, v_out, kr_out):
    hn = _rms(h_ref[...], g_ref[...]).astype(BF16)
    c = _dot(hn, win_ref[...])
    cq = _rms(c[:, :MLA_Q_RANK], qn_ref[...]).astype(BF16)
    ckv = _rms(c[:, MLA_Q_RANK:MLA_Q_RANK + MLA_KV_RANK], kvn_ref[...]).astype(BF16)
    kr = c[:, MLA_Q_RANK + MLA_KV_RANK:]
    scale = MLA_QK ** -0.5

    ang = pos_ref[...].astype(F32) * freq_ref[...]
    cos, sin = jnp.cos(ang), jnp.sin(ang)
    reps = MLA_HEADS * (MLA_ROPE // 2) // LANES
    cos_h = jnp.concatenate([cos] * reps, axis=1)
    sin_h = jnp.concatenate([sin] * reps, axis=1)

    qn_out[...] = (_dot(cq, wqn_ref[...]) * scale).astype(BF16)
    x1 = _dot(cq, wq1_ref[...])
    x2 = _dot(cq, wq2_ref[...])
    q1_out[...] = ((x1 * cos_h - x2 * sin_h) * scale).astype(BF16)
    q2_out[...] = ((x1 * sin_h + x2 * cos_h) * scale).astype(BF16)
    kn_out[...] = _dot(ckv, wkn_ref[...]).astype(BF16)
    v_out[...] = _dot(ckv, wv_ref[...]).astype(BF16)
    half = MLA_ROPE // 2
    k1, k2 = kr[:, :half], kr[:, half:]
    c32, s32 = cos[:, :half], sin[:, :half]
    kr_out[...] = jnp.concatenate([k1 * c32 - k2 * s32, k1 * s32 + k2 * c32], axis=1).astype(BF16)


def _mla_in(h, positions, gain, w_in, q_norm, w_uq, kv_norm, w_ukv, tm=256):
    T = h.shape[0]
    H, half = MLA_HEADS, MLA_ROPE // 2
    wq = w_uq.reshape(MLA_Q_RANK, H, MLA_QK)
    wqn = wq[:, :, :MLA_NOPE].reshape(MLA_Q_RANK, H * MLA_NOPE).astype(BF16)
    wq1 = wq[:, :, MLA_NOPE:MLA_NOPE + half].reshape(MLA_Q_RANK, H * half).astype(BF16)
    wq2 = wq[:, :, MLA_NOPE + half:].reshape(MLA_Q_RANK, H * half).astype(BF16)
    wkv = w_ukv.reshape(MLA_KV_RANK, H, MLA_NOPE + MLA_V)
    wkn = wkv[:, :, :MLA_NOPE].reshape(MLA_KV_RANK, H * MLA_NOPE).astype(BF16)
    wv = wkv[:, :, MLA_NOPE:].reshape(MLA_KV_RANK, H * MLA_V).astype(BF16)
    inv_freq = ROPE_BASE ** (-jnp.arange(half, dtype=F32) / half)
    freq = jnp.tile(inv_freq, LANES // half).reshape(1, LANES)
    args = [h, positions.reshape(T, 1), gain.reshape(1, -1), w_in.astype(BF16), q_norm.reshape(1, -1), wqn, wq1,
            wq2, kv_norm.reshape(1, -1), wkn, wv, freq]
    row = lambda n: pl.BlockSpec((tm, n), lambda i: (i, 0))
    full = lambda a: pl.BlockSpec(a.shape, lambda i: (0, 0))
    widths = [H * MLA_NOPE, H * half, H * half, H * MLA_NOPE, H * MLA_V, MLA_ROPE]
    return pl.pallas_call(
        _mla_in_kernel,
        grid=(T // tm,),
        in_specs=[row(D_MODEL), row(1)] + [full(a) for a in args[2:]],
        out_specs=[row(n) for n in widths],
        out_shape=[jax.ShapeDtypeStruct((T, n), BF16) for n in widths],
        compiler_params=_cparams(("arbitrary",)),
        name="mla_in",
    )(*args)


def _flash_kernel(q_ref, k_ref, v_ref, o_ref, *, tk):
    T = k_ref.shape[1]
    tq = q_ref.shape[1]
    q = q_ref[0]

    def body(j, carry):
        m, l, acc = carry
        rows = pl.ds(pl.multiple_of(j * tk, tk), tk)
        s = _dot_nt(q, k_ref[0, rows, :])
        m_new = jnp.maximum(m, jnp.max(s, axis=-1, keepdims=True))
        alpha = jnp.exp(m - m_new)
        p = jnp.exp(s - m_new)
        l = alpha * l + jnp.sum(p, axis=-1, keepdims=True)
        acc = alpha * acc + _dot(p.astype(BF16), v_ref[0, rows, :])
        return m_new, l, acc

    init = (jnp.full((tq, 1), -jnp.inf, F32), jnp.zeros((tq, 1), F32), jnp.zeros((tq, MLA_V), F32))
    m, l, acc = lax.fori_loop(0, T // tk, body, init)
    o_ref[...] = (acc / l).astype(BF16)


def _flash(q, k, v, tq=512, tk=512):
    H, T, _ = q.shape
    tq, tk = min(tq, T), min(tk, T)
    return pl.pallas_call(
        functools.partial(_flash_kernel, tk=tk),
        grid=(H, T // tq),
        in_specs=[
            pl.BlockSpec((1, tq, MLA_QK), lambda h, i: (h, i, 0)),
            pl.BlockSpec((1, T, MLA_QK), lambda h, i: (h, 0, 0)),
            pl.BlockSpec((1, T, MLA_V), lambda h, i: (h, 0, 0)),
        ],
        out_specs=pl.BlockSpec((tq, MLA_V), lambda h, i: (i, h)),
        out_shape=jax.ShapeDtypeStruct((T, H * MLA_V), BF16),
        compiler_params=_cparams(("arbitrary", "arbitrary")),
        name="mla_flash",
    )(q, k, v)


def _proj_residual_kernel(o_ref, w_ref, h_ref, out_ref):
    out_ref[...] = h_ref[...] + _dot(o_ref[...], w_ref[...])


def _proj_residual(o, w, h, tm=512):
    T, K = o.shape
    wb = w.astype(BF16)
    return pl.pallas_call(
        _proj_residual_kernel,
        grid=(T // tm,),
        in_specs=[pl.BlockSpec((tm, K), lambda i: (i, 0)), pl.BlockSpec(wb.shape, lambda i: (0, 0)),
                  pl.BlockSpec((tm, D_MODEL), lambda i: (i, 0))],
        out_specs=pl.BlockSpec((tm, D_MODEL), lambda i: (i, 0)),
        out_shape=jax.ShapeDtypeStruct((T, D_MODEL), F32),
        compiler_params=_cparams(("arbitrary",)),
        name="mla_out",
    )(o, wb, h)


def _mla_mixer(h, positions, gain, w_in, q_norm, w_uq, kv_norm, w_ukv, w_out):
    T = h.shape[0]
    H, half = MLA_HEADS, MLA_ROPE // 2
    qn, q1, q2, kn, v, kr = _mla_in(h, positions, gain, w_in, q_norm, w_uq, kv_norm, w_ukv)
    q = jnp.concatenate([qn.reshape(T, H, MLA_NOPE), q1.reshape(T, H, half), q2.reshape(T, H, half)], axis=-1)
    k = jnp.concatenate([kn.reshape(T, H, MLA_NOPE), jnp.broadcast_to(kr[:, None, :], (T, H, MLA_ROPE))], axis=-1)
    o = _flash(q.transpose(1, 0, 2), k.transpose(1, 0, 2), v.reshape(T, H, MLA_V).transpose(1, 0, 2))
    return _proj_residual(o, w_out, h)


def kernel(x, positions, mix_norm, ffn_norm, final_norm, gla_w_in, gla_w_gate_up_f, gla_b_gate_f, gla_w_gate_up_b,
           gla_b_gate_b, gla_head_norm, gla_w_out, mla_w_in, mla_q_norm, mla_w_uq, mla_kv_norm, mla_w_ukv,
           mla_w_out, moe_w_router, moe_w_gate, moe_w_up, moe_w_down):
    B, T, D = x.shape
    outs = []
    for b in range(B):
        h = x[b]
        h = _gla_mixer(h, mix_norm[0], gla_w_in[0], gla_w_gate_up_f[0], gla_b_gate_f[0], gla_w_gate_up_b[0],
                       gla_b_gate_b[0], gla_head_norm[0], gla_w_out[0])
        h = _ec_moe(h, ffn_norm[0], moe_w_router[0], moe_w_gate, moe_w_up, moe_w_down, 0)
        h = _mla_mixer(h, positions[b], mix_norm[1], mla_w_in[0], mla_q_norm[0], mla_w_uq[0], mla_kv_norm[0],
                       mla_w_ukv[0], mla_w_out[0])
        h = _ec_moe(h, ffn_norm[1], moe_w_router[1], moe_w_gate, moe_w_up, moe_w_down, 1, final_gain=final_norm)
        outs.append(h)
    return jnp.stack(outs)
```

```python
import functools
import math

import numpy as np
import jax
import jax.numpy as jnp
from jax import lax
from jax.experimental import pallas as pl
from jax.experimental.pallas import tpu as pltpu

F32 = jnp.float32
BF16 = jnp.bfloat16

D_MODEL = 1024
RMS_EPS = 1e-6

GLA_HEADS = 4
GLA_DK = 512
GLA_DV = 1024
GLA_HEAD_K = GLA_DK // GLA_HEADS
GLA_HEAD_V = GLA_DV // GLA_HEADS
GLA_GATE_RANK = 16
GLA_TAU = 16.0
GLA_CHUNK = 64
GLA_TILE = 256
GLA_LEVELS = 6

MLA_HEADS = 16
MLA_Q_RANK = 256
MLA_KV_RANK = 128
MLA_NOPE = 128
MLA_ROPE = 64
MLA_V = 128
MLA_QK = MLA_NOPE + MLA_ROPE
ROPE_BASE = 10000.0

N_EXPERTS = 16
EXPERT_FF = 2048
EC_CAPACITY_FACTOR = 2
MOE_TILE = 256
BF16_SUBLANES = 16
F32_SUBLANES = 8
LANES = 128

VMEM_LIMIT = 56 * 1024 * 1024


def _cparams(sem):
    return pltpu.CompilerParams(dimension_semantics=sem, vmem_limit_bytes=VMEM_LIMIT)


def _rms(x, g):
    return x * lax.rsqrt(jnp.mean(x * x, axis=-1, keepdims=True) + RMS_EPS) * g


def _split_bf16(x):
    hi = x.astype(BF16)
    lo = (x - hi.astype(F32)).astype(BF16)
    return hi, lo


def _dot(a, b):
    return jnp.dot(a, b, preferred_element_type=F32)


def _dot_nt(a, b):
    return lax.dot_general(a, b, (((1,), (1,)), ((), ())), preferred_element_type=F32)


def _dot_tn(a, b):
    return lax.dot_general(a, b, (((0,), (0,)), ((), ())), preferred_element_type=F32)


def _dot_split(a, b):
    ah, al = _split_bf16(a)
    bh, bl = _split_bf16(b)
    return _dot(ah, bh) + _dot(ah, bl) + _dot(al, bh)


def _gla_in_kernel(x_ref, g_ref, wqk_ref, wv_ref, wr_ref, wgd_ref, qk_ref, v_ref, r_ref, gd_ref):
    hn = _rms(x_ref[...], g_ref[...]).astype(BF16)
    qk_ref[...] = _dot(hn, wqk_ref[...])
    v_ref[...] = _dot(hn, wv_ref[...])
    r_ref[...] = _dot(hn, wr_ref[...])
    gd_ref[...] = _dot(hn, wgd_ref[...])


def _gla_in(x, gain, w_in, tm=512):
    T = x.shape[0]
    wqk = w_in[:, :2 * GLA_DK].astype(BF16)
    wv = w_in[:, 2 * GLA_DK:2 * GLA_DK + GLA_DV].astype(BF16)
    wr = w_in[:, 2 * GLA_DK + GLA_DV:2 * GLA_DK + 2 * GLA_DV].astype(BF16)
    wgd = w_in[:, 2 * GLA_DK + 2 * GLA_DV:].astype(BF16)
    ngd = 2 * GLA_GATE_RANK
    row = lambda n: pl.BlockSpec((tm, n), lambda i: (i, 0))
    full = lambda a: pl.BlockSpec(a.shape, lambda i: (0, 0))
    gain2 = gain.reshape(1, D_MODEL)
    return pl.pallas_call(
        _gla_in_kernel,
        grid=(T // tm,),
        in_specs=[row(D_MODEL), full(gain2), full(wqk), full(wv), full(wr), full(wgd)],
        out_specs=[row(2 * GLA_DK), row(GLA_DV), row(GLA_DV), row(ngd)],
        out_shape=[jax.ShapeDtypeStruct((T, 2 * GLA_DK), F32), jax.ShapeDtypeStruct((T, GLA_DV), F32),
                   jax.ShapeDtypeStruct((T, GLA_DV), F32), jax.ShapeDtypeStruct((T, ngd), F32)],
        compiler_params=_cparams(("arbitrary",)),
        name="gla_in",
    )(x, gain2, wqk, wv, wr, wgd)


def _gla_tables(reverse):
    n, c = GLA_TILE, GLA_CHUNK
    W = np.zeros((GLA_LEVELS + 3, n, n), np.float32)
    L = np.full((n, n), -1, np.int32)
    for t in range(n):
        c0 = (t // c) * c
        tt = t - c0
        for l in range(GLA_LEVELS):
            b = (c // 2) >> l
            p0 = c0 + (tt // (2 * b)) * 2 * b
            mid = p0 + b
            second = t >= mid
            if not reverse:
                if second:
                    W[l, t, mid:t + 1] = 1
                    L[t, p0:mid] = l
                else:
                    W[l, t, t + 1:mid] = 1
            else:
                if second:
                    W[l, t, mid:t] = 1
                else:
                    W[l, t, t:mid] = 1
                    L[t, mid:p0 + 2 * b] = l
        if not reverse:
            W[GLA_LEVELS, t, c0:t + 1] = 1
            W[GLA_LEVELS + 1, t, t + 1:c0 + c] = 1
            L[t, t] = GLA_LEVELS
        else:
            W[GLA_LEVELS, t, t:c0 + c] = 1
            W[GLA_LEVELS + 1, t, c0:t] = 1
        W[GLA_LEVELS + 2, t, c0:c0 + c] = 1
    return W.reshape(-1, n), L


def _gla_scan_kernel(q_ref, k_ref, v_ref, gd_ref, wup_ref, b_ref, w_ref, l_ref, o_ref, s_ref, *, reverse):
    n, c = GLA_TILE, GLA_CHUNK

    @pl.when(pl.program_id(1) == 0)
    def _():
        s_ref[...] = jnp.zeros_like(s_ref)

    q = q_ref[...] * (GLA_HEAD_K ** -0.5)
    k = k_ref[...]
    vb = v_ref[...].astype(BF16)
    z = _dot_split(gd_ref[...], wup_ref[...]) + b_ref[...]
    g = (jnp.minimum(z, 0.0) - jnp.log1p(jnp.exp(-jnp.abs(z)))) * (1.0 / GLA_TAU)
    ghi, glo = _split_bf16(g)
    w = w_ref[...]
    f = jnp.exp(_dot(w, ghi) + _dot(w, glo))
    lvl = l_ref[...]

    attn = jnp.zeros((n, n), F32)
    for l in range(GLA_LEVELS):
        fl = f[l * n:(l + 1) * n]
        p = _dot_nt((q * fl).astype(BF16), (k * fl).astype(BF16))
        attn = jnp.where(lvl == l, p, attn)
    if not reverse:
        p = _dot_nt(q.astype(BF16), k.astype(BF16))
        attn = jnp.where(lvl == GLA_LEVELS, p, attn)
    o_intra = _dot(attn.astype(BF16), vb)

    fq = f[GLA_LEVELS * n:(GLA_LEVELS + 1) * n]
    fk = f[(GLA_LEVELS + 1) * n:(GLA_LEVELS + 2) * n]
    ftot = f[(GLA_LEVELS + 2) * n:(GLA_LEVELS + 3) * n]
    qh = (q * fq).astype(BF16)
    kh = (k * fk).astype(BF16)
    st = s_ref[...]
    chunks = range(n // c)
    for j in (reversed(chunks) if reverse else chunks):
        rows = slice(j * c, (j + 1) * c)
        o_ref[rows, :] = o_intra[rows] + _dot_nt(qh[rows], st.astype(BF16))
        st = st * ftot[j * c:j * c + 1, :] + _dot_tn(vb[rows], kh[rows])
    s_ref[...] = st


def _gla_scan(qk, v, gd, w_up, b, reverse):
    T = qk.shape[0]
    n = GLA_TILE
    nt = T // n
    W, L = _gla_tables(reverse)
    W = jnp.asarray(W, BF16)
    L = jnp.asarray(L)
    tidx = (lambda i: nt - 1 - i) if reverse else (lambda i: i)
    r = GLA_GATE_RANK
    gdd = gd[:, r:2 * r] if reverse else gd[:, :r]
    b2 = b.reshape(1, GLA_DK)
    return pl.pallas_call(
        functools.partial(_gla_scan_kernel, reverse=reverse),
        grid=(GLA_HEADS, nt),
        in_specs=[
            pl.BlockSpec((n, GLA_HEAD_K), lambda h, i: (tidx(i), h)),
            pl.BlockSpec((n, GLA_HEAD_K), lambda h, i: (tidx(i), GLA_HEADS + h)),
            pl.BlockSpec((n, GLA_HEAD_V), lambda h, i: (tidx(i), h)),
            pl.BlockSpec((n, r), lambda h, i: (tidx(i), 0)),
            pl.BlockSpec((r, GLA_HEAD_K), lambda h, i: (0, h)),
            pl.BlockSpec((1, GLA_HEAD_K), lambda h, i: (0, h)),
            pl.BlockSpec(W.shape, lambda h, i: (0, 0)),
            pl.BlockSpec(L.shape, lambda h, i: (0, 0)),
        ],
        out_specs=pl.BlockSpec((n, GLA_HEAD_V), lambda h, i: (tidx(i), h)),
        out_shape=jax.ShapeDtypeStruct((T, GLA_DV), F32),
        scratch_shapes=[pltpu.VMEM((GLA_HEAD_V, GLA_HEAD_K), F32)],
        compiler_params=_cparams(("arbitrary", "arbitrary")),
        name="gla_scan_bwd" if reverse else "gla_scan_fwd",
    )(qk, qk, v, gdd, w_up, b2, W, L)


def _gla_out_kernel(of_ref, ob_ref, r_ref, x_ref, hn_ref, w_ref, o_ref):
    acc = x_ref[...]
    for h in range(GLA_HEADS):
        cols = slice(h * GLA_HEAD_V, (h + 1) * GLA_HEAD_V)
        o = _rms(of_ref[:, cols] + ob_ref[:, cols], hn_ref[...])
        r = r_ref[:, cols]
        gated = o * (r * (1.0 / (1.0 + jnp.exp(-r))))
        acc = acc + _dot(gated.astype(BF16), w_ref[cols, :])
    o_ref[...] = acc


def _gla_out(of, ob, r, x, head_norm, w_out, tm=512):
    T = x.shape[0]
    w = w_out.astype(BF16)
    hn = head_norm.reshape(1, GLA_HEAD_V)
    row = lambda n: pl.BlockSpec((tm, n), lambda i: (i, 0))
    full = lambda a: pl.BlockSpec(a.shape, lambda i: (0, 0))
    return pl.pallas_call(
        _gla_out_kernel,
        grid=(T // tm,),
        in_specs=[row(GLA_DV), row(GLA_DV), row(GLA_DV), row(D_MODEL), full(hn), full(w)],
        out_specs=row(D_MODEL),
        out_shape=jax.ShapeDtypeStruct((T, D_MODEL), F32),
        compiler_params=_cparams(("arbitrary",)),
        name="gla_out",
    )(of, ob, r, x, hn, w)


def _gla_mixer(x, gain, w_in, w_up_f, b_f, w_up_b, b_b, head_norm, w_out):
    qk, v, r, gd = _gla_in(x, gain, w_in)
    of = _gla_scan(qk, v, gd, w_up_f, b_f, reverse=False)
    ob = _gla_scan(qk, v, gd, w_up_b, b_b, reverse=True)
    return _gla_out(of, ob, r, x, head_norm, w_out)


def _router_kernel(h_ref, g_ref, w_ref, aff_ref, hn_ref):
    hn = _rms(h_ref[...], g_ref[...])
    hn_ref[...] = hn.astype(BF16)
    logits = _dot_split(hn, w_ref[...])
    e = jnp.exp(logits - jnp.max(logits, axis=-1, keepdims=True))
    aff_ref[...] = e / jnp.sum(e, axis=-1, keepdims=True)


def _router(h, gain, w_router, tm=512):
    T = h.shape[0]
    gain2 = gain.reshape(1, D_MODEL)
    row = lambda n: pl.BlockSpec((tm, n), lambda i: (i, 0))
    full = lambda a: pl.BlockSpec(a.shape, lambda i: (0, 0))
    return pl.pallas_call(
        _router_kernel,
        grid=(T // tm,),
        in_specs=[row(D_MODEL), full(gain2), full(w_router)],
        out_specs=[row(N_EXPERTS), row(D_MODEL)],
        out_shape=[jax.ShapeDtypeStruct((T, N_EXPERTS), F32), jax.ShapeDtypeStruct((T, D_MODEL), BF16)],
        compiler_params=_cparams(("arbitrary",)),
        name="moe_router",
    )(h, gain2, w_router)


def _select_kernel(aff_ref, tri_ref, posm_ref, before_ref, *, cap):
    T = aff_ref.shape[1]
    bits = pltpu.bitcast(aff_ref[...], jnp.int32)

    def search(it, thr):
        cand = thr | jnp.left_shift(jnp.int32(1), 30 - it)
        cnt = jnp.sum(jnp.where(bits >= cand, 1.0, 0.0), axis=1, keepdims=True)
        return jnp.where(cnt >= cap, cand, thr)

    thr = lax.fori_loop(0, 31, search, jnp.zeros((N_EXPERTS, 1), jnp.int32))
    n_gt = jnp.sum(jnp.where(bits > thr, 1.0, 0.0), axis=1, keepdims=True)
    need = cap - n_gt
    tri = tri_ref[...]

    def scan(j, carry):
        c_eq, c_sel = carry
        cols = pl.ds(pl.multiple_of(j * LANES, LANES), LANES)
        blk = pltpu.bitcast(aff_ref[:, cols], jnp.int32)
        eq = jnp.where(blk == thr, 1.0, 0.0)
        rank = _dot(eq.astype(BF16), tri) + c_eq - eq
        sel = jnp.where((blk > thr) | ((eq > 0.0) & (rank < need)), 1.0, 0.0)
        before = _dot(sel.astype(BF16), tri) + c_sel - sel
        before_ref[:, cols] = before.astype(jnp.int32)
        posm_ref[:, cols] = jnp.where(sel > 0.0, before, -1.0).astype(jnp.int32)
        return (c_eq + jnp.sum(eq, axis=1, keepdims=True), c_sel + jnp.sum(sel, axis=1, keepdims=True))

    zero = jnp.zeros((N_EXPERTS, 1), F32)
    lax.fori_loop(0, T // LANES, scan, (zero, zero))


def _select(aff_t, cap):
    T = aff_t.shape[1]
    tri = jnp.asarray(np.triu(np.ones((LANES, LANES), np.float32)), BF16)
    full = lambda a: pl.BlockSpec(a.shape, lambda: (0,) * a.ndim)
    out = jax.ShapeDtypeStruct((N_EXPERTS, T), jnp.int32)
    return pl.pallas_call(
        functools.partial(_select_kernel, cap=cap),
        in_specs=[full(aff_t), full(tri)],
        out_specs=[pl.BlockSpec((N_EXPERTS, T), lambda: (0, 0))] * 2,
        out_shape=[out, out],
        compiler_params=pltpu.CompilerParams(vmem_limit_bytes=VMEM_LIMIT),
        name="moe_select",
    )(aff_t, tri)


def _dispatch_window():
    return MOE_TILE + F32_SUBLANES


def _dispatch_kernel(start_ref, hn_ref, posm_ref, x_ref, acc_ref, *, cap, nt):
    e, t = pl.program_id(0), pl.program_id(1)
    win = _dispatch_window()

    @pl.when(t == 0)
    def _():
        acc_ref[...] = jnp.zeros_like(acc_ref)

    base = pl.multiple_of((start_ref[e * nt + t] // F32_SUBLANES) * F32_SUBLANES, F32_SUBLANES)
    slot = base + lax.broadcasted_iota(jnp.int32, (win, MOE_TILE), 0)
    onehot = jnp.where(posm_ref[0] == slot, 1.0, 0.0).astype(BF16)
    acc_ref[pl.ds(base, win), :] += _dot(onehot, hn_ref[...])

    @pl.when(t == nt - 1)
    def _():
        x_ref[0] = acc_ref[:cap, :].astype(BF16)


def _dispatch(hn, posm, start, cap):
    T = hn.shape[0]
    nt = T // MOE_TILE
    posm3 = posm.reshape(N_EXPERTS, 1, T)
    grid_spec = pltpu.PrefetchScalarGridSpec(
        num_scalar_prefetch=1,
        grid=(N_EXPERTS, nt),
        in_specs=[
            pl.BlockSpec((MOE_TILE, D_MODEL), lambda e, t, s: (t, 0)),
            pl.BlockSpec((1, 1, MOE_TILE), lambda e, t, s: (e, 0, t)),
        ],
        out_specs=pl.BlockSpec((1, cap, D_MODEL), lambda e, t, s: (e, 0, 0)),
        scratch_shapes=[pltpu.VMEM((cap + _dispatch_window(), D_MODEL), F32)],
    )
    return pl.pallas_call(
        functools.partial(_dispatch_kernel, cap=cap, nt=nt),
        grid_spec=grid_spec,
        out_shape=jax.ShapeDtypeStruct((N_EXPERTS, cap, D_MODEL), BF16),
        compiler_params=_cparams(("arbitrary", "arbitrary")),
        name="moe_dispatch",
    )(start.reshape(-1), hn, posm3)


def _ffn_kernel(x_ref, wg_ref, wu_ref, wd_ref, y_ref, acc_ref):
    f = pl.program_id(1)

    @pl.when(f == 0)
    def _():
        acc_ref[...] = jnp.zeros_like(acc_ref)

    x = x_ref[0]
    a = _dot(x, wg_ref[0, 0].astype(BF16))
    u = _dot(x, wu_ref[0, 0].astype(BF16))
    mid = (a * (1.0 / (1.0 + jnp.exp(-a))) * u).astype(BF16)
    acc_ref[...] += _dot(mid, wd_ref[0, 0].astype(BF16))

    @pl.when(f == pl.num_programs(1) - 1)
    def _():
        y_ref[0] = acc_ref[...].astype(BF16)


def _ffn(x, w_gate, w_up, w_down, layer, tf=512):
    cap = x.shape[1]
    return pl.pallas_call(
        _ffn_kernel,
        grid=(N_EXPERTS, EXPERT_FF // tf),
        in_specs=[
            pl.BlockSpec((1, cap, D_MODEL), lambda e, f: (e, 0, 0)),
            pl.BlockSpec((1, 1, D_MODEL, tf), lambda e, f: (layer, e, 0, f)),
            pl.BlockSpec((1, 1, D_MODEL, tf), lambda e, f: (layer, e, 0, f)),
            pl.BlockSpec((1, 1, tf, D_MODEL), lambda e, f: (layer, e, f, 0)),
        ],
        out_specs=pl.BlockSpec((1, cap, D_MODEL), lambda e, f: (e, 0, 0)),
        out_shape=jax.ShapeDtypeStruct((N_EXPERTS, cap, D_MODEL), BF16),
        scratch_shapes=[pltpu.VMEM((cap, D_MODEL), F32)],
        compiler_params=_cparams(("arbitrary", "arbitrary")),
        name="moe_ffn",
    )(x, w_gate, w_up, w_down)


def _combine_window():
    return MOE_TILE + BF16_SUBLANES


def _combine_kernel(start_ref, h_ref, aff_ref, posm_ref, g_ref, y_hbm, o_ref, buf_ref, sem_ref, *,
                    cap, nt, final_norm):
    t = pl.program_id(0)
    win = _combine_window()

    def base_of(e):
        b = (start_ref[e * nt + t] // BF16_SUBLANES) * BF16_SUBLANES
        return pl.multiple_of(jnp.minimum(b, cap - win), BF16_SUBLANES)

    def window_copy(e, slot):
        return pltpu.make_async_copy(y_hbm.at[e, pl.ds(base_of(e), win), :], buf_ref.at[slot], sem_ref.at[slot])

    window_copy(0, 0).start()
    acc = h_ref[...]
    posm = posm_ref[...]
    aff = aff_ref[...]
    for e in range(N_EXPERTS):
        slot = e % 2
        if e + 1 < N_EXPERTS:
            window_copy(e + 1, 1 - slot).start()
        window_copy(e, slot).wait()
        pos_e = posm[:, e:e + 1]
        base = base_of(e)
        hit_main = pos_e == base + lax.broadcasted_iota(jnp.int32, (MOE_TILE, MOE_TILE), 1)
        hit_tail = pos_e == base + MOE_TILE + lax.broadcasted_iota(jnp.int32, (MOE_TILE, BF16_SUBLANES), 1)
        y = _dot(jnp.where(hit_main, 1.0, 0.0).astype(BF16), buf_ref[slot, :MOE_TILE, :])
        y = y + _dot(jnp.where(hit_tail, 1.0, 0.0).astype(BF16), buf_ref[slot, MOE_TILE:, :])
        acc = acc + aff[:, e:e + 1] * y
    if final_norm:
        acc = _rms(acc, g_ref[...])
    o_ref[...] = acc


def _combine(h, aff, posm_t, start, y, cap, final_gain):
    T = h.shape[0]
    nt = T // MOE_TILE
    win = _combine_window()
    assert cap >= win and (cap - win) % BF16_SUBLANES == 0
    final_norm = final_gain is not None
    gain = (final_gain if final_norm else jnp.ones((D_MODEL,), F32)).reshape(1, D_MODEL)
    grid_spec = pltpu.PrefetchScalarGridSpec(
        num_scalar_prefetch=1,
        grid=(nt,),
        in_specs=[
            pl.BlockSpec((MOE_TILE, D_MODEL), lambda t, s: (t, 0)),
            pl.BlockSpec((MOE_TILE, N_EXPERTS), lambda t, s: (t, 0)),
            pl.BlockSpec((MOE_TILE, N_EXPERTS), lambda t, s: (t, 0)),
            pl.BlockSpec((1, D_MODEL), lambda t, s: (0, 0)),
            pl.BlockSpec(memory_space=pl.ANY),
        ],
        out_specs=pl.BlockSpec((MOE_TILE, D_MODEL), lambda t, s: (t, 0)),
        scratch_shapes=[pltpu.VMEM((2, win, D_MODEL), BF16), pltpu.SemaphoreType.DMA((2,))],
    )
    return pl.pallas_call(
        functools.partial(_combine_kernel, cap=cap, nt=nt, final_norm=final_norm),
        grid_spec=grid_spec,
        out_shape=jax.ShapeDtypeStruct((T, D_MODEL), F32),
        compiler_params=_cparams(("arbitrary",)),
        name="moe_combine",
    )(start.reshape(-1), h, aff, posm_t, gain, y)


def _ec_moe(h, gain, w_router, w_gate, w_up, w_down, layer, final_gain=None):
    T = h.shape[0]
    cap = max(1, EC_CAPACITY_FACTOR * T // N_EXPERTS)
    aff, hn = _router(h, gain, w_router)
    posm, before = _select(aff.T, cap)
    start = before[:, ::MOE_TILE]
    x = _dispatch(hn, posm, start, cap)
    y = _ffn(x, w_gate, w_up, w_down, layer)
    return _combine(h, aff, posm.T, start, y, cap, final_gain)


def _rms_cols(x, g):
    return x * lax.rsqrt(jnp.mean(x * x, axis=0, keepdims=True) + RMS_EPS) * g


def _mla_in_kernel(h_ref, posc_ref, posr_ref, g_ref, win_ref, wint_ref, qn_ref, wuqt_ref, kvn_ref, kvnc_ref,
                   wkn_ref, wvt_ref, freqr_ref, freqc_ref, qt_out, k_out, vt_out):
    half = MLA_ROPE // 2
    hn = _rms(h_ref[...], g_ref[...]).astype(BF16)
    c = _dot(hn, win_ref[...])
    ct = _dot_nt(wint_ref[...], hn)

    cqt = _rms_cols(ct[:MLA_Q_RANK], qn_ref[...]).astype(BF16)
    qt = _dot(wuqt_ref[...], cqt) * (MLA_QK ** -0.5 * math.log2(math.e))
    ang_t = freqc_ref[...] * posr_ref[...].astype(F32)
    cos_t, sin_t = jnp.cos(ang_t), jnp.sin(ang_t)
    for hd in range(MLA_HEADS):
        r0 = hd * MLA_QK
        x1 = qt[r0 + MLA_NOPE:r0 + MLA_NOPE + half]
        x2 = qt[r0 + MLA_NOPE + half:r0 + MLA_QK]
        qt_out[hd, :MLA_NOPE, :] = qt[r0:r0 + MLA_NOPE].astype(BF16)
        qt_out[hd, MLA_NOPE:MLA_NOPE + half, :] = (x1 * cos_t - x2 * sin_t).astype(BF16)
        qt_out[hd, MLA_NOPE + half:, :] = (x1 * sin_t + x2 * cos_t).astype(BF16)

    ckvt = _rms_cols(ct[MLA_Q_RANK:MLA_Q_RANK + MLA_KV_RANK], kvnc_ref[...]).astype(BF16)
    vt = _dot(wvt_ref[...], ckvt)
    for hd in range(MLA_HEADS):
        vt_out[hd] = vt[hd * MLA_V:(hd + 1) * MLA_V].astype(BF16)

    ckv = _rms(c[:, MLA_Q_RANK:MLA_Q_RANK + MLA_KV_RANK], kvn_ref[...]).astype(BF16)
    kn = _dot(ckv, wkn_ref[...])
    kr = c[:, MLA_Q_RANK + MLA_KV_RANK:]
    ang = posc_ref[...].astype(F32) * freqr_ref[...]
    cos, sin = jnp.cos(ang), jnp.sin(ang)
    k1, k2 = kr[:, :half], kr[:, half:]
    kr_rot = jnp.concatenate([k1 * cos - k2 * sin, k1 * sin + k2 * cos], axis=1).astype(BF16)
    for hd in range(MLA_HEADS):
        k_out[hd, :, :MLA_NOPE] = kn[:, hd * MLA_NOPE:(hd + 1) * MLA_NOPE].astype(BF16)
        k_out[hd, :, MLA_NOPE:] = kr_rot


def _mla_in(h, positions, gain, w_in, q_norm, w_uq, kv_norm, w_ukv, tm=256):
    T = h.shape[0]
    H, half = MLA_HEADS, MLA_ROPE // 2
    wkv = w_ukv.reshape(MLA_KV_RANK, H, MLA_NOPE + MLA_V)
    wkn = wkv[:, :, :MLA_NOPE].reshape(MLA_KV_RANK, H * MLA_NOPE).astype(BF16)
    wvt = wkv[:, :, MLA_NOPE:].reshape(MLA_KV_RANK, H * MLA_V).T.astype(BF16)
    inv_freq = ROPE_BASE ** (-jnp.arange(half, dtype=F32) / half)
    w_in_b = w_in.astype(BF16)
    args = [h, positions.reshape(T, 1), positions.reshape(1, T), gain.reshape(1, -1), w_in_b, w_in_b.T,
            q_norm.reshape(-1, 1), w_uq.T.astype(BF16), kv_norm.reshape(1, -1), kv_norm.reshape(-1, 1), wkn, wvt,
            inv_freq.reshape(1, half), inv_freq.reshape(half, 1)]
    full = lambda a: pl.BlockSpec(a.shape, lambda i: (0, 0))
    in_specs = [pl.BlockSpec((tm, D_MODEL), lambda i: (i, 0)), pl.BlockSpec((tm, 1), lambda i: (i, 0)),
                pl.BlockSpec((1, tm), lambda i: (0, i))] + [full(a) for a in args[3:]]
    return pl.pallas_call(
        _mla_in_kernel,
        grid=(T // tm,),
        in_specs=in_specs,
        out_specs=[pl.BlockSpec((H, MLA_QK, tm), lambda i: (0, 0, i)),
                   pl.BlockSpec((H, tm, MLA_QK), lambda i: (0, i, 0)),
                   pl.BlockSpec((H, MLA_V, tm), lambda i: (0, 0, i))],
        out_shape=[jax.ShapeDtypeStruct((H, MLA_QK, T), BF16), jax.ShapeDtypeStruct((H, T, MLA_QK), BF16),
                   jax.ShapeDtypeStruct((H, MLA_V, T), BF16)],
        compiler_params=_cparams(("arbitrary",)),
        name="mla_in",
    )(*args)


def _flash_kernel(qt_ref, k_ref, vt_ref, o_ref, sa_ref, sb_ref, pa_ref, pb_ref, acc_ref, *, tk):
    T = k_ref.shape[1]
    tq = qt_ref.shape[2]
    n = T // tk
    qt = qt_ref[0]

    def k_tile(j):
        return k_ref[0, pl.ds(pl.multiple_of(j * tk, tk), tk), :]

    def v_tile(j):
        return vt_ref[0, :, pl.ds(pl.multiple_of(j * tk, tk), tk)]

    def step(j, s_cur, s_nxt, p_cur, p_prv, carry):
        m, l, alpha_prv = carry
        acc_ref[...] = alpha_prv * acc_ref[...] + _dot(v_tile(jnp.maximum(j - 1, 0)), p_prv[...])
        s_nxt[...] = _dot(k_tile(jnp.minimum(j + 1, n - 1)), qt)
        s = s_cur[...]
        m_new = jnp.maximum(m, jnp.max(s, axis=0, keepdims=True))
        alpha = jnp.exp2(m - m_new)
        p = jnp.exp2(s - m_new)
        p_cur[...] = p.astype(BF16)
        return m_new, alpha * l + jnp.sum(p, axis=0, keepdims=True), alpha

    def body(i, carry):
        carry = step(2 * i, sa_ref, sb_ref, pa_ref, pb_ref, carry)
        return step(2 * i + 1, sb_ref, sa_ref, pb_ref, pa_ref, carry)

    acc_ref[...] = jnp.zeros_like(acc_ref)
    pb_ref[...] = jnp.zeros_like(pb_ref)
    sa_ref[...] = _dot(k_tile(0), qt)
    init = (jnp.full((1, tq), -jnp.inf, F32), jnp.zeros((1, tq), F32), jnp.ones((1, tq), F32))
    m, l, alpha = lax.fori_loop(0, n // 2, body, init)
    acc = alpha * acc_ref[...] + _dot(v_tile(n - 1), pb_ref[...])
    o_ref[...] = (acc / l).T.astype(BF16)


def _flash(qt, k, vt, tq=512, tk=512):
    H, T, _ = k.shape
    tq, tk = min(tq, T), min(tk, T)
    assert (T // tk) % 2 == 0
    return pl.pallas_call(
        functools.partial(_flash_kernel, tk=tk),
        grid=(H, T // tq),
        scratch_shapes=[pltpu.VMEM((tk, tq), F32), pltpu.VMEM((tk, tq), F32), pltpu.VMEM((tk, tq), BF16),
                        pltpu.VMEM((tk, tq), BF16), pltpu.VMEM((MLA_V, tq), F32)],
        in_specs=[
            pl.BlockSpec((1, MLA_QK, tq), lambda h, i: (h, 0, i)),
            pl.BlockSpec((1, T, MLA_QK), lambda h, i: (h, 0, 0)),
            pl.BlockSpec((1, MLA_V, T), lambda h, i: (h, 0, 0)),
        ],
        out_specs=pl.BlockSpec((tq, MLA_V), lambda h, i: (i, h)),
        out_shape=jax.ShapeDtypeStruct((T, H * MLA_V), BF16),
        compiler_params=_cparams(("arbitrary", "arbitrary")),
        name="mla_flash",
    )(qt, k, vt)


def _proj_residual_kernel(o_ref, w_ref, h_ref, out_ref):
    out_ref[...] = h_ref[...] + _dot(o_ref[...], w_ref[...])


def _proj_residual(o, w, h, tm=512):
    T, K = o.shape
    wb = w.astype(BF16)
    return pl.pallas_call(
        _proj_residual_kernel,
        grid=(T // tm,),
        in_specs=[pl.BlockSpec((tm, K), lambda i: (i, 0)), pl.BlockSpec(wb.shape, lambda i: (0, 0)),
                  pl.BlockSpec((tm, D_MODEL), lambda i: (i, 0))],
        out_specs=pl.BlockSpec((tm, D_MODEL), lambda i: (i, 0)),
        out_shape=jax.ShapeDtypeStruct((T, D_MODEL), F32),
        compiler_params=_cparams(("arbitrary",)),
        name="mla_out",
    )(o, wb, h)


def _mla_mixer(h, positions, gain, w_in, q_norm, w_uq, kv_norm, w_ukv, w_out):
    qt, k, vt = _mla_in(h, positions, gain, w_in, q_norm, w_uq, kv_norm, w_ukv)
    o = _flash(qt, k, vt)
    return _proj_residual(o, w_out, h)


def kernel(x, positions, mix_norm, ffn_norm, final_norm, gla_w_in, gla_w_gate_up_f, gla_b_gate_f, gla_w_gate_up_b,
           gla_b_gate_b, gla_head_norm, gla_w_out, mla_w_in, mla_q_norm, mla_w_uq, mla_kv_norm, mla_w_ukv,
           mla_w_out, moe_w_router, moe_w_gate, moe_w_up, moe_w_down):
    B, T, D = x.shape
    outs = []
    for b in range(B):
        h = x[b]
        h = _gla_mixer(h, mix_norm[0], gla_w_in[0], gla_w_gate_up_f[0], gla_b_gate_f[0], gla_w_gate_up_b[0],
                       gla_b_gate_b[0], gla_head_norm[0], gla_w_out[0])
        h = _ec_moe(h, ffn_norm[0], moe_w_router[0], moe_w_gate, moe_w_up, moe_w_down, 0)
        h = _mla_mixer(h, positions[b], mix_norm[1], mla_w_in[0], mla_q_norm[0], mla_w_uq[0], mla_kv_norm[0],
                       mla_w_ukv[0], mla_w_out[0])
        h = _ec_moe(h, ffn_norm[1], moe_w_router[1], moe_w_gate, moe_w_up, moe_w_down, 1, final_gain=final_norm)
        outs.append(h)
    return jnp.stack(outs)
```

```python
import functools
import math

import numpy as np
import jax
import jax.numpy as jnp
from jax import lax
from jax.experimental import pallas as pl
from jax.experimental.pallas import tpu as pltpu

F32 = jnp.float32
BF16 = jnp.bfloat16

D_MODEL = 1024
RMS_EPS = 1e-6

GLA_HEADS = 4
GLA_DK = 512
GLA_DV = 1024
GLA_HEAD_K = GLA_DK // GLA_HEADS
GLA_HEAD_V = GLA_DV // GLA_HEADS
GLA_GATE_RANK = 16
GLA_TAU = 16.0
GLA_CHUNK = 64
GLA_TILE = 256
GLA_LEVELS = 6

MLA_HEADS = 16
MLA_Q_RANK = 256
MLA_KV_RANK = 128
MLA_NOPE = 128
MLA_ROPE = 64
MLA_V = 128
MLA_QK = MLA_NOPE + MLA_ROPE
ROPE_BASE = 10000.0

N_EXPERTS = 16
EXPERT_FF = 2048
EC_CAPACITY_FACTOR = 2
MOE_TILE = 256
BF16_SUBLANES = 16
F32_SUBLANES = 8
LANES = 128

VMEM_LIMIT = 56 * 1024 * 1024


def _cparams(sem):
    return pltpu.CompilerParams(dimension_semantics=sem, vmem_limit_bytes=VMEM_LIMIT)


def _rms(x, g):
    return x * lax.rsqrt(jnp.mean(x * x, axis=-1, keepdims=True) + RMS_EPS) * g


def _split_bf16(x):
    hi = x.astype(BF16)
    lo = (x - hi.astype(F32)).astype(BF16)
    return hi, lo


def _dot(a, b):
    return jnp.dot(a, b, preferred_element_type=F32)


def _dot_nt(a, b):
    return lax.dot_general(a, b, (((1,), (1,)), ((), ())), preferred_element_type=F32)


def _dot_tn(a, b):
    return lax.dot_general(a, b, (((0,), (0,)), ((), ())), preferred_element_type=F32)


def _dot_split(a, b):
    ah, al = _split_bf16(a)
    bh, bl = _split_bf16(b)
    return _dot(ah, bh) + _dot(ah, bl) + _dot(al, bh)


def _gla_in_kernel(x_ref, g_ref, wqk_ref, wv_ref, wr_ref, wgd_ref, qk_ref, v_ref, r_ref, gd_ref):
    hn = _rms(x_ref[...], g_ref[...]).astype(BF16)
    qk_ref[...] = _dot(hn, wqk_ref[...])
    v_ref[...] = _dot(hn, wv_ref[...])
    r_ref[...] = _dot(hn, wr_ref[...])
    gd_ref[...] = _dot(hn, wgd_ref[...])


def _gla_in(x, gain, w_in, tm=512):
    T = x.shape[0]
    wqk = w_in[:, :2 * GLA_DK].astype(BF16)
    wv = w_in[:, 2 * GLA_DK:2 * GLA_DK + GLA_DV].astype(BF16)
    wr = w_in[:, 2 * GLA_DK + GLA_DV:2 * GLA_DK + 2 * GLA_DV].astype(BF16)
    wgd = w_in[:, 2 * GLA_DK + 2 * GLA_DV:].astype(BF16)
    ngd = 2 * GLA_GATE_RANK
    row = lambda n: pl.BlockSpec((tm, n), lambda i: (i, 0))
    full = lambda a: pl.BlockSpec(a.shape, lambda i: (0, 0))
    gain2 = gain.reshape(1, D_MODEL)
    return pl.pallas_call(
        _gla_in_kernel,
        grid=(T // tm,),
        in_specs=[row(D_MODEL), full(gain2), full(wqk), full(wv), full(wr), full(wgd)],
        out_specs=[row(2 * GLA_DK), row(GLA_DV), row(GLA_DV), row(ngd)],
        out_shape=[jax.ShapeDtypeStruct((T, 2 * GLA_DK), F32), jax.ShapeDtypeStruct((T, GLA_DV), F32),
                   jax.ShapeDtypeStruct((T, GLA_DV), F32), jax.ShapeDtypeStruct((T, ngd), F32)],
        compiler_params=_cparams(("arbitrary",)),
        name="gla_in",
    )(x, gain2, wqk, wv, wr, wgd)


def _gla_tables(reverse):
    n, c = GLA_TILE, GLA_CHUNK
    W = np.zeros((GLA_LEVELS + 3, n, n), np.float32)
    L = np.full((n, n), -1, np.int32)
    for t in range(n):
        c0 = (t // c) * c
        tt = t - c0
        for l in range(GLA_LEVELS):
            b = (c // 2) >> l
            p0 = c0 + (tt // (2 * b)) * 2 * b
            mid = p0 + b
            second = t >= mid
            if not reverse:
                if second:
                    W[l, t, mid:t + 1] = 1
                    L[t, p0:mid] = l
                else:
                    W[l, t, t + 1:mid] = 1
            else:
                if second:
                    W[l, t, mid:t] = 1
                else:
                    W[l, t, t:mid] = 1
                    L[t, mid:p0 + 2 * b] = l
        if not reverse:
            W[GLA_LEVELS, t, c0:t + 1] = 1
            W[GLA_LEVELS + 1, t, t + 1:c0 + c] = 1
            L[t, t] = GLA_LEVELS
        else:
            W[GLA_LEVELS, t, t:c0 + c] = 1
            W[GLA_LEVELS + 1, t, c0:t] = 1
        W[GLA_LEVELS + 2, t, c0:c0 + c] = 1
    return W.reshape(-1, n), L


def _gla_scan_kernel(q_ref, k_ref, v_ref, gd_ref, wup_ref, b_ref, w_ref, l_ref, o_ref, s_ref, *, reverse):
    n, c = GLA_TILE, GLA_CHUNK

    @pl.when(pl.program_id(1) == 0)
    def _():
        s_ref[...] = jnp.zeros_like(s_ref)

    q = q_ref[...] * (GLA_HEAD_K ** -0.5)
    k = k_ref[...]
    vb = v_ref[...].astype(BF16)
    z = _dot_split(gd_ref[...], wup_ref[...]) + b_ref[...]
    g = (jnp.minimum(z, 0.0) - jnp.log1p(jnp.exp(-jnp.abs(z)))) * (1.0 / GLA_TAU)
    ghi, glo = _split_bf16(g)
    w = w_ref[...]
    f = jnp.exp(_dot(w, ghi) + _dot(w, glo))
    lvl = l_ref[...]

    attn = jnp.zeros((n, n), F32)
    for l in range(GLA_LEVELS):
        fl = f[l * n:(l + 1) * n]
        p = _dot_nt((q * fl).astype(BF16), (k * fl).astype(BF16))
        attn = jnp.where(lvl == l, p, attn)
    if not reverse:
        p = _dot_nt(q.astype(BF16), k.astype(BF16))
        attn = jnp.where(lvl == GLA_LEVELS, p, attn)
    o_intra = _dot(attn.astype(BF16), vb)

    fq = f[GLA_LEVELS * n:(GLA_LEVELS + 1) * n]
    fk = f[(GLA_LEVELS + 1) * n:(GLA_LEVELS + 2) * n]
    ftot = f[(GLA_LEVELS + 2) * n:(GLA_LEVELS + 3) * n]
    qh = (q * fq).astype(BF16)
    kh = (k * fk).astype(BF16)
    st = s_ref[...]
    chunks = range(n // c)
    for j in (reversed(chunks) if reverse else chunks):
        rows = slice(j * c, (j + 1) * c)
        o_ref[rows, :] = o_intra[rows] + _dot_nt(qh[rows], st.astype(BF16))
        st = st * ftot[j * c:j * c + 1, :] + _dot_tn(vb[rows], kh[rows])
    s_ref[...] = st


def _gla_scan(qk, v, gd, w_up, b, reverse):
    T = qk.shape[0]
    n = GLA_TILE
    nt = T // n
    W, L = _gla_tables(reverse)
    W = jnp.asarray(W, BF16)
    L = jnp.asarray(L)
    tidx = (lambda i: nt - 1 - i) if reverse else (lambda i: i)
    r = GLA_GATE_RANK
    gdd = gd[:, r:2 * r] if reverse else gd[:, :r]
    b2 = b.reshape(1, GLA_DK)
    return pl.pallas_call(
        functools.partial(_gla_scan_kernel, reverse=reverse),
        grid=(GLA_HEADS, nt),
        in_specs=[
            pl.BlockSpec((n, GLA_HEAD_K), lambda h, i: (tidx(i), h)),
            pl.BlockSpec((n, GLA_HEAD_K), lambda h, i: (tidx(i), GLA_HEADS + h)),
            pl.BlockSpec((n, GLA_HEAD_V), lambda h, i: (tidx(i), h)),
            pl.BlockSpec((n, r), lambda h, i: (tidx(i), 0)),
            pl.BlockSpec((r, GLA_HEAD_K), lambda h, i: (0, h)),
            pl.BlockSpec((1, GLA_HEAD_K), lambda h, i: (0, h)),
            pl.BlockSpec(W.shape, lambda h, i: (0, 0)),
            pl.BlockSpec(L.shape, lambda h, i: (0, 0)),
        ],
        out_specs=pl.BlockSpec((n, GLA_HEAD_V), lambda h, i: (tidx(i), h)),
        out_shape=jax.ShapeDtypeStruct((T, GLA_DV), F32),
        scratch_shapes=[pltpu.VMEM((GLA_HEAD_V, GLA_HEAD_K), F32)],
        compiler_params=_cparams(("arbitrary", "arbitrary")),
        name="gla_scan_bwd" if reverse else "gla_scan_fwd",
    )(qk, qk, v, gdd, w_up, b2, W, L)


def _gla_out_kernel(of_ref, ob_ref, r_ref, x_ref, hn_ref, w_ref, o_ref):
    acc = x_ref[...]
    for h in range(GLA_HEADS):
        cols = slice(h * GLA_HEAD_V, (h + 1) * GLA_HEAD_V)
        o = _rms(of_ref[:, cols] + ob_ref[:, cols], hn_ref[...])
        r = r_ref[:, cols]
        gated = o * (r * (1.0 / (1.0 + jnp.exp(-r))))
        acc = acc + _dot(gated.astype(BF16), w_ref[cols, :])
    o_ref[...] = acc


def _gla_out(of, ob, r, x, head_norm, w_out, tm=512):
    T = x.shape[0]
    w = w_out.astype(BF16)
    hn = head_norm.reshape(1, GLA_HEAD_V)
    row = lambda n: pl.BlockSpec((tm, n), lambda i: (i, 0))
    full = lambda a: pl.BlockSpec(a.shape, lambda i: (0, 0))
    return pl.pallas_call(
        _gla_out_kernel,
        grid=(T // tm,),
        in_specs=[row(GLA_DV), row(GLA_DV), row(GLA_DV), row(D_MODEL), full(hn), full(w)],
        out_specs=row(D_MODEL),
        out_shape=jax.ShapeDtypeStruct((T, D_MODEL), F32),
        compiler_params=_cparams(("arbitrary",)),
        name="gla_out",
    )(of, ob, r, x, hn, w)


def _gla_mixer(x, gain, w_in, w_up_f, b_f, w_up_b, b_b, head_norm, w_out):
    qk, v, r, gd = _gla_in(x, gain, w_in)
    of = _gla_scan(qk, v, gd, w_up_f, b_f, reverse=False)
    ob = _gla_scan(qk, v, gd, w_up_b, b_b, reverse=True)
    return _gla_out(of, ob, r, x, head_norm, w_out)


def _router_kernel(h_ref, g_ref, w_ref, aff_ref, hn_ref):
    hn = _rms(h_ref[...], g_ref[...])
    hn_ref[...] = hn.astype(BF16)
    logits = _dot_split(hn, w_ref[...])
    e = jnp.exp(logits - jnp.max(logits, axis=-1, keepdims=True))
    aff_ref[...] = e / jnp.sum(e, axis=-1, keepdims=True)


def _router(h, gain, w_router, tm=512):
    T = h.shape[0]
    gain2 = gain.reshape(1, D_MODEL)
    row = lambda n: pl.BlockSpec((tm, n), lambda i: (i, 0))
    full = lambda a: pl.BlockSpec(a.shape, lambda i: (0, 0))
    return pl.pallas_call(
        _router_kernel,
        grid=(T // tm,),
        in_specs=[row(D_MODEL), full(gain2), full(w_router)],
        out_specs=[row(N_EXPERTS), row(D_MODEL)],
        out_shape=[jax.ShapeDtypeStruct((T, N_EXPERTS), F32), jax.ShapeDtypeStruct((T, D_MODEL), BF16)],
        compiler_params=_cparams(("arbitrary",)),
        name="moe_router",
    )(h, gain2, w_router)


def _select_kernel(aff_ref, tri_ref, posm_ref, before_ref, *, cap):
    T = aff_ref.shape[1]
    bits = pltpu.bitcast(aff_ref[...], jnp.int32)

    def search(it, thr):
        cand = thr | jnp.left_shift(jnp.int32(1), 30 - it)
        cnt = jnp.sum(jnp.where(bits >= cand, 1.0, 0.0), axis=1, keepdims=True)
        return jnp.where(cnt >= cap, cand, thr)

    thr = lax.fori_loop(0, 31, search, jnp.zeros((N_EXPERTS, 1), jnp.int32))
    n_gt = jnp.sum(jnp.where(bits > thr, 1.0, 0.0), axis=1, keepdims=True)
    need = cap - n_gt
    tri = tri_ref[...]

    def scan(j, carry):
        c_eq, c_sel = carry
        cols = pl.ds(pl.multiple_of(j * LANES, LANES), LANES)
        blk = pltpu.bitcast(aff_ref[:, cols], jnp.int32)
        eq = jnp.where(blk == thr, 1.0, 0.0)
        rank = _dot(eq.astype(BF16), tri) + c_eq - eq
        sel = jnp.where((blk > thr) | ((eq > 0.0) & (rank < need)), 1.0, 0.0)
        before = _dot(sel.astype(BF16), tri) + c_sel - sel
        before_ref[:, cols] = before.astype(jnp.int32)
        posm_ref[:, cols] = jnp.where(sel > 0.0, before, -1.0).astype(jnp.int32)
        return (c_eq + jnp.sum(eq, axis=1, keepdims=True), c_sel + jnp.sum(sel, axis=1, keepdims=True))

    zero = jnp.zeros((N_EXPERTS, 1), F32)
    lax.fori_loop(0, T // LANES, scan, (zero, zero))


def _select(aff_t, cap):
    T = aff_t.shape[1]
    tri = jnp.asarray(np.triu(np.ones((LANES, LANES), np.float32)), BF16)
    full = lambda a: pl.BlockSpec(a.shape, lambda: (0,) * a.ndim)
    out = jax.ShapeDtypeStruct((N_EXPERTS, T), jnp.int32)
    return pl.pallas_call(
        functools.partial(_select_kernel, cap=cap),
        in_specs=[full(aff_t), full(tri)],
        out_specs=[pl.BlockSpec((N_EXPERTS, T), lambda: (0, 0))] * 2,
        out_shape=[out, out],
        compiler_params=pltpu.CompilerParams(vmem_limit_bytes=VMEM_LIMIT),
        name="moe_select",
    )(aff_t, tri)


def _dispatch_window():
    return MOE_TILE + F32_SUBLANES


def _dispatch_kernel(start_ref, hn_ref, posm_ref, x_ref, acc_ref, *, cap, nt, sub):
    e, t = pl.program_id(0), pl.program_id(1)
    win = _dispatch_window()

    @pl.when(t == 0)
    def _():
        acc_ref[...] = jnp.zeros_like(acc_ref)

    for s in range(sub):
        tok = slice(s * MOE_TILE, (s + 1) * MOE_TILE)
        base = pl.multiple_of((start_ref[e * nt + t * sub + s] // F32_SUBLANES) * F32_SUBLANES, F32_SUBLANES)
        slot = base + lax.broadcasted_iota(jnp.int32, (win, MOE_TILE), 0)
        onehot = jnp.where(posm_ref[0, :, tok] == slot, 1.0, 0.0).astype(BF16)
        acc_ref[pl.ds(base, win), :] += _dot(onehot, hn_ref[tok, :])

    @pl.when(t == pl.num_programs(1) - 1)
    def _():
        x_ref[0] = acc_ref[:cap, :].astype(BF16)


def _dispatch(hn, posm, start, cap, sub=4):
    T = hn.shape[0]
    nt = T // MOE_TILE
    sub = math.gcd(sub, nt)
    blk = sub * MOE_TILE
    posm3 = posm.reshape(N_EXPERTS, 1, T)
    grid_spec = pltpu.PrefetchScalarGridSpec(
        num_scalar_prefetch=1,
        grid=(N_EXPERTS, nt // sub),
        in_specs=[
            pl.BlockSpec((blk, D_MODEL), lambda e, t, s: (t, 0)),
            pl.BlockSpec((1, 1, blk), lambda e, t, s: (e, 0, t)),
        ],
        out_specs=pl.BlockSpec((1, cap, D_MODEL), lambda e, t, s: (e, 0, 0)),
        scratch_shapes=[pltpu.VMEM((cap + _dispatch_window(), D_MODEL), F32)],
    )
    return pl.pallas_call(
        functools.partial(_dispatch_kernel, cap=cap, nt=nt, sub=sub),
        grid_spec=grid_spec,
        out_shape=jax.ShapeDtypeStruct((N_EXPERTS, cap, D_MODEL), BF16),
        compiler_params=_cparams(("arbitrary", "arbitrary")),
        name="moe_dispatch",
    )(start.reshape(-1), hn, posm3)


def _ffn_kernel(x_ref, wg_ref, wu_ref, wd_ref, y_ref, acc_ref):
    f = pl.program_id(1)

    @pl.when(f == 0)
    def _():
        acc_ref[...] = jnp.zeros_like(acc_ref)

    x = x_ref[0]
    a = _dot(x, wg_ref[0, 0].astype(BF16))
    u = _dot(x, wu_ref[0, 0].astype(BF16))
    mid = (a * (1.0 / (1.0 + jnp.exp(-a))) * u).astype(BF16)
    acc_ref[...] += _dot(mid, wd_ref[0, 0].astype(BF16))

    @pl.when(f == pl.num_programs(1) - 1)
    def _():
        y_ref[0] = acc_ref[...].astype(BF16)


def _ffn(x, w_gate, w_up, w_down, layer, tf=512):
    cap = x.shape[1]
    return pl.pallas_call(
        _ffn_kernel,
        grid=(N_EXPERTS, EXPERT_FF // tf),
        in_specs=[
            pl.BlockSpec((1, cap, D_MODEL), lambda e, f: (e, 0, 0)),
            pl.BlockSpec((1, 1, D_MODEL, tf), lambda e, f: (layer, e, 0, f)),
            pl.BlockSpec((1, 1, D_MODEL, tf), lambda e, f: (layer, e, 0, f)),
            pl.BlockSpec((1, 1, tf, D_MODEL), lambda e, f: (layer, e, f, 0)),
        ],
        out_specs=pl.BlockSpec((1, cap, D_MODEL), lambda e, f: (e, 0, 0)),
        out_shape=jax.ShapeDtypeStruct((N_EXPERTS, cap, D_MODEL), BF16),
        scratch_shapes=[pltpu.VMEM((cap, D_MODEL), F32)],
        compiler_params=_cparams(("arbitrary", "arbitrary")),
        name="moe_ffn",
    )(x, w_gate, w_up, w_down)


def _combine_window():
    return MOE_TILE + BF16_SUBLANES


def _combine_kernel(start_ref, h_ref, aff_ref, posm_ref, g_ref, y_hbm, o_ref, buf_ref, sem_ref, acc_ref, *,
                    cap, nt, final_norm):
    t = pl.program_id(0)
    win = _combine_window()

    def base_of(tile, e):
        b = (start_ref[e * (nt + 1) + tile] // BF16_SUBLANES) * BF16_SUBLANES
        return pl.multiple_of(jnp.minimum(b, cap - win), BF16_SUBLANES)

    def window_copy(tile, e, slot):
        return pltpu.make_async_copy(y_hbm.at[e, pl.ds(base_of(tile, e), win), :], buf_ref.at[slot, e],
                                     sem_ref.at[slot, e])

    def fetch(tile, slot):
        for e in range(N_EXPERTS):
            window_copy(tile, e, slot).start()

    slot = t % 2

    @pl.when(t == 0)
    def _():
        fetch(0, 0)

    @pl.when(t + 1 < nt)
    def _():
        fetch(t + 1, 1 - slot)

    def needs_tail(e):
        return start_ref[e * (nt + 1) + t + 1] > base_of(t, e) + MOE_TILE

    for e in range(N_EXPERTS):
        window_copy(t, e, slot).wait()
    acc = h_ref[...]
    for e in range(N_EXPERTS):
        hit = posm_ref[:, e:e + 1] == base_of(t, e) + lax.broadcasted_iota(jnp.int32, (MOE_TILE, MOE_TILE), 1)
        y = _dot(jnp.where(hit, 1.0, 0.0).astype(BF16), buf_ref[slot, e, :MOE_TILE, :])
        acc = acc + aff_ref[:, e:e + 1] * y
    acc_ref[...] = acc

    any_tail = needs_tail(0)
    for e in range(1, N_EXPERTS):
        any_tail = any_tail | needs_tail(e)

    @pl.when(any_tail)
    def _():
        for e in range(N_EXPERTS):
            @pl.when(needs_tail(e))
            def _():
                tail = base_of(t, e) + MOE_TILE + lax.broadcasted_iota(jnp.int32, (MOE_TILE, BF16_SUBLANES), 1)
                hit = posm_ref[:, e:e + 1] == tail
                yt = _dot(jnp.where(hit, 1.0, 0.0).astype(BF16), buf_ref[slot, e, MOE_TILE:, :])
                acc_ref[...] += aff_ref[:, e:e + 1] * yt

    acc = acc_ref[...]
    if final_norm:
        acc = _rms(acc, g_ref[...])
    o_ref[...] = acc


def _combine(h, aff, posm_t, start, y, cap, final_gain):
    T = h.shape[0]
    nt = T // MOE_TILE
    win = _combine_window()
    assert cap >= win and (cap - win) % BF16_SUBLANES == 0
    final_norm = final_gain is not None
    gain = (final_gain if final_norm else jnp.ones((D_MODEL,), F32)).reshape(1, D_MODEL)
    grid_spec = pltpu.PrefetchScalarGridSpec(
        num_scalar_prefetch=1,
        grid=(nt,),
        in_specs=[
            pl.BlockSpec((MOE_TILE, D_MODEL), lambda t, s: (t, 0)),
            pl.BlockSpec((MOE_TILE, N_EXPERTS), lambda t, s: (t, 0)),
            pl.BlockSpec((MOE_TILE, N_EXPERTS), lambda t, s: (t, 0)),
            pl.BlockSpec((1, D_MODEL), lambda t, s: (0, 0)),
            pl.BlockSpec(memory_space=pl.ANY),
        ],
        out_specs=pl.BlockSpec((MOE_TILE, D_MODEL), lambda t, s: (t, 0)),
        scratch_shapes=[pltpu.VMEM((2, N_EXPERTS, win, D_MODEL), BF16), pltpu.SemaphoreType.DMA((2, N_EXPERTS)),
                        pltpu.VMEM((MOE_TILE, D_MODEL), F32)],
    )
    return pl.pallas_call(
        functools.partial(_combine_kernel, cap=cap, nt=nt, final_norm=final_norm),
        grid_spec=grid_spec,
        out_shape=jax.ShapeDtypeStruct((T, D_MODEL), F32),
        compiler_params=_cparams(("arbitrary",)),
        name="moe_combine",
    )(start.reshape(-1), h, aff, posm_t, gain, y)


def _ec_moe(h, gain, w_router, w_gate, w_up, w_down, layer, final_gain=None):
    T = h.shape[0]
    cap = max(1, EC_CAPACITY_FACTOR * T // N_EXPERTS)
    aff, hn = _router(h, gain, w_router)
    posm, before = _select(aff.T, cap)
    start = before[:, ::MOE_TILE]
    x = _dispatch(hn, posm, start, cap)
    y = _ffn(x, w_gate, w_up, w_down, layer)
    start_end = jnp.concatenate([start, jnp.full((N_EXPERTS, 1), cap, jnp.int32)], axis=1)
    return _combine(h, aff, posm.T, start_end, y, cap, final_gain)


def _rms_cols(x, g):
    return x * lax.rsqrt(jnp.mean(x * x, axis=0, keepdims=True) + RMS_EPS) * g


def _mla_in_kernel(h_ref, posc_ref, posr_ref, g_ref, win_ref, wint_ref, qn_ref, wuqt_ref, kvn_ref, kvnc_ref,
                   wkn_ref, wvt_ref, freqr_ref, freqc_ref, qt_out, k_out, vt_out):
    half = MLA_ROPE // 2
    hn = _rms(h_ref[...], g_ref[...]).astype(BF16)
    c = _dot(hn, win_ref[...])
    ct = _dot_nt(wint_ref[...], hn)

    cqt = _rms_cols(ct[:MLA_Q_RANK], qn_ref[...]).astype(BF16)
    qt = _dot(wuqt_ref[...], cqt) * (MLA_QK ** -0.5 * math.log2(math.e))
    ang_t = freqc_ref[...] * posr_ref[...].astype(F32)
    cos_t, sin_t = jnp.cos(ang_t), jnp.sin(ang_t)
    for hd in range(MLA_HEADS):
        r0 = hd * MLA_QK
        x1 = qt[r0 + MLA_NOPE:r0 + MLA_NOPE + half]
        x2 = qt[r0 + MLA_NOPE + half:r0 + MLA_QK]
        qt_out[hd, :MLA_NOPE, :] = qt[r0:r0 + MLA_NOPE].astype(BF16)
        qt_out[hd, MLA_NOPE:MLA_NOPE + half, :] = (x1 * cos_t - x2 * sin_t).astype(BF16)
        qt_out[hd, MLA_NOPE + half:, :] = (x1 * sin_t + x2 * cos_t).astype(BF16)

    ckvt = _rms_cols(ct[MLA_Q_RANK:MLA_Q_RANK + MLA_KV_RANK], kvnc_ref[...]).astype(BF16)
    vt = _dot(wvt_ref[...], ckvt)
    for hd in range(MLA_HEADS):
        vt_out[hd] = vt[hd * MLA_V:(hd + 1) * MLA_V].astype(BF16)

    ckv = _rms(c[:, MLA_Q_RANK:MLA_Q_RANK + MLA_KV_RANK], kvn_ref[...]).astype(BF16)
    kn = _dot(ckv, wkn_ref[...])
    kr = c[:, MLA_Q_RANK + MLA_KV_RANK:]
    ang = posc_ref[...].astype(F32) * freqr_ref[...]
    cos, sin = jnp.cos(ang), jnp.sin(ang)
    k1, k2 = kr[:, :half], kr[:, half:]
    kr_rot = jnp.concatenate([k1 * cos - k2 * sin, k1 * sin + k2 * cos], axis=1).astype(BF16)
    for hd in range(MLA_HEADS):
        k_out[hd, :, :MLA_NOPE] = kn[:, hd * MLA_NOPE:(hd + 1) * MLA_NOPE].astype(BF16)
        k_out[hd, :, MLA_NOPE:] = kr_rot


def _mla_in(h, positions, gain, w_in, q_norm, w_uq, kv_norm, w_ukv, tm=256):
    T = h.shape[0]
    H, half = MLA_HEADS, MLA_ROPE // 2
    wkv = w_ukv.reshape(MLA_KV_RANK, H, MLA_NOPE + MLA_V)
    wkn = wkv[:, :, :MLA_NOPE].reshape(MLA_KV_RANK, H * MLA_NOPE).astype(BF16)
    wvt = wkv[:, :, MLA_NOPE:].reshape(MLA_KV_RANK, H * MLA_V).T.astype(BF16)
    inv_freq = ROPE_BASE ** (-jnp.arange(half, dtype=F32) / half)
    w_in_b = w_in.astype(BF16)
    args = [h, positions.reshape(T, 1), positions.reshape(1, T), gain.reshape(1, -1), w_in_b, w_in_b.T,
            q_norm.reshape(-1, 1), w_uq.T.astype(BF16), kv_norm.reshape(1, -1), kv_norm.reshape(-1, 1), wkn, wvt,
            inv_freq.reshape(1, half), inv_freq.reshape(half, 1)]
    full = lambda a: pl.BlockSpec(a.shape, lambda i: (0, 0))
    in_specs = [pl.BlockSpec((tm, D_MODEL), lambda i: (i, 0)), pl.BlockSpec((tm, 1), lambda i: (i, 0)),
                pl.BlockSpec((1, tm), lambda i: (0, i))] + [full(a) for a in args[3:]]
    return pl.pallas_call(
        _mla_in_kernel,
        grid=(T // tm,),
        in_specs=in_specs,
        out_specs=[pl.BlockSpec((H, MLA_QK, tm), lambda i: (0, 0, i)),
                   pl.BlockSpec((H, tm, MLA_QK), lambda i: (0, i, 0)),
                   pl.BlockSpec((H, MLA_V, tm), lambda i: (0, 0, i))],
        out_shape=[jax.ShapeDtypeStruct((H, MLA_QK, T), BF16), jax.ShapeDtypeStruct((H, T, MLA_QK), BF16),
                   jax.ShapeDtypeStruct((H, MLA_V, T), BF16)],
        compiler_params=_cparams(("arbitrary",)),
        name="mla_in",
    )(*args)


def _flash_kernel(qt_ref, k_ref, vt_ref, o_ref, sa_ref, sb_ref, pa_ref, pb_ref, acc_ref, *, tk):
    T = k_ref.shape[1]
    tq = qt_ref.shape[2]
    n = T // tk
    qt = qt_ref[0]

    def k_tile(j):
        return k_ref[0, pl.ds(pl.multiple_of(j * tk, tk), tk), :]

    def v_tile(j):
        return vt_ref[0, :, pl.ds(pl.multiple_of(j * tk, tk), tk)]

    def step(j, s_cur, s_nxt, p_cur, p_prv, carry):
        m, l, alpha_prv = carry
        acc_ref[...] = alpha_prv * acc_ref[...] + _dot(v_tile(jnp.maximum(j - 1, 0)), p_prv[...])
        s_nxt[...] = _dot(k_tile(jnp.minimum(j + 1, n - 1)), qt)
        s = s_cur[...]
        m_new = jnp.maximum(m, jnp.max(s, axis=0, keepdims=True))
        alpha = jnp.exp2(m - m_new)
        p = jnp.exp2(s - m_new)
        p_cur[...] = p.astype(BF16)
        return m_new, alpha * l + jnp.sum(p, axis=0, keepdims=True), alpha

    def body(i, carry):
        carry = step(2 * i, sa_ref, sb_ref, pa_ref, pb_ref, carry)
        return step(2 * i + 1, sb_ref, sa_ref, pb_ref, pa_ref, carry)

    acc_ref[...] = jnp.zeros_like(acc_ref)
    pb_ref[...] = jnp.zeros_like(pb_ref)
    sa_ref[...] = _dot(k_tile(0), qt)
    init = (jnp.full((1, tq), -jnp.inf, F32), jnp.zeros((1, tq), F32), jnp.ones((1, tq), F32))
    m, l, alpha = lax.fori_loop(0, n // 2, body, init)
    acc = alpha * acc_ref[...] + _dot(v_tile(n - 1), pb_ref[...])
    o_ref[...] = (acc / l).T.astype(BF16)


def _flash(qt, k, vt, tq=512, tk=512):
    H, T, _ = k.shape
    tq, tk = min(tq, T), min(tk, T)
    assert (T // tk) % 2 == 0
    return pl.pallas_call(
        functools.partial(_flash_kernel, tk=tk),
        grid=(H, T // tq),
        scratch_shapes=[pltpu.VMEM((tk, tq), F32), pltpu.VMEM((tk, tq), F32), pltpu.VMEM((tk, tq), BF16),
                        pltpu.VMEM((tk, tq), BF16), pltpu.VMEM((MLA_V, tq), F32)],
        in_specs=[
            pl.BlockSpec((1, MLA_QK, tq), lambda h, i: (h, 0, i)),
            pl.BlockSpec((1, T, MLA_QK), lambda h, i: (h, 0, 0)),
            pl.BlockSpec((1, MLA_V, T), lambda h, i: (h, 0, 0)),
        ],
        out_specs=pl.BlockSpec((tq, MLA_V), lambda h, i: (i, h)),
        out_shape=jax.ShapeDtypeStruct((T, H * MLA_V), BF16),
        compiler_params=_cparams(("arbitrary", "arbitrary")),
        name="mla_flash",
    )(qt, k, vt)


def _proj_residual_kernel(o_ref, w_ref, h_ref, out_ref):
    out_ref[...] = h_ref[...] + _dot(o_ref[...], w_ref[...])


def _proj_residual(o, w, h, tm=512):
    T, K = o.shape
    wb = w.astype(BF16)
    return pl.pallas_call(
        _proj_residual_kernel,
        grid=(T // tm,),
        in_specs=[pl.BlockSpec((tm, K), lambda i: (i, 0)), pl.BlockSpec(wb.shape, lambda i: (0, 0)),
                  pl.BlockSpec((tm, D_MODEL), lambda i: (i, 0))],
        out_specs=pl.BlockSpec((tm, D_MODEL), lambda i: (i, 0)),
        out_shape=jax.ShapeDtypeStruct((T, D_MODEL), F32),
        compiler_params=_cparams(("arbitrary",)),
        name="mla_out",
    )(o, wb, h)


def _mla_mixer(h, positions, gain, w_in, q_norm, w_uq, kv_norm, w_ukv, w_out):
    qt, k, vt = _mla_in(h, positions, gain, w_in, q_norm, w_uq, kv_norm, w_ukv)
    o = _flash(qt, k, vt)
    return _proj_residual(o, w_out, h)


def kernel(x, positions, mix_norm, ffn_norm, final_norm, gla_w_in, gla_w_gate_up_f, gla_b_gate_f, gla_w_gate_up_b,
           gla_b_gate_b, gla_head_norm, gla_w_out, mla_w_in, mla_q_norm, mla_w_uq, mla_kv_norm, mla_w_ukv,
           mla_w_out, moe_w_router, moe_w_gate, moe_w_up, moe_w_down):
    B, T, D = x.shape
    outs = []
    for b in range(B):
        h = x[b]
        h = _gla_mixer(h, mix_norm[0], gla_w_in[0], gla_w_gate_up_f[0], gla_b_gate_f[0], gla_w_gate_up_b[0],
                       gla_b_gate_b[0], gla_head_norm[0], gla_w_out[0])
        h = _ec_moe(h, ffn_norm[0], moe_w_router[0], moe_w_gate, moe_w_up, moe_w_down, 0)
        h = _mla_mixer(h, positions[b], mix_norm[1], mla_w_in[0], mla_q_norm[0], mla_w_uq[0], mla_kv_norm[0],
                       mla_w_ukv[0], mla_w_out[0])
        h = _ec_moe(h, ffn_norm[1], moe_w_router[1], moe_w_gate, moe_w_up, moe_w_down, 1, final_gain=final_norm)
        outs.append(h)
    return jnp.stack(outs)
```

```python
import functools
import math

import numpy as np
import jax
import jax.numpy as jnp
from jax import lax
from jax.experimental import pallas as pl
from jax.experimental.pallas import tpu as pltpu

F32 = jnp.float32
BF16 = jnp.bfloat16

D_MODEL = 1024
RMS_EPS = 1e-6

GLA_HEADS = 4
GLA_DK = 512
GLA_DV = 1024
GLA_HEAD_K = GLA_DK // GLA_HEADS
GLA_HEAD_V = GLA_DV // GLA_HEADS
GLA_GATE_RANK = 16
GLA_TAU = 16.0
GLA_CHUNK = 64
GLA_TILE = 256
GLA_LEVELS = 6

MLA_HEADS = 16
MLA_Q_RANK = 256
MLA_KV_RANK = 128
MLA_NOPE = 128
MLA_ROPE = 64
MLA_V = 128
MLA_QK = MLA_NOPE + MLA_ROPE
ROPE_BASE = 10000.0

N_EXPERTS = 16
EXPERT_FF = 2048
EC_CAPACITY_FACTOR = 2
MOE_TILE = 256
BF16_SUBLANES = 16
F32_SUBLANES = 8
LANES = 128

VMEM_LIMIT = 56 * 1024 * 1024


def _cparams(sem):
    return pltpu.CompilerParams(dimension_semantics=sem, vmem_limit_bytes=VMEM_LIMIT)


def _rms(x, g):
    return x * lax.rsqrt(jnp.mean(x * x, axis=-1, keepdims=True) + RMS_EPS) * g


def _split_bf16(x):
    hi = x.astype(BF16)
    lo = (x - hi.astype(F32)).astype(BF16)
    return hi, lo


def _dot(a, b):
    return jnp.dot(a, b, preferred_element_type=F32)


def _dot_nt(a, b):
    return lax.dot_general(a, b, (((1,), (1,)), ((), ())), preferred_element_type=F32)


def _dot_tn(a, b):
    return lax.dot_general(a, b, (((0,), (0,)), ((), ())), preferred_element_type=F32)


def _dot_split(a, b):
    ah, al = _split_bf16(a)
    bh, bl = _split_bf16(b)
    return _dot(ah, bh) + _dot(ah, bl) + _dot(al, bh)


def _gla_in_kernel(x_ref, g_ref, wqk_ref, wv_ref, wr_ref, wgd_ref, qk_ref, v_ref, r_ref, gd_ref):
    hn = _rms(x_ref[...], g_ref[...]).astype(BF16)
    qk_ref[...] = _dot(hn, wqk_ref[...])
    v_ref[...] = _dot(hn, wv_ref[...])
    r_ref[...] = _dot(hn, wr_ref[...])
    gd_ref[...] = _dot(hn, wgd_ref[...])


def _gla_in(x, gain, w_in, tm=512):
    T = x.shape[0]
    wqk = w_in[:, :2 * GLA_DK].astype(BF16)
    wv = w_in[:, 2 * GLA_DK:2 * GLA_DK + GLA_DV].astype(BF16)
    wr = w_in[:, 2 * GLA_DK + GLA_DV:2 * GLA_DK + 2 * GLA_DV].astype(BF16)
    wgd = w_in[:, 2 * GLA_DK + 2 * GLA_DV:].astype(BF16)
    ngd = 2 * GLA_GATE_RANK
    row = lambda n: pl.BlockSpec((tm, n), lambda i: (i, 0))
    full = lambda a: pl.BlockSpec(a.shape, lambda i: (0, 0))
    gain2 = gain.reshape(1, D_MODEL)
    return pl.pallas_call(
        _gla_in_kernel,
        grid=(T // tm,),
        in_specs=[row(D_MODEL), full(gain2), full(wqk), full(wv), full(wr), full(wgd)],
        out_specs=[row(2 * GLA_DK), row(GLA_DV), row(GLA_DV), row(ngd)],
        out_shape=[jax.ShapeDtypeStruct((T, 2 * GLA_DK), F32), jax.ShapeDtypeStruct((T, GLA_DV), F32),
                   jax.ShapeDtypeStruct((T, GLA_DV), F32), jax.ShapeDtypeStruct((T, ngd), F32)],
        compiler_params=_cparams(("arbitrary",)),
        name="gla_in",
    )(x, gain2, wqk, wv, wr, wgd)


def _gla_tables(reverse):
    n, c = GLA_TILE, GLA_CHUNK
    W = np.zeros((GLA_LEVELS + 3, n, n), np.float32)
    L = np.full((n, n), -1, np.int32)
    for t in range(n):
        c0 = (t // c) * c
        tt = t - c0
        for l in range(GLA_LEVELS):
            b = (c // 2) >> l
            p0 = c0 + (tt // (2 * b)) * 2 * b
            mid = p0 + b
            second = t >= mid
            if not reverse:
                if second:
                    W[l, t, mid:t + 1] = 1
                    L[t, p0:mid] = l
                else:
                    W[l, t, t + 1:mid] = 1
            else:
                if second:
                    W[l, t, mid:t] = 1
                else:
                    W[l, t, t:mid] = 1
                    L[t, mid:p0 + 2 * b] = l
        if not reverse:
            W[GLA_LEVELS, t, c0:t + 1] = 1
            W[GLA_LEVELS + 1, t, t + 1:c0 + c] = 1
            L[t, t] = GLA_LEVELS
        else:
            W[GLA_LEVELS, t, t:c0 + c] = 1
            W[GLA_LEVELS + 1, t, c0:t] = 1
        W[GLA_LEVELS + 2, t, c0:c0 + c] = 1
    return W.reshape(-1, n), L


def _gla_scan_kernel(q_ref, k_ref, v_ref, gd_ref, wup_ref, b_ref, w_ref, l_ref, o_ref, s_ref, *, reverse):
    n, c = GLA_TILE, GLA_CHUNK

    @pl.when(pl.program_id(1) == 0)
    def _():
        s_ref[...] = jnp.zeros_like(s_ref)

    q = q_ref[...] * (GLA_HEAD_K ** -0.5)
    k = k_ref[...]
    vb = v_ref[...].astype(BF16)
    z = _dot_split(gd_ref[...], wup_ref[...]) + b_ref[...]
    g = (jnp.minimum(z, 0.0) - jnp.log1p(jnp.exp(-jnp.abs(z)))) * (1.0 / GLA_TAU)
    ghi, glo = _split_bf16(g)
    w = w_ref[...]
    f = jnp.exp(_dot(w, ghi) + _dot(w, glo))
    lvl = l_ref[...]

    attn = jnp.zeros((n, n), F32)
    for l in range(GLA_LEVELS):
        fl = f[l * n:(l + 1) * n]
        p = _dot_nt((q * fl).astype(BF16), (k * fl).astype(BF16))
        attn = jnp.where(lvl == l, p, attn)
    if not reverse:
        p = _dot_nt(q.astype(BF16), k.astype(BF16))
        attn = jnp.where(lvl == GLA_LEVELS, p, attn)
    o_intra = _dot(attn.astype(BF16), vb)

    fq = f[GLA_LEVELS * n:(GLA_LEVELS + 1) * n]
    fk = f[(GLA_LEVELS + 1) * n:(GLA_LEVELS + 2) * n]
    ftot = f[(GLA_LEVELS + 2) * n:(GLA_LEVELS + 3) * n]
    qh = (q * fq).astype(BF16)
    kh = (k * fk).astype(BF16)
    st = s_ref[...]
    chunks = range(n // c)
    for j in (reversed(chunks) if reverse else chunks):
        rows = slice(j * c, (j + 1) * c)
        o_ref[rows, :] = o_intra[rows] + _dot_nt(qh[rows], st.astype(BF16))
        st = st * ftot[j * c:j * c + 1, :] + _dot_tn(vb[rows], kh[rows])
    s_ref[...] = st


def _gla_scan(qk, v, gd, w_up, b, reverse):
    T = qk.shape[0]
    n = GLA_TILE
    nt = T // n
    W, L = _gla_tables(reverse)
    W = jnp.asarray(W, BF16)
    L = jnp.asarray(L)
    tidx = (lambda i: nt - 1 - i) if reverse else (lambda i: i)
    r = GLA_GATE_RANK
    gdd = gd[:, r:2 * r] if reverse else gd[:, :r]
    b2 = b.reshape(1, GLA_DK)
    return pl.pallas_call(
        functools.partial(_gla_scan_kernel, reverse=reverse),
        grid=(GLA_HEADS, nt),
        in_specs=[
            pl.BlockSpec((n, GLA_HEAD_K), lambda h, i: (tidx(i), h)),
            pl.BlockSpec((n, GLA_HEAD_K), lambda h, i: (tidx(i), GLA_HEADS + h)),
            pl.BlockSpec((n, GLA_HEAD_V), lambda h, i: (tidx(i), h)),
            pl.BlockSpec((n, r), lambda h, i: (tidx(i), 0)),
            pl.BlockSpec((r, GLA_HEAD_K), lambda h, i: (0, h)),
            pl.BlockSpec((1, GLA_HEAD_K), lambda h, i: (0, h)),
            pl.BlockSpec(W.shape, lambda h, i: (0, 0)),
            pl.BlockSpec(L.shape, lambda h, i: (0, 0)),
        ],
        out_specs=pl.BlockSpec((n, GLA_HEAD_V), lambda h, i: (tidx(i), h)),
        out_shape=jax.ShapeDtypeStruct((T, GLA_DV), F32),
        scratch_shapes=[pltpu.VMEM((GLA_HEAD_V, GLA_HEAD_K), F32)],
        compiler_params=_cparams(("arbitrary", "arbitrary")),
        name="gla_scan_bwd" if reverse else "gla_scan_fwd",
    )(qk, qk, v, gdd, w_up, b2, W, L)


def _gla_out_kernel(of_ref, ob_ref, r_ref, x_ref, hn_ref, w_ref, o_ref):
    acc = x_ref[...]
    for h in range(GLA_HEADS):
        cols = slice(h * GLA_HEAD_V, (h + 1) * GLA_HEAD_V)
        o = _rms(of_ref[:, cols] + ob_ref[:, cols], hn_ref[...])
        r = r_ref[:, cols]
        gated = o * (r * (1.0 / (1.0 + jnp.exp(-r))))
        acc = acc + _dot(gated.astype(BF16), w_ref[cols, :])
    o_ref[...] = acc


def _gla_out(of, ob, r, x, head_norm, w_out, tm=512):
    T = x.shape[0]
    w = w_out.astype(BF16)
    hn = head_norm.reshape(1, GLA_HEAD_V)
    row = lambda n: pl.BlockSpec((tm, n), lambda i: (i, 0))
    full = lambda a: pl.BlockSpec(a.shape, lambda i: (0, 0))
    return pl.pallas_call(
        _gla_out_kernel,
        grid=(T // tm,),
        in_specs=[row(GLA_DV), row(GLA_DV), row(GLA_DV), row(D_MODEL), full(hn), full(w)],
        out_specs=row(D_MODEL),
        out_shape=jax.ShapeDtypeStruct((T, D_MODEL), F32),
        compiler_params=_cparams(("arbitrary",)),
        name="gla_out",
    )(of, ob, r, x, hn, w)


def _gla_mixer(x, gain, w_in, w_up_f, b_f, w_up_b, b_b, head_norm, w_out):
    qk, v, r, gd = _gla_in(x, gain, w_in)
    of = _gla_scan(qk, v, gd, w_up_f, b_f, reverse=False)
    ob = _gla_scan(qk, v, gd, w_up_b, b_b, reverse=True)
    return _gla_out(of, ob, r, x, head_norm, w_out)


def _router_kernel(h_ref, g_ref, w_ref, aff_ref, hn_ref):
    hn = _rms(h_ref[...], g_ref[...])
    hn_ref[...] = hn.astype(BF16)
    logits = _dot_split(hn, w_ref[...])
    e = jnp.exp(logits - jnp.max(logits, axis=-1, keepdims=True))
    aff_ref[...] = e / jnp.sum(e, axis=-1, keepdims=True)


def _router(h, gain, w_router, tm=512):
    T = h.shape[0]
    gain2 = gain.reshape(1, D_MODEL)
    row = lambda n: pl.BlockSpec((tm, n), lambda i: (i, 0))
    full = lambda a: pl.BlockSpec(a.shape, lambda i: (0, 0))
    return pl.pallas_call(
        _router_kernel,
        grid=(T // tm,),
        in_specs=[row(D_MODEL), full(gain2), full(w_router)],
        out_specs=[row(N_EXPERTS), row(D_MODEL)],
        out_shape=[jax.ShapeDtypeStruct((T, N_EXPERTS), F32), jax.ShapeDtypeStruct((T, D_MODEL), BF16)],
        compiler_params=_cparams(("arbitrary",)),
        name="moe_router",
    )(h, gain2, w_router)


def _select_kernel(aff_ref, tri_ref, posm_ref, before_ref, *, cap):
    T = aff_ref.shape[1]
    bits = pltpu.bitcast(aff_ref[...], jnp.int32)

    def search(it, thr):
        cand = thr | jnp.left_shift(jnp.int32(1), 30 - it)
        cnt = jnp.sum(jnp.where(bits >= cand, 1.0, 0.0), axis=1, keepdims=True)
        return jnp.where(cnt >= cap, cand, thr)

    thr = lax.fori_loop(0, 31, search, jnp.zeros((N_EXPERTS, 1), jnp.int32))
    n_gt = jnp.sum(jnp.where(bits > thr, 1.0, 0.0), axis=1, keepdims=True)
    need = cap - n_gt
    tri = tri_ref[...]

    def scan(j, carry):
        c_eq, c_sel = carry
        cols = pl.ds(pl.multiple_of(j * LANES, LANES), LANES)
        blk = pltpu.bitcast(aff_ref[:, cols], jnp.int32)
        eq = jnp.where(blk == thr, 1.0, 0.0)
        rank = _dot(eq.astype(BF16), tri) + c_eq - eq
        sel = jnp.where((blk > thr) | ((eq > 0.0) & (rank < need)), 1.0, 0.0)
        before = _dot(sel.astype(BF16), tri) + c_sel - sel
        before_ref[:, cols] = before.astype(jnp.int32)
        posm_ref[:, cols] = jnp.where(sel > 0.0, before, -1.0).astype(jnp.int32)
        return (c_eq + jnp.sum(eq, axis=1, keepdims=True), c_sel + jnp.sum(sel, axis=1, keepdims=True))

    zero = jnp.zeros((N_EXPERTS, 1), F32)
    lax.fori_loop(0, T // LANES, scan, (zero, zero))


def _select(aff_t, cap):
    T = aff_t.shape[1]
    tri = jnp.asarray(np.triu(np.ones((LANES, LANES), np.float32)), BF16)
    full = lambda a: pl.BlockSpec(a.shape, lambda: (0,) * a.ndim)
    out = jax.ShapeDtypeStruct((N_EXPERTS, T), jnp.int32)
    return pl.pallas_call(
        functools.partial(_select_kernel, cap=cap),
        in_specs=[full(aff_t), full(tri)],
        out_specs=[pl.BlockSpec((N_EXPERTS, T), lambda: (0, 0))] * 2,
        out_shape=[out, out],
        compiler_params=pltpu.CompilerParams(vmem_limit_bytes=VMEM_LIMIT),
        name="moe_select",
    )(aff_t, tri)


def _dispatch_window():
    return MOE_TILE + F32_SUBLANES


def _dispatch_kernel(start_ref, hn_ref, posm_ref, x_ref, acc_ref, *, cap, nt, sub):
    e, t = pl.program_id(0), pl.program_id(1)
    win = _dispatch_window()

    @pl.when(t == 0)
    def _():
        acc_ref[...] = jnp.zeros_like(acc_ref)

    for s in range(sub):
        tok = slice(s * MOE_TILE, (s + 1) * MOE_TILE)
        base = pl.multiple_of((start_ref[e * nt + t * sub + s] // F32_SUBLANES) * F32_SUBLANES, F32_SUBLANES)
        slot = base + lax.broadcasted_iota(jnp.int32, (win, MOE_TILE), 0)
        onehot = jnp.where(posm_ref[0, :, tok] == slot, 1.0, 0.0).astype(BF16)
        acc_ref[pl.ds(base, win), :] += _dot(onehot, hn_ref[tok, :])

    @pl.when(t == pl.num_programs(1) - 1)
    def _():
        x_ref[0] = acc_ref[:cap, :].astype(BF16)


def _dispatch(hn, posm, start, cap, sub=4):
    T = hn.shape[0]
    nt = T // MOE_TILE
    sub = math.gcd(sub, nt)
    blk = sub * MOE_TILE
    posm3 = posm.reshape(N_EXPERTS, 1, T)
    grid_spec = pltpu.PrefetchScalarGridSpec(
        num_scalar_prefetch=1,
        grid=(N_EXPERTS, nt // sub),
        in_specs=[
            pl.BlockSpec((blk, D_MODEL), lambda e, t, s: (t, 0)),
            pl.BlockSpec((1, 1, blk), lambda e, t, s: (e, 0, t)),
        ],
        out_specs=pl.BlockSpec((1, cap, D_MODEL), lambda e, t, s: (e, 0, 0)),
        scratch_shapes=[pltpu.VMEM((cap + _dispatch_window(), D_MODEL), F32)],
    )
    return pl.pallas_call(
        functools.partial(_dispatch_kernel, cap=cap, nt=nt, sub=sub),
        grid_spec=grid_spec,
        out_shape=jax.ShapeDtypeStruct((N_EXPERTS, cap, D_MODEL), BF16),
        compiler_params=_cparams(("arbitrary", "arbitrary")),
        name="moe_dispatch",
    )(start.reshape(-1), hn, posm3)


def _ffn_kernel(x_ref, wg_ref, wu_ref, wd_ref, y_ref, acc_ref):
    f = pl.program_id(1)

    @pl.when(f == 0)
    def _():
        acc_ref[...] = jnp.zeros_like(acc_ref)

    x = x_ref[0]
    a = _dot(x, wg_ref[0, 0].astype(BF16))
    u = _dot(x, wu_ref[0, 0].astype(BF16))
    mid = (a * (1.0 / (1.0 + jnp.exp(-a))) * u).astype(BF16)
    acc_ref[...] += _dot(mid, wd_ref[0, 0].astype(BF16))

    @pl.when(f == pl.num_programs(1) - 1)
    def _():
        y_ref[0] = acc_ref[...].astype(BF16)


def _ffn(x, w_gate, w_up, w_down, layer, tf=512):
    cap = x.shape[1]
    return pl.pallas_call(
        _ffn_kernel,
        grid=(N_EXPERTS, EXPERT_FF // tf),
        in_specs=[
            pl.BlockSpec((1, cap, D_MODEL), lambda e, f: (e, 0, 0)),
            pl.BlockSpec((1, 1, D_MODEL, tf), lambda e, f: (layer, e, 0, f)),
            pl.BlockSpec((1, 1, D_MODEL, tf), lambda e, f: (layer, e, 0, f)),
            pl.BlockSpec((1, 1, tf, D_MODEL), lambda e, f: (layer, e, f, 0)),
        ],
        out_specs=pl.BlockSpec((1, cap, D_MODEL), lambda e, f: (e, 0, 0)),
        out_shape=jax.ShapeDtypeStruct((N_EXPERTS, cap, D_MODEL), BF16),
        scratch_shapes=[pltpu.VMEM((cap, D_MODEL), F32)],
        compiler_params=_cparams(("arbitrary", "arbitrary")),
        name="moe_ffn",
    )(x, w_gate, w_up, w_down)


def _combine_window():
    return MOE_TILE + BF16_SUBLANES


def _combine_kernel(start_ref, h_ref, aff_ref, posm_ref, g_ref, y_hbm, o_ref, buf_ref, sem_ref, acc_ref, *,
                    cap, nt, final_norm):
    t = pl.program_id(0)
    win = _combine_window()

    def base_of(tile, e):
        b = (start_ref[e * (nt + 1) + tile] // BF16_SUBLANES) * BF16_SUBLANES
        return pl.multiple_of(jnp.minimum(b, cap - win), BF16_SUBLANES)

    def window_copy(tile, e, slot):
        return pltpu.make_async_copy(y_hbm.at[e, pl.ds(base_of(tile, e), win), :], buf_ref.at[slot, e],
                                     sem_ref.at[slot, e])

    def fetch(tile, slot):
        for e in range(N_EXPERTS):
            window_copy(tile, e, slot).start()

    slot = t % 2

    @pl.when(t == 0)
    def _():
        fetch(0, 0)

    @pl.when(t + 1 < nt)
    def _():
        fetch(t + 1, 1 - slot)

    def needs_tail(e):
        return start_ref[e * (nt + 1) + t + 1] > base_of(t, e) + MOE_TILE

    for e in range(N_EXPERTS):
        window_copy(t, e, slot).wait()
    acc = h_ref[...]
    for e in range(N_EXPERTS):
        hit = posm_ref[:, e:e + 1] == base_of(t, e) + lax.broadcasted_iota(jnp.int32, (MOE_TILE, MOE_TILE), 1)
        y = _dot(jnp.where(hit, 1.0, 0.0).astype(BF16), buf_ref[slot, e, :MOE_TILE, :])
        acc = acc + aff_ref[:, e:e + 1] * y
    acc_ref[...] = acc

    any_tail = needs_tail(0)
    for e in range(1, N_EXPERTS):
        any_tail = any_tail | needs_tail(e)

    @pl.when(any_tail)
    def _():
        for e in range(N_EXPERTS):
            @pl.when(needs_tail(e))
            def _():
                tail = base_of(t, e) + MOE_TILE + lax.broadcasted_iota(jnp.int32, (MOE_TILE, BF16_SUBLANES), 1)
                hit = posm_ref[:, e:e + 1] == tail
                yt = _dot(jnp.where(hit, 1.0, 0.0).astype(BF16), buf_ref[slot, e, MOE_TILE:, :])
                acc_ref[...] += aff_ref[:, e:e + 1] * yt

    acc = acc_ref[...]
    if final_norm:
        acc = _rms(acc, g_ref[...])
    o_ref[...] = acc


def _combine(h, aff, posm_t, start, y, cap, final_gain):
    T = h.shape[0]
    nt = T // MOE_TILE
    win = _combine_window()
    assert cap >= win and (cap - win) % BF16_SUBLANES == 0
    final_norm = final_gain is not None
    gain = (final_gain if final_norm else jnp.ones((D_MODEL,), F32)).reshape(1, D_MODEL)
    grid_spec = pltpu.PrefetchScalarGridSpec(
        num_scalar_prefetch=1,
        grid=(nt,),
        in_specs=[
            pl.BlockSpec((MOE_TILE, D_MODEL), lambda t, s: (t, 0)),
            pl.BlockSpec((MOE_TILE, N_EXPERTS), lambda t, s: (t, 0)),
            pl.BlockSpec((MOE_TILE, N_EXPERTS), lambda t, s: (t, 0)),
            pl.BlockSpec((1, D_MODEL), lambda t, s: (0, 0)),
            pl.BlockSpec(memory_space=pl.ANY),
        ],
        out_specs=pl.BlockSpec((MOE_TILE, D_MODEL), lambda t, s: (t, 0)),
        scratch_shapes=[pltpu.VMEM((2, N_EXPERTS, win, D_MODEL), BF16), pltpu.SemaphoreType.DMA((2, N_EXPERTS)),
                        pltpu.VMEM((MOE_TILE, D_MODEL), F32)],
    )
    return pl.pallas_call(
        functools.partial(_combine_kernel, cap=cap, nt=nt, final_norm=final_norm),
        grid_spec=grid_spec,
        out_shape=jax.ShapeDtypeStruct((T, D_MODEL), F32),
        compiler_params=_cparams(("arbitrary",)),
        name="moe_combine",
    )(start.reshape(-1), h, aff, posm_t, gain, y)


def _ec_moe(h, gain, w_router, w_gate, w_up, w_down, layer, final_gain=None):
    T = h.shape[0]
    cap = max(1, EC_CAPACITY_FACTOR * T // N_EXPERTS)
    aff, hn = _router(h, gain, w_router)
    posm, before = _select(aff.T, cap)
    start = before[:, ::MOE_TILE]
    x = _dispatch(hn, posm, start, cap)
    y = _ffn(x, w_gate, w_up, w_down, layer)
    start_end = jnp.concatenate([start, jnp.full((N_EXPERTS, 1), cap, jnp.int32)], axis=1)
    return _combine(h, aff, posm.T, start_end, y, cap, final_gain)


def _rms_cols(x, g):
    return x * lax.rsqrt(jnp.mean(x * x, axis=0, keepdims=True) + RMS_EPS) * g


MLA_QCOLS = MLA_NOPE + 2 * LANES


def _mla_in_kernel(h_ref, posc_ref, posr_ref, g_ref, win_ref, wintkv_ref, qn_ref, wq_ref, wukt_ref, kvn_ref,
                   kvnc_ref, freqr_ref, sign_ref, freqc_ref, q_out, kt_out, va_out):
    half = MLA_ROPE // 2
    hn = _rms(h_ref[...], g_ref[...]).astype(BF16)
    c = _dot(hn, win_ref[...])
    ckv_t = _dot_nt(wintkv_ref[...], hn)

    cq = _rms(c[:, :MLA_Q_RANK], qn_ref[...]).astype(BF16)
    qa = _dot(cq, wq_ref[...])
    ang = posc_ref[...].astype(F32) * freqr_ref[...]
    cos, sin_signed = jnp.cos(ang), jnp.sin(ang) * sign_ref[...]
    qscale = MLA_QK ** -0.5 * math.log2(math.e)
    for hd in range(MLA_HEADS):
        c0 = hd * MLA_QCOLS
        q_lat = _dot(qa[:, c0:c0 + MLA_NOPE].astype(BF16), wukt_ref[hd])
        rot = qa[:, c0 + MLA_NOPE:c0 + MLA_NOPE + LANES] * cos + qa[:, c0 + MLA_NOPE + LANES:c0 + MLA_QCOLS] * sin_signed
        q_out[hd, :, :MLA_NOPE] = (q_lat * qscale).astype(BF16)
        q_out[hd, :, MLA_NOPE:] = (rot[:, :MLA_ROPE] * qscale).astype(BF16)

    kt_out[:MLA_KV_RANK, :] = _rms_cols(ckv_t[:MLA_KV_RANK], kvnc_ref[...]).astype(BF16)
    ang_t = freqc_ref[...] * posr_ref[...].astype(F32)
    cos_t, sin_t = jnp.cos(ang_t), jnp.sin(ang_t)
    k1, k2 = ckv_t[MLA_KV_RANK:MLA_KV_RANK + half], ckv_t[MLA_KV_RANK + half:]
    kt_out[MLA_KV_RANK:MLA_KV_RANK + half, :] = (k1 * cos_t - k2 * sin_t).astype(BF16)
    kt_out[MLA_KV_RANK + half:, :] = (k1 * sin_t + k2 * cos_t).astype(BF16)

    ckv = _rms(c[:, MLA_Q_RANK:MLA_Q_RANK + MLA_KV_RANK], kvn_ref[...])
    va_out[:, :MLA_KV_RANK] = ckv.astype(BF16)
    lane = lax.broadcasted_iota(jnp.int32, (ckv.shape[0], LANES), 1)
    va_out[:, MLA_KV_RANK:] = jnp.where(lane == 0, 1.0, 0.0).astype(BF16)


def _mla_in(h, positions, gain, w_in, q_norm, w_uq, kv_norm, w_ukv, tm=256):
    T = h.shape[0]
    H, half = MLA_HEADS, MLA_ROPE // 2
    wq = w_uq.reshape(MLA_Q_RANK, H, MLA_QK)
    x1, x2 = wq[:, :, MLA_NOPE:MLA_NOPE + half], wq[:, :, MLA_NOPE + half:]
    pad = jnp.zeros((MLA_Q_RANK, H, LANES - MLA_ROPE), F32)
    wq_wide = jnp.concatenate([wq[:, :, :MLA_NOPE], x1, x2, pad, x2, x1, pad], axis=2)
    wq_wide = wq_wide.reshape(MLA_Q_RANK, H * MLA_QCOLS).astype(BF16)
    wuk_t = w_ukv.reshape(MLA_KV_RANK, H, MLA_NOPE + MLA_V)[:, :, :MLA_NOPE].transpose(1, 2, 0).astype(BF16)
    inv_freq = ROPE_BASE ** (-jnp.arange(half, dtype=F32) / half)
    zeros = jnp.zeros((LANES - MLA_ROPE,), F32)
    freq_row = jnp.concatenate([inv_freq, inv_freq, zeros]).reshape(1, LANES)
    sign_row = jnp.concatenate([-jnp.ones((half,), F32), jnp.ones((half,), F32), zeros]).reshape(1, LANES)
    w_in_b = w_in.astype(BF16)
    args = [h, positions.reshape(T, 1), positions.reshape(1, T), gain.reshape(1, -1), w_in_b,
            w_in_b[:, MLA_Q_RANK:].T, q_norm.reshape(1, -1), wq_wide, wuk_t, kv_norm.reshape(1, -1),
            kv_norm.reshape(-1, 1), freq_row, sign_row, inv_freq.reshape(half, 1)]
    full = lambda a: pl.BlockSpec(a.shape, lambda i: (0,) * a.ndim)
    in_specs = [pl.BlockSpec((tm, D_MODEL), lambda i: (i, 0)), pl.BlockSpec((tm, 1), lambda i: (i, 0)),
                pl.BlockSpec((1, tm), lambda i: (0, i))] + [full(a) for a in args[3:]]
    return pl.pallas_call(
        _mla_in_kernel,
        grid=(T // tm,),
        in_specs=in_specs,
        out_specs=[pl.BlockSpec((H, tm, MLA_QK), lambda i: (0, i, 0)),
                   pl.BlockSpec((MLA_QK, tm), lambda i: (0, i)),
                   pl.BlockSpec((tm, 2 * LANES), lambda i: (i, 0))],
        out_shape=[jax.ShapeDtypeStruct((H, T, MLA_QK), BF16), jax.ShapeDtypeStruct((MLA_QK, T), BF16),
                   jax.ShapeDtypeStruct((T, 2 * LANES), BF16)],
        compiler_params=_cparams(("arbitrary",)),
        name="mla_in",
    )(*args)


def _flash_kernel(q_ref, kt_ref, va_ref, o_ref, sa_ref, sb_ref, pa_ref, pb_ref, acc_ref, m_ref, alpha_ref, *, tk):
    H, tq, _ = q_ref.shape
    T = kt_ref.shape[1]
    n = T // tk
    q = q_ref[...].reshape(H * tq, MLA_QK)

    def k_tile(j):
        return kt_ref[:, pl.ds(pl.multiple_of(j * tk, tk), tk)]

    def v_tile(j):
        return va_ref[pl.ds(pl.multiple_of(j * tk, tk), tk), :]

    def step(j, s_cur, s_nxt, p_cur, p_prv):
        acc_ref[...] = alpha_ref[...] * acc_ref[...] + _dot(p_prv[...], v_tile(jnp.maximum(j - 1, 0)))
        s_nxt[...] = _dot(q, k_tile(jnp.minimum(j + 1, n - 1)))
        s = s_cur[...]
        m_old = m_ref[...]
        m_new = jnp.maximum(m_old, jnp.max(s, axis=1, keepdims=True))
        alpha_ref[...] = jnp.exp2(m_old - m_new)
        m_ref[...] = m_new
        p_cur[...] = jnp.exp2(s - m_new).astype(BF16)

    def body(i, carry):
        step(2 * i, sa_ref, sb_ref, pa_ref, pb_ref)
        step(2 * i + 1, sb_ref, sa_ref, pb_ref, pa_ref)
        return carry

    acc_ref[...] = jnp.zeros_like(acc_ref)
    pb_ref[...] = jnp.zeros_like(pb_ref)
    alpha_ref[...] = jnp.ones_like(alpha_ref)
    m_ref[...] = jnp.full_like(m_ref, -jnp.inf)
    sa_ref[...] = _dot(q, k_tile(0))
    lax.fori_loop(0, n // 2, body, 0)
    acc = alpha_ref[...] * acc_ref[...] + _dot(pb_ref[...], v_tile(n - 1))
    o_lat = (acc[:, :MLA_KV_RANK] / acc[:, MLA_KV_RANK:MLA_KV_RANK + 1]).astype(BF16)
    for hd in range(H):
        o_ref[:, hd * MLA_KV_RANK:(hd + 1) * MLA_KV_RANK] = o_lat[hd * tq:(hd + 1) * tq]


def _flash(q, kt, va, tq=64, tk=512):
    H, T, _ = q.shape
    tq, tk = min(tq, T), min(tk, T)
    assert (T // tk) % 2 == 0 and tq % BF16_SUBLANES == 0
    rows = H * tq
    return pl.pallas_call(
        functools.partial(_flash_kernel, tk=tk),
        grid=(T // tq,),
        scratch_shapes=[pltpu.VMEM((rows, tk), F32), pltpu.VMEM((rows, tk), F32), pltpu.VMEM((rows, tk), BF16),
                        pltpu.VMEM((rows, tk), BF16), pltpu.VMEM((rows, 2 * LANES), F32),
                        pltpu.VMEM((rows, 1), F32), pltpu.VMEM((rows, 1), F32)],
        in_specs=[
            pl.BlockSpec((H, tq, MLA_QK), lambda i: (0, i, 0)),
            pl.BlockSpec((MLA_QK, T), lambda i: (0, 0)),
            pl.BlockSpec((T, 2 * LANES), lambda i: (0, 0)),
        ],
        out_specs=pl.BlockSpec((tq, H * MLA_KV_RANK), lambda i: (i, 0)),
        out_shape=jax.ShapeDtypeStruct((T, H * MLA_KV_RANK), BF16),
        compiler_params=_cparams(("arbitrary",)),
        name="mla_flash",
    )(q, kt, va)


def _mla_out_kernel(o_ref, wuv_ref, w_ref, h_ref, out_ref):
    v = [_dot(o_ref[:, hd * MLA_KV_RANK:(hd + 1) * MLA_KV_RANK], wuv_ref[hd]).astype(BF16)
         for hd in range(MLA_HEADS)]
    out_ref[...] = h_ref[...] + _dot(jnp.concatenate(v, axis=1), w_ref[...])


def _mla_out(o_lat, w_ukv, w_out, h, tm=512):
    T = h.shape[0]
    H = MLA_HEADS
    wuv = w_ukv.reshape(MLA_KV_RANK, H, MLA_NOPE + MLA_V)[:, :, MLA_NOPE:].transpose(1, 0, 2).astype(BF16)
    wb = w_out.astype(BF16)
    return pl.pallas_call(
        _mla_out_kernel,
        grid=(T // tm,),
        in_specs=[pl.BlockSpec((tm, H * MLA_KV_RANK), lambda i: (i, 0)), pl.BlockSpec(wuv.shape, lambda i: (0, 0, 0)),
                  pl.BlockSpec(wb.shape, lambda i: (0, 0)), pl.BlockSpec((tm, D_MODEL), lambda i: (i, 0))],
        out_specs=pl.BlockSpec((tm, D_MODEL), lambda i: (i, 0)),
        out_shape=jax.ShapeDtypeStruct((T, D_MODEL), F32),
        compiler_params=_cparams(("arbitrary",)),
        name="mla_out",
    )(o_lat, wuv, wb, h)


def _mla_mixer(h, positions, gain, w_in, q_norm, w_uq, kv_norm, w_ukv, w_out):
    q, kt, va = _mla_in(h, positions, gain, w_in, q_norm, w_uq, kv_norm, w_ukv)
    o_lat = _flash(q, kt, va)
    return _mla_out(o_lat, w_ukv, w_out, h)


def kernel(x, positions, mix_norm, ffn_norm, final_norm, gla_w_in, gla_w_gate_up_f, gla_b_gate_f, gla_w_gate_up_b,
           gla_b_gate_b, gla_head_norm, gla_w_out, mla_w_in, mla_q_norm, mla_w_uq, mla_kv_norm, mla_w_ukv,
           mla_w_out, moe_w_router, moe_w_gate, moe_w_up, moe_w_down):
    B, T, D = x.shape
    outs = []
    for b in range(B):
        h = x[b]
        h = _gla_mixer(h, mix_norm[0], gla_w_in[0], gla_w_gate_up_f[0], gla_b_gate_f[0], gla_w_gate_up_b[0],
                       gla_b_gate_b[0], gla_head_norm[0], gla_w_out[0])
        h = _ec_moe(h, ffn_norm[0], moe_w_router[0], moe_w_gate, moe_w_up, moe_w_down, 0)
        h = _mla_mixer(h, positions[b], mix_norm[1], mla_w_in[0], mla_q_norm[0], mla_w_uq[0], mla_kv_norm[0],
                       mla_w_ukv[0], mla_w_out[0])
        h = _ec_moe(h, ffn_norm[1], moe_w_router[1], moe_w_gate, moe_w_up, moe_w_down, 1, final_gain=final_norm)
        outs.append(h)
    return jnp.stack(outs)
```

```python
import functools
import math

import numpy as np
import jax
import jax.numpy as jnp
from jax import lax
from jax.experimental import pallas as pl
from jax.experimental.pallas import tpu as pltpu

F32 = jnp.float32
BF16 = jnp.bfloat16

D_MODEL = 1024
RMS_EPS = 1e-6

GLA_HEADS = 4
GLA_DK = 512
GLA_DV = 1024
GLA_HEAD_K = GLA_DK // GLA_HEADS
GLA_HEAD_V = GLA_DV // GLA_HEADS
GLA_GATE_RANK = 16
GLA_TAU = 16.0
GLA_CHUNK = 64
GLA_TILE = 256
GLA_LEVELS = 6

MLA_HEADS = 16
MLA_Q_RANK = 256
MLA_KV_RANK = 128
MLA_NOPE = 128
MLA_ROPE = 64
MLA_V = 128
MLA_QK = MLA_NOPE + MLA_ROPE
ROPE_BASE = 10000.0

N_EXPERTS = 16
EXPERT_FF = 2048
EC_CAPACITY_FACTOR = 2
MOE_TILE = 256
BF16_SUBLANES = 16
F32_SUBLANES = 8
LANES = 128

VMEM_LIMIT = 56 * 1024 * 1024


def _cparams(sem):
    return pltpu.CompilerParams(dimension_semantics=sem, vmem_limit_bytes=VMEM_LIMIT)


def _rms(x, g):
    return x * lax.rsqrt(jnp.mean(x * x, axis=-1, keepdims=True) + RMS_EPS) * g


def _split_bf16(x):
    hi = x.astype(BF16)
    lo = (x - hi.astype(F32)).astype(BF16)
    return hi, lo


def _dot(a, b):
    return jnp.dot(a, b, preferred_element_type=F32)


def _dot_nt(a, b):
    return lax.dot_general(a, b, (((1,), (1,)), ((), ())), preferred_element_type=F32)


def _dot_tn(a, b):
    return lax.dot_general(a, b, (((0,), (0,)), ((), ())), preferred_element_type=F32)


def _dot_split(a, b):
    ah, al = _split_bf16(a)
    bh, bl = _split_bf16(b)
    return _dot(ah, bh) + _dot(ah, bl) + _dot(al, bh)


def _gla_in_kernel(x_ref, g_ref, wqk_ref, wv_ref, wr_ref, wgd_ref, qk_ref, v_ref, r_ref, gd_ref):
    hn = _rms(x_ref[...], g_ref[...]).astype(BF16)
    qk_ref[...] = _dot(hn, wqk_ref[...])
    v_ref[...] = _dot(hn, wv_ref[...])
    r_ref[...] = _dot(hn, wr_ref[...])
    gd_ref[...] = _dot(hn, wgd_ref[...])


def _gla_in(x, gain, w_in, tm=512):
    T = x.shape[0]
    wqk = w_in[:, :2 * GLA_DK].astype(BF16)
    wv = w_in[:, 2 * GLA_DK:2 * GLA_DK + GLA_DV].astype(BF16)
    wr = w_in[:, 2 * GLA_DK + GLA_DV:2 * GLA_DK + 2 * GLA_DV].astype(BF16)
    wgd = w_in[:, 2 * GLA_DK + 2 * GLA_DV:].astype(BF16)
    ngd = 2 * GLA_GATE_RANK
    row = lambda n: pl.BlockSpec((tm, n), lambda i: (i, 0))
    full = lambda a: pl.BlockSpec(a.shape, lambda i: (0, 0))
    gain2 = gain.reshape(1, D_MODEL)
    return pl.pallas_call(
        _gla_in_kernel,
        grid=(T // tm,),
        in_specs=[row(D_MODEL), full(gain2), full(wqk), full(wv), full(wr), full(wgd)],
        out_specs=[row(2 * GLA_DK), row(GLA_DV), row(GLA_DV), row(ngd)],
        out_shape=[jax.ShapeDtypeStruct((T, 2 * GLA_DK), F32), jax.ShapeDtypeStruct((T, GLA_DV), F32),
                   jax.ShapeDtypeStruct((T, GLA_DV), F32), jax.ShapeDtypeStruct((T, ngd), F32)],
        compiler_params=_cparams(("arbitrary",)),
        name="gla_in",
    )(x, gain2, wqk, wv, wr, wgd)


def _gla_tables(reverse):
    n, c = GLA_TILE, GLA_CHUNK
    W = np.zeros((GLA_LEVELS + 3, n, n), np.float32)
    L = np.full((n, n), -1, np.int32)
    for t in range(n):
        c0 = (t // c) * c
        tt = t - c0
        for l in range(GLA_LEVELS):
            b = (c // 2) >> l
            p0 = c0 + (tt // (2 * b)) * 2 * b
            mid = p0 + b
            second = t >= mid
            if not reverse:
                if second:
                    W[l, t, mid:t + 1] = 1
                    L[t, p0:mid] = l
                else:
                    W[l, t, t + 1:mid] = 1
            else:
                if second:
                    W[l, t, mid:t] = 1
                else:
                    W[l, t, t:mid] = 1
                    L[t, mid:p0 + 2 * b] = l
        if not reverse:
            W[GLA_LEVELS, t, c0:t + 1] = 1
            W[GLA_LEVELS + 1, t, t + 1:c0 + c] = 1
            L[t, t] = GLA_LEVELS
        else:
            W[GLA_LEVELS, t, t:c0 + c] = 1
            W[GLA_LEVELS + 1, t, c0:t] = 1
        W[GLA_LEVELS + 2, t, c0:c0 + c] = 1
    return W.reshape(-1, n), L


def _gla_scan_kernel(q_ref, k_ref, v_ref, gd_ref, wup_ref, b_ref, w_ref, l_ref, o_ref, s_ref, *, reverse):
    n, c = GLA_TILE, GLA_CHUNK

    @pl.when(pl.program_id(1) == 0)
    def _():
        s_ref[...] = jnp.zeros_like(s_ref)

    q = q_ref[...] * (GLA_HEAD_K ** -0.5)
    k = k_ref[...]
    vb = v_ref[...].astype(BF16)
    z = _dot_split(gd_ref[...], wup_ref[...]) + b_ref[...]
    g = (jnp.minimum(z, 0.0) - jnp.log1p(jnp.exp(-jnp.abs(z)))) * (1.0 / GLA_TAU)
    ghi, glo = _split_bf16(g)
    w = w_ref[...]
    f = jnp.exp(_dot(w, ghi) + _dot(w, glo))
    lvl = l_ref[...]

    attn = jnp.zeros((n, n), F32)
    for l in range(GLA_LEVELS):
        fl = f[l * n:(l + 1) * n]
        p = _dot_nt((q * fl).astype(BF16), (k * fl).astype(BF16))
        attn = jnp.where(lvl == l, p, attn)
    if not reverse:
        p = _dot_nt(q.astype(BF16), k.astype(BF16))
        attn = jnp.where(lvl == GLA_LEVELS, p, attn)
    o_intra = _dot(attn.astype(BF16), vb)

    fq = f[GLA_LEVELS * n:(GLA_LEVELS + 1) * n]
    fk = f[(GLA_LEVELS + 1) * n:(GLA_LEVELS + 2) * n]
    ftot = f[(GLA_LEVELS + 2) * n:(GLA_LEVELS + 3) * n]
    qh = (q * fq).astype(BF16)
    kh = (k * fk).astype(BF16)
    st = s_ref[...]
    chunks = range(n // c)
    for j in (reversed(chunks) if reverse else chunks):
        rows = slice(j * c, (j + 1) * c)
        o_ref[rows, :] = o_intra[rows] + _dot_nt(qh[rows], st.astype(BF16))
        st = st * ftot[j * c:j * c + 1, :] + _dot_tn(vb[rows], kh[rows])
    s_ref[...] = st


def _gla_scan(qk, v, gd, w_up, b, reverse):
    T = qk.shape[0]
    n = GLA_TILE
    nt = T // n
    W, L = _gla_tables(reverse)
    W = jnp.asarray(W, BF16)
    L = jnp.asarray(L)
    tidx = (lambda i: nt - 1 - i) if reverse else (lambda i: i)
    r = GLA_GATE_RANK
    gdd = gd[:, r:2 * r] if reverse else gd[:, :r]
    b2 = b.reshape(1, GLA_DK)
    return pl.pallas_call(
        functools.partial(_gla_scan_kernel, reverse=reverse),
        grid=(GLA_HEADS, nt),
        in_specs=[
            pl.BlockSpec((n, GLA_HEAD_K), lambda h, i: (tidx(i), h)),
            pl.BlockSpec((n, GLA_HEAD_K), lambda h, i: (tidx(i), GLA_HEADS + h)),
            pl.BlockSpec((n, GLA_HEAD_V), lambda h, i: (tidx(i), h)),
            pl.BlockSpec((n, r), lambda h, i: (tidx(i), 0)),
            pl.BlockSpec((r, GLA_HEAD_K), lambda h, i: (0, h)),
            pl.BlockSpec((1, GLA_HEAD_K), lambda h, i: (0, h)),
            pl.BlockSpec(W.shape, lambda h, i: (0, 0)),
            pl.BlockSpec(L.shape, lambda h, i: (0, 0)),
        ],
        out_specs=pl.BlockSpec((n, GLA_HEAD_V), lambda h, i: (tidx(i), h)),
        out_shape=jax.ShapeDtypeStruct((T, GLA_DV), F32),
        scratch_shapes=[pltpu.VMEM((GLA_HEAD_V, GLA_HEAD_K), F32)],
        compiler_params=_cparams(("arbitrary", "arbitrary")),
        name="gla_scan_bwd" if reverse else "gla_scan_fwd",
    )(qk, qk, v, gdd, w_up, b2, W, L)


def _gla_out_kernel(of_ref, ob_ref, r_ref, x_ref, hn_ref, w_ref, o_ref):
    acc = x_ref[...]
    for h in range(GLA_HEADS):
        cols = slice(h * GLA_HEAD_V, (h + 1) * GLA_HEAD_V)
        o = _rms(of_ref[:, cols] + ob_ref[:, cols], hn_ref[...])
        r = r_ref[:, cols]
        gated = o * (r * (1.0 / (1.0 + jnp.exp(-r))))
        acc = acc + _dot(gated.astype(BF16), w_ref[cols, :])
    o_ref[...] = acc


def _gla_out(of, ob, r, x, head_norm, w_out, tm=512):
    T = x.shape[0]
    w = w_out.astype(BF16)
    hn = head_norm.reshape(1, GLA_HEAD_V)
    row = lambda n: pl.BlockSpec((tm, n), lambda i: (i, 0))
    full = lambda a: pl.BlockSpec(a.shape, lambda i: (0, 0))
    return pl.pallas_call(
        _gla_out_kernel,
        grid=(T // tm,),
        in_specs=[row(GLA_DV), row(GLA_DV), row(GLA_DV), row(D_MODEL), full(hn), full(w)],
        out_specs=row(D_MODEL),
        out_shape=jax.ShapeDtypeStruct((T, D_MODEL), F32),
        compiler_params=_cparams(("arbitrary",)),
        name="gla_out",
    )(of, ob, r, x, hn, w)


def _gla_mixer(x, gain, w_in, w_up_f, b_f, w_up_b, b_b, head_norm, w_out):
    qk, v, r, gd = _gla_in(x, gain, w_in)
    of = _gla_scan(qk, v, gd, w_up_f, b_f, reverse=False)
    ob = _gla_scan(qk, v, gd, w_up_b, b_b, reverse=True)
    return _gla_out(of, ob, r, x, head_norm, w_out)


def _router_kernel(h_ref, g_ref, w_ref, aff_ref, hn_ref):
    hn = _rms(h_ref[...], g_ref[...])
    hn_ref[...] = hn.astype(BF16)
    logits = _dot_split(hn, w_ref[...])
    e = jnp.exp(logits - jnp.max(logits, axis=-1, keepdims=True))
    aff_ref[...] = e / jnp.sum(e, axis=-1, keepdims=True)


def _router(h, gain, w_router, tm=512):
    T = h.shape[0]
    gain2 = gain.reshape(1, D_MODEL)
    row = lambda n: pl.BlockSpec((tm, n), lambda i: (i, 0))
    full = lambda a: pl.BlockSpec(a.shape, lambda i: (0, 0))
    return pl.pallas_call(
        _router_kernel,
        grid=(T // tm,),
        in_specs=[row(D_MODEL), full(gain2), full(w_router)],
        out_specs=[row(N_EXPERTS), row(D_MODEL)],
        out_shape=[jax.ShapeDtypeStruct((T, N_EXPERTS), F32), jax.ShapeDtypeStruct((T, D_MODEL), BF16)],
        compiler_params=_cparams(("arbitrary",)),
        name="moe_router",
    )(h, gain2, w_router)


def _select_kernel(aff_ref, tri_ref, posm_ref, before_ref, *, cap):
    T = aff_ref.shape[1]
    bits = pltpu.bitcast(aff_ref[...], jnp.int32)

    def search(it, thr):
        cand = thr | jnp.left_shift(jnp.int32(1), 30 - it)
        cnt = jnp.sum(jnp.where(bits >= cand, 1.0, 0.0), axis=1, keepdims=True)
        return jnp.where(cnt >= cap, cand, thr)

    thr = lax.fori_loop(0, 31, search, jnp.zeros((N_EXPERTS, 1), jnp.int32))
    n_gt = jnp.sum(jnp.where(bits > thr, 1.0, 0.0), axis=1, keepdims=True)
    need = cap - n_gt
    tri = tri_ref[...]

    def scan(j, carry):
        c_eq, c_sel = carry
        cols = pl.ds(pl.multiple_of(j * LANES, LANES), LANES)
        blk = pltpu.bitcast(aff_ref[:, cols], jnp.int32)
        eq = jnp.where(blk == thr, 1.0, 0.0)
        rank = _dot(eq.astype(BF16), tri) + c_eq - eq
        sel = jnp.where((blk > thr) | ((eq > 0.0) & (rank < need)), 1.0, 0.0)
        before = _dot(sel.astype(BF16), tri) + c_sel - sel
        before_ref[:, cols] = before.astype(jnp.int32)
        posm_ref[:, cols] = jnp.where(sel > 0.0, before, -1.0).astype(jnp.int32)
        return (c_eq + jnp.sum(eq, axis=1, keepdims=True), c_sel + jnp.sum(sel, axis=1, keepdims=True))

    zero = jnp.zeros((N_EXPERTS, 1), F32)
    lax.fori_loop(0, T // LANES, scan, (zero, zero))


def _select(aff_t, cap):
    T = aff_t.shape[1]
    tri = jnp.asarray(np.triu(np.ones((LANES, LANES), np.float32)), BF16)
    full = lambda a: pl.BlockSpec(a.shape, lambda: (0,) * a.ndim)
    out = jax.ShapeDtypeStruct((N_EXPERTS, T), jnp.int32)
    return pl.pallas_call(
        functools.partial(_select_kernel, cap=cap),
        in_specs=[full(aff_t), full(tri)],
        out_specs=[pl.BlockSpec((N_EXPERTS, T), lambda: (0, 0))] * 2,
        out_shape=[out, out],
        compiler_params=pltpu.CompilerParams(vmem_limit_bytes=VMEM_LIMIT),
        name="moe_select",
    )(aff_t, tri)


def _dispatch_window():
    return MOE_TILE + F32_SUBLANES


def _dispatch_kernel(start_ref, hn_ref, posm_ref, x_ref, acc_ref, *, cap, nt, sub):
    e, t = pl.program_id(0), pl.program_id(1)
    win = _dispatch_window()

    @pl.when(t == 0)
    def _():
        acc_ref[...] = jnp.zeros_like(acc_ref)

    for s in range(sub):
        tok = slice(s * MOE_TILE, (s + 1) * MOE_TILE)
        base = pl.multiple_of((start_ref[e * nt + t * sub + s] // F32_SUBLANES) * F32_SUBLANES, F32_SUBLANES)
        slot = base + lax.broadcasted_iota(jnp.int32, (win, MOE_TILE), 0)
        onehot = jnp.where(posm_ref[0, :, tok] == slot, 1.0, 0.0).astype(BF16)
        acc_ref[pl.ds(base, win), :] += _dot(onehot, hn_ref[tok, :])

    @pl.when(t == pl.num_programs(1) - 1)
    def _():
        x_ref[0] = acc_ref[:cap, :].astype(BF16)


def _dispatch(hn, posm, start, cap, sub=4):
    T = hn.shape[0]
    nt = T // MOE_TILE
    sub = math.gcd(sub, nt)
    blk = sub * MOE_TILE
    posm3 = posm.reshape(N_EXPERTS, 1, T)
    grid_spec = pltpu.PrefetchScalarGridSpec(
        num_scalar_prefetch=1,
        grid=(N_EXPERTS, nt // sub),
        in_specs=[
            pl.BlockSpec((blk, D_MODEL), lambda e, t, s: (t, 0)),
            pl.BlockSpec((1, 1, blk), lambda e, t, s: (e, 0, t)),
        ],
        out_specs=pl.BlockSpec((1, cap, D_MODEL), lambda e, t, s: (e, 0, 0)),
        scratch_shapes=[pltpu.VMEM((cap + _dispatch_window(), D_MODEL), F32)],
    )
    return pl.pallas_call(
        functools.partial(_dispatch_kernel, cap=cap, nt=nt, sub=sub),
        grid_spec=grid_spec,
        out_shape=jax.ShapeDtypeStruct((N_EXPERTS, cap, D_MODEL), BF16),
        compiler_params=_cparams(("arbitrary", "arbitrary")),
        name="moe_dispatch",
    )(start.reshape(-1), hn, posm3)


def _ffn_kernel(x_ref, wg_ref, wu_ref, wd_ref, y_ref, acc_ref):
    f = pl.program_id(1)

    @pl.when(f == 0)
    def _():
        acc_ref[...] = jnp.zeros_like(acc_ref)

    x = x_ref[0]
    a = _dot(x, wg_ref[0, 0].astype(BF16))
    u = _dot(x, wu_ref[0, 0].astype(BF16))
    mid = (a * (1.0 / (1.0 + jnp.exp(-a))) * u).astype(BF16)
    acc_ref[...] += _dot(mid, wd_ref[0, 0].astype(BF16))

    @pl.when(f == pl.num_programs(1) - 1)
    def _():
        y_ref[0] = acc_ref[...].astype(BF16)


def _ffn(x, w_gate, w_up, w_down, layer, tf=512):
    cap = x.shape[1]
    return pl.pallas_call(
        _ffn_kernel,
        grid=(N_EXPERTS, EXPERT_FF // tf),
        in_specs=[
            pl.BlockSpec((1, cap, D_MODEL), lambda e, f: (e, 0, 0)),
            pl.BlockSpec((1, 1, D_MODEL, tf), lambda e, f: (layer, e, 0, f)),
            pl.BlockSpec((1, 1, D_MODEL, tf), lambda e, f: (layer, e, 0, f)),
            pl.BlockSpec((1, 1, tf, D_MODEL), lambda e, f: (layer, e, f, 0)),
        ],
        out_specs=pl.BlockSpec((1, cap, D_MODEL), lambda e, f: (e, 0, 0)),
        out_shape=jax.ShapeDtypeStruct((N_EXPERTS, cap, D_MODEL), BF16),
        scratch_shapes=[pltpu.VMEM((cap, D_MODEL), F32)],
        compiler_params=_cparams(("arbitrary", "arbitrary")),
        name="moe_ffn",
    )(x, w_gate, w_up, w_down)


def _combine_window():
    return MOE_TILE + BF16_SUBLANES


def _combine_kernel(start_ref, h_ref, aff_ref, posm_ref, g_ref, y_hbm, o_ref, buf_ref, sem_ref, acc_ref, *,
                    cap, nt, final_norm):
    t = pl.program_id(0)
    win = _combine_window()

    def base_of(tile, e):
        b = (start_ref[e * (nt + 1) + tile] // BF16_SUBLANES) * BF16_SUBLANES
        return pl.multiple_of(jnp.minimum(b, cap - win), BF16_SUBLANES)

    def window_copy(tile, e, slot):
        return pltpu.make_async_copy(y_hbm.at[e, pl.ds(base_of(tile, e), win), :], buf_ref.at[slot, e],
                                     sem_ref.at[slot, e])

    def fetch(tile, slot):
        for e in range(N_EXPERTS):
            window_copy(tile, e, slot).start()

    slot = t % 2

    @pl.when(t == 0)
    def _():
        fetch(0, 0)

    @pl.when(t + 1 < nt)
    def _():
        fetch(t + 1, 1 - slot)

    def needs_tail(e):
        return start_ref[e * (nt + 1) + t + 1] > base_of(t, e) + MOE_TILE

    for e in range(N_EXPERTS):
        window_copy(t, e, slot).wait()
    acc = h_ref[...]
    for e in range(N_EXPERTS):
        hit = posm_ref[:, e:e + 1] == base_of(t, e) + lax.broadcasted_iota(jnp.int32, (MOE_TILE, MOE_TILE), 1)
        y = _dot(jnp.where(hit, 1.0, 0.0).astype(BF16), buf_ref[slot, e, :MOE_TILE, :])
        acc = acc + aff_ref[:, e:e + 1] * y
    acc_ref[...] = acc

    any_tail = needs_tail(0)
    for e in range(1, N_EXPERTS):
        any_tail = any_tail | needs_tail(e)

    @pl.when(any_tail)
    def _():
        for e in range(N_EXPERTS):
            @pl.when(needs_tail(e))
            def _():
                tail = base_of(t, e) + MOE_TILE + lax.broadcasted_iota(jnp.int32, (MOE_TILE, BF16_SUBLANES), 1)
                hit = posm_ref[:, e:e + 1] == tail
                yt = _dot(jnp.where(hit, 1.0, 0.0).astype(BF16), buf_ref[slot, e, MOE_TILE:, :])
                acc_ref[...] += aff_ref[:, e:e + 1] * yt

    acc = acc_ref[...]
    if final_norm:
        acc = _rms(acc, g_ref[...])
    o_ref[...] = acc


def _combine(h, aff, posm_t, start, y, cap, final_gain):
    T = h.shape[0]
    nt = T // MOE_TILE
    win = _combine_window()
    assert cap >= win and (cap - win) % BF16_SUBLANES == 0
    final_norm = final_gain is not None
    gain = (final_gain if final_norm else jnp.ones((D_MODEL,), F32)).reshape(1, D_MODEL)
    grid_spec = pltpu.PrefetchScalarGridSpec(
        num_scalar_prefetch=1,
        grid=(nt,),
        in_specs=[
            pl.BlockSpec((MOE_TILE, D_MODEL), lambda t, s: (t, 0)),
            pl.BlockSpec((MOE_TILE, N_EXPERTS), lambda t, s: (t, 0)),
            pl.BlockSpec((MOE_TILE, N_EXPERTS), lambda t, s: (t, 0)),
            pl.BlockSpec((1, D_MODEL), lambda t, s: (0, 0)),
            pl.BlockSpec(memory_space=pl.ANY),
        ],
        out_specs=pl.BlockSpec((MOE_TILE, D_MODEL), lambda t, s: (t, 0)),
        scratch_shapes=[pltpu.VMEM((2, N_EXPERTS, win, D_MODEL), BF16), pltpu.SemaphoreType.DMA((2, N_EXPERTS)),
                        pltpu.VMEM((MOE_TILE, D_MODEL), F32)],
    )
    return pl.pallas_call(
        functools.partial(_combine_kernel, cap=cap, nt=nt, final_norm=final_norm),
        grid_spec=grid_spec,
        out_shape=jax.ShapeDtypeStruct((T, D_MODEL), F32),
        compiler_params=_cparams(("arbitrary",)),
        name="moe_combine",
    )(start.reshape(-1), h, aff, posm_t, gain, y)


def _ec_moe(h, gain, w_router, w_gate, w_up, w_down, layer, final_gain=None):
    T = h.shape[0]
    cap = max(1, EC_CAPACITY_FACTOR * T // N_EXPERTS)
    aff, hn = _router(h, gain, w_router)
    posm, before = _select(aff.T, cap)
    start = before[:, ::MOE_TILE]
    x = _dispatch(hn, posm, start, cap)
    y = _ffn(x, w_gate, w_up, w_down, layer)
    start_end = jnp.concatenate([start, jnp.full((N_EXPERTS, 1), cap, jnp.int32)], axis=1)
    return _combine(h, aff, posm.T, start_end, y, cap, final_gain)


def _rms_cols(x, g):
    return x * lax.rsqrt(jnp.mean(x * x, axis=0, keepdims=True) + RMS_EPS) * g


MLA_QCOLS = MLA_NOPE + 2 * LANES


def _mla_in_kernel(h_ref, posc_ref, posr_ref, g_ref, win_ref, wintkv_ref, qn_ref, wq_ref, wukt_ref, kvn_ref,
                   kvnc_ref, freqr_ref, sign_ref, freqc_ref, q_out, kt_out, va_out):
    half = MLA_ROPE // 2
    hn = _rms(h_ref[...], g_ref[...]).astype(BF16)
    c = _dot(hn, win_ref[...])
    ckv_t = _dot_nt(wintkv_ref[...], hn)

    cq = _rms(c[:, :MLA_Q_RANK], qn_ref[...]).astype(BF16)
    qa = _dot(cq, wq_ref[...])
    ang = posc_ref[...].astype(F32) * freqr_ref[...]
    cos, sin_signed = jnp.cos(ang), jnp.sin(ang) * sign_ref[...]
    qscale = MLA_QK ** -0.5 * math.log2(math.e)
    for hd in range(MLA_HEADS):
        c0 = hd * MLA_QCOLS
        q_lat = _dot(qa[:, c0:c0 + MLA_NOPE].astype(BF16), wukt_ref[hd])
        rot = qa[:, c0 + MLA_NOPE:c0 + MLA_NOPE + LANES] * cos + qa[:, c0 + MLA_NOPE + LANES:c0 + MLA_QCOLS] * sin_signed
        q_out[hd, :, :MLA_NOPE] = (q_lat * qscale).astype(BF16)
        q_out[hd, :, MLA_NOPE:] = (rot[:, :MLA_ROPE] * qscale).astype(BF16)

    kt_out[:MLA_KV_RANK, :] = _rms_cols(ckv_t[:MLA_KV_RANK], kvnc_ref[...]).astype(BF16)
    ang_t = freqc_ref[...] * posr_ref[...].astype(F32)
    cos_t, sin_t = jnp.cos(ang_t), jnp.sin(ang_t)
    k1, k2 = ckv_t[MLA_KV_RANK:MLA_KV_RANK + half], ckv_t[MLA_KV_RANK + half:]
    kt_out[MLA_KV_RANK:MLA_KV_RANK + half, :] = (k1 * cos_t - k2 * sin_t).astype(BF16)
    kt_out[MLA_KV_RANK + half:, :] = (k1 * sin_t + k2 * cos_t).astype(BF16)

    ckv = _rms(c[:, MLA_Q_RANK:MLA_Q_RANK + MLA_KV_RANK], kvn_ref[...])
    va_out[:, :MLA_KV_RANK] = ckv.astype(BF16)
    lane = lax.broadcasted_iota(jnp.int32, (ckv.shape[0], LANES), 1)
    va_out[:, MLA_KV_RANK:] = jnp.where(lane == 0, 1.0, 0.0).astype(BF16)


def _mla_in(h, positions, gain, w_in, q_norm, w_uq, kv_norm, w_ukv, tm=256):
    T = h.shape[0]
    H, half = MLA_HEADS, MLA_ROPE // 2
    wq = w_uq.reshape(MLA_Q_RANK, H, MLA_QK)
    x1, x2 = wq[:, :, MLA_NOPE:MLA_NOPE + half], wq[:, :, MLA_NOPE + half:]
    pad = jnp.zeros((MLA_Q_RANK, H, LANES - MLA_ROPE), F32)
    wq_wide = jnp.concatenate([wq[:, :, :MLA_NOPE], x1, x2, pad, x2, x1, pad], axis=2)
    wq_wide = wq_wide.reshape(MLA_Q_RANK, H * MLA_QCOLS).astype(BF16)
    wuk_t = w_ukv.reshape(MLA_KV_RANK, H, MLA_NOPE + MLA_V)[:, :, :MLA_NOPE].transpose(1, 2, 0).astype(BF16)
    inv_freq = ROPE_BASE ** (-jnp.arange(half, dtype=F32) / half)
    zeros = jnp.zeros((LANES - MLA_ROPE,), F32)
    freq_row = jnp.concatenate([inv_freq, inv_freq, zeros]).reshape(1, LANES)
    sign_row = jnp.concatenate([-jnp.ones((half,), F32), jnp.ones((half,), F32), zeros]).reshape(1, LANES)
    w_in_b = w_in.astype(BF16)
    args = [h, positions.reshape(T, 1), positions.reshape(1, T), gain.reshape(1, -1), w_in_b,
            w_in_b[:, MLA_Q_RANK:].T, q_norm.reshape(1, -1), wq_wide, wuk_t, kv_norm.reshape(1, -1),
            kv_norm.reshape(-1, 1), freq_row, sign_row, inv_freq.reshape(half, 1)]
    full = lambda a: pl.BlockSpec(a.shape, lambda i: (0,) * a.ndim)
    in_specs = [pl.BlockSpec((tm, D_MODEL), lambda i: (i, 0)), pl.BlockSpec((tm, 1), lambda i: (i, 0)),
                pl.BlockSpec((1, tm), lambda i: (0, i))] + [full(a) for a in args[3:]]
    return pl.pallas_call(
        _mla_in_kernel,
        grid=(T // tm,),
        in_specs=in_specs,
        out_specs=[pl.BlockSpec((H, tm, MLA_QK), lambda i: (0, i, 0)),
                   pl.BlockSpec((MLA_QK, tm), lambda i: (0, i)),
                   pl.BlockSpec((tm, 2 * LANES), lambda i: (i, 0))],
        out_shape=[jax.ShapeDtypeStruct((H, T, MLA_QK), BF16), jax.ShapeDtypeStruct((MLA_QK, T), BF16),
                   jax.ShapeDtypeStruct((T, 2 * LANES), BF16)],
        compiler_params=_cparams(("arbitrary",)),
        name="mla_in",
    )(*args)


def _flash_kernel(q_ref, qn_ref, kt_ref, va_ref, o_ref, sa_ref, sb_ref, pa_ref, pb_ref, acc_ref, m_ref, alpha_ref,
                  *, tk, unroll):
    H, tq, _ = q_ref.shape
    T = kt_ref.shape[1]
    n = T // tk
    q = q_ref[...].reshape(H * tq, MLA_QK)
    q_next = qn_ref[...].reshape(H * tq, MLA_QK)

    def k_tile(j):
        return kt_ref[:, pl.ds(pl.multiple_of(j * tk, tk), tk)]

    def v_tile(j):
        return va_ref[pl.ds(pl.multiple_of(j * tk, tk), tk), :]

    def lanes_rep(x, width):
        return jnp.concatenate([x] * (width // LANES), axis=1)

    def rescaled_acc():
        return lanes_rep(alpha_ref[...], 2 * LANES) * acc_ref[...]

    def step(j, s_cur, s_nxt, p_cur, p_prv):
        acc_ref[...] = rescaled_acc() + _dot(p_prv[...], v_tile(jnp.maximum(j - 1, 0)))
        last = j == n - 1
        s_nxt[...] = _dot(jnp.where(last, q_next, q), k_tile(jnp.where(last, 0, j + 1)))
        s = s_cur[...]
        m_old = m_ref[...]
        m_new = jnp.maximum(m_old, jnp.max(s, axis=1, keepdims=True))
        alpha_ref[...] = jnp.exp2(m_old - m_new)
        m_ref[...] = m_new
        p_cur[...] = jnp.exp2(s - lanes_rep(m_new, tk)).astype(BF16)

    def body(i, carry):
        for u in range(unroll // 2):
            step(unroll * i + 2 * u, sa_ref, sb_ref, pa_ref, pb_ref)
            step(unroll * i + 2 * u + 1, sb_ref, sa_ref, pb_ref, pa_ref)
        return carry

    acc_ref[...] = jnp.zeros_like(acc_ref)
    pb_ref[...] = jnp.zeros_like(pb_ref)
    alpha_ref[...] = jnp.ones_like(alpha_ref)
    m_ref[...] = jnp.full_like(m_ref, -jnp.inf)

    @pl.when(pl.program_id(0) == 0)
    def _():
        sa_ref[...] = _dot(q, k_tile(0))

    lax.fori_loop(0, n // unroll, body, 0)
    acc = rescaled_acc() + _dot(pb_ref[...], v_tile(n - 1))
    o_lat = (acc[:, :MLA_KV_RANK] / acc[:, MLA_KV_RANK:MLA_KV_RANK + 1]).astype(BF16)
    for hd in range(H):
        o_ref[:, hd * MLA_KV_RANK:(hd + 1) * MLA_KV_RANK] = o_lat[hd * tq:(hd + 1) * tq]


def _flash(q, kt, va, tq=64, tk=1024):
    H, T, _ = q.shape
    tq, tk = min(tq, T), min(tk, T)
    unroll = math.gcd(T // tk, 8)
    assert (T // tk) % unroll == 0 and unroll % 2 == 0 and tq % BF16_SUBLANES == 0
    rows = H * tq
    steps = T // tq
    return pl.pallas_call(
        functools.partial(_flash_kernel, tk=tk, unroll=unroll),
        grid=(steps,),
        scratch_shapes=[pltpu.VMEM((rows, tk), F32), pltpu.VMEM((rows, tk), F32), pltpu.VMEM((rows, tk), BF16),
                        pltpu.VMEM((rows, tk), BF16), pltpu.VMEM((rows, 2 * LANES), F32),
                        pltpu.VMEM((rows, LANES), F32), pltpu.VMEM((rows, LANES), F32)],
        in_specs=[
            pl.BlockSpec((H, tq, MLA_QK), lambda i: (0, i, 0)),
            pl.BlockSpec((H, tq, MLA_QK), lambda i: (0, jnp.minimum(i + 1, steps - 1), 0)),
            pl.BlockSpec((MLA_QK, T), lambda i: (0, 0)),
            pl.BlockSpec((T, 2 * LANES), lambda i: (0, 0)),
        ],
        out_specs=pl.BlockSpec((tq, H * MLA_KV_RANK), lambda i: (i, 0)),
        out_shape=jax.ShapeDtypeStruct((T, H * MLA_KV_RANK), BF16),
        compiler_params=_cparams(("arbitrary",)),
        name="mla_flash",
    )(q, q, kt, va)


def _mla_out_kernel(o_ref, wuv_ref, w_ref, h_ref, out_ref):
    v = [_dot(o_ref[:, hd * MLA_KV_RANK:(hd + 1) * MLA_KV_RANK], wuv_ref[hd]).astype(BF16)
         for hd in range(MLA_HEADS)]
    out_ref[...] = h_ref[...] + _dot(jnp.concatenate(v, axis=1), w_ref[...])


def _mla_out(o_lat, w_ukv, w_out, h, tm=512):
    T = h.shape[0]
    H = MLA_HEADS
    wuv = w_ukv.reshape(MLA_KV_RANK, H, MLA_NOPE + MLA_V)[:, :, MLA_NOPE:].transpose(1, 0, 2).astype(BF16)
    wb = w_out.astype(BF16)
    return pl.pallas_call(
        _mla_out_kernel,
        grid=(T // tm,),
        in_specs=[pl.BlockSpec((tm, H * MLA_KV_RANK), lambda i: (i, 0)), pl.BlockSpec(wuv.shape, lambda i: (0, 0, 0)),
                  pl.BlockSpec(wb.shape, lambda i: (0, 0)), pl.BlockSpec((tm, D_MODEL), lambda i: (i, 0))],
        out_specs=pl.BlockSpec((tm, D_MODEL), lambda i: (i, 0)),
        out_shape=jax.ShapeDtypeStruct((T, D_MODEL), F32),
        compiler_params=_cparams(("arbitrary",)),
        name="mla_out",
    )(o_lat, wuv, wb, h)


def _mla_mixer(h, positions, gain, w_in, q_norm, w_uq, kv_norm, w_ukv, w_out):
    q, kt, va = _mla_in(h, positions, gain, w_in, q_norm, w_uq, kv_norm, w_ukv)
    o_lat = _flash(q, kt, va)
    return _mla_out(o_lat, w_ukv, w_out, h)


def kernel(x, positions, mix_norm, ffn_norm, final_norm, gla_w_in, gla_w_gate_up_f, gla_b_gate_f, gla_w_gate_up_b,
           gla_b_gate_b, gla_head_norm, gla_w_out, mla_w_in, mla_q_norm, mla_w_uq, mla_kv_norm, mla_w_ukv,
           mla_w_out, moe_w_router, moe_w_gate, moe_w_up, moe_w_down):
    B, T, D = x.shape
    outs = []
    for b in range(B):
        h = x[b]
        h = _gla_mixer(h, mix_norm[0], gla_w_in[0], gla_w_gate_up_f[0], gla_b_gate_f[0], gla_w_gate_up_b[0],
                       gla_b_gate_b[0], gla_head_norm[0], gla_w_out[0])
        h = _ec_moe(h, ffn_norm[0], moe_w_router[0], moe_w_gate, moe_w_up, moe_w_down, 0)
        h = _mla_mixer(h, positions[b], mix_norm[1], mla_w_in[0], mla_q_norm[0], mla_w_uq[0], mla_kv_norm[0],
                       mla_w_ukv[0], mla_w_out[0])
        h = _ec_moe(h, ffn_norm[1], moe_w_router[1], moe_w_gate, moe_w_up, moe_w_down, 1, final_gain=final_norm)
        outs.append(h)
    return jnp.stack(outs)
```

```python
import functools
import math

import numpy as np
import jax
import jax.numpy as jnp
from jax import lax
from jax.experimental import pallas as pl
from jax.experimental.pallas import tpu as pltpu

F32 = jnp.float32
BF16 = jnp.bfloat16

D_MODEL = 1024
RMS_EPS = 1e-6

GLA_HEADS = 4
GLA_DK = 512
GLA_DV = 1024
GLA_HEAD_K = GLA_DK // GLA_HEADS
GLA_HEAD_V = GLA_DV // GLA_HEADS
GLA_GATE_RANK = 16
GLA_TAU = 16.0
GLA_CHUNK = 64
GLA_TILE = 256
GLA_LEVELS = 6

MLA_HEADS = 16
MLA_Q_RANK = 256
MLA_KV_RANK = 128
MLA_NOPE = 128
MLA_ROPE = 64
MLA_V = 128
MLA_QK = MLA_NOPE + MLA_ROPE
ROPE_BASE = 10000.0

N_EXPERTS = 16
EXPERT_FF = 2048
EC_CAPACITY_FACTOR = 2
MOE_TILE = 256
BF16_SUBLANES = 16
F32_SUBLANES = 8
LANES = 128

VMEM_LIMIT = 56 * 1024 * 1024


def _cparams(sem):
    return pltpu.CompilerParams(dimension_semantics=sem, vmem_limit_bytes=VMEM_LIMIT)


def _rms(x, g):
    return x * lax.rsqrt(jnp.mean(x * x, axis=-1, keepdims=True) + RMS_EPS) * g


def _split_bf16(x):
    hi = x.astype(BF16)
    lo = (x - hi.astype(F32)).astype(BF16)
    return hi, lo


def _dot(a, b):
    return jnp.dot(a, b, preferred_element_type=F32)


def _dot_nt(a, b):
    return lax.dot_general(a, b, (((1,), (1,)), ((), ())), preferred_element_type=F32)


def _dot_tn(a, b):
    return lax.dot_general(a, b, (((0,), (0,)), ((), ())), preferred_element_type=F32)


def _dot_split(a, b):
    ah, al = _split_bf16(a)
    bh, bl = _split_bf16(b)
    return _dot(ah, bh) + _dot(ah, bl) + _dot(al, bh)


def _gla_in_kernel(x_ref, g_ref, wqk_ref, wv_ref, wr_ref, wgd_ref, qk_ref, v_ref, r_ref, gd_ref):
    hn = _rms(x_ref[...], g_ref[...]).astype(BF16)
    qk_ref[...] = _dot(hn, wqk_ref[...])
    v_ref[...] = _dot(hn, wv_ref[...])
    r_ref[...] = _dot(hn, wr_ref[...])
    gd_ref[...] = _dot(hn, wgd_ref[...])


def _gla_in(x, gain, w_in, tm=512):
    T = x.shape[0]
    wqk = w_in[:, :2 * GLA_DK].astype(BF16)
    wv = w_in[:, 2 * GLA_DK:2 * GLA_DK + GLA_DV].astype(BF16)
    wr = w_in[:, 2 * GLA_DK + GLA_DV:2 * GLA_DK + 2 * GLA_DV].astype(BF16)
    wgd = w_in[:, 2 * GLA_DK + 2 * GLA_DV:].astype(BF16)
    ngd = 2 * GLA_GATE_RANK
    row = lambda n: pl.BlockSpec((tm, n), lambda i: (i, 0))
    full = lambda a: pl.BlockSpec(a.shape, lambda i: (0, 0))
    gain2 = gain.reshape(1, D_MODEL)
    return pl.pallas_call(
        _gla_in_kernel,
        grid=(T // tm,),
        in_specs=[row(D_MODEL), full(gain2), full(wqk), full(wv), full(wr), full(wgd)],
        out_specs=[row(2 * GLA_DK), row(GLA_DV), row(GLA_DV), row(ngd)],
        out_shape=[jax.ShapeDtypeStruct((T, 2 * GLA_DK), F32), jax.ShapeDtypeStruct((T, GLA_DV), F32),
                   jax.ShapeDtypeStruct((T, GLA_DV), F32), jax.ShapeDtypeStruct((T, ngd), F32)],
        compiler_params=_cparams(("arbitrary",)),
        name="gla_in",
    )(x, gain2, wqk, wv, wr, wgd)


def _gla_tables(reverse):
    n, c = GLA_TILE, GLA_CHUNK
    W = np.zeros((GLA_LEVELS + 3, n, n), np.float32)
    L = np.full((n, n), -1, np.int32)
    for t in range(n):
        c0 = (t // c) * c
        tt = t - c0
        for l in range(GLA_LEVELS):
            b = (c // 2) >> l
            p0 = c0 + (tt // (2 * b)) * 2 * b
            mid = p0 + b
            second = t >= mid
            if not reverse:
                if second:
                    W[l, t, mid:t + 1] = 1
                    L[t, p0:mid] = l
                else:
                    W[l, t, t + 1:mid] = 1
            else:
                if second:
                    W[l, t, mid:t] = 1
                else:
                    W[l, t, t:mid] = 1
                    L[t, mid:p0 + 2 * b] = l
        if not reverse:
            W[GLA_LEVELS, t, c0:t + 1] = 1
            W[GLA_LEVELS + 1, t, t + 1:c0 + c] = 1
            L[t, t] = GLA_LEVELS
        else:
            W[GLA_LEVELS, t, t:c0 + c] = 1
            W[GLA_LEVELS + 1, t, c0:t] = 1
        W[GLA_LEVELS + 2, t, c0:c0 + c] = 1
    return W.reshape(-1, n), L


def _gla_scan_kernel(qkf_ref, vf_ref, gdf_ref, qkb_ref, vb_ref, gdb_ref, wupf_ref, bf_ref, wupb_ref, bb_ref,
                     wf_ref, lf_ref, wb_ref, lb_ref, of_ref, ob_ref, s_ref):
    n, c, r = GLA_TILE, GLA_CHUNK, GLA_GATE_RANK

    @pl.when(pl.program_id(0) == 0)
    def _():
        s_ref[...] = jnp.zeros_like(s_ref)

    dirs = [(qkf_ref, vf_ref, gdf_ref[:, :r], wupf_ref, bf_ref, wf_ref, lf_ref, of_ref, False),
            (qkb_ref, vb_ref, gdb_ref[:, r:], wupb_ref, bb_ref, wb_ref, lb_ref, ob_ref, True)]

    factors = []
    for qk_ref, v_ref, gd, wup_ref, b_ref, w_ref, l_ref, o_ref, reverse in dirs:
        z = _dot_split(gd, wup_ref[...]) + b_ref[...]
        g = (jnp.minimum(z, 0.0) - jnp.log1p(jnp.exp(-jnp.abs(z)))) * (1.0 / GLA_TAU)
        ghi, glo = _split_bf16(g)
        w = w_ref[...]
        factors.append(jnp.exp(_dot(w, ghi) + _dot(w, glo)))

    chains = []
    for d, (qk_ref, v_ref, gd, wup_ref, b_ref, w_ref, l_ref, o_ref, reverse) in enumerate(dirs):
        lvl = l_ref[...]
        for h in range(GLA_HEADS):
            kcols = slice(h * GLA_HEAD_K, (h + 1) * GLA_HEAD_K)
            vcols = slice(h * GLA_HEAD_V, (h + 1) * GLA_HEAD_V)
            f = factors[d][:, kcols]
            q = qk_ref[:, kcols] * (GLA_HEAD_K ** -0.5)
            k = qk_ref[:, GLA_DK + h * GLA_HEAD_K:GLA_DK + (h + 1) * GLA_HEAD_K]
            vb = v_ref[:, vcols].astype(BF16)
            attn = jnp.zeros((n, n), F32)
            for l in range(GLA_LEVELS):
                fl = f[l * n:(l + 1) * n]
                p = _dot_nt((q * fl).astype(BF16), (k * fl).astype(BF16))
                attn = jnp.where(lvl == l, p, attn)
            if not reverse:
                p = _dot_nt(q.astype(BF16), k.astype(BF16))
                attn = jnp.where(lvl == GLA_LEVELS, p, attn)
            o_intra = _dot(attn.astype(BF16), vb)
            qh = (q * f[GLA_LEVELS * n:(GLA_LEVELS + 1) * n]).astype(BF16)
            kh = (k * f[(GLA_LEVELS + 1) * n:(GLA_LEVELS + 2) * n]).astype(BF16)
            ftot = f[(GLA_LEVELS + 2) * n:(GLA_LEVELS + 3) * n]
            chains.append(dict(d=d, h=h, vcols=vcols, o_ref=o_ref, reverse=reverse, vb=vb, o_intra=o_intra, qh=qh,
                               kh=kh, ftot=ftot, st=s_ref[d, h]))

    nchunks = n // c
    for step in range(nchunks):
        for ch in chains:
            j = nchunks - 1 - step if ch["reverse"] else step
            rows = slice(j * c, (j + 1) * c)
            st = ch["st"]
            ch["o_ref"][rows, ch["vcols"]] = ch["o_intra"][rows] + _dot_nt(ch["qh"][rows], st.astype(BF16))
            ch["st"] = st * ch["ftot"][j * c:j * c + 1, :] + _dot_tn(ch["vb"][rows], ch["kh"][rows])
    for ch in chains:
        s_ref[ch["d"], ch["h"]] = ch["st"]


def _gla_scan(qk, v, gd, w_up_f, b_f, w_up_b, b_b):
    T = qk.shape[0]
    n = GLA_TILE
    nt = T // n
    tables = []
    for reverse in (False, True):
        W, L = _gla_tables(reverse)
        tables += [jnp.asarray(W, BF16), jnp.asarray(L)]
    fwd = lambda width: pl.BlockSpec((n, width), lambda i: (i, 0))
    bwd = lambda width: pl.BlockSpec((n, width), lambda i: (nt - 1 - i, 0))
    full = lambda a: pl.BlockSpec(a.shape, lambda i: (0, 0))
    consts = [w_up_f, b_f.reshape(1, GLA_DK), w_up_b, b_b.reshape(1, GLA_DK)] + tables
    ngd = 2 * GLA_GATE_RANK
    out = jax.ShapeDtypeStruct((T, GLA_DV), F32)
    return pl.pallas_call(
        _gla_scan_kernel,
        grid=(nt,),
        in_specs=[fwd(2 * GLA_DK), fwd(GLA_DV), fwd(ngd), bwd(2 * GLA_DK), bwd(GLA_DV), bwd(ngd)]
                 + [full(a) for a in consts],
        out_specs=[fwd(GLA_DV), bwd(GLA_DV)],
        out_shape=[out, out],
        scratch_shapes=[pltpu.VMEM((2, GLA_HEADS, GLA_HEAD_V, GLA_HEAD_K), F32)],
        compiler_params=_cparams(("arbitrary",)),
        name="gla_scan",
    )(qk, v, gd, qk, v, gd, *consts)


def _gla_out_kernel(of_ref, ob_ref, r_ref, x_ref, hn_ref, w_ref, o_ref):
    acc = x_ref[...]
    for h in range(GLA_HEADS):
        cols = slice(h * GLA_HEAD_V, (h + 1) * GLA_HEAD_V)
        o = _rms(of_ref[:, cols] + ob_ref[:, cols], hn_ref[...])
        r = r_ref[:, cols]
        gated = o * (r * (1.0 / (1.0 + jnp.exp(-r))))
        acc = acc + _dot(gated.astype(BF16), w_ref[cols, :])
    o_ref[...] = acc


def _gla_out(of, ob, r, x, head_norm, w_out, tm=512):
    T = x.shape[0]
    w = w_out.astype(BF16)
    hn = head_norm.reshape(1, GLA_HEAD_V)
    row = lambda n: pl.BlockSpec((tm, n), lambda i: (i, 0))
    full = lambda a: pl.BlockSpec(a.shape, lambda i: (0, 0))
    return pl.pallas_call(
        _gla_out_kernel,
        grid=(T // tm,),
        in_specs=[row(GLA_DV), row(GLA_DV), row(GLA_DV), row(D_MODEL), full(hn), full(w)],
        out_specs=row(D_MODEL),
        out_shape=jax.ShapeDtypeStruct((T, D_MODEL), F32),
        compiler_params=_cparams(("arbitrary",)),
        name="gla_out",
    )(of, ob, r, x, hn, w)


def _gla_mixer(x, gain, w_in, w_up_f, b_f, w_up_b, b_b, head_norm, w_out):
    qk, v, r, gd = _gla_in(x, gain, w_in)
    of, ob = _gla_scan(qk, v, gd, w_up_f, b_f, w_up_b, b_b)
    return _gla_out(of, ob, r, x, head_norm, w_out)


def _router_kernel(h_ref, g_ref, w_ref, aff_ref, hn_ref):
    hn = _rms(h_ref[...], g_ref[...])
    hn_ref[...] = hn.astype(BF16)
    logits = _dot_split(hn, w_ref[...])
    e = jnp.exp(logits - jnp.max(logits, axis=-1, keepdims=True))
    aff_ref[...] = e / jnp.sum(e, axis=-1, keepdims=True)


def _router(h, gain, w_router, tm=512):
    T = h.shape[0]
    gain2 = gain.reshape(1, D_MODEL)
    row = lambda n: pl.BlockSpec((tm, n), lambda i: (i, 0))
    full = lambda a: pl.BlockSpec(a.shape, lambda i: (0, 0))
    return pl.pallas_call(
        _router_kernel,
        grid=(T // tm,),
        in_specs=[row(D_MODEL), full(gain2), full(w_router)],
        out_specs=[row(N_EXPERTS), row(D_MODEL)],
        out_shape=[jax.ShapeDtypeStruct((T, N_EXPERTS), F32), jax.ShapeDtypeStruct((T, D_MODEL), BF16)],
        compiler_params=_cparams(("arbitrary",)),
        name="moe_router",
    )(h, gain2, w_router)


def _select_kernel(aff_ref, tri_ref, posm_ref, before_ref, *, cap):
    T = aff_ref.shape[1]
    bits = pltpu.bitcast(aff_ref[...], jnp.int32)

    def search(it, thr):
        cand = thr | jnp.left_shift(jnp.int32(1), 30 - it)
        cnt = jnp.sum(jnp.where(bits >= cand, 1.0, 0.0), axis=1, keepdims=True)
        return jnp.where(cnt >= cap, cand, thr)

    thr = lax.fori_loop(0, 31, search, jnp.zeros((N_EXPERTS, 1), jnp.int32))
    n_gt = jnp.sum(jnp.where(bits > thr, 1.0, 0.0), axis=1, keepdims=True)
    need = cap - n_gt
    tri = tri_ref[...]

    def scan(j, carry):
        c_eq, c_sel = carry
        cols = pl.ds(pl.multiple_of(j * LANES, LANES), LANES)
        blk = pltpu.bitcast(aff_ref[:, cols], jnp.int32)
        eq = jnp.where(blk == thr, 1.0, 0.0)
        rank = _dot(eq.astype(BF16), tri) + c_eq - eq
        sel = jnp.where((blk > thr) | ((eq > 0.0) & (rank < need)), 1.0, 0.0)
        before = _dot(sel.astype(BF16), tri) + c_sel - sel
        before_ref[:, cols] = before.astype(jnp.int32)
        posm_ref[:, cols] = jnp.where(sel > 0.0, before, -1.0).astype(jnp.int32)
        return (c_eq + jnp.sum(eq, axis=1, keepdims=True), c_sel + jnp.sum(sel, axis=1, keepdims=True))

    zero = jnp.zeros((N_EXPERTS, 1), F32)
    lax.fori_loop(0, T // LANES, scan, (zero, zero))


def _select(aff_t, cap):
    T = aff_t.shape[1]
    tri = jnp.asarray(np.triu(np.ones((LANES, LANES), np.float32)), BF16)
    full = lambda a: pl.BlockSpec(a.shape, lambda: (0,) * a.ndim)
    out = jax.ShapeDtypeStruct((N_EXPERTS, T), jnp.int32)
    return pl.pallas_call(
        functools.partial(_select_kernel, cap=cap),
        in_specs=[full(aff_t), full(tri)],
        out_specs=[pl.BlockSpec((N_EXPERTS, T), lambda: (0, 0))] * 2,
        out_shape=[out, out],
        compiler_params=pltpu.CompilerParams(vmem_limit_bytes=VMEM_LIMIT),
        name="moe_select",
    )(aff_t, tri)


def _dispatch_window():
    return MOE_TILE + F32_SUBLANES


def _dispatch_kernel(start_ref, hn_ref, posm_ref, x_ref, acc_ref, *, cap, nt, sub):
    e, t = pl.program_id(0), pl.program_id(1)
    win = _dispatch_window()

    @pl.when(t == 0)
    def _():
        acc_ref[...] = jnp.zeros_like(acc_ref)

    for s in range(sub):
        tok = slice(s * MOE_TILE, (s + 1) * MOE_TILE)
        base = pl.multiple_of((start_ref[e * nt + t * sub + s] // F32_SUBLANES) * F32_SUBLANES, F32_SUBLANES)
        slot = base + lax.broadcasted_iota(jnp.int32, (win, MOE_TILE), 0)
        onehot = jnp.where(posm_ref[0, :, tok] == slot, 1.0, 0.0).astype(BF16)
        acc_ref[pl.ds(base, win), :] += _dot(onehot, hn_ref[tok, :])

    @pl.when(t == pl.num_programs(1) - 1)
    def _():
        x_ref[0] = acc_ref[:cap, :].astype(BF16)


def _dispatch(hn, posm, start, cap, sub=8):
    T = hn.shape[0]
    nt = T // MOE_TILE
    sub = math.gcd(sub, nt)
    blk = sub * MOE_TILE
    posm3 = posm.reshape(N_EXPERTS, 1, T)
    grid_spec = pltpu.PrefetchScalarGridSpec(
        num_scalar_prefetch=1,
        grid=(N_EXPERTS, nt // sub),
        in_specs=[
            pl.BlockSpec((blk, D_MODEL), lambda e, t, s: (t, 0)),
            pl.BlockSpec((1, 1, blk), lambda e, t, s: (e, 0, t)),
        ],
        out_specs=pl.BlockSpec((1, cap, D_MODEL), lambda e, t, s: (e, 0, 0)),
        scratch_shapes=[pltpu.VMEM((cap + _dispatch_window(), D_MODEL), F32)],
    )
    return pl.pallas_call(
        functools.partial(_dispatch_kernel, cap=cap, nt=nt, sub=sub),
        grid_spec=grid_spec,
        out_shape=jax.ShapeDtypeStruct((N_EXPERTS, cap, D_MODEL), BF16),
        compiler_params=_cparams(("arbitrary", "arbitrary")),
        name="moe_dispatch",
    )(start.reshape(-1), hn, posm3)


def _ffn_kernel(x_ref, wg_ref, wu_ref, wd_ref, y_ref, acc_ref):
    f = pl.program_id(1)

    @pl.when(f == 0)
    def _():
        acc_ref[...] = jnp.zeros_like(acc_ref)

    x = x_ref[0]
    a = _dot(x, wg_ref[0, 0].astype(BF16))
    u = _dot(x, wu_ref[0, 0].astype(BF16))
    mid = (a * (1.0 / (1.0 + jnp.exp(-a))) * u).astype(BF16)
    acc_ref[...] += _dot(mid, wd_ref[0, 0].astype(BF16))

    @pl.when(f == pl.num_programs(1) - 1)
    def _():
        y_ref[0] = acc_ref[...].astype(BF16)


def _ffn(x, w_gate, w_up, w_down, layer, tf=512):
    cap = x.shape[1]
    return pl.pallas_call(
        _ffn_kernel,
        grid=(N_EXPERTS, EXPERT_FF // tf),
        in_specs=[
            pl.BlockSpec((1, cap, D_MODEL), lambda e, f: (e, 0, 0)),
            pl.BlockSpec((1, 1, D_MODEL, tf), lambda e, f: (layer, e, 0, f)),
            pl.BlockSpec((1, 1, D_MODEL, tf), lambda e, f: (layer, e, 0, f)),
            pl.BlockSpec((1, 1, tf, D_MODEL), lambda e, f: (layer, e, f, 0)),
        ],
        out_specs=pl.BlockSpec((1, cap, D_MODEL), lambda e, f: (e, 0, 0)),
        out_shape=jax.ShapeDtypeStruct((N_EXPERTS, cap, D_MODEL), BF16),
        scratch_shapes=[pltpu.VMEM((cap, D_MODEL), F32)],
        compiler_params=_cparams(("arbitrary", "arbitrary")),
        name="moe_ffn",
    )(x, w_gate, w_up, w_down)


def _combine_window():
    return MOE_TILE + BF16_SUBLANES


def _combine_kernel(start_ref, h_ref, aff_ref, posm_ref, g_ref, y_hbm, o_ref, buf_ref, sem_ref, acc_ref, *,
                    cap, nt, final_norm):
    t = pl.program_id(0)
    win = _combine_window()

    def base_of(tile, e):
        b = (start_ref[e * (nt + 1) + tile] // BF16_SUBLANES) * BF16_SUBLANES
        return pl.multiple_of(jnp.minimum(b, cap - win), BF16_SUBLANES)

    def window_copy(tile, e, slot):
        return pltpu.make_async_copy(y_hbm.at[e, pl.ds(base_of(tile, e), win), :], buf_ref.at[slot, e],
                                     sem_ref.at[slot, e])

    def fetch(tile, slot):
        for e in range(N_EXPERTS):
            window_copy(tile, e, slot).start()

    slot = t % 2

    @pl.when(t == 0)
    def _():
        fetch(0, 0)

    @pl.when(t + 1 < nt)
    def _():
        fetch(t + 1, 1 - slot)

    def needs_tail(e):
        return start_ref[e * (nt + 1) + t + 1] > base_of(t, e) + MOE_TILE

    for e in range(N_EXPERTS):
        window_copy(t, e, slot).wait()
    acc = h_ref[...]
    for e in range(N_EXPERTS):
        hit = posm_ref[:, e:e + 1] == base_of(t, e) + lax.broadcasted_iota(jnp.int32, (MOE_TILE, MOE_TILE), 1)
        y = _dot(jnp.where(hit, 1.0, 0.0).astype(BF16), buf_ref[slot, e, :MOE_TILE, :])
        acc = acc + aff_ref[:, e:e + 1] * y
    acc_ref[...] = acc

    any_tail = needs_tail(0)
    for e in range(1, N_EXPERTS):
        any_tail = any_tail | needs_tail(e)

    @pl.when(any_tail)
    def _():
        for e in range(N_EXPERTS):
            @pl.when(needs_tail(e))
            def _():
                tail = base_of(t, e) + MOE_TILE + lax.broadcasted_iota(jnp.int32, (MOE_TILE, BF16_SUBLANES), 1)
                hit = posm_ref[:, e:e + 1] == tail
                yt = _dot(jnp.where(hit, 1.0, 0.0).astype(BF16), buf_ref[slot, e, MOE_TILE:, :])
                acc_ref[...] += aff_ref[:, e:e + 1] * yt

    acc = acc_ref[...]
    if final_norm:
        acc = _rms(acc, g_ref[...])
    o_ref[...] = acc


def _combine(h, aff, posm_t, start, y, cap, final_gain):
    T = h.shape[0]
    nt = T // MOE_TILE
    win = _combine_window()
    assert cap >= win and (cap - win) % BF16_SUBLANES == 0
    final_norm = final_gain is not None
    gain = (final_gain if final_norm else jnp.ones((D_MODEL,), F32)).reshape(1, D_MODEL)
    grid_spec = pltpu.PrefetchScalarGridSpec(
        num_scalar_prefetch=1,
        grid=(nt,),
        in_specs=[
            pl.BlockSpec((MOE_TILE, D_MODEL), lambda t, s: (t, 0)),
            pl.BlockSpec((MOE_TILE, N_EXPERTS), lambda t, s: (t, 0)),
            pl.BlockSpec((MOE_TILE, N_EXPERTS), lambda t, s: (t, 0)),
            pl.BlockSpec((1, D_MODEL), lambda t, s: (0, 0)),
            pl.BlockSpec(memory_space=pl.ANY),
        ],
        out_specs=pl.BlockSpec((MOE_TILE, D_MODEL), lambda t, s: (t, 0)),
        scratch_shapes=[pltpu.VMEM((2, N_EXPERTS, win, D_MODEL), BF16), pltpu.SemaphoreType.DMA((2, N_EXPERTS)),
                        pltpu.VMEM((MOE_TILE, D_MODEL), F32)],
    )
    return pl.pallas_call(
        functools.partial(_combine_kernel, cap=cap, nt=nt, final_norm=final_norm),
        grid_spec=grid_spec,
        out_shape=jax.ShapeDtypeStruct((T, D_MODEL), F32),
        compiler_params=_cparams(("arbitrary",)),
        name="moe_combine",
    )(start.reshape(-1), h, aff, posm_t, gain, y)


def _ec_moe(h, gain, w_router, w_gate, w_up, w_down, layer, final_gain=None):
    T = h.shape[0]
    cap = max(1, EC_CAPACITY_FACTOR * T // N_EXPERTS)
    aff, hn = _router(h, gain, w_router)
    posm, before = _select(aff.T, cap)
    start = before[:, ::MOE_TILE]
    x = _dispatch(hn, posm, start, cap)
    y = _ffn(x, w_gate, w_up, w_down, layer)
    start_end = jnp.concatenate([start, jnp.full((N_EXPERTS, 1), cap, jnp.int32)], axis=1)
    return _combine(h, aff, posm.T, start_end, y, cap, final_gain)


def _rms_cols(x, g):
    return x * lax.rsqrt(jnp.mean(x * x, axis=0, keepdims=True) + RMS_EPS) * g


MLA_QCOLS = MLA_NOPE + 2 * LANES


def _mla_in_kernel(h_ref, posc_ref, posr_ref, g_ref, win_ref, wintkv_ref, qn_ref, wq_ref, wukt_ref, kvn_ref,
                   kvnc_ref, freqr_ref, sign_ref, freqc_ref, q_out, kt_out, va_out):
    half = MLA_ROPE // 2
    hn = _rms(h_ref[...], g_ref[...]).astype(BF16)
    c = _dot(hn, win_ref[...])
    ckv_t = _dot_nt(wintkv_ref[...], hn)

    cq = _rms(c[:, :MLA_Q_RANK], qn_ref[...]).astype(BF16)
    qa = _dot(cq, wq_ref[...])
    ang = posc_ref[...].astype(F32) * freqr_ref[...]
    cos, sin_signed = jnp.cos(ang), jnp.sin(ang) * sign_ref[...]
    qscale = MLA_QK ** -0.5 * math.log2(math.e)
    for hd in range(MLA_HEADS):
        c0 = hd * MLA_QCOLS
        q_lat = _dot(qa[:, c0:c0 + MLA_NOPE].astype(BF16), wukt_ref[hd])
        rot = qa[:, c0 + MLA_NOPE:c0 + MLA_NOPE + LANES] * cos + qa[:, c0 + MLA_NOPE + LANES:c0 + MLA_QCOLS] * sin_signed
        q_out[hd, :, :MLA_NOPE] = (q_lat * qscale).astype(BF16)
        q_out[hd, :, MLA_NOPE:] = (rot[:, :MLA_ROPE] * qscale).astype(BF16)

    kt_out[:MLA_KV_RANK, :] = _rms_cols(ckv_t[:MLA_KV_RANK], kvnc_ref[...]).astype(BF16)
    ang_t = freqc_ref[...] * posr_ref[...].astype(F32)
    cos_t, sin_t = jnp.cos(ang_t), jnp.sin(ang_t)
    k1, k2 = ckv_t[MLA_KV_RANK:MLA_KV_RANK + half], ckv_t[MLA_KV_RANK + half:]
    kt_out[MLA_KV_RANK:MLA_KV_RANK + half, :] = (k1 * cos_t - k2 * sin_t).astype(BF16)
    kt_out[MLA_KV_RANK + half:, :] = (k1 * sin_t + k2 * cos_t).astype(BF16)

    ckv = _rms(c[:, MLA_Q_RANK:MLA_Q_RANK + MLA_KV_RANK], kvn_ref[...])
    va_out[:, :MLA_KV_RANK] = ckv.astype(BF16)
    lane = lax.broadcasted_iota(jnp.int32, (ckv.shape[0], LANES), 1)
    va_out[:, MLA_KV_RANK:] = jnp.where(lane == 0, 1.0, 0.0).astype(BF16)


def _mla_in(h, positions, gain, w_in, q_norm, w_uq, kv_norm, w_ukv, tm=256):
    T = h.shape[0]
    H, half = MLA_HEADS, MLA_ROPE // 2
    wq = w_uq.reshape(MLA_Q_RANK, H, MLA_QK)
    x1, x2 = wq[:, :, MLA_NOPE:MLA_NOPE + half], wq[:, :, MLA_NOPE + half:]
    pad = jnp.zeros((MLA_Q_RANK, H, LANES - MLA_ROPE), F32)
    wq_wide = jnp.concatenate([wq[:, :, :MLA_NOPE], x1, x2, pad, x2, x1, pad], axis=2)
    wq_wide = wq_wide.reshape(MLA_Q_RANK, H * MLA_QCOLS).astype(BF16)
    wuk_t = w_ukv.reshape(MLA_KV_RANK, H, MLA_NOPE + MLA_V)[:, :, :MLA_NOPE].transpose(1, 2, 0).astype(BF16)
    inv_freq = ROPE_BASE ** (-jnp.arange(half, dtype=F32) / half)
    zeros = jnp.zeros((LANES - MLA_ROPE,), F32)
    freq_row = jnp.concatenate([inv_freq, inv_freq, zeros]).reshape(1, LANES)
    sign_row = jnp.concatenate([-jnp.ones((half,), F32), jnp.ones((half,), F32), zeros]).reshape(1, LANES)
    w_in_b = w_in.astype(BF16)
    args = [h, positions.reshape(T, 1), positions.reshape(1, T), gain.reshape(1, -1), w_in_b,
            w_in_b[:, MLA_Q_RANK:].T, q_norm.reshape(1, -1), wq_wide, wuk_t, kv_norm.reshape(1, -1),
            kv_norm.reshape(-1, 1), freq_row, sign_row, inv_freq.reshape(half, 1)]
    full = lambda a: pl.BlockSpec(a.shape, lambda i: (0,) * a.ndim)
    in_specs = [pl.BlockSpec((tm, D_MODEL), lambda i: (i, 0)), pl.BlockSpec((tm, 1), lambda i: (i, 0)),
                pl.BlockSpec((1, tm), lambda i: (0, i))] + [full(a) for a in args[3:]]
    return pl.pallas_call(
        _mla_in_kernel,
        grid=(T // tm,),
        in_specs=in_specs,
        out_specs=[pl.BlockSpec((H, tm, MLA_QK), lambda i: (0, i, 0)),
                   pl.BlockSpec((MLA_QK, tm), lambda i: (0, i)),
                   pl.BlockSpec((tm, 2 * LANES), lambda i: (i, 0))],
        out_shape=[jax.ShapeDtypeStruct((H, T, MLA_QK), BF16), jax.ShapeDtypeStruct((MLA_QK, T), BF16),
                   jax.ShapeDtypeStruct((T, 2 * LANES), BF16)],
        compiler_params=_cparams(("arbitrary",)),
        name="mla_in",
    )(*args)


def _flash_kernel(q_ref, qn_ref, kt_ref, va_ref, o_ref, sa_ref, sb_ref, pa_ref, pb_ref, acc_ref, m_ref, alpha_ref,
                  *, tk, unroll):
    H, tq, _ = q_ref.shape
    T = kt_ref.shape[1]
    n = T // tk
    q = q_ref[...].reshape(H * tq, MLA_QK)
    q_next = qn_ref[...].reshape(H * tq, MLA_QK)

    def k_tile(j):
        return kt_ref[:, pl.ds(pl.multiple_of(j * tk, tk), tk)]

    def v_tile(j):
        return va_ref[pl.ds(pl.multiple_of(j * tk, tk), tk), :]

    def lanes_rep(x, width):
        return jnp.concatenate([x] * (width // LANES), axis=1)

    def rescaled_acc():
        return lanes_rep(alpha_ref[...], 2 * LANES) * acc_ref[...]

    def step(j, s_cur, s_nxt, p_cur, p_prv):
        acc_ref[...] = rescaled_acc() + _dot(p_prv[...], v_tile(jnp.maximum(j - 1, 0)))
        last = j == n - 1
        s_nxt[...] = _dot(jnp.where(last, q_next, q), k_tile(jnp.where(last, 0, j + 1)))
        s = s_cur[...]
        m_old = m_ref[...]
        m_new = jnp.maximum(m_old, jnp.max(s, axis=1, keepdims=True))
        alpha_ref[...] = jnp.exp2(m_old - m_new)
        m_ref[...] = m_new
        p_cur[...] = jnp.exp2(s - lanes_rep(m_new, tk)).astype(BF16)

    def body(i, carry):
        for u in range(unroll // 2):
            step(unroll * i + 2 * u, sa_ref, sb_ref, pa_ref, pb_ref)
            step(unroll * i + 2 * u + 1, sb_ref, sa_ref, pb_ref, pa_ref)
        return carry

    acc_ref[...] = jnp.zeros_like(acc_ref)
    pb_ref[...] = jnp.zeros_like(pb_ref)
    alpha_ref[...] = jnp.ones_like(alpha_ref)
    m_ref[...] = jnp.full_like(m_ref, -jnp.inf)

    @pl.when(pl.program_id(0) == 0)
    def _():
        sa_ref[...] = _dot(q, k_tile(0))

    lax.fori_loop(0, n // unroll, body, 0)
    acc = rescaled_acc() + _dot(pb_ref[...], v_tile(n - 1))
    o_lat = (acc[:, :MLA_KV_RANK] / acc[:, MLA_KV_RANK:MLA_KV_RANK + 1]).astype(BF16)
    for hd in range(H):
        o_ref[:, hd * MLA_KV_RANK:(hd + 1) * MLA_KV_RANK] = o_lat[hd * tq:(hd + 1) * tq]


def _flash(q, kt, va, tq=64, tk=1024):
    H, T, _ = q.shape
    tq, tk = min(tq, T), min(tk, T)
    unroll = math.gcd(T // tk, 8)
    assert (T // tk) % unroll == 0 and unroll % 2 == 0 and tq % BF16_SUBLANES == 0
    rows = H * tq
    steps = T // tq
    return pl.pallas_call(
        functools.partial(_flash_kernel, tk=tk, unroll=unroll),
        grid=(steps,),
        scratch_shapes=[pltpu.VMEM((rows, tk), F32), pltpu.VMEM((rows, tk), F32), pltpu.VMEM((rows, tk), BF16),
                        pltpu.VMEM((rows, tk), BF16), pltpu.VMEM((rows, 2 * LANES), F32),
                        pltpu.VMEM((rows, LANES), F32), pltpu.VMEM((rows, LANES), F32)],
        in_specs=[
            pl.BlockSpec((H, tq, MLA_QK), lambda i: (0, i, 0)),
            pl.BlockSpec((H, tq, MLA_QK), lambda i: (0, jnp.minimum(i + 1, steps - 1), 0)),
            pl.BlockSpec((MLA_QK, T), lambda i: (0, 0)),
            pl.BlockSpec((T, 2 * LANES), lambda i: (0, 0)),
        ],
        out_specs=pl.BlockSpec((tq, H * MLA_KV_RANK), lambda i: (i, 0)),
        out_shape=jax.ShapeDtypeStruct((T, H * MLA_KV_RANK), BF16),
        compiler_params=_cparams(("arbitrary",)),
        name="mla_flash",
    )(q, q, kt, va)


def _mla_out_kernel(o_ref, wuv_ref, w_ref, h_ref, out_ref):
    v = [_dot(o_ref[:, hd * MLA_KV_RANK:(hd + 1) * MLA_KV_RANK], wuv_ref[hd]).astype(BF16)
         for hd in range(MLA_HEADS)]
    out_ref[...] = h_ref[...] + _dot(jnp.concatenate(v, axis=1), w_ref[...])


def _mla_out(o_lat, w_ukv, w_out, h, tm=512):
    T = h.shape[0]
    H = MLA_HEADS
    wuv = w_ukv.reshape(MLA_KV_RANK, H, MLA_NOPE + MLA_V)[:, :, MLA_NOPE:].transpose(1, 0, 2).astype(BF16)
    wb = w_out.astype(BF16)
    return pl.pallas_call(
        _mla_out_kernel,
        grid=(T // tm,),
        in_specs=[pl.BlockSpec((tm, H * MLA_KV_RANK), lambda i: (i, 0)), pl.BlockSpec(wuv.shape, lambda i: (0, 0, 0)),
                  pl.BlockSpec(wb.shape, lambda i: (0, 0)), pl.BlockSpec((tm, D_MODEL), lambda i: (i, 0))],
        out_specs=pl.BlockSpec((tm, D_MODEL), lambda i: (i, 0)),
        out_shape=jax.ShapeDtypeStruct((T, D_MODEL), F32),
        compiler_params=_cparams(("arbitrary",)),
        name="mla_out",
    )(o_lat, wuv, wb, h)


def _mla_mixer(h, positions, gain, w_in, q_norm, w_uq, kv_norm, w_ukv, w_out):
    q, kt, va = _mla_in(h, positions, gain, w_in, q_norm, w_uq, kv_norm, w_ukv)
    o_lat = _flash(q, kt, va)
    return _mla_out(o_lat, w_ukv, w_out, h)


def kernel(x, positions, mix_norm, ffn_norm, final_norm, gla_w_in, gla_w_gate_up_f, gla_b_gate_f, gla_w_gate_up_b,
           gla_b_gate_b, gla_head_norm, gla_w_out, mla_w_in, mla_q_norm, mla_w_uq, mla_kv_norm, mla_w_ukv,
           mla_w_out, moe_w_router, moe_w_gate, moe_w_up, moe_w_down):
    B, T, D = x.shape
    outs = []
    for b in range(B):
        h = x[b]
        h = _gla_mixer(h, mix_norm[0], gla_w_in[0], gla_w_gate_up_f[0], gla_b_gate_f[0], gla_w_gate_up_b[0],
                       gla_b_gate_b[0], gla_head_norm[0], gla_w_out[0])
        h = _ec_moe(h, ffn_norm[0], moe_w_router[0], moe_w_gate, moe_w_up, moe_w_down, 0)
        h = _mla_mixer(h, positions[b], mix_norm[1], mla_w_in[0], mla_q_norm[0], mla_w_uq[0], mla_kv_norm[0],
                       mla_w_ukv[0], mla_w_out[0])
        h = _ec_moe(h, ffn_norm[1], moe_w_router[1], moe_w_gate, moe_w_up, moe_w_down, 1, final_gain=final_norm)
        outs.append(h)
    return jnp.stack(outs)
```

```python
import functools
import math

import numpy as np
import jax
import jax.numpy as jnp
from jax import lax
from jax.experimental import pallas as pl
from jax.experimental.pallas import tpu as pltpu

F32 = jnp.float32
BF16 = jnp.bfloat16

D_MODEL = 1024
RMS_EPS = 1e-6

GLA_HEADS = 4
GLA_DK = 512
GLA_DV = 1024
GLA_HEAD_K = GLA_DK // GLA_HEADS
GLA_HEAD_V = GLA_DV // GLA_HEADS
GLA_GATE_RANK = 16
GLA_TAU = 16.0
GLA_CHUNK = 64
GLA_TILE = 256
GLA_LEVELS = 6

MLA_HEADS = 16
MLA_Q_RANK = 256
MLA_KV_RANK = 128
MLA_NOPE = 128
MLA_ROPE = 64
MLA_V = 128
MLA_QK = MLA_NOPE + MLA_ROPE
ROPE_BASE = 10000.0

N_EXPERTS = 16
EXPERT_FF = 2048
EC_CAPACITY_FACTOR = 2
MOE_TILE = 256
BF16_SUBLANES = 16
F32_SUBLANES = 8
LANES = 128

VMEM_LIMIT = 56 * 1024 * 1024


def _cparams(sem):
    return pltpu.CompilerParams(dimension_semantics=sem, vmem_limit_bytes=VMEM_LIMIT)


def _rms(x, g):
    return x * lax.rsqrt(jnp.mean(x * x, axis=-1, keepdims=True) + RMS_EPS) * g


def _split_bf16(x):
    hi = x.astype(BF16)
    lo = (x - hi.astype(F32)).astype(BF16)
    return hi, lo


def _dot(a, b):
    return jnp.dot(a, b, preferred_element_type=F32)


def _dot_nt(a, b):
    return lax.dot_general(a, b, (((1,), (1,)), ((), ())), preferred_element_type=F32)


def _dot_tn(a, b):
    return lax.dot_general(a, b, (((0,), (0,)), ((), ())), preferred_element_type=F32)


def _dot_split(a, b):
    ah, al = _split_bf16(a)
    bh, bl = _split_bf16(b)
    return _dot(ah, bh) + _dot(ah, bl) + _dot(al, bh)


def _gla_in_kernel(x_ref, g_ref, wqk_ref, wv_ref, wr_ref, wgd_ref, qk_ref, v_ref, r_ref, gd_ref):
    hn = _rms(x_ref[...], g_ref[...]).astype(BF16)
    qk_ref[...] = _dot(hn, wqk_ref[...])
    v_ref[...] = _dot(hn, wv_ref[...])
    r_ref[...] = _dot(hn, wr_ref[...])
    gd_ref[...] = _dot(hn, wgd_ref[...])


def _gla_in(x, gain, w_in, tm=512):
    T = x.shape[0]
    wqk = w_in[:, :2 * GLA_DK].astype(BF16)
    wv = w_in[:, 2 * GLA_DK:2 * GLA_DK + GLA_DV].astype(BF16)
    wr = w_in[:, 2 * GLA_DK + GLA_DV:2 * GLA_DK + 2 * GLA_DV].astype(BF16)
    wgd = w_in[:, 2 * GLA_DK + 2 * GLA_DV:].astype(BF16)
    ngd = 2 * GLA_GATE_RANK
    row = lambda n: pl.BlockSpec((tm, n), lambda i: (i, 0))
    full = lambda a: pl.BlockSpec(a.shape, lambda i: (0, 0))
    gain2 = gain.reshape(1, D_MODEL)
    return pl.pallas_call(
        _gla_in_kernel,
        grid=(T // tm,),
        in_specs=[row(D_MODEL), full(gain2), full(wqk), full(wv), full(wr), full(wgd)],
        out_specs=[row(2 * GLA_DK), row(GLA_DV), row(GLA_DV), row(ngd)],
        out_shape=[jax.ShapeDtypeStruct((T, 2 * GLA_DK), F32), jax.ShapeDtypeStruct((T, GLA_DV), F32),
                   jax.ShapeDtypeStruct((T, GLA_DV), F32), jax.ShapeDtypeStruct((T, ngd), F32)],
        compiler_params=_cparams(("arbitrary",)),
        name="gla_in",
    )(x, gain2, wqk, wv, wr, wgd)


def _gla_tables(reverse):
    n, c = GLA_TILE, GLA_CHUNK
    W = np.zeros((GLA_LEVELS + 3, n, n), np.float32)
    L = np.full((n, n), -1, np.int32)
    for t in range(n):
        c0 = (t // c) * c
        tt = t - c0
        for l in range(GLA_LEVELS):
            b = (c // 2) >> l
            p0 = c0 + (tt // (2 * b)) * 2 * b
            mid = p0 + b
            second = t >= mid
            if not reverse:
                if second:
                    W[l, t, mid:t + 1] = 1
                    L[t, p0:mid] = l
                else:
                    W[l, t, t + 1:mid] = 1
            else:
                if second:
                    W[l, t, mid:t] = 1
                else:
                    W[l, t, t:mid] = 1
                    L[t, mid:p0 + 2 * b] = l
        if not reverse:
            W[GLA_LEVELS, t, c0:t + 1] = 1
            W[GLA_LEVELS + 1, t, t + 1:c0 + c] = 1
            L[t, t] = GLA_LEVELS
        else:
            W[GLA_LEVELS, t, t:c0 + c] = 1
            W[GLA_LEVELS + 1, t, c0:t] = 1
        W[GLA_LEVELS + 2, t, c0:c0 + c] = 1
    return W.reshape(-1, n), L


def _gla_scan_kernel(qkf_ref, vf_ref, gdf_ref, qkb_ref, vb_ref, gdb_ref, wupf_ref, bf_ref, wupb_ref, bb_ref,
                     wf_ref, lf_ref, wb_ref, lb_ref, of_ref, ob_ref, s_ref):
    n, c, r = GLA_TILE, GLA_CHUNK, GLA_GATE_RANK

    @pl.when(pl.program_id(0) == 0)
    def _():
        s_ref[...] = jnp.zeros_like(s_ref)

    dirs = [(qkf_ref, vf_ref, gdf_ref[:, :r], wupf_ref, bf_ref, wf_ref, lf_ref, of_ref, False),
            (qkb_ref, vb_ref, gdb_ref[:, r:], wupb_ref, bb_ref, wb_ref, lb_ref, ob_ref, True)]

    factors = []
    for qk_ref, v_ref, gd, wup_ref, b_ref, w_ref, l_ref, o_ref, reverse in dirs:
        z = _dot_split(gd, wup_ref[...]) + b_ref[...]
        g = (jnp.minimum(z, 0.0) - jnp.log1p(jnp.exp(-jnp.abs(z)))) * (1.0 / GLA_TAU)
        ghi, glo = _split_bf16(g)
        w = w_ref[...]
        factors.append(jnp.exp(_dot(w, ghi) + _dot(w, glo)))

    chains = []
    for d, (qk_ref, v_ref, gd, wup_ref, b_ref, w_ref, l_ref, o_ref, reverse) in enumerate(dirs):
        lvl = l_ref[...]
        for h in range(GLA_HEADS):
            kcols = slice(h * GLA_HEAD_K, (h + 1) * GLA_HEAD_K)
            vcols = slice(h * GLA_HEAD_V, (h + 1) * GLA_HEAD_V)
            f = factors[d][:, kcols]
            q = qk_ref[:, kcols] * (GLA_HEAD_K ** -0.5)
            k = qk_ref[:, GLA_DK + h * GLA_HEAD_K:GLA_DK + (h + 1) * GLA_HEAD_K]
            vb = v_ref[:, vcols].astype(BF16)
            attn = jnp.zeros((n, n), F32)
            for l in range(GLA_LEVELS):
                fl = f[l * n:(l + 1) * n]
                p = _dot_nt((q * fl).astype(BF16), (k * fl).astype(BF16))
                attn = jnp.where(lvl == l, p, attn)
            if not reverse:
                p = _dot_nt(q.astype(BF16), k.astype(BF16))
                attn = jnp.where(lvl == GLA_LEVELS, p, attn)
            o_intra = _dot(attn.astype(BF16), vb)
            qh = (q * f[GLA_LEVELS * n:(GLA_LEVELS + 1) * n]).astype(BF16)
            kh = (k * f[(GLA_LEVELS + 1) * n:(GLA_LEVELS + 2) * n]).astype(BF16)
            ftot = f[(GLA_LEVELS + 2) * n:(GLA_LEVELS + 3) * n]
            chains.append(dict(d=d, h=h, vcols=vcols, o_ref=o_ref, reverse=reverse, vb=vb, o_intra=o_intra, qh=qh,
                               kh=kh, ftot=ftot, st=s_ref[d, h]))

    nchunks = n // c
    for step in range(nchunks):
        for ch in chains:
            j = nchunks - 1 - step if ch["reverse"] else step
            rows = slice(j * c, (j + 1) * c)
            st = ch["st"]
            ch["o_ref"][rows, ch["vcols"]] = ch["o_intra"][rows] + _dot_nt(ch["qh"][rows], st.astype(BF16))
            ch["st"] = st * ch["ftot"][j * c:j * c + 1, :] + _dot_tn(ch["vb"][rows], ch["kh"][rows])
    for ch in chains:
        s_ref[ch["d"], ch["h"]] = ch["st"]


def _gla_scan(qk, v, gd, w_up_f, b_f, w_up_b, b_b):
    T = qk.shape[0]
    n = GLA_TILE
    nt = T // n
    tables = []
    for reverse in (False, True):
        W, L = _gla_tables(reverse)
        tables += [jnp.asarray(W, BF16), jnp.asarray(L)]
    fwd = lambda width: pl.BlockSpec((n, width), lambda i: (i, 0))
    bwd = lambda width: pl.BlockSpec((n, width), lambda i: (nt - 1 - i, 0))
    full = lambda a: pl.BlockSpec(a.shape, lambda i: (0, 0))
    consts = [w_up_f, b_f.reshape(1, GLA_DK), w_up_b, b_b.reshape(1, GLA_DK)] + tables
    ngd = 2 * GLA_GATE_RANK
    out = jax.ShapeDtypeStruct((T, GLA_DV), F32)
    return pl.pallas_call(
        _gla_scan_kernel,
        grid=(nt,),
        in_specs=[fwd(2 * GLA_DK), fwd(GLA_DV), fwd(ngd), bwd(2 * GLA_DK), bwd(GLA_DV), bwd(ngd)]
                 + [full(a) for a in consts],
        out_specs=[fwd(GLA_DV), bwd(GLA_DV)],
        out_shape=[out, out],
        scratch_shapes=[pltpu.VMEM((2, GLA_HEADS, GLA_HEAD_V, GLA_HEAD_K), F32)],
        compiler_params=_cparams(("arbitrary",)),
        name="gla_scan",
    )(qk, v, gd, qk, v, gd, *consts)


def _gla_out_kernel(of_ref, ob_ref, r_ref, x_ref, hn_ref, w_ref, o_ref):
    acc = x_ref[...]
    for h in range(GLA_HEADS):
        cols = slice(h * GLA_HEAD_V, (h + 1) * GLA_HEAD_V)
        o = _rms(of_ref[:, cols] + ob_ref[:, cols], hn_ref[...])
        r = r_ref[:, cols]
        gated = o * (r * (1.0 / (1.0 + jnp.exp(-r))))
        acc = acc + _dot(gated.astype(BF16), w_ref[cols, :])
    o_ref[...] = acc


def _gla_out(of, ob, r, x, head_norm, w_out, tm=512):
    T = x.shape[0]
    w = w_out.astype(BF16)
    hn = head_norm.reshape(1, GLA_HEAD_V)
    row = lambda n: pl.BlockSpec((tm, n), lambda i: (i, 0))
    full = lambda a: pl.BlockSpec(a.shape, lambda i: (0, 0))
    return pl.pallas_call(
        _gla_out_kernel,
        grid=(T // tm,),
        in_specs=[row(GLA_DV), row(GLA_DV), row(GLA_DV), row(D_MODEL), full(hn), full(w)],
        out_specs=row(D_MODEL),
        out_shape=jax.ShapeDtypeStruct((T, D_MODEL), F32),
        compiler_params=_cparams(("arbitrary",)),
        name="gla_out",
    )(of, ob, r, x, hn, w)


def _gla_mixer(x, gain, w_in, w_up_f, b_f, w_up_b, b_b, head_norm, w_out):
    qk, v, r, gd = _gla_in(x, gain, w_in)
    of, ob = _gla_scan(qk, v, gd, w_up_f, b_f, w_up_b, b_b)
    return _gla_out(of, ob, r, x, head_norm, w_out)


def _router_kernel(h_ref, g_ref, w_ref, aff_ref, hn_ref):
    hn = _rms(h_ref[...], g_ref[...])
    hn_ref[...] = hn.astype(BF16)
    logits = _dot_split(hn, w_ref[...])
    e = jnp.exp(logits - jnp.max(logits, axis=-1, keepdims=True))
    aff_ref[...] = e / jnp.sum(e, axis=-1, keepdims=True)


def _router(h, gain, w_router, tm=512):
    T = h.shape[0]
    gain2 = gain.reshape(1, D_MODEL)
    row = lambda n: pl.BlockSpec((tm, n), lambda i: (i, 0))
    full = lambda a: pl.BlockSpec(a.shape, lambda i: (0, 0))
    return pl.pallas_call(
        _router_kernel,
        grid=(T // tm,),
        in_specs=[row(D_MODEL), full(gain2), full(w_router)],
        out_specs=[row(N_EXPERTS), row(D_MODEL)],
        out_shape=[jax.ShapeDtypeStruct((T, N_EXPERTS), F32), jax.ShapeDtypeStruct((T, D_MODEL), BF16)],
        compiler_params=_cparams(("arbitrary",)),
        name="moe_router",
    )(h, gain2, w_router)


def _select_kernel(aff_ref, tri_ref, posm_ref, before_ref, *, cap):
    T = aff_ref.shape[1]
    bits = pltpu.bitcast(aff_ref[...], jnp.int32)

    def search(it, thr):
        cand = thr | jnp.left_shift(jnp.int32(1), 30 - it)
        cnt = jnp.sum(jnp.where(bits >= cand, 1.0, 0.0), axis=1, keepdims=True)
        return jnp.where(cnt >= cap, cand, thr)

    thr = lax.fori_loop(0, 31, search, jnp.zeros((N_EXPERTS, 1), jnp.int32))
    n_gt = jnp.sum(jnp.where(bits > thr, 1.0, 0.0), axis=1, keepdims=True)
    need = cap - n_gt
    tri = tri_ref[...]

    def scan(j, carry):
        c_eq, c_sel = carry
        cols = pl.ds(pl.multiple_of(j * LANES, LANES), LANES)
        blk = pltpu.bitcast(aff_ref[:, cols], jnp.int32)
        eq = jnp.where(blk == thr, 1.0, 0.0)
        rank = _dot(eq.astype(BF16), tri) + c_eq - eq
        sel = jnp.where((blk > thr) | ((eq > 0.0) & (rank < need)), 1.0, 0.0)
        before = _dot(sel.astype(BF16), tri) + c_sel - sel
        before_ref[:, cols] = before.astype(jnp.int32)
        posm_ref[:, cols] = jnp.where(sel > 0.0, before, -1.0).astype(jnp.int32)
        return (c_eq + jnp.sum(eq, axis=1, keepdims=True), c_sel + jnp.sum(sel, axis=1, keepdims=True))

    zero = jnp.zeros((N_EXPERTS, 1), F32)
    lax.fori_loop(0, T // LANES, scan, (zero, zero))


def _select(aff_t, cap):
    T = aff_t.shape[1]
    tri = jnp.asarray(np.triu(np.ones((LANES, LANES), np.float32)), BF16)
    full = lambda a: pl.BlockSpec(a.shape, lambda: (0,) * a.ndim)
    out = jax.ShapeDtypeStruct((N_EXPERTS, T), jnp.int32)
    return pl.pallas_call(
        functools.partial(_select_kernel, cap=cap),
        in_specs=[full(aff_t), full(tri)],
        out_specs=[pl.BlockSpec((N_EXPERTS, T), lambda: (0, 0))] * 2,
        out_shape=[out, out],
        compiler_params=pltpu.CompilerParams(vmem_limit_bytes=VMEM_LIMIT),
        name="moe_select",
    )(aff_t, tri)


def _dispatch_window():
    return MOE_TILE + F32_SUBLANES


def _dispatch_kernel(start_ref, hn_ref, posm_ref, x_ref, acc_ref, *, cap, nt, sub):
    e, t = pl.program_id(0), pl.program_id(1)
    win = _dispatch_window()

    @pl.when(t == 0)
    def _():
        acc_ref[...] = jnp.zeros_like(acc_ref)

    for s in range(sub):
        tok = slice(s * MOE_TILE, (s + 1) * MOE_TILE)
        base = pl.multiple_of((start_ref[e * nt + t * sub + s] // F32_SUBLANES) * F32_SUBLANES, F32_SUBLANES)
        slot = base + lax.broadcasted_iota(jnp.int32, (win, MOE_TILE), 0)
        onehot = jnp.where(posm_ref[0, :, tok] == slot, 1.0, 0.0).astype(BF16)
        acc_ref[pl.ds(base, win), :] += _dot(onehot, hn_ref[tok, :])

    @pl.when(t == pl.num_programs(1) - 1)
    def _():
        x_ref[0] = acc_ref[:cap, :].astype(BF16)


def _dispatch(hn, posm, start, cap, sub=8):
    T = hn.shape[0]
    nt = T // MOE_TILE
    sub = math.gcd(sub, nt)
    blk = sub * MOE_TILE
    posm3 = posm.reshape(N_EXPERTS, 1, T)
    grid_spec = pltpu.PrefetchScalarGridSpec(
        num_scalar_prefetch=1,
        grid=(N_EXPERTS, nt // sub),
        in_specs=[
            pl.BlockSpec((blk, D_MODEL), lambda e, t, s: (t, 0)),
            pl.BlockSpec((1, 1, blk), lambda e, t, s: (e, 0, t)),
        ],
        out_specs=pl.BlockSpec((1, cap, D_MODEL), lambda e, t, s: (e, 0, 0)),
        scratch_shapes=[pltpu.VMEM((cap + _dispatch_window(), D_MODEL), F32)],
    )
    return pl.pallas_call(
        functools.partial(_dispatch_kernel, cap=cap, nt=nt, sub=sub),
        grid_spec=grid_spec,
        out_shape=jax.ShapeDtypeStruct((N_EXPERTS, cap, D_MODEL), BF16),
        compiler_params=_cparams(("arbitrary", "arbitrary")),
        name="moe_dispatch",
    )(start.reshape(-1), hn, posm3)


def _ffn_kernel(x_ref, wg_ref, wu_ref, wd_ref, y_ref, acc_ref):
    f = pl.program_id(1)

    @pl.when(f == 0)
    def _():
        acc_ref[...] = jnp.zeros_like(acc_ref)

    x = x_ref[0]
    a = _dot(x, wg_ref[0, 0].astype(BF16))
    u = _dot(x, wu_ref[0, 0].astype(BF16))
    mid = (a * (1.0 / (1.0 + jnp.exp(-a))) * u).astype(BF16)
    acc_ref[...] += _dot(mid, wd_ref[0, 0].astype(BF16))

    @pl.when(f == pl.num_programs(1) - 1)
    def _():
        y_ref[0] = acc_ref[...].astype(BF16)


def _ffn(x, w_gate, w_up, w_down, layer, tf=512):
    cap = x.shape[1]
    return pl.pallas_call(
        _ffn_kernel,
        grid=(N_EXPERTS, EXPERT_FF // tf),
        in_specs=[
            pl.BlockSpec((1, cap, D_MODEL), lambda e, f: (e, 0, 0)),
            pl.BlockSpec((1, 1, D_MODEL, tf), lambda e, f: (layer, e, 0, f)),
            pl.BlockSpec((1, 1, D_MODEL, tf), lambda e, f: (layer, e, 0, f)),
            pl.BlockSpec((1, 1, tf, D_MODEL), lambda e, f: (layer, e, f, 0)),
        ],
        out_specs=pl.BlockSpec((1, cap, D_MODEL), lambda e, f: (e, 0, 0)),
        out_shape=jax.ShapeDtypeStruct((N_EXPERTS, cap, D_MODEL), BF16),
        scratch_shapes=[pltpu.VMEM((cap, D_MODEL), F32)],
        compiler_params=_cparams(("arbitrary", "arbitrary")),
        name="moe_ffn",
    )(x, w_gate, w_up, w_down)


def _combine_window():
    return MOE_TILE + BF16_SUBLANES


def _combine_kernel(start_ref, h_ref, aff_ref, posm_ref, g_ref, y_hbm, o_ref, buf_ref, sem_ref, acc_ref, *,
                    cap, nt, final_norm):
    t = pl.program_id(0)
    win = _combine_window()

    def base_of(tile, e):
        b = (start_ref[e * (nt + 1) + tile] // BF16_SUBLANES) * BF16_SUBLANES
        return pl.multiple_of(jnp.minimum(b, cap - win), BF16_SUBLANES)

    def window_copy(tile, e, slot):
        return pltpu.make_async_copy(y_hbm.at[e, pl.ds(base_of(tile, e), win), :], buf_ref.at[slot, e],
                                     sem_ref.at[slot, e])

    def fetch(tile, slot):
        for e in range(N_EXPERTS):
            window_copy(tile, e, slot).start()

    slot = t % 2

    @pl.when(t == 0)
    def _():
        fetch(0, 0)

    @pl.when(t + 1 < nt)
    def _():
        fetch(t + 1, 1 - slot)

    def needs_tail(e):
        return start_ref[e * (nt + 1) + t + 1] > base_of(t, e) + MOE_TILE

    for e in range(N_EXPERTS):
        window_copy(t, e, slot).wait()
    acc = h_ref[...]
    for e in range(N_EXPERTS):
        hit = posm_ref[:, e:e + 1] == base_of(t, e) + lax.broadcasted_iota(jnp.int32, (MOE_TILE, MOE_TILE), 1)
        y = _dot(jnp.where(hit, 1.0, 0.0).astype(BF16), buf_ref[slot, e, :MOE_TILE, :])
        acc = acc + aff_ref[:, e:e + 1] * y
    acc_ref[...] = acc

    any_tail = needs_tail(0)
    for e in range(1, N_EXPERTS):
        any_tail = any_tail | needs_tail(e)

    @pl.when(any_tail)
    def _():
        for e in range(N_EXPERTS):
            @pl.when(needs_tail(e))
            def _():
                tail = base_of(t, e) + MOE_TILE + lax.broadcasted_iota(jnp.int32, (MOE_TILE, BF16_SUBLANES), 1)
                hit = posm_ref[:, e:e + 1] == tail
                yt = _dot(jnp.where(hit, 1.0, 0.0).astype(BF16), buf_ref[slot, e, MOE_TILE:, :])
                acc_ref[...] += aff_ref[:, e:e + 1] * yt

    acc = acc_ref[...]
    if final_norm:
        acc = _rms(acc, g_ref[...])
    o_ref[...] = acc


def _combine(h, aff, posm_t, start, y, cap, final_gain):
    T = h.shape[0]
    nt = T // MOE_TILE
    win = _combine_window()
    assert cap >= win and (cap - win) % BF16_SUBLANES == 0
    final_norm = final_gain is not None
    gain = (final_gain if final_norm else jnp.ones((D_MODEL,), F32)).reshape(1, D_MODEL)
    grid_spec = pltpu.PrefetchScalarGridSpec(
        num_scalar_prefetch=1,
        grid=(nt,),
        in_specs=[
            pl.BlockSpec((MOE_TILE, D_MODEL), lambda t, s: (t, 0)),
            pl.BlockSpec((MOE_TILE, N_EXPERTS), lambda t, s: (t, 0)),
            pl.BlockSpec((MOE_TILE, N_EXPERTS), lambda t, s: (t, 0)),
            pl.BlockSpec((1, D_MODEL), lambda t, s: (0, 0)),
            pl.BlockSpec(memory_space=pl.ANY),
        ],
        out_specs=pl.BlockSpec((MOE_TILE, D_MODEL), lambda t, s: (t, 0)),
        scratch_shapes=[pltpu.VMEM((2, N_EXPERTS, win, D_MODEL), BF16), pltpu.SemaphoreType.DMA((2, N_EXPERTS)),
                        pltpu.VMEM((MOE_TILE, D_MODEL), F32)],
    )
    return pl.pallas_call(
        functools.partial(_combine_kernel, cap=cap, nt=nt, final_norm=final_norm),
        grid_spec=grid_spec,
        out_shape=jax.ShapeDtypeStruct((T, D_MODEL), F32),
        compiler_params=_cparams(("arbitrary",)),
        name="moe_combine",
    )(start.reshape(-1), h, aff, posm_t, gain, y)


def _ec_moe(h, gain, w_router, w_gate, w_up, w_down, layer, final_gain=None):
    T = h.shape[0]
    cap = max(1, EC_CAPACITY_FACTOR * T // N_EXPERTS)
    aff, hn = _router(h, gain, w_router)
    posm, before = _select(aff.T, cap)
    start = before[:, ::MOE_TILE]
    x = _dispatch(hn, posm, start, cap)
    y = _ffn(x, w_gate, w_up, w_down, layer)
    start_end = jnp.concatenate([start, jnp.full((N_EXPERTS, 1), cap, jnp.int32)], axis=1)
    return _combine(h, aff, posm.T, start_end, y, cap, final_gain)


def _rms_cols(x, g):
    return x * lax.rsqrt(jnp.mean(x * x, axis=0, keepdims=True) + RMS_EPS) * g


MLA_QCOLS = MLA_NOPE + 2 * LANES


def _mla_in_kernel(h_ref, posc_ref, posr_ref, g_ref, win_ref, wintkv_ref, qn_ref, wq_ref, wukt_ref, kvn_ref,
                   kvnc_ref, freqr_ref, sign_ref, freqc_ref, q_out, kt_out, va_out):
    half = MLA_ROPE // 2
    hn = _rms(h_ref[...], g_ref[...]).astype(BF16)
    c = _dot(hn, win_ref[...])
    ckv_t = _dot_nt(wintkv_ref[...], hn)

    cq = _rms(c[:, :MLA_Q_RANK], qn_ref[...]).astype(BF16)
    qa = _dot(cq, wq_ref[...])
    ang = posc_ref[...].astype(F32) * freqr_ref[...]
    cos, sin_signed = jnp.cos(ang), jnp.sin(ang) * sign_ref[...]
    qscale = MLA_QK ** -0.5 * math.log2(math.e)
    for hd in range(MLA_HEADS):
        c0 = hd * MLA_QCOLS
        q_lat = _dot(qa[:, c0:c0 + MLA_NOPE].astype(BF16), wukt_ref[hd])
        rot = qa[:, c0 + MLA_NOPE:c0 + MLA_NOPE + LANES] * cos + qa[:, c0 + MLA_NOPE + LANES:c0 + MLA_QCOLS] * sin_signed
        q_out[hd, :, :MLA_NOPE] = (q_lat * qscale).astype(BF16)
        q_out[hd, :, MLA_NOPE:] = (rot[:, :MLA_ROPE] * qscale).astype(BF16)

    kt_out[:MLA_KV_RANK, :] = _rms_cols(ckv_t[:MLA_KV_RANK], kvnc_ref[...]).astype(BF16)
    ang_t = freqc_ref[...] * posr_ref[...].astype(F32)
    cos_t, sin_t = jnp.cos(ang_t), jnp.sin(ang_t)
    k1, k2 = ckv_t[MLA_KV_RANK:MLA_KV_RANK + half], ckv_t[MLA_KV_RANK + half:]
    kt_out[MLA_KV_RANK:MLA_KV_RANK + half, :] = (k1 * cos_t - k2 * sin_t).astype(BF16)
    kt_out[MLA_KV_RANK + half:, :] = (k1 * sin_t + k2 * cos_t).astype(BF16)

    ckv = _rms(c[:, MLA_Q_RANK:MLA_Q_RANK + MLA_KV_RANK], kvn_ref[...])
    va_out[:, :MLA_KV_RANK] = ckv.astype(BF16)
    lane = lax.broadcasted_iota(jnp.int32, (ckv.shape[0], LANES), 1)
    va_out[:, MLA_KV_RANK:] = jnp.where(lane == 0, 1.0, 0.0).astype(BF16)


def _mla_in(h, positions, gain, w_in, q_norm, w_uq, kv_norm, w_ukv, tm=256):
    T = h.shape[0]
    H, half = MLA_HEADS, MLA_ROPE // 2
    wq = w_uq.reshape(MLA_Q_RANK, H, MLA_QK)
    x1, x2 = wq[:, :, MLA_NOPE:MLA_NOPE + half], wq[:, :, MLA_NOPE + half:]
    pad = jnp.zeros((MLA_Q_RANK, H, LANES - MLA_ROPE), F32)
    wq_wide = jnp.concatenate([wq[:, :, :MLA_NOPE], x1, x2, pad, x2, x1, pad], axis=2)
    wq_wide = wq_wide.reshape(MLA_Q_RANK, H * MLA_QCOLS).astype(BF16)
    wuk_t = w_ukv.reshape(MLA_KV_RANK, H, MLA_NOPE + MLA_V)[:, :, :MLA_NOPE].transpose(1, 2, 0).astype(BF16)
    inv_freq = ROPE_BASE ** (-jnp.arange(half, dtype=F32) / half)
    zeros = jnp.zeros((LANES - MLA_ROPE,), F32)
    freq_row = jnp.concatenate([inv_freq, inv_freq, zeros]).reshape(1, LANES)
    sign_row = jnp.concatenate([-jnp.ones((half,), F32), jnp.ones((half,), F32), zeros]).reshape(1, LANES)
    w_in_b = w_in.astype(BF16)
    args = [h, positions.reshape(T, 1), positions.reshape(1, T), gain.reshape(1, -1), w_in_b,
            w_in_b[:, MLA_Q_RANK:].T, q_norm.reshape(1, -1), wq_wide, wuk_t, kv_norm.reshape(1, -1),
            kv_norm.reshape(-1, 1), freq_row, sign_row, inv_freq.reshape(half, 1)]
    full = lambda a: pl.BlockSpec(a.shape, lambda i: (0,) * a.ndim)
    in_specs = [pl.BlockSpec((tm, D_MODEL), lambda i: (i, 0)), pl.BlockSpec((tm, 1), lambda i: (i, 0)),
                pl.BlockSpec((1, tm), lambda i: (0, i))] + [full(a) for a in args[3:]]
    return pl.pallas_call(
        _mla_in_kernel,
        grid=(T // tm,),
        in_specs=in_specs,
        out_specs=[pl.BlockSpec((H, tm, MLA_QK), lambda i: (0, i, 0)),
                   pl.BlockSpec((MLA_QK, tm), lambda i: (0, i)),
                   pl.BlockSpec((tm, 2 * LANES), lambda i: (i, 0))],
        out_shape=[jax.ShapeDtypeStruct((H, T, MLA_QK), BF16), jax.ShapeDtypeStruct((MLA_QK, T), BF16),
                   jax.ShapeDtypeStruct((T, 2 * LANES), BF16)],
        compiler_params=_cparams(("arbitrary",)),
        name="mla_in",
    )(*args)


def _flash_kernel(q_ref, qn_ref, kt_ref, va_ref, o_ref, sa_ref, sb_ref, pa_ref, pb_ref, acc_ref, m_ref, alpha_ref,
                  *, tk, nblk):
    H, tq, _ = q_ref.shape
    T = kt_ref.shape[1]
    n = T // tk
    q = q_ref[...].reshape(H * tq, MLA_QK)
    q_next = qn_ref[...].reshape(H * tq, MLA_QK)

    def k_tile(j):
        return kt_ref[:, j * tk:(j + 1) * tk]

    def v_tile(j):
        return va_ref[j * tk:(j + 1) * tk, :]

    def lanes_rep(x, width):
        return jnp.concatenate([x] * (width // LANES), axis=1)

    blk = H * tq // nblk
    row_blocks = [slice(rb * blk, (rb + 1) * blk) for rb in range(nblk)]

    def add_pv(rows, p_ref, j):
        acc_ref[rows, :] = (lanes_rep(alpha_ref[rows, :], 2 * LANES) * acc_ref[rows, :]
                            + _dot(p_ref[rows, :], v_tile(j)))

    def step(j, s_cur, s_nxt, p_cur, p_prv):
        q_sel, k_nxt = (q, k_tile(j + 1)) if j + 1 < n else (q_next, k_tile(0))
        for rows in row_blocks:
            if j > 0:
                add_pv(rows, p_prv, j - 1)
            s_nxt[rows, :] = _dot(q_sel[rows], k_nxt)
            s = s_cur[rows, :]
            m_old = m_ref[rows, :]
            m_new = jnp.maximum(m_old, jnp.max(s, axis=1, keepdims=True))
            alpha_ref[rows, :] = jnp.exp2(m_old - m_new)
            m_ref[rows, :] = m_new
            p_cur[rows, :] = jnp.exp2(s - lanes_rep(m_new, tk)).astype(BF16)

    acc_ref[...] = jnp.zeros_like(acc_ref)
    m_ref[...] = jnp.full_like(m_ref, -jnp.inf)

    @pl.when(pl.program_id(0) == 0)
    def _():
        sa_ref[...] = _dot(q, k_tile(0))

    for j in range(n):
        if j % 2 == 0:
            step(j, sa_ref, sb_ref, pa_ref, pb_ref)
        else:
            step(j, sb_ref, sa_ref, pb_ref, pa_ref)
    p_last = pb_ref if n % 2 == 0 else pa_ref
    for rows in row_blocks:
        add_pv(rows, p_last, n - 1)
    acc = acc_ref[...]
    o_lat = (acc[:, :MLA_KV_RANK] / acc[:, MLA_KV_RANK:MLA_KV_RANK + 1]).astype(BF16)
    for hd in range(H):
        o_ref[:, hd * MLA_KV_RANK:(hd + 1) * MLA_KV_RANK] = o_lat[hd * tq:(hd + 1) * tq]


def _flash(q, kt, va, tq=64, tk=1024):
    H, T, _ = q.shape
    tq, tk = min(tq, T), min(tk, T)
    assert T % (2 * tk) == 0 and tq % BF16_SUBLANES == 0
    rows = H * tq
    steps = T // tq
    return pl.pallas_call(
        functools.partial(_flash_kernel, tk=tk, nblk=4),
        grid=(steps,),
        scratch_shapes=[pltpu.VMEM((rows, tk), F32), pltpu.VMEM((rows, tk), F32), pltpu.VMEM((rows, tk), BF16),
                        pltpu.VMEM((rows, tk), BF16), pltpu.VMEM((rows, 2 * LANES), F32),
                        pltpu.VMEM((rows, LANES), F32), pltpu.VMEM((rows, LANES), F32)],
        in_specs=[
            pl.BlockSpec((H, tq, MLA_QK), lambda i: (0, i, 0)),
            pl.BlockSpec((H, tq, MLA_QK), lambda i: (0, jnp.minimum(i + 1, steps - 1), 0)),
            pl.BlockSpec((MLA_QK, T), lambda i: (0, 0)),
            pl.BlockSpec((T, 2 * LANES), lambda i: (0, 0)),
        ],
        out_specs=pl.BlockSpec((tq, H * MLA_KV_RANK), lambda i: (i, 0)),
        out_shape=jax.ShapeDtypeStruct((T, H * MLA_KV_RANK), BF16),
        compiler_params=_cparams(("arbitrary",)),
        name="mla_flash",
    )(q, q, kt, va)


def _mla_out_kernel(o_ref, wuv_ref, w_ref, h_ref, out_ref):
    v = [_dot(o_ref[:, hd * MLA_KV_RANK:(hd + 1) * MLA_KV_RANK], wuv_ref[hd]).astype(BF16)
         for hd in range(MLA_HEADS)]
    out_ref[...] = h_ref[...] + _dot(jnp.concatenate(v, axis=1), w_ref[...])


def _mla_out(o_lat, w_ukv, w_out, h, tm=512):
    T = h.shape[0]
    H = MLA_HEADS
    wuv = w_ukv.reshape(MLA_KV_RANK, H, MLA_NOPE + MLA_V)[:, :, MLA_NOPE:].transpose(1, 0, 2).astype(BF16)
    wb = w_out.astype(BF16)
    return pl.pallas_call(
        _mla_out_kernel,
        grid=(T // tm,),
        in_specs=[pl.BlockSpec((tm, H * MLA_KV_RANK), lambda i: (i, 0)), pl.BlockSpec(wuv.shape, lambda i: (0, 0, 0)),
                  pl.BlockSpec(wb.shape, lambda i: (0, 0)), pl.BlockSpec((tm, D_MODEL), lambda i: (i, 0))],
        out_specs=pl.BlockSpec((tm, D_MODEL), lambda i: (i, 0)),
        out_shape=jax.ShapeDtypeStruct((T, D_MODEL), F32),
        compiler_params=_cparams(("arbitrary",)),
        name="mla_out",
    )(o_lat, wuv, wb, h)


def _mla_mixer(h, positions, gain, w_in, q_norm, w_uq, kv_norm, w_ukv, w_out):
    q, kt, va = _mla_in(h, positions, gain, w_in, q_norm, w_uq, kv_norm, w_ukv)
    o_lat = _flash(q, kt, va)
    return _mla_out(o_lat, w_ukv, w_out, h)


def kernel(x, positions, mix_norm, ffn_norm, final_norm, gla_w_in, gla_w_gate_up_f, gla_b_gate_f, gla_w_gate_up_b,
           gla_b_gate_b, gla_head_norm, gla_w_out, mla_w_in, mla_q_norm, mla_w_uq, mla_kv_norm, mla_w_ukv,
           mla_w_out, moe_w_router, moe_w_gate, moe_w_up, moe_w_down):
    B, T, D = x.shape
    outs = []
    for b in range(B):
        h = x[b]
        h = _gla_mixer(h, mix_norm[0], gla_w_in[0], gla_w_gate_up_f[0], gla_b_gate_f[0], gla_w_gate_up_b[0],
                       gla_b_gate_b[0], gla_head_norm[0], gla_w_out[0])
        h = _ec_moe(h, ffn_norm[0], moe_w_router[0], moe_w_gate, moe_w_up, moe_w_down, 0)
        h = _mla_mixer(h, positions[b], mix_norm[1], mla_w_in[0], mla_q_norm[0], mla_w_uq[0], mla_kv_norm[0],
                       mla_w_ukv[0], mla_w_out[0])
        h = _ec_moe(h, ffn_norm[1], moe_w_router[1], moe_w_gate, moe_w_up, moe_w_down, 1, final_gain=final_norm)
        outs.append(h)
    return jnp.stack(outs)
```

```python
import functools
import math

import numpy as np
import jax
import jax.numpy as jnp
from jax import lax
from jax.experimental import pallas as pl
from jax.experimental.pallas import tpu as pltpu

F32 = jnp.float32
BF16 = jnp.bfloat16

D_MODEL = 1024
RMS_EPS = 1e-6

GLA_HEADS = 4
GLA_DK = 512
GLA_DV = 1024
GLA_HEAD_K = GLA_DK // GLA_HEADS
GLA_HEAD_V = GLA_DV // GLA_HEADS
GLA_GATE_RANK = 16
GLA_TAU = 16.0
GLA_CHUNK = 64
GLA_TILE = 256
GLA_LEVELS = 6

MLA_HEADS = 16
MLA_Q_RANK = 256
MLA_KV_RANK = 128
MLA_NOPE = 128
MLA_ROPE = 64
MLA_V = 128
MLA_QK = MLA_NOPE + MLA_ROPE
ROPE_BASE = 10000.0

N_EXPERTS = 16
EXPERT_FF = 2048
EC_CAPACITY_FACTOR = 2
MOE_TILE = 256
COMBINE_HEAD = 64
BF16_SUBLANES = 16
F32_SUBLANES = 8
LANES = 128

VMEM_LIMIT = 56 * 1024 * 1024


def _cparams(sem):
    return pltpu.CompilerParams(dimension_semantics=sem, vmem_limit_bytes=VMEM_LIMIT)


def _rms(x, g):
    return x * lax.rsqrt(jnp.mean(x * x, axis=-1, keepdims=True) + RMS_EPS) * g


def _split_bf16(x):
    hi = x.astype(BF16)
    lo = (x - hi.astype(F32)).astype(BF16)
    return hi, lo


def _dot(a, b):
    return jnp.dot(a, b, preferred_element_type=F32)


def _dot_nt(a, b):
    return lax.dot_general(a, b, (((1,), (1,)), ((), ())), preferred_element_type=F32)


def _dot_tn(a, b):
    return lax.dot_general(a, b, (((0,), (0,)), ((), ())), preferred_element_type=F32)


def _dot_split(a, b):
    ah, al = _split_bf16(a)
    bh, bl = _split_bf16(b)
    return _dot(ah, bh) + _dot(ah, bl) + _dot(al, bh)


def _gla_in_kernel(x_ref, g_ref, wqk_ref, wv_ref, wr_ref, wgd_ref, qk_ref, v_ref, r_ref, gd_ref):
    hn = _rms(x_ref[...], g_ref[...]).astype(BF16)
    qk_ref[...] = _dot(hn, wqk_ref[...])
    v_ref[...] = _dot(hn, wv_ref[...])
    r_ref[...] = _dot(hn, wr_ref[...])
    gd_ref[...] = _dot(hn, wgd_ref[...])


def _gla_in(x, gain, w_in, tm=512):
    T = x.shape[0]
    wqk = w_in[:, :2 * GLA_DK].astype(BF16)
    wv = w_in[:, 2 * GLA_DK:2 * GLA_DK + GLA_DV].astype(BF16)
    wr = w_in[:, 2 * GLA_DK + GLA_DV:2 * GLA_DK + 2 * GLA_DV].astype(BF16)
    wgd = w_in[:, 2 * GLA_DK + 2 * GLA_DV:].astype(BF16)
    ngd = 2 * GLA_GATE_RANK
    row = lambda n: pl.BlockSpec((tm, n), lambda i: (i, 0))
    full = lambda a: pl.BlockSpec(a.shape, lambda i: (0, 0))
    gain2 = gain.reshape(1, D_MODEL)
    return pl.pallas_call(
        _gla_in_kernel,
        grid=(T // tm,),
        in_specs=[row(D_MODEL), full(gain2), full(wqk), full(wv), full(wr), full(wgd)],
        out_specs=[row(2 * GLA_DK), row(GLA_DV), row(GLA_DV), row(ngd)],
        out_shape=[jax.ShapeDtypeStruct((T, 2 * GLA_DK), F32), jax.ShapeDtypeStruct((T, GLA_DV), F32),
                   jax.ShapeDtypeStruct((T, GLA_DV), F32), jax.ShapeDtypeStruct((T, ngd), F32)],
        compiler_params=_cparams(("arbitrary",)),
        name="gla_in",
    )(x, gain2, wqk, wv, wr, wgd)


def _gla_tables(reverse):
    n, c = GLA_TILE, GLA_CHUNK
    W = np.zeros((GLA_LEVELS + 3, n, n), np.float32)
    L = np.full((n, n), -1, np.int32)
    for t in range(n):
        c0 = (t // c) * c
        tt = t - c0
        for l in range(GLA_LEVELS):
            b = (c // 2) >> l
            p0 = c0 + (tt // (2 * b)) * 2 * b
            mid = p0 + b
            second = t >= mid
            if not reverse:
                if second:
                    W[l, t, mid:t + 1] = 1
                    L[t, p0:mid] = l
                else:
                    W[l, t, t + 1:mid] = 1
            else:
                if second:
                    W[l, t, mid:t] = 1
                else:
                    W[l, t, t:mid] = 1
                    L[t, mid:p0 + 2 * b] = l
        if not reverse:
            W[GLA_LEVELS, t, c0:t + 1] = 1
            W[GLA_LEVELS + 1, t, t + 1:c0 + c] = 1
            L[t, t] = GLA_LEVELS
        else:
            W[GLA_LEVELS, t, t:c0 + c] = 1
            W[GLA_LEVELS + 1, t, c0:t] = 1
        W[GLA_LEVELS + 2, t, c0:c0 + c] = 1
    return W.reshape(-1, n), L


def _gla_scan_kernel(qkf_ref, vf_ref, gdf_ref, qkb_ref, vb_ref, gdb_ref, wupf_ref, bf_ref, wupb_ref, bb_ref,
                     wf_ref, lf_ref, wb_ref, lb_ref, of_ref, ob_ref, s_ref):
    n, c, r = GLA_TILE, GLA_CHUNK, GLA_GATE_RANK

    @pl.when(pl.program_id(0) == 0)
    def _():
        s_ref[...] = jnp.zeros_like(s_ref)

    dirs = [(qkf_ref, vf_ref, gdf_ref[:, :r], wupf_ref, bf_ref, wf_ref, lf_ref, of_ref, False),
            (qkb_ref, vb_ref, gdb_ref[:, r:], wupb_ref, bb_ref, wb_ref, lb_ref, ob_ref, True)]

    factors = []
    for qk_ref, v_ref, gd, wup_ref, b_ref, w_ref, l_ref, o_ref, reverse in dirs:
        z = _dot_split(gd, wup_ref[...]) + b_ref[...]
        g = (jnp.minimum(z, 0.0) - jnp.log1p(jnp.exp(-jnp.abs(z)))) * (1.0 / GLA_TAU)
        ghi, glo = _split_bf16(g)
        w = w_ref[...]
        factors.append(jnp.exp(_dot(w, ghi) + _dot(w, glo)))

    chains = []
    for d, (qk_ref, v_ref, gd, wup_ref, b_ref, w_ref, l_ref, o_ref, reverse) in enumerate(dirs):
        lvl = l_ref[...]
        for h in range(GLA_HEADS):
            kcols = slice(h * GLA_HEAD_K, (h + 1) * GLA_HEAD_K)
            vcols = slice(h * GLA_HEAD_V, (h + 1) * GLA_HEAD_V)
            f = factors[d][:, kcols]
            q = qk_ref[:, kcols] * (GLA_HEAD_K ** -0.5)
            k = qk_ref[:, GLA_DK + h * GLA_HEAD_K:GLA_DK + (h + 1) * GLA_HEAD_K]
            vb = v_ref[:, vcols].astype(BF16)
            attn = jnp.zeros((n, n), F32)
            for l in range(GLA_LEVELS):
                fl = f[l * n:(l + 1) * n]
                p = _dot_nt((q * fl).astype(BF16), (k * fl).astype(BF16))
                attn = jnp.where(lvl == l, p, attn)
            if not reverse:
                p = _dot_nt(q.astype(BF16), k.astype(BF16))
                attn = jnp.where(lvl == GLA_LEVELS, p, attn)
            o_intra = _dot(attn.astype(BF16), vb)
            qh = (q * f[GLA_LEVELS * n:(GLA_LEVELS + 1) * n]).astype(BF16)
            kh = (k * f[(GLA_LEVELS + 1) * n:(GLA_LEVELS + 2) * n]).astype(BF16)
            ftot = f[(GLA_LEVELS + 2) * n:(GLA_LEVELS + 3) * n]
            chains.append(dict(d=d, h=h, vcols=vcols, o_ref=o_ref, reverse=reverse, vb=vb, o_intra=o_intra, qh=qh,
                               kh=kh, ftot=ftot, st=s_ref[d, h]))

    nchunks = n // c
    for step in range(nchunks):
        for ch in chains:
            j = nchunks - 1 - step if ch["reverse"] else step
            rows = slice(j * c, (j + 1) * c)
            st = ch["st"]
            ch["o_ref"][rows, ch["vcols"]] = ch["o_intra"][rows] + _dot_nt(ch["qh"][rows], st.astype(BF16))
            ch["st"] = st * ch["ftot"][j * c:j * c + 1, :] + _dot_tn(ch["vb"][rows], ch["kh"][rows])
    for ch in chains:
        s_ref[ch["d"], ch["h"]] = ch["st"]


def _gla_scan(qk, v, gd, w_up_f, b_f, w_up_b, b_b):
    T = qk.shape[0]
    n = GLA_TILE
    nt = T // n
    tables = []
    for reverse in (False, True):
        W, L = _gla_tables(reverse)
        tables += [jnp.asarray(W, BF16), jnp.asarray(L)]
    fwd = lambda width: pl.BlockSpec((n, width), lambda i: (i, 0))
    bwd = lambda width: pl.BlockSpec((n, width), lambda i: (nt - 1 - i, 0))
    full = lambda a: pl.BlockSpec(a.shape, lambda i: (0, 0))
    consts = [w_up_f, b_f.reshape(1, GLA_DK), w_up_b, b_b.reshape(1, GLA_DK)] + tables
    ngd = 2 * GLA_GATE_RANK
    out = jax.ShapeDtypeStruct((T, GLA_DV), F32)
    return pl.pallas_call(
        _gla_scan_kernel,
        grid=(nt,),
        in_specs=[fwd(2 * GLA_DK), fwd(GLA_DV), fwd(ngd), bwd(2 * GLA_DK), bwd(GLA_DV), bwd(ngd)]
                 + [full(a) for a in consts],
        out_specs=[fwd(GLA_DV), bwd(GLA_DV)],
        out_shape=[out, out],
        scratch_shapes=[pltpu.VMEM((2, GLA_HEADS, GLA_HEAD_V, GLA_HEAD_K), F32)],
        compiler_params=_cparams(("arbitrary",)),
        name="gla_scan",
    )(qk, v, gd, qk, v, gd, *consts)


def _gla_out_kernel(of_ref, ob_ref, r_ref, x_ref, hn_ref, w_ref, o_ref):
    acc = x_ref[...]
    for h in range(GLA_HEADS):
        cols = slice(h * GLA_HEAD_V, (h + 1) * GLA_HEAD_V)
        o = _rms(of_ref[:, cols] + ob_ref[:, cols], hn_ref[...])
        r = r_ref[:, cols]
        gated = o * (r * (1.0 / (1.0 + jnp.exp(-r))))
        acc = acc + _dot(gated.astype(BF16), w_ref[cols, :])
    o_ref[...] = acc


def _gla_out(of, ob, r, x, head_norm, w_out, tm=512):
    T = x.shape[0]
    w = w_out.astype(BF16)
    hn = head_norm.reshape(1, GLA_HEAD_V)
    row = lambda n: pl.BlockSpec((tm, n), lambda i: (i, 0))
    full = lambda a: pl.BlockSpec(a.shape, lambda i: (0, 0))
    return pl.pallas_call(
        _gla_out_kernel,
        grid=(T // tm,),
        in_specs=[row(GLA_DV), row(GLA_DV), row(GLA_DV), row(D_MODEL), full(hn), full(w)],
        out_specs=row(D_MODEL),
        out_shape=jax.ShapeDtypeStruct((T, D_MODEL), F32),
        compiler_params=_cparams(("arbitrary",)),
        name="gla_out",
    )(of, ob, r, x, hn, w)


def _gla_mixer(x, gain, w_in, w_up_f, b_f, w_up_b, b_b, head_norm, w_out):
    qk, v, r, gd = _gla_in(x, gain, w_in)
    of, ob = _gla_scan(qk, v, gd, w_up_f, b_f, w_up_b, b_b)
    return _gla_out(of, ob, r, x, head_norm, w_out)


def _router_kernel(h_ref, g_ref, w_ref, aff_ref, hn_ref):
    hn = _rms(h_ref[...], g_ref[...])
    hn_ref[...] = hn.astype(BF16)
    logits = _dot_split(hn, w_ref[...])
    e = jnp.exp(logits - jnp.max(logits, axis=-1, keepdims=True))
    aff_ref[...] = e / jnp.sum(e, axis=-1, keepdims=True)


def _router(h, gain, w_router, tm=512):
    T = h.shape[0]
    gain2 = gain.reshape(1, D_MODEL)
    row = lambda n: pl.BlockSpec((tm, n), lambda i: (i, 0))
    full = lambda a: pl.BlockSpec(a.shape, lambda i: (0, 0))
    return pl.pallas_call(
        _router_kernel,
        grid=(T // tm,),
        in_specs=[row(D_MODEL), full(gain2), full(w_router)],
        out_specs=[row(N_EXPERTS), row(D_MODEL)],
        out_shape=[jax.ShapeDtypeStruct((T, N_EXPERTS), F32), jax.ShapeDtypeStruct((T, D_MODEL), BF16)],
        compiler_params=_cparams(("arbitrary",)),
        name="moe_router",
    )(h, gain2, w_router)


def _select_kernel(aff_ref, tri_ref, posm_ref, before_ref, *, cap):
    T = aff_ref.shape[1]
    bits = pltpu.bitcast(aff_ref[...], jnp.int32)

    def search(it, thr):
        cand = thr | jnp.left_shift(jnp.int32(1), 30 - it)
        cnt = jnp.sum(jnp.where(bits >= cand, 1.0, 0.0), axis=1, keepdims=True)
        return jnp.where(cnt >= cap, cand, thr)

    thr = lax.fori_loop(0, 31, search, jnp.zeros((N_EXPERTS, 1), jnp.int32))
    n_gt = jnp.sum(jnp.where(bits > thr, 1.0, 0.0), axis=1, keepdims=True)
    need = cap - n_gt
    tri = tri_ref[...]

    def scan(j, carry):
        c_eq, c_sel = carry
        cols = pl.ds(pl.multiple_of(j * LANES, LANES), LANES)
        blk = pltpu.bitcast(aff_ref[:, cols], jnp.int32)
        eq = jnp.where(blk == thr, 1.0, 0.0)
        rank = _dot(eq.astype(BF16), tri) + c_eq - eq
        sel = jnp.where((blk > thr) | ((eq > 0.0) & (rank < need)), 1.0, 0.0)
        before = _dot(sel.astype(BF16), tri) + c_sel - sel
        before_ref[:, cols] = before.astype(jnp.int32)
        posm_ref[:, cols] = jnp.where(sel > 0.0, before, -1.0).astype(jnp.int32)
        return (c_eq + jnp.sum(eq, axis=1, keepdims=True), c_sel + jnp.sum(sel, axis=1, keepdims=True))

    zero = jnp.zeros((N_EXPERTS, 1), F32)
    lax.fori_loop(0, T // LANES, scan, (zero, zero))


def _select(aff_t, cap):
    T = aff_t.shape[1]
    tri = jnp.asarray(np.triu(np.ones((LANES, LANES), np.float32)), BF16)
    full = lambda a: pl.BlockSpec(a.shape, lambda: (0,) * a.ndim)
    out = jax.ShapeDtypeStruct((N_EXPERTS, T), jnp.int32)
    return pl.pallas_call(
        functools.partial(_select_kernel, cap=cap),
        in_specs=[full(aff_t), full(tri)],
        out_specs=[pl.BlockSpec((N_EXPERTS, T), lambda: (0, 0))] * 2,
        out_shape=[out, out],
        compiler_params=pltpu.CompilerParams(vmem_limit_bytes=VMEM_LIMIT),
        name="moe_select",
    )(aff_t, tri)


def _dispatch_window():
    return MOE_TILE + F32_SUBLANES


def _dispatch_kernel(start_ref, hn_ref, posm_ref, x_ref, acc_ref, *, cap, nt, sub):
    e, t = pl.program_id(0), pl.program_id(1)
    win = _dispatch_window()

    @pl.when(t == 0)
    def _():
        acc_ref[...] = jnp.zeros_like(acc_ref)

    for s in range(sub):
        tok = slice(s * MOE_TILE, (s + 1) * MOE_TILE)
        base = pl.multiple_of((start_ref[e * nt + t * sub + s] // F32_SUBLANES) * F32_SUBLANES, F32_SUBLANES)
        slot = base + lax.broadcasted_iota(jnp.int32, (win, MOE_TILE), 0)
        onehot = jnp.where(posm_ref[0, :, tok] == slot, 1.0, 0.0).astype(BF16)
        acc_ref[pl.ds(base, win), :] += _dot(onehot, hn_ref[tok, :])

    @pl.when(t == pl.num_programs(1) - 1)
    def _():
        x_ref[0] = acc_ref[:cap, :].astype(BF16)


def _dispatch(hn, posm, start, cap, sub=8):
    T = hn.shape[0]
    nt = T // MOE_TILE
    sub = math.gcd(sub, nt)
    blk = sub * MOE_TILE
    posm3 = posm.reshape(N_EXPERTS, 1, T)
    grid_spec = pltpu.PrefetchScalarGridSpec(
        num_scalar_prefetch=1,
        grid=(N_EXPERTS, nt // sub),
        in_specs=[
            pl.BlockSpec((blk, D_MODEL), lambda e, t, s: (t, 0)),
            pl.BlockSpec((1, 1, blk), lambda e, t, s: (e, 0, t)),
        ],
        out_specs=pl.BlockSpec((1, cap, D_MODEL), lambda e, t, s: (e, 0, 0)),
        scratch_shapes=[pltpu.VMEM((cap + _dispatch_window(), D_MODEL), F32)],
    )
    return pl.pallas_call(
        functools.partial(_dispatch_kernel, cap=cap, nt=nt, sub=sub),
        grid_spec=grid_spec,
        out_shape=jax.ShapeDtypeStruct((N_EXPERTS, cap, D_MODEL), BF16),
        compiler_params=_cparams(("arbitrary", "arbitrary")),
        name="moe_dispatch",
    )(start.reshape(-1), hn, posm3)


def _ffn_kernel(x_ref, wg_ref, wu_ref, wd_ref, y_ref, acc_ref):
    f = pl.program_id(1)

    @pl.when(f == 0)
    def _():
        acc_ref[...] = jnp.zeros_like(acc_ref)

    wg = wg_ref[0, 0].astype(BF16)
    wu = wu_ref[0, 0].astype(BF16)
    wd = wd_ref[0, 0].astype(BF16)
    cap = x_ref.shape[1]
    blk = cap // math.gcd(cap // F32_SUBLANES, 4)
    for r0 in range(0, cap, blk):
        rows = slice(r0, r0 + blk)
        x = x_ref[0, rows, :]
        a = _dot(x, wg)
        u = _dot(x, wu)
        mid = (a * (1.0 / (1.0 + jnp.exp(-a))) * u).astype(BF16)
        acc_ref[rows, :] += _dot(mid, wd)

    @pl.when(f == pl.num_programs(1) - 1)
    def _():
        y_ref[0] = acc_ref[...].astype(BF16)


def _ffn(x, w_gate, w_up, w_down, layer, tf=512):
    cap = x.shape[1]
    return pl.pallas_call(
        _ffn_kernel,
        grid=(N_EXPERTS, EXPERT_FF // tf),
        in_specs=[
            pl.BlockSpec((1, cap, D_MODEL), lambda e, f: (e, 0, 0)),
            pl.BlockSpec((1, 1, D_MODEL, tf), lambda e, f: (layer, e, 0, f)),
            pl.BlockSpec((1, 1, D_MODEL, tf), lambda e, f: (layer, e, 0, f)),
            pl.BlockSpec((1, 1, tf, D_MODEL), lambda e, f: (layer, e, f, 0)),
        ],
        out_specs=pl.BlockSpec((1, cap, D_MODEL), lambda e, f: (e, 0, 0)),
        out_shape=jax.ShapeDtypeStruct((N_EXPERTS, cap, D_MODEL), BF16),
        scratch_shapes=[pltpu.VMEM((cap, D_MODEL), F32)],
        compiler_params=_cparams(("arbitrary", "arbitrary")),
        name="moe_ffn",
    )(x, w_gate, w_up, w_down)


def _combine_window():
    return MOE_TILE + BF16_SUBLANES


def _combine_kernel(start_ref, h_ref, aff_ref, posm_ref, g_ref, y_hbm, o_ref, buf_ref, sem_ref, acc_ref, *,
                    cap, nt, final_norm):
    t = pl.program_id(0)
    win = _combine_window()
    head = COMBINE_HEAD

    def base_of(tile, e):
        b = (start_ref[e * (nt + 1) + tile] // BF16_SUBLANES) * BF16_SUBLANES
        return pl.multiple_of(jnp.minimum(b, cap - win), BF16_SUBLANES)

    def needs_rest(tile, e):
        return start_ref[e * (nt + 1) + tile + 1] > base_of(tile, e) + head

    def head_copy(tile, e, slot):
        return pltpu.make_async_copy(y_hbm.at[e, pl.ds(base_of(tile, e), head), :],
                                     buf_ref.at[slot, e, pl.ds(0, head), :], sem_ref.at[0, slot, e])

    def rest_copy(tile, e, slot):
        return pltpu.make_async_copy(y_hbm.at[e, pl.ds(base_of(tile, e) + head, win - head), :],
                                     buf_ref.at[slot, e, pl.ds(head, win - head), :], sem_ref.at[1, slot, e])

    def fetch(tile, slot):
        for e in range(N_EXPERTS):
            head_copy(tile, e, slot).start()

            @pl.when(needs_rest(tile, e))
            def _():
                rest_copy(tile, e, slot).start()

    slot = t % 2

    @pl.when(t == 0)
    def _():
        fetch(0, 0)

    @pl.when(t + 1 < nt)
    def _():
        fetch(t + 1, 1 - slot)

    def expand(e, lo, hi):
        slots = base_of(t, e) + lo + lax.broadcasted_iota(jnp.int32, (MOE_TILE, hi - lo), 1)
        onehot = jnp.where(posm_ref[:, e:e + 1] == slots, 1.0, 0.0).astype(BF16)
        return aff_ref[:, e:e + 1] * _dot(onehot, buf_ref[slot, e, lo:hi, :])

    for e in range(N_EXPERTS):
        head_copy(t, e, slot).wait()
    acc = h_ref[...]
    for e in range(N_EXPERTS):
        acc = acc + expand(e, 0, head)
    acc_ref[...] = acc

    any_rest = needs_rest(t, 0)
    for e in range(1, N_EXPERTS):
        any_rest = any_rest | needs_rest(t, e)

    @pl.when(any_rest)
    def _():
        for e in range(N_EXPERTS):
            @pl.when(needs_rest(t, e))
            def _():
                rest_copy(t, e, slot).wait()
                acc_ref[...] += expand(e, head, MOE_TILE) + expand(e, MOE_TILE, win)

    acc = acc_ref[...]
    if final_norm:
        acc = _rms(acc, g_ref[...])
    o_ref[...] = acc


def _combine(h, aff, posm_t, start, y, cap, final_gain):
    T = h.shape[0]
    nt = T // MOE_TILE
    win = _combine_window()
    assert cap >= win and (cap - win) % BF16_SUBLANES == 0
    final_norm = final_gain is not None
    gain = (final_gain if final_norm else jnp.ones((D_MODEL,), F32)).reshape(1, D_MODEL)
    grid_spec = pltpu.PrefetchScalarGridSpec(
        num_scalar_prefetch=1,
        grid=(nt,),
        in_specs=[
            pl.BlockSpec((MOE_TILE, D_MODEL), lambda t, s: (t, 0)),
            pl.BlockSpec((MOE_TILE, N_EXPERTS), lambda t, s: (t, 0)),
            pl.BlockSpec((MOE_TILE, N_EXPERTS), lambda t, s: (t, 0)),
            pl.BlockSpec((1, D_MODEL), lambda t, s: (0, 0)),
            pl.BlockSpec(memory_space=pl.ANY),
        ],
        out_specs=pl.BlockSpec((MOE_TILE, D_MODEL), lambda t, s: (t, 0)),
        scratch_shapes=[pltpu.VMEM((2, N_EXPERTS, win, D_MODEL), BF16), pltpu.SemaphoreType.DMA((2, 2, N_EXPERTS)),
                        pltpu.VMEM((MOE_TILE, D_MODEL), F32)],
    )
    return pl.pallas_call(
        functools.partial(_combine_kernel, cap=cap, nt=nt, final_norm=final_norm),
        grid_spec=grid_spec,
        out_shape=jax.ShapeDtypeStruct((T, D_MODEL), F32),
        compiler_params=_cparams(("arbitrary",)),
        name="moe_combine",
    )(start.reshape(-1), h, aff, posm_t, gain, y)


def _ec_moe(h, gain, w_router, w_gate, w_up, w_down, layer, final_gain=None):
    T = h.shape[0]
    cap = max(1, EC_CAPACITY_FACTOR * T // N_EXPERTS)
    aff, hn = _router(h, gain, w_router)
    posm, before = _select(aff.T, cap)
    start = before[:, ::MOE_TILE]
    x = _dispatch(hn, posm, start, cap)
    y = _ffn(x, w_gate, w_up, w_down, layer)
    start_end = jnp.concatenate([start, jnp.full((N_EXPERTS, 1), cap, jnp.int32)], axis=1)
    return _combine(h, aff, posm.T, start_end, y, cap, final_gain)


def _rms_cols(x, g):
    return x * lax.rsqrt(jnp.mean(x * x, axis=0, keepdims=True) + RMS_EPS) * g


MLA_QCOLS = MLA_NOPE + 2 * LANES


def _mla_in_kernel(h_ref, posc_ref, posr_ref, g_ref, win_ref, wintkv_ref, qn_ref, wq_ref, wukt_ref, kvn_ref,
                   kvnc_ref, freqr_ref, sign_ref, freqc_ref, q_out, kt_out, va_out):
    half = MLA_ROPE // 2
    hn = _rms(h_ref[...], g_ref[...]).astype(BF16)
    c = _dot(hn, win_ref[...])
    ckv_t = _dot_nt(wintkv_ref[...], hn)

    cq = _rms(c[:, :MLA_Q_RANK], qn_ref[...]).astype(BF16)
    qa = _dot(cq, wq_ref[...])
    ang = posc_ref[...].astype(F32) * freqr_ref[...]
    cos, sin_signed = jnp.cos(ang), jnp.sin(ang) * sign_ref[...]
    qscale = MLA_QK ** -0.5 * math.log2(math.e)
    for hd in range(MLA_HEADS):
        c0 = hd * MLA_QCOLS
        q_lat = _dot(qa[:, c0:c0 + MLA_NOPE].astype(BF16), wukt_ref[hd])
        rot = qa[:, c0 + MLA_NOPE:c0 + MLA_NOPE + LANES] * cos + qa[:, c0 + MLA_NOPE + LANES:c0 + MLA_QCOLS] * sin_signed
        q_out[hd, :, :MLA_NOPE] = (q_lat * qscale).astype(BF16)
        q_out[hd, :, MLA_NOPE:] = (rot[:, :MLA_ROPE] * qscale).astype(BF16)

    kt_out[:MLA_KV_RANK, :] = _rms_cols(ckv_t[:MLA_KV_RANK], kvnc_ref[...]).astype(BF16)
    ang_t = freqc_ref[...] * posr_ref[...].astype(F32)
    cos_t, sin_t = jnp.cos(ang_t), jnp.sin(ang_t)
    k1, k2 = ckv_t[MLA_KV_RANK:MLA_KV_RANK + half], ckv_t[MLA_KV_RANK + half:]
    kt_out[MLA_KV_RANK:MLA_KV_RANK + half, :] = (k1 * cos_t - k2 * sin_t).astype(BF16)
    kt_out[MLA_KV_RANK + half:, :] = (k1 * sin_t + k2 * cos_t).astype(BF16)

    ckv = _rms(c[:, MLA_Q_RANK:MLA_Q_RANK + MLA_KV_RANK], kvn_ref[...])
    va_out[:, :MLA_KV_RANK] = ckv.astype(BF16)
    lane = lax.broadcasted_iota(jnp.int32, (ckv.shape[0], LANES), 1)
    va_out[:, MLA_KV_RANK:] = jnp.where(lane == 0, 1.0, 0.0).astype(BF16)


def _mla_in(h, positions, gain, w_in, q_norm, w_uq, kv_norm, w_ukv, tm=256):
    T = h.shape[0]
    H, half = MLA_HEADS, MLA_ROPE // 2
    wq = w_uq.reshape(MLA_Q_RANK, H, MLA_QK)
    x1, x2 = wq[:, :, MLA_NOPE:MLA_NOPE + half], wq[:, :, MLA_NOPE + half:]
    pad = jnp.zeros((MLA_Q_RANK, H, LANES - MLA_ROPE), F32)
    wq_wide = jnp.concatenate([wq[:, :, :MLA_NOPE], x1, x2, pad, x2, x1, pad], axis=2)
    wq_wide = wq_wide.reshape(MLA_Q_RANK, H * MLA_QCOLS).astype(BF16)
    wuk_t = w_ukv.reshape(MLA_KV_RANK, H, MLA_NOPE + MLA_V)[:, :, :MLA_NOPE].transpose(1, 2, 0).astype(BF16)
    inv_freq = ROPE_BASE ** (-jnp.arange(half, dtype=F32) / half)
    zeros = jnp.zeros((LANES - MLA_ROPE,), F32)
    freq_row = jnp.concatenate([inv_freq, inv_freq, zeros]).reshape(1, LANES)
    sign_row = jnp.concatenate([-jnp.ones((half,), F32), jnp.ones((half,), F32), zeros]).reshape(1, LANES)
    w_in_b = w_in.astype(BF16)
    args = [h, positions.reshape(T, 1), positions.reshape(1, T), gain.reshape(1, -1), w_in_b,
            w_in_b[:, MLA_Q_RANK:].T, q_norm.reshape(1, -1), wq_wide, wuk_t, kv_norm.reshape(1, -1),
            kv_norm.reshape(-1, 1), freq_row, sign_row, inv_freq.reshape(half, 1)]
    full = lambda a: pl.BlockSpec(a.shape, lambda i: (0,) * a.ndim)
    in_specs = [pl.BlockSpec((tm, D_MODEL), lambda i: (i, 0)), pl.BlockSpec((tm, 1), lambda i: (i, 0)),
                pl.BlockSpec((1, tm), lambda i: (0, i))] + [full(a) for a in args[3:]]
    return pl.pallas_call(
        _mla_in_kernel,
        grid=(T // tm,),
        in_specs=in_specs,
        out_specs=[pl.BlockSpec((H, tm, MLA_QK), lambda i: (0, i, 0)),
                   pl.BlockSpec((MLA_QK, tm), lambda i: (0, i)),
                   pl.BlockSpec((tm, 2 * LANES), lambda i: (i, 0))],
        out_shape=[jax.ShapeDtypeStruct((H, T, MLA_QK), BF16), jax.ShapeDtypeStruct((MLA_QK, T), BF16),
                   jax.ShapeDtypeStruct((T, 2 * LANES), BF16)],
        compiler_params=_cparams(("arbitrary",)),
        name="mla_in",
    )(*args)


def _flash_kernel(q_ref, qn_ref, kt_ref, va_ref, o_ref, sa_ref, sb_ref, pa_ref, pb_ref, acc_ref, m_ref, alpha_ref,
                  *, tk, nblk):
    H, tq, _ = q_ref.shape
    T = kt_ref.shape[1]
    n = T // tk
    q = q_ref[...].reshape(H * tq, MLA_QK)
    q_next = qn_ref[...].reshape(H * tq, MLA_QK)

    def k_tile(j):
        return kt_ref[:, j * tk:(j + 1) * tk]

    def v_tile(j):
        return va_ref[j * tk:(j + 1) * tk, :]

    def lanes_rep(x, width):
        return jnp.concatenate([x] * (width // LANES), axis=1)

    blk = H * tq // nblk
    row_blocks = [slice(rb * blk, (rb + 1) * blk) for rb in range(nblk)]

    def add_pv(rows, p_ref, j):
        acc_ref[rows, :] = (lanes_rep(alpha_ref[rows, :], 2 * LANES) * acc_ref[rows, :]
                            + _dot(p_ref[rows, :], v_tile(j)))

    def step(j, s_cur, s_nxt, p_cur, p_prv):
        q_sel, k_nxt = (q, k_tile(j + 1)) if j + 1 < n else (q_next, k_tile(0))
        for rows in row_blocks:
            if j > 0:
                add_pv(rows, p_prv, j - 1)
            s_nxt[rows, :] = _dot(q_sel[rows], k_nxt)
            s = s_cur[rows, :]
            m_old = m_ref[rows, :]
            m_new = jnp.maximum(m_old, jnp.max(s, axis=1, keepdims=True))
            alpha_ref[rows, :] = jnp.exp2(m_old - m_new)
            m_ref[rows, :] = m_new
            p_cur[rows, :] = jnp.exp2(s - lanes_rep(m_new, tk)).astype(BF16)

    acc_ref[...] = jnp.zeros_like(acc_ref)
    m_ref[...] = jnp.full_like(m_ref, -jnp.inf)

    @pl.when(pl.program_id(0) == 0)
    def _():
        sa_ref[...] = _dot(q, k_tile(0))

    for j in range(n):
        if j % 2 == 0:
            step(j, sa_ref, sb_ref, pa_ref, pb_ref)
        else:
            step(j, sb_ref, sa_ref, pb_ref, pa_ref)
    p_last = pb_ref if n % 2 == 0 else pa_ref
    for rows in row_blocks:
        add_pv(rows, p_last, n - 1)
    acc = acc_ref[...]
    o_lat = (acc[:, :MLA_KV_RANK] / acc[:, MLA_KV_RANK:MLA_KV_RANK + 1]).astype(BF16)
    for hd in range(H):
        o_ref[:, hd * MLA_KV_RANK:(hd + 1) * MLA_KV_RANK] = o_lat[hd * tq:(hd + 1) * tq]


def _flash(q, kt, va, tq=64, tk=1024):
    H, T, _ = q.shape
    tq, tk = min(tq, T), min(tk, T)
    assert T % (2 * tk) == 0 and tq % BF16_SUBLANES == 0
    rows = H * tq
    steps = T // tq
    return pl.pallas_call(
        functools.partial(_flash_kernel, tk=tk, nblk=4),
        grid=(steps,),
        scratch_shapes=[pltpu.VMEM((rows, tk), F32), pltpu.VMEM((rows, tk), F32), pltpu.VMEM((rows, tk), BF16),
                        pltpu.VMEM((rows, tk), BF16), pltpu.VMEM((rows, 2 * LANES), F32),
                        pltpu.VMEM((rows, LANES), F32), pltpu.VMEM((rows, LANES), F32)],
        in_specs=[
            pl.BlockSpec((H, tq, MLA_QK), lambda i: (0, i, 0)),
            pl.BlockSpec((H, tq, MLA_QK), lambda i: (0, jnp.minimum(i + 1, steps - 1), 0)),
            pl.BlockSpec((MLA_QK, T), lambda i: (0, 0)),
            pl.BlockSpec((T, 2 * LANES), lambda i: (0, 0)),
        ],
        out_specs=pl.BlockSpec((tq, H * MLA_KV_RANK), lambda i: (i, 0)),
        out_shape=jax.ShapeDtypeStruct((T, H * MLA_KV_RANK), BF16),
        compiler_params=_cparams(("arbitrary",)),
        name="mla_flash",
    )(q, q, kt, va)


def _mla_out_kernel(o_ref, wuv_ref, w_ref, h_ref, out_ref):
    v = [_dot(o_ref[:, hd * MLA_KV_RANK:(hd + 1) * MLA_KV_RANK], wuv_ref[hd]).astype(BF16)
         for hd in range(MLA_HEADS)]
    out_ref[...] = h_ref[...] + _dot(jnp.concatenate(v, axis=1), w_ref[...])


def _mla_out(o_lat, w_ukv, w_out, h, tm=512):
    T = h.shape[0]
    H = MLA_HEADS
    wuv = w_ukv.reshape(MLA_KV_RANK, H, MLA_NOPE + MLA_V)[:, :, MLA_NOPE:].transpose(1, 0, 2).astype(BF16)
    wb = w_out.astype(BF16)
    return pl.pallas_call(
        _mla_out_kernel,
        grid=(T // tm,),
        in_specs=[pl.BlockSpec((tm, H * MLA_KV_RANK), lambda i: (i, 0)), pl.BlockSpec(wuv.shape, lambda i: (0, 0, 0)),
                  pl.BlockSpec(wb.shape, lambda i: (0, 0)), pl.BlockSpec((tm, D_MODEL), lambda i: (i, 0))],
        out_specs=pl.BlockSpec((tm, D_MODEL), lambda i: (i, 0)),
        out_shape=jax.ShapeDtypeStruct((T, D_MODEL), F32),
        compiler_params=_cparams(("arbitrary",)),
        name="mla_out",
    )(o_lat, wuv, wb, h)


def _mla_mixer(h, positions, gain, w_in, q_norm, w_uq, kv_norm, w_ukv, w_out):
    q, kt, va = _mla_in(h, positions, gain, w_in, q_norm, w_uq, kv_norm, w_ukv)
    o_lat = _flash(q, kt, va)
    return _mla_out(o_lat, w_ukv, w_out, h)


def kernel(x, positions, mix_norm, ffn_norm, final_norm, gla_w_in, gla_w_gate_up_f, gla_b_gate_f, gla_w_gate_up_b,
           gla_b_gate_b, gla_head_norm, gla_w_out, mla_w_in, mla_q_norm, mla_w_uq, mla_kv_norm, mla_w_ukv,
           mla_w_out, moe_w_router, moe_w_gate, moe_w_up, moe_w_down):
    B, T, D = x.shape
    outs = []
    for b in range(B):
        h = x[b]
        h = _gla_mixer(h, mix_norm[0], gla_w_in[0], gla_w_gate_up_f[0], gla_b_gate_f[0], gla_w_gate_up_b[0],
                       gla_b_gate_b[0], gla_head_norm[0], gla_w_out[0])
        h = _ec_moe(h, ffn_norm[0], moe_w_router[0], moe_w_gate, moe_w_up, moe_w_down, 0)
        h = _mla_mixer(h, positions[b], mix_norm[1], mla_w_in[0], mla_q_norm[0], mla_w_uq[0], mla_kv_norm[0],
                       mla_w_ukv[0], mla_w_out[0])
        h = _ec_moe(h, ffn_norm[1], moe_w_router[1], moe_w_gate, moe_w_up, moe_w_down, 1, final_gain=final_norm)
        outs.append(h)
    return jnp.stack(outs)
```

```python
import functools
import math

import numpy as np
import jax
import jax.numpy as jnp
from jax import lax
from jax.experimental import pallas as pl
from jax.experimental.pallas import tpu as pltpu

F32 = jnp.float32
BF16 = jnp.bfloat16

D_MODEL = 1024
RMS_EPS = 1e-6

GLA_HEADS = 4
GLA_DK = 512
GLA_DV = 1024
GLA_HEAD_K = GLA_DK // GLA_HEADS
GLA_HEAD_V = GLA_DV // GLA_HEADS
GLA_GATE_RANK = 16
GLA_TAU = 16.0
GLA_CHUNK = 64
GLA_TILE = 256
GLA_LEVELS = 6

MLA_HEADS = 16
MLA_Q_RANK = 256
MLA_KV_RANK = 128
MLA_NOPE = 128
MLA_ROPE = 64
MLA_V = 128
MLA_QK = MLA_NOPE + MLA_ROPE
ROPE_BASE = 10000.0

N_EXPERTS = 16
EXPERT_FF = 2048
EC_CAPACITY_FACTOR = 2
MOE_TILE = 256
COMBINE_HEAD = 64
BF16_SUBLANES = 16
F32_SUBLANES = 8
LANES = 128

VMEM_LIMIT = 56 * 1024 * 1024


def _cparams(sem):
    return pltpu.CompilerParams(dimension_semantics=sem, vmem_limit_bytes=VMEM_LIMIT)


def _rms(x, g):
    return x * lax.rsqrt(jnp.mean(x * x, axis=-1, keepdims=True) + RMS_EPS) * g


def _split_bf16(x):
    hi = x.astype(BF16)
    lo = (x - hi.astype(F32)).astype(BF16)
    return hi, lo


def _dot(a, b):
    return jnp.dot(a, b, preferred_element_type=F32)


def _dot_nt(a, b):
    return lax.dot_general(a, b, (((1,), (1,)), ((), ())), preferred_element_type=F32)


def _dot_tn(a, b):
    return lax.dot_general(a, b, (((0,), (0,)), ((), ())), preferred_element_type=F32)


def _dot_split(a, b):
    ah, al = _split_bf16(a)
    bh, bl = _split_bf16(b)
    return _dot(ah, bh) + _dot(ah, bl) + _dot(al, bh)


def _gla_in_kernel(x_ref, g_ref, wqk_ref, wv_ref, wr_ref, wgd_ref, qk_ref, v_ref, r_ref, gd_ref):
    hn = _rms(x_ref[...], g_ref[...]).astype(BF16)
    qk_ref[...] = _dot(hn, wqk_ref[...])
    v_ref[...] = _dot(hn, wv_ref[...])
    r_ref[...] = _dot(hn, wr_ref[...])
    gd_ref[...] = _dot(hn, wgd_ref[...])


def _gla_in(x, gain, w_in, tm=512):
    T = x.shape[0]
    wqk = w_in[:, :2 * GLA_DK].astype(BF16)
    wv = w_in[:, 2 * GLA_DK:2 * GLA_DK + GLA_DV].astype(BF16)
    wr = w_in[:, 2 * GLA_DK + GLA_DV:2 * GLA_DK + 2 * GLA_DV].astype(BF16)
    wgd = w_in[:, 2 * GLA_DK + 2 * GLA_DV:].astype(BF16)
    ngd = 2 * GLA_GATE_RANK
    row = lambda n: pl.BlockSpec((tm, n), lambda i: (i, 0))
    full = lambda a: pl.BlockSpec(a.shape, lambda i: (0, 0))
    gain2 = gain.reshape(1, D_MODEL)
    return pl.pallas_call(
        _gla_in_kernel,
        grid=(T // tm,),
        in_specs=[row(D_MODEL), full(gain2), full(wqk), full(wv), full(wr), full(wgd)],
        out_specs=[row(2 * GLA_DK), row(GLA_DV), row(GLA_DV), row(ngd)],
        out_shape=[jax.ShapeDtypeStruct((T, 2 * GLA_DK), F32), jax.ShapeDtypeStruct((T, GLA_DV), F32),
                   jax.ShapeDtypeStruct((T, GLA_DV), F32), jax.ShapeDtypeStruct((T, ngd), F32)],
        compiler_params=_cparams(("arbitrary",)),
        name="gla_in",
    )(x, gain2, wqk, wv, wr, wgd)


def _gla_tables(reverse):
    n, c = GLA_TILE, GLA_CHUNK
    W = np.zeros((GLA_LEVELS + 3, n, n), np.float32)
    L = np.full((n, n), -1, np.int32)
    for t in range(n):
        c0 = (t // c) * c
        tt = t - c0
        for l in range(GLA_LEVELS):
            b = (c // 2) >> l
            p0 = c0 + (tt // (2 * b)) * 2 * b
            mid = p0 + b
            second = t >= mid
            if not reverse:
                if second:
                    W[l, t, mid:t + 1] = 1
                    L[t, p0:mid] = l
                else:
                    W[l, t, t + 1:mid] = 1
            else:
                if second:
                    W[l, t, mid:t] = 1
                else:
                    W[l, t, t:mid] = 1
                    L[t, mid:p0 + 2 * b] = l
        if not reverse:
            W[GLA_LEVELS, t, c0:t + 1] = 1
            W[GLA_LEVELS + 1, t, t + 1:c0 + c] = 1
            L[t, t] = GLA_LEVELS
        else:
            W[GLA_LEVELS, t, t:c0 + c] = 1
            W[GLA_LEVELS + 1, t, c0:t] = 1
        W[GLA_LEVELS + 2, t, c0:c0 + c] = 1
    return W.reshape(-1, n), L


def _gla_scan_kernel(qkf_ref, vf_ref, gdf_ref, qkb_ref, vb_ref, gdb_ref, wupf_ref, bf_ref, wupb_ref, bb_ref,
                     wf_ref, lf_ref, wb_ref, lb_ref, of_ref, ob_ref, s_ref):
    n, c, r = GLA_TILE, GLA_CHUNK, GLA_GATE_RANK

    @pl.when(pl.program_id(0) == 0)
    def _():
        s_ref[...] = jnp.zeros_like(s_ref)

    dirs = [(qkf_ref, vf_ref, gdf_ref[:, :r], wupf_ref, bf_ref, wf_ref, lf_ref, of_ref, False),
            (qkb_ref, vb_ref, gdb_ref[:, r:], wupb_ref, bb_ref, wb_ref, lb_ref, ob_ref, True)]

    factors = []
    for qk_ref, v_ref, gd, wup_ref, b_ref, w_ref, l_ref, o_ref, reverse in dirs:
        z = _dot_split(gd, wup_ref[...]) + b_ref[...]
        g = (jnp.minimum(z, 0.0) - jnp.log1p(jnp.exp(-jnp.abs(z)))) * (1.0 / GLA_TAU)
        ghi, glo = _split_bf16(g)
        w = w_ref[...]
        factors.append(jnp.exp(_dot(w, ghi) + _dot(w, glo)))

    chains = []
    for d, (qk_ref, v_ref, gd, wup_ref, b_ref, w_ref, l_ref, o_ref, reverse) in enumerate(dirs):
        lvl = l_ref[...]
        for h in range(GLA_HEADS):
            kcols = slice(h * GLA_HEAD_K, (h + 1) * GLA_HEAD_K)
            vcols = slice(h * GLA_HEAD_V, (h + 1) * GLA_HEAD_V)
            f = factors[d][:, kcols]
            q = qk_ref[:, kcols] * (GLA_HEAD_K ** -0.5)
            k = qk_ref[:, GLA_DK + h * GLA_HEAD_K:GLA_DK + (h + 1) * GLA_HEAD_K]
            vb = v_ref[:, vcols].astype(BF16)
            attn = jnp.zeros((n, n), F32)
            for l in range(GLA_LEVELS):
                fl = f[l * n:(l + 1) * n]
                p = _dot_nt((q * fl).astype(BF16), (k * fl).astype(BF16))
                attn = jnp.where(lvl == l, p, attn)
            if not reverse:
                p = _dot_nt(q.astype(BF16), k.astype(BF16))
                attn = jnp.where(lvl == GLA_LEVELS, p, attn)
            o_intra = _dot(attn.astype(BF16), vb)
            qh = (q * f[GLA_LEVELS * n:(GLA_LEVELS + 1) * n]).astype(BF16)
            kh = (k * f[(GLA_LEVELS + 1) * n:(GLA_LEVELS + 2) * n]).astype(BF16)
            ftot = f[(GLA_LEVELS + 2) * n:(GLA_LEVELS + 3) * n]
            chains.append(dict(d=d, h=h, vcols=vcols, o_ref=o_ref, reverse=reverse, vb=vb, o_intra=o_intra, qh=qh,
                               kh=kh, ftot=ftot, st=s_ref[d, h]))

    nchunks = n // c
    for step in range(nchunks):
        for ch in chains:
            j = nchunks - 1 - step if ch["reverse"] else step
            rows = slice(j * c, (j + 1) * c)
            st = ch["st"]
            ch["o_ref"][rows, ch["vcols"]] = ch["o_intra"][rows] + _dot_nt(ch["qh"][rows], st.astype(BF16))
            ch["st"] = st * ch["ftot"][j * c:j * c + 1, :] + _dot_tn(ch["vb"][rows], ch["kh"][rows])
    for ch in chains:
        s_ref[ch["d"], ch["h"]] = ch["st"]


def _gla_scan(qk, v, gd, w_up_f, b_f, w_up_b, b_b):
    T = qk.shape[0]
    n = GLA_TILE
    nt = T // n
    tables = []
    for reverse in (False, True):
        W, L = _gla_tables(reverse)
        tables += [jnp.asarray(W, BF16), jnp.asarray(L)]
    fwd = lambda width: pl.BlockSpec((n, width), lambda i: (i, 0))
    bwd = lambda width: pl.BlockSpec((n, width), lambda i: (nt - 1 - i, 0))
    full = lambda a: pl.BlockSpec(a.shape, lambda i: (0, 0))
    consts = [w_up_f, b_f.reshape(1, GLA_DK), w_up_b, b_b.reshape(1, GLA_DK)] + tables
    ngd = 2 * GLA_GATE_RANK
    out = jax.ShapeDtypeStruct((T, GLA_DV), F32)
    return pl.pallas_call(
        _gla_scan_kernel,
        grid=(nt,),
        in_specs=[fwd(2 * GLA_DK), fwd(GLA_DV), fwd(ngd), bwd(2 * GLA_DK), bwd(GLA_DV), bwd(ngd)]
                 + [full(a) for a in consts],
        out_specs=[fwd(GLA_DV), bwd(GLA_DV)],
        out_shape=[out, out],
        scratch_shapes=[pltpu.VMEM((2, GLA_HEADS, GLA_HEAD_V, GLA_HEAD_K), F32)],
        compiler_params=_cparams(("arbitrary",)),
        name="gla_scan",
    )(qk, v, gd, qk, v, gd, *consts)


def _gla_out_kernel(of_ref, ob_ref, r_ref, x_ref, hn_ref, w_ref, o_ref):
    acc = x_ref[...]
    for h in range(GLA_HEADS):
        cols = slice(h * GLA_HEAD_V, (h + 1) * GLA_HEAD_V)
        o = _rms(of_ref[:, cols] + ob_ref[:, cols], hn_ref[...])
        r = r_ref[:, cols]
        gated = o * (r * (1.0 / (1.0 + jnp.exp(-r))))
        acc = acc + _dot(gated.astype(BF16), w_ref[cols, :])
    o_ref[...] = acc


def _gla_out(of, ob, r, x, head_norm, w_out, tm=512):
    T = x.shape[0]
    w = w_out.astype(BF16)
    hn = head_norm.reshape(1, GLA_HEAD_V)
    row = lambda n: pl.BlockSpec((tm, n), lambda i: (i, 0))
    full = lambda a: pl.BlockSpec(a.shape, lambda i: (0, 0))
    return pl.pallas_call(
        _gla_out_kernel,
        grid=(T // tm,),
        in_specs=[row(GLA_DV), row(GLA_DV), row(GLA_DV), row(D_MODEL), full(hn), full(w)],
        out_specs=row(D_MODEL),
        out_shape=jax.ShapeDtypeStruct((T, D_MODEL), F32),
        compiler_params=_cparams(("arbitrary",)),
        name="gla_out",
    )(of, ob, r, x, hn, w)


def _gla_mixer(x, gain, w_in, w_up_f, b_f, w_up_b, b_b, head_norm, w_out):
    qk, v, r, gd = _gla_in(x, gain, w_in)
    of, ob = _gla_scan(qk, v, gd, w_up_f, b_f, w_up_b, b_b)
    return _gla_out(of, ob, r, x, head_norm, w_out)


def _router_kernel(h_ref, g_ref, w_ref, aff_ref, hn_ref):
    hn = _rms(h_ref[...], g_ref[...])
    hn_ref[...] = hn.astype(BF16)
    logits = _dot_split(hn, w_ref[...])
    e = jnp.exp(logits - jnp.max(logits, axis=-1, keepdims=True))
    aff_ref[...] = e / jnp.sum(e, axis=-1, keepdims=True)


def _router(h, gain, w_router, tm=512):
    T = h.shape[0]
    gain2 = gain.reshape(1, D_MODEL)
    row = lambda n: pl.BlockSpec((tm, n), lambda i: (i, 0))
    full = lambda a: pl.BlockSpec(a.shape, lambda i: (0, 0))
    return pl.pallas_call(
        _router_kernel,
        grid=(T // tm,),
        in_specs=[row(D_MODEL), full(gain2), full(w_router)],
        out_specs=[row(N_EXPERTS), row(D_MODEL)],
        out_shape=[jax.ShapeDtypeStruct((T, N_EXPERTS), F32), jax.ShapeDtypeStruct((T, D_MODEL), BF16)],
        compiler_params=_cparams(("arbitrary",)),
        name="moe_router",
    )(h, gain2, w_router)


def _select_kernel(aff_ref, tri_ref, posm_ref, before_ref, *, cap):
    T = aff_ref.shape[1]
    bits = pltpu.bitcast(aff_ref[...], jnp.int32)

    def search(it, thr):
        cand = thr | jnp.left_shift(jnp.int32(1), 30 - it)
        cnt = jnp.sum(jnp.where(bits >= cand, 1.0, 0.0), axis=1, keepdims=True)
        return jnp.where(cnt >= cap, cand, thr)

    thr = lax.fori_loop(0, 31, search, jnp.zeros((N_EXPERTS, 1), jnp.int32))
    n_gt = jnp.sum(jnp.where(bits > thr, 1.0, 0.0), axis=1, keepdims=True)
    need = cap - n_gt
    tri = tri_ref[...]

    def scan(j, carry):
        c_eq, c_sel = carry
        cols = pl.ds(pl.multiple_of(j * LANES, LANES), LANES)
        blk = pltpu.bitcast(aff_ref[:, cols], jnp.int32)
        eq = jnp.where(blk == thr, 1.0, 0.0)
        rank = _dot(eq.astype(BF16), tri) + c_eq - eq
        sel = jnp.where((blk > thr) | ((eq > 0.0) & (rank < need)), 1.0, 0.0)
        before = _dot(sel.astype(BF16), tri) + c_sel - sel
        before_ref[:, cols] = before.astype(jnp.int32)
        posm_ref[:, cols] = jnp.where(sel > 0.0, before, -1.0).astype(jnp.int32)
        return (c_eq + jnp.sum(eq, axis=1, keepdims=True), c_sel + jnp.sum(sel, axis=1, keepdims=True))

    zero = jnp.zeros((N_EXPERTS, 1), F32)
    lax.fori_loop(0, T // LANES, scan, (zero, zero))


def _select(aff_t, cap):
    T = aff_t.shape[1]
    tri = jnp.asarray(np.triu(np.ones((LANES, LANES), np.float32)), BF16)
    full = lambda a: pl.BlockSpec(a.shape, lambda: (0,) * a.ndim)
    out = jax.ShapeDtypeStruct((N_EXPERTS, T), jnp.int32)
    return pl.pallas_call(
        functools.partial(_select_kernel, cap=cap),
        in_specs=[full(aff_t), full(tri)],
        out_specs=[pl.BlockSpec((N_EXPERTS, T), lambda: (0, 0))] * 2,
        out_shape=[out, out],
        compiler_params=pltpu.CompilerParams(vmem_limit_bytes=VMEM_LIMIT),
        name="moe_select",
    )(aff_t, tri)


def _dispatch_window():
    return MOE_TILE + F32_SUBLANES


def _dispatch_kernel(start_ref, hn_ref, posm_ref, x_ref, acc_ref, *, cap, nt, sub):
    e, t = pl.program_id(0), pl.program_id(1)
    win = _dispatch_window()

    @pl.when(t == 0)
    def _():
        acc_ref[...] = jnp.zeros_like(acc_ref)

    for s in range(sub):
        tok = slice(s * MOE_TILE, (s + 1) * MOE_TILE)
        base = pl.multiple_of((start_ref[e * nt + t * sub + s] // F32_SUBLANES) * F32_SUBLANES, F32_SUBLANES)
        slot = base + lax.broadcasted_iota(jnp.int32, (win, MOE_TILE), 0)
        onehot = jnp.where(posm_ref[0, :, tok] == slot, 1.0, 0.0).astype(BF16)
        acc_ref[pl.ds(base, win), :] += _dot(onehot, hn_ref[tok, :])

    @pl.when(t == pl.num_programs(1) - 1)
    def _():
        x_ref[0] = acc_ref[:cap, :].astype(BF16)


def _dispatch(hn, posm, start, cap, sub=8):
    T = hn.shape[0]
    nt = T // MOE_TILE
    sub = math.gcd(sub, nt)
    blk = sub * MOE_TILE
    posm3 = posm.reshape(N_EXPERTS, 1, T)
    grid_spec = pltpu.PrefetchScalarGridSpec(
        num_scalar_prefetch=1,
        grid=(N_EXPERTS, nt // sub),
        in_specs=[
            pl.BlockSpec((blk, D_MODEL), lambda e, t, s: (t, 0)),
            pl.BlockSpec((1, 1, blk), lambda e, t, s: (e, 0, t)),
        ],
        out_specs=pl.BlockSpec((1, cap, D_MODEL), lambda e, t, s: (e, 0, 0)),
        scratch_shapes=[pltpu.VMEM((cap + _dispatch_window(), D_MODEL), F32)],
    )
    return pl.pallas_call(
        functools.partial(_dispatch_kernel, cap=cap, nt=nt, sub=sub),
        grid_spec=grid_spec,
        out_shape=jax.ShapeDtypeStruct((N_EXPERTS, cap, D_MODEL), BF16),
        compiler_params=_cparams(("arbitrary", "arbitrary")),
        name="moe_dispatch",
    )(start.reshape(-1), hn, posm3)


def _ffn_kernel(x_ref, wg_ref, wu_ref, wd_ref, y_ref, acc_ref):
    f = pl.program_id(1)

    @pl.when(f == 0)
    def _():
        acc_ref[...] = jnp.zeros_like(acc_ref)

    wg = wg_ref[0, 0].astype(BF16)
    wu = wu_ref[0, 0].astype(BF16)
    wd = wd_ref[0, 0].astype(BF16)
    cap = x_ref.shape[1]
    blk = cap // math.gcd(cap // F32_SUBLANES, 4)
    for r0 in range(0, cap, blk):
        rows = slice(r0, r0 + blk)
        x = x_ref[0, rows, :]
        a = _dot(x, wg)
        u = _dot(x, wu)
        mid = (a * (1.0 / (1.0 + jnp.exp(-a))) * u).astype(BF16)
        acc_ref[rows, :] += _dot(mid, wd)

    @pl.when(f == pl.num_programs(1) - 1)
    def _():
        y_ref[0] = acc_ref[...].astype(BF16)


def _ffn(x, w_gate, w_up, w_down, layer, tf=512):
    cap = x.shape[1]
    return pl.pallas_call(
        _ffn_kernel,
        grid=(N_EXPERTS, EXPERT_FF // tf),
        in_specs=[
            pl.BlockSpec((1, cap, D_MODEL), lambda e, f: (e, 0, 0)),
            pl.BlockSpec((1, 1, D_MODEL, tf), lambda e, f: (layer, e, 0, f)),
            pl.BlockSpec((1, 1, D_MODEL, tf), lambda e, f: (layer, e, 0, f)),
            pl.BlockSpec((1, 1, tf, D_MODEL), lambda e, f: (layer, e, f, 0)),
        ],
        out_specs=pl.BlockSpec((1, cap, D_MODEL), lambda e, f: (e, 0, 0)),
        out_shape=jax.ShapeDtypeStruct((N_EXPERTS, cap, D_MODEL), BF16),
        scratch_shapes=[pltpu.VMEM((cap, D_MODEL), F32)],
        compiler_params=_cparams(("arbitrary", "arbitrary")),
        name="moe_ffn",
    )(x, w_gate, w_up, w_down)


def _combine_window():
    return MOE_TILE + BF16_SUBLANES


def _combine_kernel(start_ref, h_ref, aff_ref, posm_ref, g_ref, y_hbm, o_ref, hbuf_ref, buf_ref, sem_ref, acc_ref,
                    *, cap, nt, final_norm):
    t = pl.program_id(0)
    win = _combine_window()
    head = COMBINE_HEAD

    def base_of(tile, e):
        b = (start_ref[e * (nt + 1) + tile] // BF16_SUBLANES) * BF16_SUBLANES
        return pl.multiple_of(jnp.minimum(b, cap - win), BF16_SUBLANES)

    def needs_rest(tile, e):
        return start_ref[e * (nt + 1) + tile + 1] > base_of(tile, e) + head

    def head_copy(tile, e, slot):
        return pltpu.make_async_copy(y_hbm.at[e, pl.ds(base_of(tile, e), head), :],
                                     hbuf_ref.at[slot, pl.ds(e * head, head), :], sem_ref.at[0, slot, e])

    def rest_copy(tile, e, slot):
        return pltpu.make_async_copy(y_hbm.at[e, pl.ds(base_of(tile, e) + head, win - head), :],
                                     buf_ref.at[slot, e], sem_ref.at[1, slot, e])

    def fetch(tile, slot):
        for e in range(N_EXPERTS):
            head_copy(tile, e, slot).start()

            @pl.when(needs_rest(tile, e))
            def _():
                rest_copy(tile, e, slot).start()

    slot = t % 2

    @pl.when(t == 0)
    def _():
        fetch(0, 0)

    @pl.when(t + 1 < nt)
    def _():
        fetch(t + 1, 1 - slot)

    def expand(e, lo, hi):
        slots = base_of(t, e) + lo + lax.broadcasted_iota(jnp.int32, (MOE_TILE, hi - lo), 1)
        onehot = jnp.where(posm_ref[:, e:e + 1] == slots, 1.0, 0.0).astype(BF16)
        return aff_ref[:, e:e + 1] * _dot(onehot, buf_ref[slot, e, lo - head:hi - head, :])

    for e in range(N_EXPERTS):
        head_copy(t, e, slot).wait()

    per_group = LANES // head
    a_hi, a_lo = _split_bf16(aff_ref[...])
    a_hi, a_lo = a_hi.astype(F32), a_lo.astype(F32)
    lane = lax.broadcasted_iota(jnp.int32, (MOE_TILE, LANES), 1)
    w_hi, w_lo = [], []
    for grp in range(N_EXPERTS // per_group):
        target = jnp.zeros((MOE_TILE, LANES), jnp.int32)
        pos = jnp.zeros((MOE_TILE, LANES), jnp.int32) - 1
        hi, lo = jnp.zeros((MOE_TILE, LANES), F32), jnp.zeros((MOE_TILE, LANES), F32)
        for k in range(per_group):
            e = grp * per_group + k
            mine = (lane >= k * head) & (lane < (k + 1) * head)
            target = jnp.where(mine, base_of(t, e) + lane - k * head, target)
            pos = jnp.where(mine, posm_ref[:, e:e + 1], pos)
            hi = jnp.where(mine, a_hi[:, e:e + 1], hi)
            lo = jnp.where(mine, a_lo[:, e:e + 1], lo)
        hit = pos == target
        w_hi.append(jnp.where(hit, hi, 0.0).astype(BF16))
        w_lo.append(jnp.where(hit, lo, 0.0).astype(BF16))
    y_heads = hbuf_ref[slot]
    acc_ref[...] = (h_ref[...] + _dot(jnp.concatenate(w_hi, axis=1), y_heads)
                    + _dot(jnp.concatenate(w_lo, axis=1), y_heads))

    any_rest = needs_rest(t, 0)
    for e in range(1, N_EXPERTS):
        any_rest = any_rest | needs_rest(t, e)

    @pl.when(any_rest)
    def _():
        for e in range(N_EXPERTS):
            @pl.when(needs_rest(t, e))
            def _():
                rest_copy(t, e, slot).wait()
                acc_ref[...] += expand(e, head, MOE_TILE) + expand(e, MOE_TILE, win)

    acc = acc_ref[...]
    if final_norm:
        acc = _rms(acc, g_ref[...])
    o_ref[...] = acc


def _combine(h, aff, posm_t, start, y, cap, final_gain):
    T = h.shape[0]
    nt = T // MOE_TILE
    win = _combine_window()
    assert cap >= win and (cap - win) % BF16_SUBLANES == 0
    final_norm = final_gain is not None
    gain = (final_gain if final_norm else jnp.ones((D_MODEL,), F32)).reshape(1, D_MODEL)
    grid_spec = pltpu.PrefetchScalarGridSpec(
        num_scalar_prefetch=1,
        grid=(nt,),
        in_specs=[
            pl.BlockSpec((MOE_TILE, D_MODEL), lambda t, s: (t, 0)),
            pl.BlockSpec((MOE_TILE, N_EXPERTS), lambda t, s: (t, 0)),
            pl.BlockSpec((MOE_TILE, N_EXPERTS), lambda t, s: (t, 0)),
            pl.BlockSpec((1, D_MODEL), lambda t, s: (0, 0)),
            pl.BlockSpec(memory_space=pl.ANY),
        ],
        out_specs=pl.BlockSpec((MOE_TILE, D_MODEL), lambda t, s: (t, 0)),
        scratch_shapes=[pltpu.VMEM((2, N_EXPERTS * COMBINE_HEAD, D_MODEL), BF16),
                        pltpu.VMEM((2, N_EXPERTS, win - COMBINE_HEAD, D_MODEL), BF16),
                        pltpu.SemaphoreType.DMA((2, 2, N_EXPERTS)), pltpu.VMEM((MOE_TILE, D_MODEL), F32)],
    )
    return pl.pallas_call(
        functools.partial(_combine_kernel, cap=cap, nt=nt, final_norm=final_norm),
        grid_spec=grid_spec,
        out_shape=jax.ShapeDtypeStruct((T, D_MODEL), F32),
        compiler_params=_cparams(("arbitrary",)),
        name="moe_combine",
    )(start.reshape(-1), h, aff, posm_t, gain, y)


def _ec_moe(h, gain, w_router, w_gate, w_up, w_down, layer, final_gain=None):
    T = h.shape[0]
    cap = max(1, EC_CAPACITY_FACTOR * T // N_EXPERTS)
    aff, hn = _router(h, gain, w_router)
    posm, before = _select(aff.T, cap)
    start = before[:, ::MOE_TILE]
    x = _dispatch(hn, posm, start, cap)
    y = _ffn(x, w_gate, w_up, w_down, layer)
    start_end = jnp.concatenate([start, jnp.full((N_EXPERTS, 1), cap, jnp.int32)], axis=1)
    return _combine(h, aff, posm.T, start_end, y, cap, final_gain)


def _rms_cols(x, g):
    return x * lax.rsqrt(jnp.mean(x * x, axis=0, keepdims=True) + RMS_EPS) * g


MLA_QCOLS = MLA_NOPE + 2 * LANES


def _mla_in_kernel(h_ref, posc_ref, posr_ref, g_ref, win_ref, wintkv_ref, qn_ref, wq_ref, wukt_ref, kvn_ref,
                   kvnc_ref, freqr_ref, sign_ref, freqc_ref, q_out, kt_out, va_out):
    half = MLA_ROPE // 2
    hn = _rms(h_ref[...], g_ref[...]).astype(BF16)
    c = _dot(hn, win_ref[...])
    ckv_t = _dot_nt(wintkv_ref[...], hn)

    cq = _rms(c[:, :MLA_Q_RANK], qn_ref[...]).astype(BF16)
    qa = _dot(cq, wq_ref[...])
    ang = posc_ref[...].astype(F32) * freqr_ref[...]
    cos, sin_signed = jnp.cos(ang), jnp.sin(ang) * sign_ref[...]
    qscale = MLA_QK ** -0.5 * math.log2(math.e)
    for hd in range(MLA_HEADS):
        c0 = hd * MLA_QCOLS
        q_lat = _dot(qa[:, c0:c0 + MLA_NOPE].astype(BF16), wukt_ref[hd])
        rot = qa[:, c0 + MLA_NOPE:c0 + MLA_NOPE + LANES] * cos + qa[:, c0 + MLA_NOPE + LANES:c0 + MLA_QCOLS] * sin_signed
        q_out[hd, :, :MLA_NOPE] = (q_lat * qscale).astype(BF16)
        q_out[hd, :, MLA_NOPE:] = (rot[:, :MLA_ROPE] * qscale).astype(BF16)

    kt_out[:MLA_KV_RANK, :] = _rms_cols(ckv_t[:MLA_KV_RANK], kvnc_ref[...]).astype(BF16)
    ang_t = freqc_ref[...] * posr_ref[...].astype(F32)
    cos_t, sin_t = jnp.cos(ang_t), jnp.sin(ang_t)
    k1, k2 = ckv_t[MLA_KV_RANK:MLA_KV_RANK + half], ckv_t[MLA_KV_RANK + half:]
    kt_out[MLA_KV_RANK:MLA_KV_RANK + half, :] = (k1 * cos_t - k2 * sin_t).astype(BF16)
    kt_out[MLA_KV_RANK + half:, :] = (k1 * sin_t + k2 * cos_t).astype(BF16)

    ckv = _rms(c[:, MLA_Q_RANK:MLA_Q_RANK + MLA_KV_RANK], kvn_ref[...])
    va_out[:, :MLA_KV_RANK] = ckv.astype(BF16)
    lane = lax.broadcasted_iota(jnp.int32, (ckv.shape[0], LANES), 1)
    va_out[:, MLA_KV_RANK:] = jnp.where(lane == 0, 1.0, 0.0).astype(BF16)


def _mla_in(h, positions, gain, w_in, q_norm, w_uq, kv_norm, w_ukv, tm=256):
    T = h.shape[0]
    H, half = MLA_HEADS, MLA_ROPE // 2
    wq = w_uq.reshape(MLA_Q_RANK, H, MLA_QK)
    x1, x2 = wq[:, :, MLA_NOPE:MLA_NOPE + half], wq[:, :, MLA_NOPE + half:]
    pad = jnp.zeros((MLA_Q_RANK, H, LANES - MLA_ROPE), F32)
    wq_wide = jnp.concatenate([wq[:, :, :MLA_NOPE], x1, x2, pad, x2, x1, pad], axis=2)
    wq_wide = wq_wide.reshape(MLA_Q_RANK, H * MLA_QCOLS).astype(BF16)
    wuk_t = w_ukv.reshape(MLA_KV_RANK, H, MLA_NOPE + MLA_V)[:, :, :MLA_NOPE].transpose(1, 2, 0).astype(BF16)
    inv_freq = ROPE_BASE ** (-jnp.arange(half, dtype=F32) / half)
    zeros = jnp.zeros((LANES - MLA_ROPE,), F32)
    freq_row = jnp.concatenate([inv_freq, inv_freq, zeros]).reshape(1, LANES)
    sign_row = jnp.concatenate([-jnp.ones((half,), F32), jnp.ones((half,), F32), zeros]).reshape(1, LANES)
    w_in_b = w_in.astype(BF16)
    args = [h, positions.reshape(T, 1), positions.reshape(1, T), gain.reshape(1, -1), w_in_b,
            w_in_b[:, MLA_Q_RANK:].T, q_norm.reshape(1, -1), wq_wide, wuk_t, kv_norm.reshape(1, -1),
            kv_norm.reshape(-1, 1), freq_row, sign_row, inv_freq.reshape(half, 1)]
    full = lambda a: pl.BlockSpec(a.shape, lambda i: (0,) * a.ndim)
    in_specs = [pl.BlockSpec((tm, D_MODEL), lambda i: (i, 0)), pl.BlockSpec((tm, 1), lambda i: (i, 0)),
                pl.BlockSpec((1, tm), lambda i: (0, i))] + [full(a) for a in args[3:]]
    return pl.pallas_call(
        _mla_in_kernel,
        grid=(T // tm,),
        in_specs=in_specs,
        out_specs=[pl.BlockSpec((H, tm, MLA_QK), lambda i: (0, i, 0)),
                   pl.BlockSpec((MLA_QK, tm), lambda i: (0, i)),
                   pl.BlockSpec((tm, 2 * LANES), lambda i: (i, 0))],
        out_shape=[jax.ShapeDtypeStruct((H, T, MLA_QK), BF16), jax.ShapeDtypeStruct((MLA_QK, T), BF16),
                   jax.ShapeDtypeStruct((T, 2 * LANES), BF16)],
        compiler_params=_cparams(("arbitrary",)),
        name="mla_in",
    )(*args)


def _flash_kernel(q_ref, qn_ref, kt_ref, va_ref, o_ref, sa_ref, sb_ref, pa_ref, pb_ref, acc_ref, m_ref, alpha_ref,
                  *, tk, nblk):
    H, tq, _ = q_ref.shape
    T = kt_ref.shape[1]
    n = T // tk
    q = q_ref[...].reshape(H * tq, MLA_QK)
    q_next = qn_ref[...].reshape(H * tq, MLA_QK)

    def k_tile(j):
        return kt_ref[:, j * tk:(j + 1) * tk]

    def v_tile(j):
        return va_ref[j * tk:(j + 1) * tk, :]

    def lanes_rep(x, width):
        return jnp.concatenate([x] * (width // LANES), axis=1)

    blk = H * tq // nblk
    row_blocks = [slice(rb * blk, (rb + 1) * blk) for rb in range(nblk)]

    def add_pv(rows, p_ref, j):
        acc_ref[rows, :] = (lanes_rep(alpha_ref[rows, :], 2 * LANES) * acc_ref[rows, :]
                            + _dot(p_ref[rows, :], v_tile(j)))

    def step(j, s_cur, s_nxt, p_cur, p_prv):
        q_sel, k_nxt = (q, k_tile(j + 1)) if j + 1 < n else (q_next, k_tile(0))
        for rows in row_blocks:
            if j > 0:
                add_pv(rows, p_prv, j - 1)
            s_nxt[rows, :] = _dot(q_sel[rows], k_nxt)
            s = s_cur[rows, :]
            m_old = m_ref[rows, :]
            m_new = jnp.maximum(m_old, jnp.max(s, axis=1, keepdims=True))
            alpha_ref[rows, :] = jnp.exp2(m_old - m_new)
            m_ref[rows, :] = m_new
            p_cur[rows, :] = jnp.exp2(s - lanes_rep(m_new, tk)).astype(BF16)

    acc_ref[...] = jnp.zeros_like(acc_ref)
    m_ref[...] = jnp.full_like(m_ref, -jnp.inf)

    @pl.when(pl.program_id(0) == 0)
    def _():
        sa_ref[...] = _dot(q, k_tile(0))

    for j in range(n):
        if j % 2 == 0:
            step(j, sa_ref, sb_ref, pa_ref, pb_ref)
        else:
            step(j, sb_ref, sa_ref, pb_ref, pa_ref)
    p_last = pb_ref if n % 2 == 0 else pa_ref
    for rows in row_blocks:
        add_pv(rows, p_last, n - 1)
    acc = acc_ref[...]
    o_lat = (acc[:, :MLA_KV_RANK] / acc[:, MLA_KV_RANK:MLA_KV_RANK + 1]).astype(BF16)
    for hd in range(H):
        o_ref[:, hd * MLA_KV_RANK:(hd + 1) * MLA_KV_RANK] = o_lat[hd * tq:(hd + 1) * tq]


def _flash(q, kt, va, tq=64, tk=1024):
    H, T, _ = q.shape
    tq, tk = min(tq, T), min(tk, T)
    assert T % (2 * tk) == 0 and tq % BF16_SUBLANES == 0
    rows = H * tq
    steps = T // tq
    return pl.pallas_call(
        functools.partial(_flash_kernel, tk=tk, nblk=4),
        grid=(steps,),
        scratch_shapes=[pltpu.VMEM((rows, tk), F32), pltpu.VMEM((rows, tk), F32), pltpu.VMEM((rows, tk), BF16),
                        pltpu.VMEM((rows, tk), BF16), pltpu.VMEM((rows, 2 * LANES), F32),
                        pltpu.VMEM((rows, LANES), F32), pltpu.VMEM((rows, LANES), F32)],
        in_specs=[
            pl.BlockSpec((H, tq, MLA_QK), lambda i: (0, i, 0)),
            pl.BlockSpec((H, tq, MLA_QK), lambda i: (0, jnp.minimum(i + 1, steps - 1), 0)),
            pl.BlockSpec((MLA_QK, T), lambda i: (0, 0)),
            pl.BlockSpec((T, 2 * LANES), lambda i: (0, 0)),
        ],
        out_specs=pl.BlockSpec((tq, H * MLA_KV_RANK), lambda i: (i, 0)),
        out_shape=jax.ShapeDtypeStruct((T, H * MLA_KV_RANK), BF16),
        compiler_params=_cparams(("arbitrary",)),
        name="mla_flash",
    )(q, q, kt, va)


def _mla_out_kernel(o_ref, wuv_ref, w_ref, h_ref, out_ref):
    v = [_dot(o_ref[:, hd * MLA_KV_RANK:(hd + 1) * MLA_KV_RANK], wuv_ref[hd]).astype(BF16)
         for hd in range(MLA_HEADS)]
    out_ref[...] = h_ref[...] + _dot(jnp.concatenate(v, axis=1), w_ref[...])


def _mla_out(o_lat, w_ukv, w_out, h, tm=512):
    T = h.shape[0]
    H = MLA_HEADS
    wuv = w_ukv.reshape(MLA_KV_RANK, H, MLA_NOPE + MLA_V)[:, :, MLA_NOPE:].transpose(1, 0, 2).astype(BF16)
    wb = w_out.astype(BF16)
    return pl.pallas_call(
        _mla_out_kernel,
        grid=(T // tm,),
        in_specs=[pl.BlockSpec((tm, H * MLA_KV_RANK), lambda i: (i, 0)), pl.BlockSpec(wuv.shape, lambda i: (0, 0, 0)),
                  pl.BlockSpec(wb.shape, lambda i: (0, 0)), pl.BlockSpec((tm, D_MODEL), lambda i: (i, 0))],
        out_specs=pl.BlockSpec((tm, D_MODEL), lambda i: (i, 0)),
        out_shape=jax.ShapeDtypeStruct((T, D_MODEL), F32),
        compiler_params=_cparams(("arbitrary",)),
        name="mla_out",
    )(o_lat, wuv, wb, h)


def _mla_mixer(h, positions, gain, w_in, q_norm, w_uq, kv_norm, w_ukv, w_out):
    q, kt, va = _mla_in(h, positions, gain, w_in, q_norm, w_uq, kv_norm, w_ukv)
    o_lat = _flash(q, kt, va)
    return _mla_out(o_lat, w_ukv, w_out, h)


def kernel(x, positions, mix_norm, ffn_norm, final_norm, gla_w_in, gla_w_gate_up_f, gla_b_gate_f, gla_w_gate_up_b,
           gla_b_gate_b, gla_head_norm, gla_w_out, mla_w_in, mla_q_norm, mla_w_uq, mla_kv_norm, mla_w_ukv,
           mla_w_out, moe_w_router, moe_w_gate, moe_w_up, moe_w_down):
    B, T, D = x.shape
    outs = []
    for b in range(B):
        h = x[b]
        h = _gla_mixer(h, mix_norm[0], gla_w_in[0], gla_w_gate_up_f[0], gla_b_gate_f[0], gla_w_gate_up_b[0],
                       gla_b_gate_b[0], gla_head_norm[0], gla_w_out[0])
        h = _ec_moe(h, ffn_norm[0], moe_w_router[0], moe_w_gate, moe_w_up, moe_w_down, 0)
        h = _mla_mixer(h, positions[b], mix_norm[1], mla_w_in[0], mla_q_norm[0], mla_w_uq[0], mla_kv_norm[0],
                       mla_w_ukv[0], mla_w_out[0])
        h = _ec_moe(h, ffn_norm[1], moe_w_router[1], moe_w_gate, moe_w_up, moe_w_down, 1, final_gain=final_norm)
        outs.append(h)
    return jnp.stack(outs)
```

```python
import functools
import math

import numpy as np
import jax
import jax.numpy as jnp
from jax import lax
from jax.experimental import pallas as pl
from jax.experimental.pallas import tpu as pltpu

F32 = jnp.float32
BF16 = jnp.bfloat16

D_MODEL = 1024
RMS_EPS = 1e-6

GLA_HEADS = 4
GLA_DK = 512
GLA_DV = 1024
GLA_HEAD_K = GLA_DK // GLA_HEADS
GLA_HEAD_V = GLA_DV // GLA_HEADS
GLA_GATE_RANK = 16
GLA_TAU = 16.0
GLA_CHUNK = 64
GLA_TILE = 256
GLA_LEVELS = 6

MLA_HEADS = 16
MLA_Q_RANK = 256
MLA_KV_RANK = 128
MLA_NOPE = 128
MLA_ROPE = 64
MLA_V = 128
MLA_QK = MLA_NOPE + MLA_ROPE
ROPE_BASE = 10000.0

N_EXPERTS = 16
EXPERT_FF = 2048
EC_CAPACITY_FACTOR = 2
MOE_TILE = 256
COMBINE_HEAD = 64
BF16_SUBLANES = 16
F32_SUBLANES = 8
LANES = 128

VMEM_LIMIT = 56 * 1024 * 1024


def _cparams(sem):
    return pltpu.CompilerParams(dimension_semantics=sem, vmem_limit_bytes=VMEM_LIMIT)


def _rms(x, g):
    return x * lax.rsqrt(jnp.mean(x * x, axis=-1, keepdims=True) + RMS_EPS) * g


def _split_bf16(x):
    hi = x.astype(BF16)
    lo = (x - hi.astype(F32)).astype(BF16)
    return hi, lo


def _dot(a, b):
    return jnp.dot(a, b, preferred_element_type=F32)


def _dot_nt(a, b):
    return lax.dot_general(a, b, (((1,), (1,)), ((), ())), preferred_element_type=F32)


def _dot_tn(a, b):
    return lax.dot_general(a, b, (((0,), (0,)), ((), ())), preferred_element_type=F32)


def _dot_split(a, b):
    ah, al = _split_bf16(a)
    bh, bl = _split_bf16(b)
    return _dot(ah, bh) + _dot(ah, bl) + _dot(al, bh)


def _gla_in_kernel(x_ref, g_ref, wqk_ref, wv_ref, wr_ref, wgd_ref, qk_ref, v_ref, r_ref, gd_ref):
    hn = _rms(x_ref[...], g_ref[...]).astype(BF16)
    qk_ref[...] = _dot(hn, wqk_ref[...])
    v_ref[...] = _dot(hn, wv_ref[...])
    r_ref[...] = _dot(hn, wr_ref[...])
    gd_ref[...] = _dot(hn, wgd_ref[...])


def _gla_in(x, gain, w_in, tm=512):
    T = x.shape[0]
    wqk = w_in[:, :2 * GLA_DK].astype(BF16)
    wv = w_in[:, 2 * GLA_DK:2 * GLA_DK + GLA_DV].astype(BF16)
    wr = w_in[:, 2 * GLA_DK + GLA_DV:2 * GLA_DK + 2 * GLA_DV].astype(BF16)
    wgd = w_in[:, 2 * GLA_DK + 2 * GLA_DV:].astype(BF16)
    ngd = 2 * GLA_GATE_RANK
    row = lambda n: pl.BlockSpec((tm, n), lambda i: (i, 0))
    full = lambda a: pl.BlockSpec(a.shape, lambda i: (0, 0))
    gain2 = gain.reshape(1, D_MODEL)
    return pl.pallas_call(
        _gla_in_kernel,
        grid=(T // tm,),
        in_specs=[row(D_MODEL), full(gain2), full(wqk), full(wv), full(wr), full(wgd)],
        out_specs=[row(2 * GLA_DK), row(GLA_DV), row(GLA_DV), row(ngd)],
        out_shape=[jax.ShapeDtypeStruct((T, 2 * GLA_DK), F32), jax.ShapeDtypeStruct((T, GLA_DV), F32),
                   jax.ShapeDtypeStruct((T, GLA_DV), F32), jax.ShapeDtypeStruct((T, ngd), F32)],
        compiler_params=_cparams(("arbitrary",)),
        name="gla_in",
    )(x, gain2, wqk, wv, wr, wgd)


def _gla_tables(reverse):
    n, c = GLA_TILE, GLA_CHUNK
    W = np.zeros((GLA_LEVELS + 3, n, n), np.float32)
    L = np.full((n, n), -1, np.int32)
    for t in range(n):
        c0 = (t // c) * c
        tt = t - c0
        for l in range(GLA_LEVELS):
            b = (c // 2) >> l
            p0 = c0 + (tt // (2 * b)) * 2 * b
            mid = p0 + b
            second = t >= mid
            if not reverse:
                if second:
                    W[l, t, mid:t + 1] = 1
                    L[t, p0:mid] = l
                else:
                    W[l, t, t + 1:mid] = 1
            else:
                if second:
                    W[l, t, mid:t] = 1
                else:
                    W[l, t, t:mid] = 1
                    L[t, mid:p0 + 2 * b] = l
        if not reverse:
            W[GLA_LEVELS, t, c0:t + 1] = 1
            W[GLA_LEVELS + 1, t, t + 1:c0 + c] = 1
            L[t, t] = GLA_LEVELS
        else:
            W[GLA_LEVELS, t, t:c0 + c] = 1
            W[GLA_LEVELS + 1, t, c0:t] = 1
        W[GLA_LEVELS + 2, t, c0:c0 + c] = 1
    return W.reshape(-1, n), L


def _gla_scan_kernel(qkf_ref, vf_ref, gdf_ref, qkb_ref, vb_ref, gdb_ref, wupf_ref, bf_ref, wupb_ref, bb_ref,
                     wf_ref, lf_ref, wb_ref, lb_ref, of_ref, ob_ref, s_ref):
    n, c, r = GLA_TILE, GLA_CHUNK, GLA_GATE_RANK

    @pl.when(pl.program_id(0) == 0)
    def _():
        s_ref[...] = jnp.zeros_like(s_ref)

    dirs = [(qkf_ref, vf_ref, gdf_ref[:, :r], wupf_ref, bf_ref, wf_ref, lf_ref, of_ref, False),
            (qkb_ref, vb_ref, gdb_ref[:, r:], wupb_ref, bb_ref, wb_ref, lb_ref, ob_ref, True)]

    factors = []
    for qk_ref, v_ref, gd, wup_ref, b_ref, w_ref, l_ref, o_ref, reverse in dirs:
        z = _dot_split(gd, wup_ref[...]) + b_ref[...]
        g = (jnp.minimum(z, 0.0) - jnp.log1p(jnp.exp(-jnp.abs(z)))) * (1.0 / GLA_TAU)
        ghi, glo = _split_bf16(g)
        w = w_ref[...]
        factors.append(jnp.exp(_dot(w, ghi) + _dot(w, glo)))

    chains = []
    for d, (qk_ref, v_ref, gd, wup_ref, b_ref, w_ref, l_ref, o_ref, reverse) in enumerate(dirs):
        lvl = l_ref[...]
        for h in range(GLA_HEADS):
            kcols = slice(h * GLA_HEAD_K, (h + 1) * GLA_HEAD_K)
            vcols = slice(h * GLA_HEAD_V, (h + 1) * GLA_HEAD_V)
            f = factors[d][:, kcols]
            q = qk_ref[:, kcols] * (GLA_HEAD_K ** -0.5)
            k = qk_ref[:, GLA_DK + h * GLA_HEAD_K:GLA_DK + (h + 1) * GLA_HEAD_K]
            vb = v_ref[:, vcols].astype(BF16)
            attn = jnp.zeros((n, n), F32)
            for l in range(GLA_LEVELS):
                fl = f[l * n:(l + 1) * n]
                p = _dot_nt((q * fl).astype(BF16), (k * fl).astype(BF16))
                attn = jnp.where(lvl == l, p, attn)
            if not reverse:
                p = _dot_nt(q.astype(BF16), k.astype(BF16))
                attn = jnp.where(lvl == GLA_LEVELS, p, attn)
            o_intra = _dot(attn.astype(BF16), vb)
            qh = (q * f[GLA_LEVELS * n:(GLA_LEVELS + 1) * n]).astype(BF16)
            kh = (k * f[(GLA_LEVELS + 1) * n:(GLA_LEVELS + 2) * n]).astype(BF16)
            ftot = f[(GLA_LEVELS + 2) * n:(GLA_LEVELS + 3) * n]
            chains.append(dict(d=d, h=h, vcols=vcols, o_ref=o_ref, reverse=reverse, vb=vb, o_intra=o_intra, qh=qh,
                               kh=kh, ftot=ftot, st=s_ref[d, h]))

    nchunks = n // c
    for step in range(nchunks):
        for ch in chains:
            j = nchunks - 1 - step if ch["reverse"] else step
            rows = slice(j * c, (j + 1) * c)
            st = ch["st"]
            ch["o_ref"][rows, ch["vcols"]] = ch["o_intra"][rows] + _dot_nt(ch["qh"][rows], st.astype(BF16))
            ch["st"] = st * ch["ftot"][j * c:j * c + 1, :] + _dot_tn(ch["vb"][rows], ch["kh"][rows])
    for ch in chains:
        s_ref[ch["d"], ch["h"]] = ch["st"]


def _gla_scan(qk, v, gd, w_up_f, b_f, w_up_b, b_b):
    T = qk.shape[0]
    n = GLA_TILE
    nt = T // n
    tables = []
    for reverse in (False, True):
        W, L = _gla_tables(reverse)
        tables += [jnp.asarray(W, BF16), jnp.asarray(L)]
    fwd = lambda width: pl.BlockSpec((n, width), lambda i: (i, 0))
    bwd = lambda width: pl.BlockSpec((n, width), lambda i: (nt - 1 - i, 0))
    full = lambda a: pl.BlockSpec(a.shape, lambda i: (0, 0))
    consts = [w_up_f, b_f.reshape(1, GLA_DK), w_up_b, b_b.reshape(1, GLA_DK)] + tables
    ngd = 2 * GLA_GATE_RANK
    out = jax.ShapeDtypeStruct((T, GLA_DV), F32)
    return pl.pallas_call(
        _gla_scan_kernel,
        grid=(nt,),
        in_specs=[fwd(2 * GLA_DK), fwd(GLA_DV), fwd(ngd), bwd(2 * GLA_DK), bwd(GLA_DV), bwd(ngd)]
                 + [full(a) for a in consts],
        out_specs=[fwd(GLA_DV), bwd(GLA_DV)],
        out_shape=[out, out],
        scratch_shapes=[pltpu.VMEM((2, GLA_HEADS, GLA_HEAD_V, GLA_HEAD_K), F32)],
        compiler_params=_cparams(("arbitrary",)),
        name="gla_scan",
    )(qk, v, gd, qk, v, gd, *consts)


def _gla_out_kernel(of_ref, ob_ref, r_ref, x_ref, hn_ref, w_ref, o_ref):
    acc = x_ref[...]
    for h in range(GLA_HEADS):
        cols = slice(h * GLA_HEAD_V, (h + 1) * GLA_HEAD_V)
        o = _rms(of_ref[:, cols] + ob_ref[:, cols], hn_ref[...])
        r = r_ref[:, cols]
        gated = o * (r * (1.0 / (1.0 + jnp.exp(-r))))
        acc = acc + _dot(gated.astype(BF16), w_ref[cols, :])
    o_ref[...] = acc


def _gla_out(of, ob, r, x, head_norm, w_out, tm=512):
    T = x.shape[0]
    w = w_out.astype(BF16)
    hn = head_norm.reshape(1, GLA_HEAD_V)
    row = lambda n: pl.BlockSpec((tm, n), lambda i: (i, 0))
    full = lambda a: pl.BlockSpec(a.shape, lambda i: (0, 0))
    return pl.pallas_call(
        _gla_out_kernel,
        grid=(T // tm,),
        in_specs=[row(GLA_DV), row(GLA_DV), row(GLA_DV), row(D_MODEL), full(hn), full(w)],
        out_specs=row(D_MODEL),
        out_shape=jax.ShapeDtypeStruct((T, D_MODEL), F32),
        compiler_params=_cparams(("arbitrary",)),
        name="gla_out",
    )(of, ob, r, x, hn, w)


def _gla_mixer(x, gain, w_in, w_up_f, b_f, w_up_b, b_b, head_norm, w_out):
    qk, v, r, gd = _gla_in(x, gain, w_in)
    of, ob = _gla_scan(qk, v, gd, w_up_f, b_f, w_up_b, b_b)
    return _gla_out(of, ob, r, x, head_norm, w_out)


def _router_kernel(h_ref, g_ref, w_ref, aff_ref, hn_ref):
    hn = _rms(h_ref[...], g_ref[...])
    hn_ref[...] = hn.astype(BF16)
    logits = _dot_split(hn, w_ref[...])
    e = jnp.exp(logits - jnp.max(logits, axis=-1, keepdims=True))
    aff_ref[...] = e / jnp.sum(e, axis=-1, keepdims=True)


def _router(h, gain, w_router, tm=512):
    T = h.shape[0]
    gain2 = gain.reshape(1, D_MODEL)
    row = lambda n: pl.BlockSpec((tm, n), lambda i: (i, 0))
    full = lambda a: pl.BlockSpec(a.shape, lambda i: (0, 0))
    return pl.pallas_call(
        _router_kernel,
        grid=(T // tm,),
        in_specs=[row(D_MODEL), full(gain2), full(w_router)],
        out_specs=[row(N_EXPERTS), row(D_MODEL)],
        out_shape=[jax.ShapeDtypeStruct((T, N_EXPERTS), F32), jax.ShapeDtypeStruct((T, D_MODEL), BF16)],
        compiler_params=_cparams(("arbitrary",)),
        name="moe_router",
    )(h, gain2, w_router)


def _select_kernel(aff_ref, tri_ref, posm_ref, before_ref, *, cap):
    T = aff_ref.shape[1]
    bits = pltpu.bitcast(aff_ref[...], jnp.int32)

    def search(it, thr):
        cand = thr | jnp.left_shift(jnp.int32(1), 30 - it)
        cnt = jnp.sum(jnp.where(bits >= cand, 1.0, 0.0), axis=1, keepdims=True)
        return jnp.where(cnt >= cap, cand, thr)

    thr = lax.fori_loop(0, 31, search, jnp.zeros((N_EXPERTS, 1), jnp.int32))
    n_gt = jnp.sum(jnp.where(bits > thr, 1.0, 0.0), axis=1, keepdims=True)
    need = cap - n_gt
    tri = tri_ref[...]

    def scan(j, carry):
        c_eq, c_sel = carry
        cols = pl.ds(pl.multiple_of(j * LANES, LANES), LANES)
        blk = pltpu.bitcast(aff_ref[:, cols], jnp.int32)
        eq = jnp.where(blk == thr, 1.0, 0.0)
        rank = _dot(eq.astype(BF16), tri) + c_eq - eq
        sel = jnp.where((blk > thr) | ((eq > 0.0) & (rank < need)), 1.0, 0.0)
        before = _dot(sel.astype(BF16), tri) + c_sel - sel
        before_ref[:, cols] = before.astype(jnp.int32)
        posm_ref[:, cols] = jnp.where(sel > 0.0, before, -1.0).astype(jnp.int32)
        return (c_eq + jnp.sum(eq, axis=1, keepdims=True), c_sel + jnp.sum(sel, axis=1, keepdims=True))

    zero = jnp.zeros((N_EXPERTS, 1), F32)
    lax.fori_loop(0, T // LANES, scan, (zero, zero))


def _select(aff_t, cap):
    T = aff_t.shape[1]
    tri = jnp.asarray(np.triu(np.ones((LANES, LANES), np.float32)), BF16)
    full = lambda a: pl.BlockSpec(a.shape, lambda: (0,) * a.ndim)
    out = jax.ShapeDtypeStruct((N_EXPERTS, T), jnp.int32)
    return pl.pallas_call(
        functools.partial(_select_kernel, cap=cap),
        in_specs=[full(aff_t), full(tri)],
        out_specs=[pl.BlockSpec((N_EXPERTS, T), lambda: (0, 0))] * 2,
        out_shape=[out, out],
        compiler_params=pltpu.CompilerParams(vmem_limit_bytes=VMEM_LIMIT),
        name="moe_select",
    )(aff_t, tri)


def _dispatch_window():
    return MOE_TILE + F32_SUBLANES


def _dispatch_kernel(start_ref, hn_ref, posm_ref, x_ref, acc_ref, *, cap, nt, sub):
    e, t = pl.program_id(0), pl.program_id(1)
    win = _dispatch_window()

    @pl.when(t == 0)
    def _():
        acc_ref[...] = jnp.zeros_like(acc_ref)

    for s in range(sub):
        tok = slice(s * MOE_TILE, (s + 1) * MOE_TILE)
        base = pl.multiple_of((start_ref[e * nt + t * sub + s] // F32_SUBLANES) * F32_SUBLANES, F32_SUBLANES)
        slot = base + lax.broadcasted_iota(jnp.int32, (win, MOE_TILE), 0)
        onehot = jnp.where(posm_ref[0, :, tok] == slot, 1.0, 0.0).astype(BF16)
        acc_ref[pl.ds(base, win), :] += _dot(onehot, hn_ref[tok, :])

    @pl.when(t == pl.num_programs(1) - 1)
    def _():
        x_ref[0] = acc_ref[:cap, :].astype(BF16)


def _dispatch(hn, posm, start, cap, sub=8):
    T = hn.shape[0]
    nt = T // MOE_TILE
    sub = math.gcd(sub, nt)
    blk = sub * MOE_TILE
    posm3 = posm.reshape(N_EXPERTS, 1, T)
    grid_spec = pltpu.PrefetchScalarGridSpec(
        num_scalar_prefetch=1,
        grid=(N_EXPERTS, nt // sub),
        in_specs=[
            pl.BlockSpec((blk, D_MODEL), lambda e, t, s: (t, 0)),
            pl.BlockSpec((1, 1, blk), lambda e, t, s: (e, 0, t)),
        ],
        out_specs=pl.BlockSpec((1, cap, D_MODEL), lambda e, t, s: (e, 0, 0)),
        scratch_shapes=[pltpu.VMEM((cap + _dispatch_window(), D_MODEL), F32)],
    )
    return pl.pallas_call(
        functools.partial(_dispatch_kernel, cap=cap, nt=nt, sub=sub),
        grid_spec=grid_spec,
        out_shape=jax.ShapeDtypeStruct((N_EXPERTS, cap, D_MODEL), BF16),
        compiler_params=_cparams(("arbitrary", "arbitrary")),
        name="moe_dispatch",
    )(start.reshape(-1), hn, posm3)


def _ffn_kernel(x_ref, wg_ref, wu_ref, wd_ref, y_ref, acc_ref):
    f = pl.program_id(1)

    @pl.when(f == 0)
    def _():
        acc_ref[...] = jnp.zeros_like(acc_ref)

    wg = wg_ref[0, 0].astype(BF16)
    wu = wu_ref[0, 0].astype(BF16)
    wd = wd_ref[0, 0].astype(BF16)
    cap = x_ref.shape[1]
    blk = cap // math.gcd(cap // F32_SUBLANES, 4)
    for r0 in range(0, cap, blk):
        rows = slice(r0, r0 + blk)
        x = x_ref[0, rows, :]
        a = _dot(x, wg)
        u = _dot(x, wu)
        mid = (a * (1.0 / (1.0 + jnp.exp(-a))) * u).astype(BF16)
        acc_ref[rows, :] += _dot(mid, wd)

    @pl.when(f == pl.num_programs(1) - 1)
    def _():
        y_ref[0] = acc_ref[...].astype(BF16)


def _ffn(x, w_gate, w_up, w_down, layer, tf=512):
    cap = x.shape[1]
    return pl.pallas_call(
        _ffn_kernel,
        grid=(N_EXPERTS, EXPERT_FF // tf),
        in_specs=[
            pl.BlockSpec((1, cap, D_MODEL), lambda e, f: (e, 0, 0)),
            pl.BlockSpec((1, 1, D_MODEL, tf), lambda e, f: (layer, e, 0, f)),
            pl.BlockSpec((1, 1, D_MODEL, tf), lambda e, f: (layer, e, 0, f)),
            pl.BlockSpec((1, 1, tf, D_MODEL), lambda e, f: (layer, e, f, 0)),
        ],
        out_specs=pl.BlockSpec((1, cap, D_MODEL), lambda e, f: (e, 0, 0)),
        out_shape=jax.ShapeDtypeStruct((N_EXPERTS, cap, D_MODEL), BF16),
        scratch_shapes=[pltpu.VMEM((cap, D_MODEL), F32)],
        compiler_params=_cparams(("arbitrary", "arbitrary")),
        name="moe_ffn",
    )(x, w_gate, w_up, w_down)


def _combine_window():
    return MOE_TILE + BF16_SUBLANES


def _combine_kernel(start_ref, h_ref, aff_ref, posm_ref, g_ref, y_hbm, o_ref, hbuf_ref, buf_ref, hsem_ref, rsem_ref,
                    acc_ref, *, cap, nt, final_norm):
    t = pl.program_id(0)
    win = _combine_window()
    head = COMBINE_HEAD

    def base_of(tile, e):
        b = (start_ref[e * (nt + 1) + tile] // BF16_SUBLANES) * BF16_SUBLANES
        return pl.multiple_of(jnp.minimum(b, cap - win), BF16_SUBLANES)

    def needs_rest(tile, e):
        return start_ref[e * (nt + 1) + tile + 1] > base_of(tile, e) + head

    def head_copy(tile, e, slot):
        return pltpu.make_async_copy(y_hbm.at[e, pl.ds(base_of(tile, e), head), :],
                                     hbuf_ref.at[slot, pl.ds(e * head, head), :], hsem_ref.at[slot])

    def rest_copy(tile, e, slot):
        return pltpu.make_async_copy(y_hbm.at[e, pl.ds(base_of(tile, e) + head, win - head), :],
                                     buf_ref.at[slot, e], rsem_ref.at[slot, e])

    def fetch(tile, slot):
        for e in range(N_EXPERTS):
            head_copy(tile, e, slot).start()

            @pl.when(needs_rest(tile, e))
            def _():
                rest_copy(tile, e, slot).start()

    slot = t % 2

    @pl.when(t == 0)
    def _():
        fetch(0, 0)

    @pl.when(t + 1 < nt)
    def _():
        fetch(t + 1, 1 - slot)

    def expand(e, lo, hi):
        slots = base_of(t, e) + lo + lax.broadcasted_iota(jnp.int32, (MOE_TILE, hi - lo), 1)
        onehot = jnp.where(posm_ref[:, e:e + 1] == slots, 1.0, 0.0).astype(BF16)
        return aff_ref[:, e:e + 1] * _dot(onehot, buf_ref[slot, e, lo - head:hi - head, :])

    for e in range(N_EXPERTS):
        head_copy(t, e, slot).wait()

    per_group = LANES // head
    a_hi, a_lo = _split_bf16(aff_ref[...])
    a_hi, a_lo = a_hi.astype(F32), a_lo.astype(F32)
    lane = lax.broadcasted_iota(jnp.int32, (MOE_TILE, LANES), 1)
    w_hi, w_lo = [], []
    for grp in range(N_EXPERTS // per_group):
        target = jnp.zeros((MOE_TILE, LANES), jnp.int32)
        pos = jnp.zeros((MOE_TILE, LANES), jnp.int32) - 1
        hi, lo = jnp.zeros((MOE_TILE, LANES), F32), jnp.zeros((MOE_TILE, LANES), F32)
        for k in range(per_group):
            e = grp * per_group + k
            mine = (lane >= k * head) & (lane < (k + 1) * head)
            target = jnp.where(mine, base_of(t, e) + lane - k * head, target)
            pos = jnp.where(mine, posm_ref[:, e:e + 1], pos)
            hi = jnp.where(mine, a_hi[:, e:e + 1], hi)
            lo = jnp.where(mine, a_lo[:, e:e + 1], lo)
        hit = pos == target
        w_hi.append(jnp.where(hit, hi, 0.0).astype(BF16))
        w_lo.append(jnp.where(hit, lo, 0.0).astype(BF16))
    y_heads = hbuf_ref[slot]
    acc_ref[...] = (h_ref[...] + _dot(jnp.concatenate(w_hi, axis=1), y_heads)
                    + _dot(jnp.concatenate(w_lo, axis=1), y_heads))

    any_rest = needs_rest(t, 0)
    for e in range(1, N_EXPERTS):
        any_rest = any_rest | needs_rest(t, e)

    @pl.when(any_rest)
    def _():
        for e in range(N_EXPERTS):
            @pl.when(needs_rest(t, e))
            def _():
                rest_copy(t, e, slot).wait()
                acc_ref[...] += expand(e, head, MOE_TILE) + expand(e, MOE_TILE, win)

    acc = acc_ref[...]
    if final_norm:
        acc = _rms(acc, g_ref[...])
    o_ref[...] = acc


def _combine(h, aff, posm_t, start, y, cap, final_gain):
    T = h.shape[0]
    nt = T // MOE_TILE
    win = _combine_window()
    assert cap >= win and (cap - win) % BF16_SUBLANES == 0
    final_norm = final_gain is not None
    gain = (final_gain if final_norm else jnp.ones((D_MODEL,), F32)).reshape(1, D_MODEL)
    grid_spec = pltpu.PrefetchScalarGridSpec(
        num_scalar_prefetch=1,
        grid=(nt,),
        in_specs=[
            pl.BlockSpec((MOE_TILE, D_MODEL), lambda t, s: (t, 0)),
            pl.BlockSpec((MOE_TILE, N_EXPERTS), lambda t, s: (t, 0)),
            pl.BlockSpec((MOE_TILE, N_EXPERTS), lambda t, s: (t, 0)),
            pl.BlockSpec((1, D_MODEL), lambda t, s: (0, 0)),
            pl.BlockSpec(memory_space=pl.ANY),
        ],
        out_specs=pl.BlockSpec((MOE_TILE, D_MODEL), lambda t, s: (t, 0)),
        scratch_shapes=[pltpu.VMEM((2, N_EXPERTS * COMBINE_HEAD, D_MODEL), BF16),
                        pltpu.VMEM((2, N_EXPERTS, win - COMBINE_HEAD, D_MODEL), BF16),
                        pltpu.SemaphoreType.DMA((2,)), pltpu.SemaphoreType.DMA((2, N_EXPERTS)),
                        pltpu.VMEM((MOE_TILE, D_MODEL), F32)],
    )
    return pl.pallas_call(
        functools.partial(_combine_kernel, cap=cap, nt=nt, final_norm=final_norm),
        grid_spec=grid_spec,
        out_shape=jax.ShapeDtypeStruct((T, D_MODEL), F32),
        compiler_params=_cparams(("arbitrary",)),
        name="moe_combine",
    )(start.reshape(-1), h, aff, posm_t, gain, y)


def _ec_moe(h, gain, w_router, w_gate, w_up, w_down, layer, final_gain=None):
    T = h.shape[0]
    cap = max(1, EC_CAPACITY_FACTOR * T // N_EXPERTS)
    aff, hn = _router(h, gain, w_router)
    posm, before = _select(aff.T, cap)
    start = before[:, ::MOE_TILE]
    x = _dispatch(hn, posm, start, cap)
    y = _ffn(x, w_gate, w_up, w_down, layer)
    start_end = jnp.concatenate([start, jnp.full((N_EXPERTS, 1), cap, jnp.int32)], axis=1)
    return _combine(h, aff, posm.T, start_end, y, cap, final_gain)


def _rms_cols(x, g):
    return x * lax.rsqrt(jnp.mean(x * x, axis=0, keepdims=True) + RMS_EPS) * g


MLA_QCOLS = MLA_NOPE + 2 * LANES


def _mla_in_kernel(h_ref, posc_ref, posr_ref, g_ref, win_ref, wintkv_ref, qn_ref, wq_ref, wukt_ref, kvn_ref,
                   kvnc_ref, freqr_ref, sign_ref, freqc_ref, q_out, kt_out, va_out):
    half = MLA_ROPE // 2
    hn = _rms(h_ref[...], g_ref[...]).astype(BF16)
    c = _dot(hn, win_ref[...])
    ckv_t = _dot_nt(wintkv_ref[...], hn)

    cq = _rms(c[:, :MLA_Q_RANK], qn_ref[...]).astype(BF16)
    qa = _dot(cq, wq_ref[...])
    ang = posc_ref[...].astype(F32) * freqr_ref[...]
    cos, sin_signed = jnp.cos(ang), jnp.sin(ang) * sign_ref[...]
    qscale = MLA_QK ** -0.5 * math.log2(math.e)
    for hd in range(MLA_HEADS):
        c0 = hd * MLA_QCOLS
        q_lat = _dot(qa[:, c0:c0 + MLA_NOPE].astype(BF16), wukt_ref[hd])
        rot = qa[:, c0 + MLA_NOPE:c0 + MLA_NOPE + LANES] * cos + qa[:, c0 + MLA_NOPE + LANES:c0 + MLA_QCOLS] * sin_signed
        q_out[hd, :, :MLA_NOPE] = (q_lat * qscale).astype(BF16)
        q_out[hd, :, MLA_NOPE:] = (rot[:, :MLA_ROPE] * qscale).astype(BF16)

    kt_out[:MLA_KV_RANK, :] = _rms_cols(ckv_t[:MLA_KV_RANK], kvnc_ref[...]).astype(BF16)
    ang_t = freqc_ref[...] * posr_ref[...].astype(F32)
    cos_t, sin_t = jnp.cos(ang_t), jnp.sin(ang_t)
    k1, k2 = ckv_t[MLA_KV_RANK:MLA_KV_RANK + half], ckv_t[MLA_KV_RANK + half:]
    kt_out[MLA_KV_RANK:MLA_KV_RANK + half, :] = (k1 * cos_t - k2 * sin_t).astype(BF16)
    kt_out[MLA_KV_RANK + half:, :] = (k1 * sin_t + k2 * cos_t).astype(BF16)

    ckv = _rms(c[:, MLA_Q_RANK:MLA_Q_RANK + MLA_KV_RANK], kvn_ref[...])
    va_out[:, :MLA_KV_RANK] = ckv.astype(BF16)
    lane = lax.broadcasted_iota(jnp.int32, (ckv.shape[0], LANES), 1)
    va_out[:, MLA_KV_RANK:] = jnp.where(lane == 0, 1.0, 0.0).astype(BF16)


def _mla_in(h, positions, gain, w_in, q_norm, w_uq, kv_norm, w_ukv, tm=256):
    T = h.shape[0]
    H, half = MLA_HEADS, MLA_ROPE // 2
    wq = w_uq.reshape(MLA_Q_RANK, H, MLA_QK)
    x1, x2 = wq[:, :, MLA_NOPE:MLA_NOPE + half], wq[:, :, MLA_NOPE + half:]
    pad = jnp.zeros((MLA_Q_RANK, H, LANES - MLA_ROPE), F32)
    wq_wide = jnp.concatenate([wq[:, :, :MLA_NOPE], x1, x2, pad, x2, x1, pad], axis=2)
    wq_wide = wq_wide.reshape(MLA_Q_RANK, H * MLA_QCOLS).astype(BF16)
    wuk_t = w_ukv.reshape(MLA_KV_RANK, H, MLA_NOPE + MLA_V)[:, :, :MLA_NOPE].transpose(1, 2, 0).astype(BF16)
    inv_freq = ROPE_BASE ** (-jnp.arange(half, dtype=F32) / half)
    zeros = jnp.zeros((LANES - MLA_ROPE,), F32)
    freq_row = jnp.concatenate([inv_freq, inv_freq, zeros]).reshape(1, LANES)
    sign_row = jnp.concatenate([-jnp.ones((half,), F32), jnp.ones((half,), F32), zeros]).reshape(1, LANES)
    w_in_b = w_in.astype(BF16)
    args = [h, positions.reshape(T, 1), positions.reshape(1, T), gain.reshape(1, -1), w_in_b,
            w_in_b[:, MLA_Q_RANK:].T, q_norm.reshape(1, -1), wq_wide, wuk_t, kv_norm.reshape(1, -1),
            kv_norm.reshape(-1, 1), freq_row, sign_row, inv_freq.reshape(half, 1)]
    full = lambda a: pl.BlockSpec(a.shape, lambda i: (0,) * a.ndim)
    in_specs = [pl.BlockSpec((tm, D_MODEL), lambda i: (i, 0)), pl.BlockSpec((tm, 1), lambda i: (i, 0)),
                pl.BlockSpec((1, tm), lambda i: (0, i))] + [full(a) for a in args[3:]]
    return pl.pallas_call(
        _mla_in_kernel,
        grid=(T // tm,),
        in_specs=in_specs,
        out_specs=[pl.BlockSpec((H, tm, MLA_QK), lambda i: (0, i, 0)),
                   pl.BlockSpec((MLA_QK, tm), lambda i: (0, i)),
                   pl.BlockSpec((tm, 2 * LANES), lambda i: (i, 0))],
        out_shape=[jax.ShapeDtypeStruct((H, T, MLA_QK), BF16), jax.ShapeDtypeStruct((MLA_QK, T), BF16),
                   jax.ShapeDtypeStruct((T, 2 * LANES), BF16)],
        compiler_params=_cparams(("arbitrary",)),
        name="mla_in",
    )(*args)


def _flash_kernel(q_ref, qn_ref, kt_ref, va_ref, o_ref, sa_ref, sb_ref, pa_ref, pb_ref, acc_ref, m_ref, alpha_ref,
                  *, tk, nblk):
    H, tq, _ = q_ref.shape
    T = kt_ref.shape[1]
    n = T // tk
    q = q_ref[...].reshape(H * tq, MLA_QK)
    q_next = qn_ref[...].reshape(H * tq, MLA_QK)

    def k_tile(j):
        return kt_ref[:, j * tk:(j + 1) * tk]

    def v_tile(j):
        return va_ref[j * tk:(j + 1) * tk, :]

    def lanes_rep(x, width):
        return jnp.concatenate([x] * (width // LANES), axis=1)

    blk = H * tq // nblk
    row_blocks = [slice(rb * blk, (rb + 1) * blk) for rb in range(nblk)]

    def add_pv(rows, p_ref, j):
        acc_ref[rows, :] = (lanes_rep(alpha_ref[rows, :], 2 * LANES) * acc_ref[rows, :]
                            + _dot(p_ref[rows, :], v_tile(j)))

    def step(j, s_cur, s_nxt, p_cur, p_prv):
        q_sel, k_nxt = (q, k_tile(j + 1)) if j + 1 < n else (q_next, k_tile(0))
        for rows in row_blocks:
            if j > 0:
                add_pv(rows, p_prv, j - 1)
            s_nxt[rows, :] = _dot(q_sel[rows], k_nxt)
            s = s_cur[rows, :]
            m_old = m_ref[rows, :]
            m_new = jnp.maximum(m_old, jnp.max(s, axis=1, keepdims=True))
            alpha_ref[rows, :] = jnp.exp2(m_old - m_new)
            m_ref[rows, :] = m_new
            p_cur[rows, :] = jnp.exp2(s - lanes_rep(m_new, tk)).astype(BF16)

    acc_ref[...] = jnp.zeros_like(acc_ref)
    m_ref[...] = jnp.full_like(m_ref, -jnp.inf)

    @pl.when(pl.program_id(0) == 0)
    def _():
        sa_ref[...] = _dot(q, k_tile(0))

    for j in range(n):
        if j % 2 == 0:
            step(j, sa_ref, sb_ref, pa_ref, pb_ref)
        else:
            step(j, sb_ref, sa_ref, pb_ref, pa_ref)
    p_last = pb_ref if n % 2 == 0 else pa_ref
    for rows in row_blocks:
        add_pv(rows, p_last, n - 1)
    acc = acc_ref[...]
    o_lat = (acc[:, :MLA_KV_RANK] / acc[:, MLA_KV_RANK:MLA_KV_RANK + 1]).astype(BF16)
    for hd in range(H):
        o_ref[:, hd * MLA_KV_RANK:(hd + 1) * MLA_KV_RANK] = o_lat[hd * tq:(hd + 1) * tq]


def _flash(q, kt, va, tq=64, tk=1024):
    H, T, _ = q.shape
    tq, tk = min(tq, T), min(tk, T)
    assert T % (2 * tk) == 0 and tq % BF16_SUBLANES == 0
    rows = H * tq
    steps = T // tq
    return pl.pallas_call(
        functools.partial(_flash_kernel, tk=tk, nblk=4),
        grid=(steps,),
        scratch_shapes=[pltpu.VMEM((rows, tk), F32), pltpu.VMEM((rows, tk), F32), pltpu.VMEM((rows, tk), BF16),
                        pltpu.VMEM((rows, tk), BF16), pltpu.VMEM((rows, 2 * LANES), F32),
                        pltpu.VMEM((rows, LANES), F32), pltpu.VMEM((rows, LANES), F32)],
        in_specs=[
            pl.BlockSpec((H, tq, MLA_QK), lambda i: (0, i, 0)),
            pl.BlockSpec((H, tq, MLA_QK), lambda i: (0, jnp.minimum(i + 1, steps - 1), 0)),
            pl.BlockSpec((MLA_QK, T), lambda i: (0, 0)),
            pl.BlockSpec((T, 2 * LANES), lambda i: (0, 0)),
        ],
        out_specs=pl.BlockSpec((tq, H * MLA_KV_RANK), lambda i: (i, 0)),
        out_shape=jax.ShapeDtypeStruct((T, H * MLA_KV_RANK), BF16),
        compiler_params=_cparams(("arbitrary",)),
        name="mla_flash",
    )(q, q, kt, va)


def _mla_out_kernel(o_ref, wuv_ref, w_ref, h_ref, out_ref):
    v = [_dot(o_ref[:, hd * MLA_KV_RANK:(hd + 1) * MLA_KV_RANK], wuv_ref[hd]).astype(BF16)
         for hd in range(MLA_HEADS)]
    out_ref[...] = h_ref[...] + _dot(jnp.concatenate(v, axis=1), w_ref[...])


def _mla_out(o_lat, w_ukv, w_out, h, tm=512):
    T = h.shape[0]
    H = MLA_HEADS
    wuv = w_ukv.reshape(MLA_KV_RANK, H, MLA_NOPE + MLA_V)[:, :, MLA_NOPE:].transpose(1, 0, 2).astype(BF16)
    wb = w_out.astype(BF16)
    return pl.pallas_call(
        _mla_out_kernel,
        grid=(T // tm,),
        in_specs=[pl.BlockSpec((tm, H * MLA_KV_RANK), lambda i: (i, 0)), pl.BlockSpec(wuv.shape, lambda i: (0, 0, 0)),
                  pl.BlockSpec(wb.shape, lambda i: (0, 0)), pl.BlockSpec((tm, D_MODEL), lambda i: (i, 0))],
        out_specs=pl.BlockSpec((tm, D_MODEL), lambda i: (i, 0)),
        out_shape=jax.ShapeDtypeStruct((T, D_MODEL), F32),
        compiler_params=_cparams(("arbitrary",)),
        name="mla_out",
    )(o_lat, wuv, wb, h)


def _mla_mixer(h, positions, gain, w_in, q_norm, w_uq, kv_norm, w_ukv, w_out):
    q, kt, va = _mla_in(h, positions, gain, w_in, q_norm, w_uq, kv_norm, w_ukv)
    o_lat = _flash(q, kt, va)
    return _mla_out(o_lat, w_ukv, w_out, h)


def kernel(x, positions, mix_norm, ffn_norm, final_norm, gla_w_in, gla_w_gate_up_f, gla_b_gate_f, gla_w_gate_up_b,
           gla_b_gate_b, gla_head_norm, gla_w_out, mla_w_in, mla_q_norm, mla_w_uq, mla_kv_norm, mla_w_ukv,
           mla_w_out, moe_w_router, moe_w_gate, moe_w_up, moe_w_down):
    B, T, D = x.shape
    outs = []
    for b in range(B):
        h = x[b]
        h = _gla_mixer(h, mix_norm[0], gla_w_in[0], gla_w_gate_up_f[0], gla_b_gate_f[0], gla_w_gate_up_b[0],
                       gla_b_gate_b[0], gla_head_norm[0], gla_w_out[0])
        h = _ec_moe(h, ffn_norm[0], moe_w_router[0], moe_w_gate, moe_w_up, moe_w_down, 0)
        h = _mla_mixer(h, positions[b], mix_norm[1], mla_w_in[0], mla_q_norm[0], mla_w_uq[0], mla_kv_norm[0],
                       mla_w_ukv[0], mla_w_out[0])
        h = _ec_moe(h, ffn_norm[1], moe_w_router[1], moe_w_gate, moe_w_up, moe_w_down, 1, final_gain=final_norm)
        outs.append(h)
    return jnp.stack(outs)
```

```python
import functools
import math

import numpy as np
import jax
import jax.numpy as jnp
from jax import lax
from jax.experimental import pallas as pl
from jax.experimental.pallas import tpu as pltpu

F32 = jnp.float32
BF16 = jnp.bfloat16

D_MODEL = 1024
RMS_EPS = 1e-6

GLA_HEADS = 4
GLA_DK = 512
GLA_DV = 1024
GLA_HEAD_K = GLA_DK // GLA_HEADS
GLA_HEAD_V = GLA_DV // GLA_HEADS
GLA_GATE_RANK = 16
GLA_TAU = 16.0
GLA_CHUNK = 64
GLA_TILE = 256
GLA_LEVELS = 6

MLA_HEADS = 16
MLA_Q_RANK = 256
MLA_KV_RANK = 128
MLA_NOPE = 128
MLA_ROPE = 64
MLA_V = 128
MLA_QK = MLA_NOPE + MLA_ROPE
ROPE_BASE = 10000.0

N_EXPERTS = 16
EXPERT_FF = 2048
EC_CAPACITY_FACTOR = 2
MOE_TILE = 256
COMBINE_HEAD = 64
BF16_SUBLANES = 16
F32_SUBLANES = 8
LANES = 128

VMEM_LIMIT = 56 * 1024 * 1024


def _cparams(sem):
    return pltpu.CompilerParams(dimension_semantics=sem, vmem_limit_bytes=VMEM_LIMIT)


def _rms(x, g):
    return x * lax.rsqrt(jnp.mean(x * x, axis=-1, keepdims=True) + RMS_EPS) * g


def _split_bf16(x):
    hi = x.astype(BF16)
    lo = (x - hi.astype(F32)).astype(BF16)
    return hi, lo


def _dot(a, b):
    return jnp.dot(a, b, preferred_element_type=F32)


def _dot_nt(a, b):
    return lax.dot_general(a, b, (((1,), (1,)), ((), ())), preferred_element_type=F32)


def _dot_tn(a, b):
    return lax.dot_general(a, b, (((0,), (0,)), ((), ())), preferred_element_type=F32)


def _dot_split(a, b):
    ah, al = _split_bf16(a)
    bh, bl = _split_bf16(b)
    return _dot(ah, bh) + _dot(ah, bl) + _dot(al, bh)


def _gla_in_kernel(x_ref, g_ref, wqk_ref, wv_ref, wr_ref, wgd_ref, qk_ref, v_ref, r_ref, gd_ref):
    hn = _rms(x_ref[...], g_ref[...]).astype(BF16)
    qk_ref[...] = _dot(hn, wqk_ref[...])
    v_ref[...] = _dot(hn, wv_ref[...])
    r_ref[...] = _dot(hn, wr_ref[...])
    gd_ref[...] = _dot(hn, wgd_ref[...])


def _gla_in(x, gain, w_in, tm=512):
    T = x.shape[0]
    wqk = w_in[:, :2 * GLA_DK].astype(BF16)
    wv = w_in[:, 2 * GLA_DK:2 * GLA_DK + GLA_DV].astype(BF16)
    wr = w_in[:, 2 * GLA_DK + GLA_DV:2 * GLA_DK + 2 * GLA_DV].astype(BF16)
    wgd = w_in[:, 2 * GLA_DK + 2 * GLA_DV:].astype(BF16)
    ngd = 2 * GLA_GATE_RANK
    row = lambda n: pl.BlockSpec((tm, n), lambda i: (i, 0))
    full = lambda a: pl.BlockSpec(a.shape, lambda i: (0, 0))
    gain2 = gain.reshape(1, D_MODEL)
    return pl.pallas_call(
        _gla_in_kernel,
        grid=(T // tm,),
        in_specs=[row(D_MODEL), full(gain2), full(wqk), full(wv), full(wr), full(wgd)],
        out_specs=[row(2 * GLA_DK), row(GLA_DV), row(GLA_DV), row(ngd)],
        out_shape=[jax.ShapeDtypeStruct((T, 2 * GLA_DK), F32), jax.ShapeDtypeStruct((T, GLA_DV), F32),
                   jax.ShapeDtypeStruct((T, GLA_DV), F32), jax.ShapeDtypeStruct((T, ngd), F32)],
        compiler_params=_cparams(("arbitrary",)),
        name="gla_in",
    )(x, gain2, wqk, wv, wr, wgd)


def _gla_tables(reverse):
    n, c = GLA_TILE, GLA_CHUNK
    W = np.zeros((GLA_LEVELS + 3, n, n), np.float32)
    L = np.full((n, n), -1, np.int32)
    for t in range(n):
        c0 = (t // c) * c
        tt = t - c0
        for l in range(GLA_LEVELS):
            b = (c // 2) >> l
            p0 = c0 + (tt // (2 * b)) * 2 * b
            mid = p0 + b
            second = t >= mid
            if not reverse:
                if second:
                    W[l, t, mid:t + 1] = 1
                    L[t, p0:mid] = l
                else:
                    W[l, t, t + 1:mid] = 1
            else:
                if second:
                    W[l, t, mid:t] = 1
                else:
                    W[l, t, t:mid] = 1
                    L[t, mid:p0 + 2 * b] = l
        if not reverse:
            W[GLA_LEVELS, t, c0:t + 1] = 1
            W[GLA_LEVELS + 1, t, t + 1:c0 + c] = 1
            L[t, t] = GLA_LEVELS
        else:
            W[GLA_LEVELS, t, t:c0 + c] = 1
            W[GLA_LEVELS + 1, t, c0:t] = 1
        W[GLA_LEVELS + 2, t, c0:c0 + c] = 1
    return W.reshape(-1, n), L


def _gla_scan_kernel(qkf_ref, vf_ref, gdf_ref, qkb_ref, vb_ref, gdb_ref, wupf_ref, bf_ref, wupb_ref, bb_ref,
                     wf_ref, lf_ref, wb_ref, lb_ref, of_ref, ob_ref, s_ref):
    n, c, r = GLA_TILE, GLA_CHUNK, GLA_GATE_RANK

    @pl.when(pl.program_id(0) == 0)
    def _():
        s_ref[...] = jnp.zeros_like(s_ref)

    dirs = [(qkf_ref, vf_ref, gdf_ref[:, :r], wupf_ref, bf_ref, wf_ref, lf_ref, of_ref, False),
            (qkb_ref, vb_ref, gdb_ref[:, r:], wupb_ref, bb_ref, wb_ref, lb_ref, ob_ref, True)]

    factors = []
    for qk_ref, v_ref, gd, wup_ref, b_ref, w_ref, l_ref, o_ref, reverse in dirs:
        z = _dot_split(gd, wup_ref[...]) + b_ref[...]
        g = (jnp.minimum(z, 0.0) - jnp.log1p(jnp.exp(-jnp.abs(z)))) * (1.0 / GLA_TAU)
        ghi, glo = _split_bf16(g)
        w = w_ref[...]
        factors.append(jnp.exp(_dot(w, ghi) + _dot(w, glo)))

    chains = []
    for d, (qk_ref, v_ref, gd, wup_ref, b_ref, w_ref, l_ref, o_ref, reverse) in enumerate(dirs):
        lvl = l_ref[...]
        for h in range(GLA_HEADS):
            kcols = slice(h * GLA_HEAD_K, (h + 1) * GLA_HEAD_K)
            vcols = slice(h * GLA_HEAD_V, (h + 1) * GLA_HEAD_V)
            f = factors[d][:, kcols]
            q = qk_ref[:, kcols] * (GLA_HEAD_K ** -0.5)
            k = qk_ref[:, GLA_DK + h * GLA_HEAD_K:GLA_DK + (h + 1) * GLA_HEAD_K]
            vb = v_ref[:, vcols].astype(BF16)
            attn = jnp.zeros((n, n), F32)
            for l in range(GLA_LEVELS):
                fl = f[l * n:(l + 1) * n]
                p = _dot_nt((q * fl).astype(BF16), (k * fl).astype(BF16))
                attn = jnp.where(lvl == l, p, attn)
            if not reverse:
                p = _dot_nt(q.astype(BF16), k.astype(BF16))
                attn = jnp.where(lvl == GLA_LEVELS, p, attn)
            o_intra = _dot(attn.astype(BF16), vb)
            qh = (q * f[GLA_LEVELS * n:(GLA_LEVELS + 1) * n]).astype(BF16)
            kh = (k * f[(GLA_LEVELS + 1) * n:(GLA_LEVELS + 2) * n]).astype(BF16)
            ftot = f[(GLA_LEVELS + 2) * n:(GLA_LEVELS + 3) * n]
            chains.append(dict(d=d, h=h, vcols=vcols, o_ref=o_ref, reverse=reverse, vb=vb, o_intra=o_intra, qh=qh,
                               kh=kh, ftot=ftot, st=s_ref[d, h]))

    nchunks = n // c
    for step in range(nchunks):
        for ch in chains:
            j = nchunks - 1 - step if ch["reverse"] else step
            rows = slice(j * c, (j + 1) * c)
            st = ch["st"]
            ch["o_ref"][rows, ch["vcols"]] = ch["o_intra"][rows] + _dot_nt(ch["qh"][rows], st.astype(BF16))
            ch["st"] = st * ch["ftot"][j * c:j * c + 1, :] + _dot_tn(ch["vb"][rows], ch["kh"][rows])
    for ch in chains:
        s_ref[ch["d"], ch["h"]] = ch["st"]


def _gla_scan(qk, v, gd, w_up_f, b_f, w_up_b, b_b):
    T = qk.shape[0]
    n = GLA_TILE
    nt = T // n
    tables = []
    for reverse in (False, True):
        W, L = _gla_tables(reverse)
        tables += [jnp.asarray(W, BF16), jnp.asarray(L)]
    fwd = lambda width: pl.BlockSpec((n, width), lambda i: (i, 0))
    bwd = lambda width: pl.BlockSpec((n, width), lambda i: (nt - 1 - i, 0))
    full = lambda a: pl.BlockSpec(a.shape, lambda i: (0, 0))
    consts = [w_up_f, b_f.reshape(1, GLA_DK), w_up_b, b_b.reshape(1, GLA_DK)] + tables
    ngd = 2 * GLA_GATE_RANK
    out = jax.ShapeDtypeStruct((T, GLA_DV), F32)
    return pl.pallas_call(
        _gla_scan_kernel,
        grid=(nt,),
        in_specs=[fwd(2 * GLA_DK), fwd(GLA_DV), fwd(ngd), bwd(2 * GLA_DK), bwd(GLA_DV), bwd(ngd)]
                 + [full(a) for a in consts],
        out_specs=[fwd(GLA_DV), bwd(GLA_DV)],
        out_shape=[out, out],
        scratch_shapes=[pltpu.VMEM((2, GLA_HEADS, GLA_HEAD_V, GLA_HEAD_K), F32)],
        compiler_params=_cparams(("arbitrary",)),
        name="gla_scan",
    )(qk, v, gd, qk, v, gd, *consts)


def _gla_out_kernel(of_ref, ob_ref, r_ref, x_ref, hn_ref, w_ref, o_ref):
    acc = x_ref[...]
    for h in range(GLA_HEADS):
        cols = slice(h * GLA_HEAD_V, (h + 1) * GLA_HEAD_V)
        o = _rms(of_ref[:, cols] + ob_ref[:, cols], hn_ref[...])
        r = r_ref[:, cols]
        gated = o * (r * (1.0 / (1.0 + jnp.exp(-r))))
        acc = acc + _dot(gated.astype(BF16), w_ref[cols, :])
    o_ref[...] = acc


def _gla_out(of, ob, r, x, head_norm, w_out, tm=512):
    T = x.shape[0]
    w = w_out.astype(BF16)
    hn = head_norm.reshape(1, GLA_HEAD_V)
    row = lambda n: pl.BlockSpec((tm, n), lambda i: (i, 0))
    full = lambda a: pl.BlockSpec(a.shape, lambda i: (0, 0))
    return pl.pallas_call(
        _gla_out_kernel,
        grid=(T // tm,),
        in_specs=[row(GLA_DV), row(GLA_DV), row(GLA_DV), row(D_MODEL), full(hn), full(w)],
        out_specs=row(D_MODEL),
        out_shape=jax.ShapeDtypeStruct((T, D_MODEL), F32),
        compiler_params=_cparams(("arbitrary",)),
        name="gla_out",
    )(of, ob, r, x, hn, w)


def _gla_mixer(x, gain, w_in, w_up_f, b_f, w_up_b, b_b, head_norm, w_out):
    qk, v, r, gd = _gla_in(x, gain, w_in)
    of, ob = _gla_scan(qk, v, gd, w_up_f, b_f, w_up_b, b_b)
    return _gla_out(of, ob, r, x, head_norm, w_out)


def _router_kernel(h_ref, g_ref, w_ref, aff_ref, hn_ref):
    hn = _rms(h_ref[...], g_ref[...])
    hn_ref[...] = hn.astype(BF16)
    logits = _dot_split(hn, w_ref[...])
    e = jnp.exp(logits - jnp.max(logits, axis=-1, keepdims=True))
    aff_ref[...] = e / jnp.sum(e, axis=-1, keepdims=True)


def _router(h, gain, w_router, tm=512):
    T = h.shape[0]
    gain2 = gain.reshape(1, D_MODEL)
    row = lambda n: pl.BlockSpec((tm, n), lambda i: (i, 0))
    full = lambda a: pl.BlockSpec(a.shape, lambda i: (0, 0))
    return pl.pallas_call(
        _router_kernel,
        grid=(T // tm,),
        in_specs=[row(D_MODEL), full(gain2), full(w_router)],
        out_specs=[row(N_EXPERTS), row(D_MODEL)],
        out_shape=[jax.ShapeDtypeStruct((T, N_EXPERTS), F32), jax.ShapeDtypeStruct((T, D_MODEL), BF16)],
        compiler_params=_cparams(("arbitrary",)),
        name="moe_router",
    )(h, gain2, w_router)


def _select_kernel(aff_ref, tri_ref, posm_ref, before_ref, *, cap):
    T = aff_ref.shape[1]
    bits = pltpu.bitcast(aff_ref[...], jnp.int32)

    def search(it, thr):
        cand = thr | jnp.left_shift(jnp.int32(1), 30 - it)
        cnt = jnp.sum(jnp.where(bits >= cand, 1.0, 0.0), axis=1, keepdims=True)
        return jnp.where(cnt >= cap, cand, thr)

    thr = lax.fori_loop(0, 31, search, jnp.zeros((N_EXPERTS, 1), jnp.int32))
    n_gt = jnp.sum(jnp.where(bits > thr, 1.0, 0.0), axis=1, keepdims=True)
    need = cap - n_gt
    tri = tri_ref[...]

    def scan(j, carry):
        c_eq, c_sel = carry
        cols = pl.ds(pl.multiple_of(j * LANES, LANES), LANES)
        blk = pltpu.bitcast(aff_ref[:, cols], jnp.int32)
        eq = jnp.where(blk == thr, 1.0, 0.0)
        rank = _dot(eq.astype(BF16), tri) + c_eq - eq
        sel = jnp.where((blk > thr) | ((eq > 0.0) & (rank < need)), 1.0, 0.0)
        before = _dot(sel.astype(BF16), tri) + c_sel - sel
        before_ref[:, cols] = before.astype(jnp.int32)
        posm_ref[:, cols] = jnp.where(sel > 0.0, before, -1.0).astype(jnp.int32)
        return (c_eq + jnp.sum(eq, axis=1, keepdims=True), c_sel + jnp.sum(sel, axis=1, keepdims=True))

    zero = jnp.zeros((N_EXPERTS, 1), F32)
    lax.fori_loop(0, T // LANES, scan, (zero, zero))


def _select(aff_t, cap):
    T = aff_t.shape[1]
    tri = jnp.asarray(np.triu(np.ones((LANES, LANES), np.float32)), BF16)
    full = lambda a: pl.BlockSpec(a.shape, lambda: (0,) * a.ndim)
    out = jax.ShapeDtypeStruct((N_EXPERTS, T), jnp.int32)
    return pl.pallas_call(
        functools.partial(_select_kernel, cap=cap),
        in_specs=[full(aff_t), full(tri)],
        out_specs=[pl.BlockSpec((N_EXPERTS, T), lambda: (0, 0))] * 2,
        out_shape=[out, out],
        compiler_params=pltpu.CompilerParams(vmem_limit_bytes=VMEM_LIMIT),
        name="moe_select",
    )(aff_t, tri)


def _dispatch_window():
    return MOE_TILE + F32_SUBLANES


def _dispatch_kernel(start_ref, hn_ref, posm_ref, x_ref, acc_ref, *, cap, nt, sub):
    e, t = pl.program_id(0), pl.program_id(1)
    win = _dispatch_window()

    @pl.when(t == 0)
    def _():
        acc_ref[...] = jnp.zeros_like(acc_ref)

    for s in range(sub):
        tok = slice(s * MOE_TILE, (s + 1) * MOE_TILE)
        base = pl.multiple_of((start_ref[e * nt + t * sub + s] // F32_SUBLANES) * F32_SUBLANES, F32_SUBLANES)
        slot = base + lax.broadcasted_iota(jnp.int32, (win, MOE_TILE), 0)
        onehot = jnp.where(posm_ref[0, :, tok] == slot, 1.0, 0.0).astype(BF16)
        acc_ref[pl.ds(base, win), :] += _dot(onehot, hn_ref[tok, :])

    @pl.when(t == pl.num_programs(1) - 1)
    def _():
        x_ref[0] = acc_ref[:cap, :].astype(BF16)


def _dispatch(hn, posm, start, cap, sub=16):
    T = hn.shape[0]
    nt = T // MOE_TILE
    sub = math.gcd(sub, nt)
    blk = sub * MOE_TILE
    posm3 = posm.reshape(N_EXPERTS, 1, T)
    grid_spec = pltpu.PrefetchScalarGridSpec(
        num_scalar_prefetch=1,
        grid=(N_EXPERTS, nt // sub),
        in_specs=[
            pl.BlockSpec((blk, D_MODEL), lambda e, t, s: (t, 0)),
            pl.BlockSpec((1, 1, blk), lambda e, t, s: (e, 0, t)),
        ],
        out_specs=pl.BlockSpec((1, cap, D_MODEL), lambda e, t, s: (e, 0, 0)),
        scratch_shapes=[pltpu.VMEM((cap + _dispatch_window(), D_MODEL), F32)],
    )
    return pl.pallas_call(
        functools.partial(_dispatch_kernel, cap=cap, nt=nt, sub=sub),
        grid_spec=grid_spec,
        out_shape=jax.ShapeDtypeStruct((N_EXPERTS, cap, D_MODEL), BF16),
        compiler_params=_cparams(("arbitrary", "arbitrary")),
        name="moe_dispatch",
    )(start.reshape(-1), hn, posm3)


def _ffn_kernel(x_ref, wg_ref, wu_ref, wd_ref, y_ref, acc_ref):
    f = pl.program_id(1)

    @pl.when(f == 0)
    def _():
        acc_ref[...] = jnp.zeros_like(acc_ref)

    wg = wg_ref[0, 0].astype(BF16)
    wu = wu_ref[0, 0].astype(BF16)
    wd = wd_ref[0, 0].astype(BF16)
    cap = x_ref.shape[1]
    blk = cap // math.gcd(cap // F32_SUBLANES, 4)
    for r0 in range(0, cap, blk):
        rows = slice(r0, r0 + blk)
        x = x_ref[0, rows, :]
        a = _dot(x, wg)
        u = _dot(x, wu)
        mid = (a * (1.0 / (1.0 + jnp.exp(-a))) * u).astype(BF16)
        acc_ref[rows, :] += _dot(mid, wd)

    @pl.when(f == pl.num_programs(1) - 1)
    def _():
        y_ref[0] = acc_ref[...].astype(BF16)


def _ffn(x, w_gate, w_up, w_down, layer, tf=512):
    cap = x.shape[1]
    return pl.pallas_call(
        _ffn_kernel,
        grid=(N_EXPERTS, EXPERT_FF // tf),
        in_specs=[
            pl.BlockSpec((1, cap, D_MODEL), lambda e, f: (e, 0, 0)),
            pl.BlockSpec((1, 1, D_MODEL, tf), lambda e, f: (layer, e, 0, f)),
            pl.BlockSpec((1, 1, D_MODEL, tf), lambda e, f: (layer, e, 0, f)),
            pl.BlockSpec((1, 1, tf, D_MODEL), lambda e, f: (layer, e, f, 0)),
        ],
        out_specs=pl.BlockSpec((1, cap, D_MODEL), lambda e, f: (e, 0, 0)),
        out_shape=jax.ShapeDtypeStruct((N_EXPERTS, cap, D_MODEL), BF16),
        scratch_shapes=[pltpu.VMEM((cap, D_MODEL), F32)],
        compiler_params=_cparams(("arbitrary", "arbitrary")),
        name="moe_ffn",
    )(x, w_gate, w_up, w_down)


def _combine_window():
    return MOE_TILE + BF16_SUBLANES


def _combine_kernel(start_ref, h_ref, aff_ref, posm_ref, g_ref, y_hbm, o_ref, hbuf_ref, buf_ref, hsem_ref, rsem_ref,
                    acc_ref, *, cap, nt, final_norm):
    t = pl.program_id(0)
    win = _combine_window()
    head = COMBINE_HEAD

    def base_of(tile, e):
        b = (start_ref[e * (nt + 1) + tile] // BF16_SUBLANES) * BF16_SUBLANES
        return pl.multiple_of(jnp.minimum(b, cap - win), BF16_SUBLANES)

    def needs_rest(tile, e):
        return start_ref[e * (nt + 1) + tile + 1] > base_of(tile, e) + head

    def head_copy(tile, e, slot):
        return pltpu.make_async_copy(y_hbm.at[e, pl.ds(base_of(tile, e), head), :],
                                     hbuf_ref.at[slot, pl.ds(e * head, head), :], hsem_ref.at[slot])

    def rest_copy(tile, e, slot):
        return pltpu.make_async_copy(y_hbm.at[e, pl.ds(base_of(tile, e) + head, win - head), :],
                                     buf_ref.at[slot, e], rsem_ref.at[slot, e])

    def fetch(tile, slot):
        for e in range(N_EXPERTS):
            head_copy(tile, e, slot).start()

            @pl.when(needs_rest(tile, e))
            def _():
                rest_copy(tile, e, slot).start()

    slot = t % 2

    @pl.when(t == 0)
    def _():
        fetch(0, 0)

    @pl.when(t + 1 < nt)
    def _():
        fetch(t + 1, 1 - slot)

    def expand(e, lo, hi):
        slots = base_of(t, e) + lo + lax.broadcasted_iota(jnp.int32, (MOE_TILE, hi - lo), 1)
        onehot = jnp.where(posm_ref[:, e:e + 1] == slots, 1.0, 0.0).astype(BF16)
        return aff_ref[:, e:e + 1] * _dot(onehot, buf_ref[slot, e, lo - head:hi - head, :])

    for e in range(N_EXPERTS):
        head_copy(t, e, slot).wait()

    per_group = LANES // head
    a_hi, a_lo = _split_bf16(aff_ref[...])
    a_hi, a_lo = a_hi.astype(F32), a_lo.astype(F32)
    lane = lax.broadcasted_iota(jnp.int32, (MOE_TILE, LANES), 1)
    w_hi, w_lo = [], []
    for grp in range(N_EXPERTS // per_group):
        target = jnp.zeros((MOE_TILE, LANES), jnp.int32)
        pos = jnp.zeros((MOE_TILE, LANES), jnp.int32) - 1
        hi, lo = jnp.zeros((MOE_TILE, LANES), F32), jnp.zeros((MOE_TILE, LANES), F32)
        for k in range(per_group):
            e = grp * per_group + k
            mine = (lane >= k * head) & (lane < (k + 1) * head)
            target = jnp.where(mine, base_of(t, e) + lane - k * head, target)
            pos = jnp.where(mine, posm_ref[:, e:e + 1], pos)
            hi = jnp.where(mine, a_hi[:, e:e + 1], hi)
            lo = jnp.where(mine, a_lo[:, e:e + 1], lo)
        hit = pos == target
        w_hi.append(jnp.where(hit, hi, 0.0).astype(BF16))
        w_lo.append(jnp.where(hit, lo, 0.0).astype(BF16))
    y_heads = hbuf_ref[slot]
    acc_ref[...] = (h_ref[...] + _dot(jnp.concatenate(w_hi, axis=1), y_heads)
                    + _dot(jnp.concatenate(w_lo, axis=1), y_heads))

    any_rest = needs_rest(t, 0)
    for e in range(1, N_EXPERTS):
        any_rest = any_rest | needs_rest(t, e)

    @pl.when(any_rest)
    def _():
        for e in range(N_EXPERTS):
            @pl.when(needs_rest(t, e))
            def _():
                rest_copy(t, e, slot).wait()
                acc_ref[...] += expand(e, head, MOE_TILE) + expand(e, MOE_TILE, win)

    acc = acc_ref[...]
    if final_norm:
        acc = _rms(acc, g_ref[...])
    o_ref[...] = acc


def _combine(h, aff, posm_t, start, y, cap, final_gain):
    T = h.shape[0]
    nt = T // MOE_TILE
    win = _combine_window()
    assert cap >= win and (cap - win) % BF16_SUBLANES == 0
    final_norm = final_gain is not None
    gain = (final_gain if final_norm else jnp.ones((D_MODEL,), F32)).reshape(1, D_MODEL)
    grid_spec = pltpu.PrefetchScalarGridSpec(
        num_scalar_prefetch=1,
        grid=(nt,),
        in_specs=[
            pl.BlockSpec((MOE_TILE, D_MODEL), lambda t, s: (t, 0)),
            pl.BlockSpec((MOE_TILE, N_EXPERTS), lambda t, s: (t, 0)),
            pl.BlockSpec((MOE_TILE, N_EXPERTS), lambda t, s: (t, 0)),
            pl.BlockSpec((1, D_MODEL), lambda t, s: (0, 0)),
            pl.BlockSpec(memory_space=pl.ANY),
        ],
        out_specs=pl.BlockSpec((MOE_TILE, D_MODEL), lambda t, s: (t, 0)),
        scratch_shapes=[pltpu.VMEM((2, N_EXPERTS * COMBINE_HEAD, D_MODEL), BF16),
                        pltpu.VMEM((2, N_EXPERTS, win - COMBINE_HEAD, D_MODEL), BF16),
                        pltpu.SemaphoreType.DMA((2,)), pltpu.SemaphoreType.DMA((2, N_EXPERTS)),
                        pltpu.VMEM((MOE_TILE, D_MODEL), F32)],
    )
    return pl.pallas_call(
        functools.partial(_combine_kernel, cap=cap, nt=nt, final_norm=final_norm),
        grid_spec=grid_spec,
        out_shape=jax.ShapeDtypeStruct((T, D_MODEL), F32),
        compiler_params=_cparams(("arbitrary",)),
        name="moe_combine",
    )(start.reshape(-1), h, aff, posm_t, gain, y)


def _ec_moe(h, gain, w_router, w_gate, w_up, w_down, layer, final_gain=None):
    T = h.shape[0]
    cap = max(1, EC_CAPACITY_FACTOR * T // N_EXPERTS)
    aff, hn = _router(h, gain, w_router)
    posm, before = _select(aff.T, cap)
    start = before[:, ::MOE_TILE]
    x = _dispatch(hn, posm, start, cap)
    y = _ffn(x, w_gate, w_up, w_down, layer)
    start_end = jnp.concatenate([start, jnp.full((N_EXPERTS, 1), cap, jnp.int32)], axis=1)
    return _combine(h, aff, posm.T, start_end, y, cap, final_gain)


def _rms_cols(x, g):
    return x * lax.rsqrt(jnp.mean(x * x, axis=0, keepdims=True) + RMS_EPS) * g


MLA_QCOLS = MLA_NOPE + 2 * LANES


def _mla_in_kernel(h_ref, posc_ref, posr_ref, g_ref, win_ref, wintkv_ref, qn_ref, wq_ref, wukt_ref, kvn_ref,
                   kvnc_ref, freqr_ref, sign_ref, freqc_ref, q_out, kt_out, va_out):
    half = MLA_ROPE // 2
    hn = _rms(h_ref[...], g_ref[...]).astype(BF16)
    c = _dot(hn, win_ref[...])
    ckv_t = _dot_nt(wintkv_ref[...], hn)

    cq = _rms(c[:, :MLA_Q_RANK], qn_ref[...]).astype(BF16)
    qa = _dot(cq, wq_ref[...])
    ang = posc_ref[...].astype(F32) * freqr_ref[...]
    cos, sin_signed = jnp.cos(ang), jnp.sin(ang) * sign_ref[...]
    qscale = MLA_QK ** -0.5 * math.log2(math.e)
    for hd in range(MLA_HEADS):
        c0 = hd * MLA_QCOLS
        q_lat = _dot(qa[:, c0:c0 + MLA_NOPE].astype(BF16), wukt_ref[hd])
        rot = qa[:, c0 + MLA_NOPE:c0 + MLA_NOPE + LANES] * cos + qa[:, c0 + MLA_NOPE + LANES:c0 + MLA_QCOLS] * sin_signed
        q_out[hd, :, :MLA_NOPE] = (q_lat * qscale).astype(BF16)
        q_out[hd, :, MLA_NOPE:] = (rot[:, :MLA_ROPE] * qscale).astype(BF16)

    kt_out[:MLA_KV_RANK, :] = _rms_cols(ckv_t[:MLA_KV_RANK], kvnc_ref[...]).astype(BF16)
    ang_t = freqc_ref[...] * posr_ref[...].astype(F32)
    cos_t, sin_t = jnp.cos(ang_t), jnp.sin(ang_t)
    k1, k2 = ckv_t[MLA_KV_RANK:MLA_KV_RANK + half], ckv_t[MLA_KV_RANK + half:]
    kt_out[MLA_KV_RANK:MLA_KV_RANK + half, :] = (k1 * cos_t - k2 * sin_t).astype(BF16)
    kt_out[MLA_KV_RANK + half:, :] = (k1 * sin_t + k2 * cos_t).astype(BF16)

    ckv = _rms(c[:, MLA_Q_RANK:MLA_Q_RANK + MLA_KV_RANK], kvn_ref[...])
    va_out[:, :MLA_KV_RANK] = ckv.astype(BF16)
    lane = lax.broadcasted_iota(jnp.int32, (ckv.shape[0], LANES), 1)
    va_out[:, MLA_KV_RANK:] = jnp.where(lane == 0, 1.0, 0.0).astype(BF16)


def _mla_in(h, positions, gain, w_in, q_norm, w_uq, kv_norm, w_ukv, tm=256):
    T = h.shape[0]
    H, half = MLA_HEADS, MLA_ROPE // 2
    wq = w_uq.reshape(MLA_Q_RANK, H, MLA_QK)
    x1, x2 = wq[:, :, MLA_NOPE:MLA_NOPE + half], wq[:, :, MLA_NOPE + half:]
    pad = jnp.zeros((MLA_Q_RANK, H, LANES - MLA_ROPE), F32)
    wq_wide = jnp.concatenate([wq[:, :, :MLA_NOPE], x1, x2, pad, x2, x1, pad], axis=2)
    wq_wide = wq_wide.reshape(MLA_Q_RANK, H * MLA_QCOLS).astype(BF16)
    wuk_t = w_ukv.reshape(MLA_KV_RANK, H, MLA_NOPE + MLA_V)[:, :, :MLA_NOPE].transpose(1, 2, 0).astype(BF16)
    inv_freq = ROPE_BASE ** (-jnp.arange(half, dtype=F32) / half)
    zeros = jnp.zeros((LANES - MLA_ROPE,), F32)
    freq_row = jnp.concatenate([inv_freq, inv_freq, zeros]).reshape(1, LANES)
    sign_row = jnp.concatenate([-jnp.ones((half,), F32), jnp.ones((half,), F32), zeros]).reshape(1, LANES)
    w_in_b = w_in.astype(BF16)
    args = [h, positions.reshape(T, 1), positions.reshape(1, T), gain.reshape(1, -1), w_in_b,
            w_in_b[:, MLA_Q_RANK:].T, q_norm.reshape(1, -1), wq_wide, wuk_t, kv_norm.reshape(1, -1),
            kv_norm.reshape(-1, 1), freq_row, sign_row, inv_freq.reshape(half, 1)]
    full = lambda a: pl.BlockSpec(a.shape, lambda i: (0,) * a.ndim)
    in_specs = [pl.BlockSpec((tm, D_MODEL), lambda i: (i, 0)), pl.BlockSpec((tm, 1), lambda i: (i, 0)),
                pl.BlockSpec((1, tm), lambda i: (0, i))] + [full(a) for a in args[3:]]
    return pl.pallas_call(
        _mla_in_kernel,
        grid=(T // tm,),
        in_specs=in_specs,
        out_specs=[pl.BlockSpec((H, tm, MLA_QK), lambda i: (0, i, 0)),
                   pl.BlockSpec((MLA_QK, tm), lambda i: (0, i)),
                   pl.BlockSpec((tm, 2 * LANES), lambda i: (i, 0))],
        out_shape=[jax.ShapeDtypeStruct((H, T, MLA_QK), BF16), jax.ShapeDtypeStruct((MLA_QK, T), BF16),
                   jax.ShapeDtypeStruct((T, 2 * LANES), BF16)],
        compiler_params=_cparams(("arbitrary",)),
        name="mla_in",
    )(*args)


def _flash_kernel(q_ref, qn_ref, kt_ref, va_ref, o_ref, sa_ref, sb_ref, pa_ref, pb_ref, acc_ref, m_ref, alpha_ref,
                  *, tk, nblk):
    H, tq, _ = qn_ref.shape
    T = kt_ref.shape[1]
    n = T // tk
    tiles = q_ref.shape[1] // tq

    def q_rows(u):
        blk = qn_ref[...] if u == tiles else q_ref[:, u * tq:(u + 1) * tq, :]
        return blk.reshape(H * tq, MLA_QK)

    def k_tile(j):
        return kt_ref[:, j * tk:(j + 1) * tk]

    def v_tile(j):
        return va_ref[j * tk:(j + 1) * tk, :]

    def lanes_rep(x, width):
        return jnp.concatenate([x] * (width // LANES), axis=1)

    blk = H * tq // nblk
    row_blocks = [slice(rb * blk, (rb + 1) * blk) for rb in range(nblk)]

    def add_pv(rows, p_ref, j):
        acc_ref[rows, :] = (lanes_rep(alpha_ref[rows, :], 2 * LANES) * acc_ref[rows, :]
                            + _dot(p_ref[rows, :], v_tile(j)))

    def step(q, q_next, j, s_cur, s_nxt, p_cur, p_prv):
        q_sel, k_nxt = (q, k_tile(j + 1)) if j + 1 < n else (q_next, k_tile(0))
        for rows in row_blocks:
            if j > 0:
                add_pv(rows, p_prv, j - 1)
            s_nxt[rows, :] = _dot(q_sel[rows], k_nxt)
            s = s_cur[rows, :]
            m_old = m_ref[rows, :]
            m_new = jnp.maximum(m_old, jnp.max(s, axis=1, keepdims=True))
            alpha_ref[rows, :] = jnp.exp2(m_old - m_new)
            m_ref[rows, :] = m_new
            p_cur[rows, :] = jnp.exp2(s - lanes_rep(m_new, tk)).astype(BF16)

    @pl.when(pl.program_id(0) == 0)
    def _():
        sa_ref[...] = _dot(q_rows(0), k_tile(0))

    for u in range(tiles):
        q, q_next = q_rows(u), q_rows(u + 1)
        acc_ref[...] = jnp.zeros_like(acc_ref)
        m_ref[...] = jnp.full_like(m_ref, -jnp.inf)
        for j in range(n):
            if j % 2 == 0:
                step(q, q_next, j, sa_ref, sb_ref, pa_ref, pb_ref)
            else:
                step(q, q_next, j, sb_ref, sa_ref, pb_ref, pa_ref)
        p_last = pb_ref if n % 2 == 0 else pa_ref
        for rows in row_blocks:
            add_pv(rows, p_last, n - 1)
        acc = acc_ref[...]
        o_lat = (acc[:, :MLA_KV_RANK] / acc[:, MLA_KV_RANK:MLA_KV_RANK + 1]).astype(BF16)
        for hd in range(H):
            o_ref[u * tq:(u + 1) * tq, hd * MLA_KV_RANK:(hd + 1) * MLA_KV_RANK] = o_lat[hd * tq:(hd + 1) * tq]


def _flash(q, kt, va, tq=64, tk=1024, tiles=2):
    H, T, _ = q.shape
    tq, tk = min(tq, T), min(tk, T)
    assert T % (2 * tk) == 0 and tq % BF16_SUBLANES == 0
    assert T % (tiles * tq) == 0
    rows = H * tq
    steps = T // (tiles * tq)
    last_tile = T // tq - 1
    return pl.pallas_call(
        functools.partial(_flash_kernel, tk=tk, nblk=4),
        grid=(steps,),
        scratch_shapes=[pltpu.VMEM((rows, tk), F32), pltpu.VMEM((rows, tk), F32), pltpu.VMEM((rows, tk), BF16),
                        pltpu.VMEM((rows, tk), BF16), pltpu.VMEM((rows, 2 * LANES), F32),
                        pltpu.VMEM((rows, LANES), F32), pltpu.VMEM((rows, LANES), F32)],
        in_specs=[
            pl.BlockSpec((H, tiles * tq, MLA_QK), lambda i: (0, i, 0)),
            pl.BlockSpec((H, tq, MLA_QK), lambda i: (0, jnp.minimum((i + 1) * tiles, last_tile), 0)),
            pl.BlockSpec((MLA_QK, T), lambda i: (0, 0)),
            pl.BlockSpec((T, 2 * LANES), lambda i: (0, 0)),
        ],
        out_specs=pl.BlockSpec((tiles * tq, H * MLA_KV_RANK), lambda i: (i, 0)),
        out_shape=jax.ShapeDtypeStruct((T, H * MLA_KV_RANK), BF16),
        compiler_params=_cparams(("arbitrary",)),
        name="mla_flash",
    )(q, q, kt, va)


def _mla_out_kernel(o_ref, wuv_ref, w_ref, h_ref, out_ref):
    v = [_dot(o_ref[:, hd * MLA_KV_RANK:(hd + 1) * MLA_KV_RANK], wuv_ref[hd]).astype(BF16)
         for hd in range(MLA_HEADS)]
    out_ref[...] = h_ref[...] + _dot(jnp.concatenate(v, axis=1), w_ref[...])


def _mla_out(o_lat, w_ukv, w_out, h, tm=512):
    T = h.shape[0]
    H = MLA_HEADS
    wuv = w_ukv.reshape(MLA_KV_RANK, H, MLA_NOPE + MLA_V)[:, :, MLA_NOPE:].transpose(1, 0, 2).astype(BF16)
    wb = w_out.astype(BF16)
    return pl.pallas_call(
        _mla_out_kernel,
        grid=(T // tm,),
        in_specs=[pl.BlockSpec((tm, H * MLA_KV_RANK), lambda i: (i, 0)), pl.BlockSpec(wuv.shape, lambda i: (0, 0, 0)),
                  pl.BlockSpec(wb.shape, lambda i: (0, 0)), pl.BlockSpec((tm, D_MODEL), lambda i: (i, 0))],
        out_specs=pl.BlockSpec((tm, D_MODEL), lambda i: (i, 0)),
        out_shape=jax.ShapeDtypeStruct((T, D_MODEL), F32),
        compiler_params=_cparams(("arbitrary",)),
        name="mla_out",
    )(o_lat, wuv, wb, h)


def _mla_mixer(h, positions, gain, w_in, q_norm, w_uq, kv_norm, w_ukv, w_out):
    q, kt, va = _mla_in(h, positions, gain, w_in, q_norm, w_uq, kv_norm, w_ukv)
    o_lat = _flash(q, kt, va)
    return _mla_out(o_lat, w_ukv, w_out, h)


def kernel(x, positions, mix_norm, ffn_norm, final_norm, gla_w_in, gla_w_gate_up_f, gla_b_gate_f, gla_w_gate_up_b,
           gla_b_gate_b, gla_head_norm, gla_w_out, mla_w_in, mla_q_norm, mla_w_uq, mla_kv_norm, mla_w_ukv,
           mla_w_out, moe_w_router, moe_w_gate, moe_w_up, moe_w_down):
    B, T, D = x.shape
    outs = []
    for b in range(B):
        h = x[b]
        h = _gla_mixer(h, mix_norm[0], gla_w_in[0], gla_w_gate_up_f[0], gla_b_gate_f[0], gla_w_gate_up_b[0],
                       gla_b_gate_b[0], gla_head_norm[0], gla_w_out[0])
        h = _ec_moe(h, ffn_norm[0], moe_w_router[0], moe_w_gate, moe_w_up, moe_w_down, 0)
        h = _mla_mixer(h, positions[b], mix_norm[1], mla_w_in[0], mla_q_norm[0], mla_w_uq[0], mla_kv_norm[0],
                       mla_w_ukv[0], mla_w_out[0])
        h = _ec_moe(h, ffn_norm[1], moe_w_router[1], moe_w_gate, moe_w_up, moe_w_down, 1, final_gain=final_norm)
        outs.append(h)
    return jnp.stack(outs)
```

```python
import functools
import math

import numpy as np
import jax
import jax.numpy as jnp
from jax import lax
from jax.experimental import pallas as pl
from jax.experimental.pallas import tpu as pltpu

F32 = jnp.float32
BF16 = jnp.bfloat16

D_MODEL = 1024
RMS_EPS = 1e-6

GLA_HEADS = 4
GLA_DK = 512
GLA_DV = 1024
GLA_HEAD_K = GLA_DK // GLA_HEADS
GLA_HEAD_V = GLA_DV // GLA_HEADS
GLA_GATE_RANK = 16
GLA_TAU = 16.0
GLA_CHUNK = 64
GLA_TILE = 256
GLA_LEVELS = 6

MLA_HEADS = 16
MLA_Q_RANK = 256
MLA_KV_RANK = 128
MLA_NOPE = 128
MLA_ROPE = 64
MLA_V = 128
MLA_QK = MLA_NOPE + MLA_ROPE
ROPE_BASE = 10000.0

N_EXPERTS = 16
EXPERT_FF = 2048
EC_CAPACITY_FACTOR = 2
MOE_TILE = 256
COMBINE_HEAD = 64
BF16_SUBLANES = 16
F32_SUBLANES = 8
LANES = 128

VMEM_LIMIT = 56 * 1024 * 1024


def _cparams(sem):
    return pltpu.CompilerParams(dimension_semantics=sem, vmem_limit_bytes=VMEM_LIMIT)


def _rms(x, g):
    return x * lax.rsqrt(jnp.mean(x * x, axis=-1, keepdims=True) + RMS_EPS) * g


def _split_bf16(x):
    hi = x.astype(BF16)
    lo = (x - hi.astype(F32)).astype(BF16)
    return hi, lo


def _dot(a, b):
    return jnp.dot(a, b, preferred_element_type=F32)


def _dot_nt(a, b):
    return lax.dot_general(a, b, (((1,), (1,)), ((), ())), preferred_element_type=F32)


def _dot_tn(a, b):
    return lax.dot_general(a, b, (((0,), (0,)), ((), ())), preferred_element_type=F32)


def _dot_split(a, b):
    ah, al = _split_bf16(a)
    bh, bl = _split_bf16(b)
    return _dot(ah, bh) + _dot(ah, bl) + _dot(al, bh)


def _gla_in_kernel(x_ref, g_ref, wqk_ref, wv_ref, wr_ref, wgd_ref, qk_ref, v_ref, r_ref, gd_ref):
    hn = _rms(x_ref[...], g_ref[...]).astype(BF16)
    qk_ref[...] = _dot(hn, wqk_ref[...])
    v_ref[...] = _dot(hn, wv_ref[...])
    r_ref[...] = _dot(hn, wr_ref[...])
    gd_ref[...] = _dot(hn, wgd_ref[...])


def _gla_in(x, gain, w_in, tm=512):
    T = x.shape[0]
    wqk = w_in[:, :2 * GLA_DK].astype(BF16)
    wv = w_in[:, 2 * GLA_DK:2 * GLA_DK + GLA_DV].astype(BF16)
    wr = w_in[:, 2 * GLA_DK + GLA_DV:2 * GLA_DK + 2 * GLA_DV].astype(BF16)
    wgd = w_in[:, 2 * GLA_DK + 2 * GLA_DV:].astype(BF16)
    ngd = 2 * GLA_GATE_RANK
    row = lambda n: pl.BlockSpec((tm, n), lambda i: (i, 0))
    full = lambda a: pl.BlockSpec(a.shape, lambda i: (0, 0))
    gain2 = gain.reshape(1, D_MODEL)
    return pl.pallas_call(
        _gla_in_kernel,
        grid=(T // tm,),
        in_specs=[row(D_MODEL), full(gain2), full(wqk), full(wv), full(wr), full(wgd)],
        out_specs=[row(2 * GLA_DK), row(GLA_DV), row(GLA_DV), row(ngd)],
        out_shape=[jax.ShapeDtypeStruct((T, 2 * GLA_DK), F32), jax.ShapeDtypeStruct((T, GLA_DV), F32),
                   jax.ShapeDtypeStruct((T, GLA_DV), F32), jax.ShapeDtypeStruct((T, ngd), F32)],
        compiler_params=_cparams(("arbitrary",)),
        name="gla_in",
    )(x, gain2, wqk, wv, wr, wgd)


def _gla_tables(reverse):
    n, c = GLA_TILE, GLA_CHUNK
    W = np.zeros((GLA_LEVELS + 3, n, n), np.float32)
    L = np.full((n, n), -1, np.int32)
    for t in range(n):
        c0 = (t // c) * c
        tt = t - c0
        for l in range(GLA_LEVELS):
            b = (c // 2) >> l
            p0 = c0 + (tt // (2 * b)) * 2 * b
            mid = p0 + b
            second = t >= mid
            if not reverse:
                if second:
                    W[l, t, mid:t + 1] = 1
                    L[t, p0:mid] = l
                else:
                    W[l, t, t + 1:mid] = 1
            else:
                if second:
                    W[l, t, mid:t] = 1
                else:
                    W[l, t, t:mid] = 1
                    L[t, mid:p0 + 2 * b] = l
        if not reverse:
            W[GLA_LEVELS, t, c0:t + 1] = 1
            W[GLA_LEVELS + 1, t, t + 1:c0 + c] = 1
            L[t, t] = GLA_LEVELS
        else:
            W[GLA_LEVELS, t, t:c0 + c] = 1
            W[GLA_LEVELS + 1, t, c0:t] = 1
        W[GLA_LEVELS + 2, t, c0:c0 + c] = 1
    return W.reshape(-1, n), L


def _gla_scan_kernel(qkf_ref, vf_ref, gdf_ref, qkb_ref, vb_ref, gdb_ref, wupf_ref, bf_ref, wupb_ref, bb_ref,
                     wf_ref, lf_ref, wb_ref, lb_ref, of_ref, ob_ref, s_ref):
    n, c, r = GLA_TILE, GLA_CHUNK, GLA_GATE_RANK

    @pl.when(pl.program_id(0) == 0)
    def _():
        s_ref[...] = jnp.zeros_like(s_ref)

    dirs = [(qkf_ref, vf_ref, gdf_ref[:, :r], wupf_ref, bf_ref, wf_ref, lf_ref, of_ref, False),
            (qkb_ref, vb_ref, gdb_ref[:, r:], wupb_ref, bb_ref, wb_ref, lb_ref, ob_ref, True)]

    factors = []
    for qk_ref, v_ref, gd, wup_ref, b_ref, w_ref, l_ref, o_ref, reverse in dirs:
        z = _dot_split(gd, wup_ref[...]) + b_ref[...]
        g = (jnp.minimum(z, 0.0) - jnp.log1p(jnp.exp(-jnp.abs(z)))) * (1.0 / GLA_TAU)
        ghi, glo = _split_bf16(g)
        w = w_ref[...]
        factors.append(jnp.exp(_dot(w, ghi) + _dot(w, glo)))

    chains = []
    for d, (qk_ref, v_ref, gd, wup_ref, b_ref, w_ref, l_ref, o_ref, reverse) in enumerate(dirs):
        lvl = l_ref[...]
        for h in range(GLA_HEADS):
            kcols = slice(h * GLA_HEAD_K, (h + 1) * GLA_HEAD_K)
            vcols = slice(h * GLA_HEAD_V, (h + 1) * GLA_HEAD_V)
            f = factors[d][:, kcols]
            q = qk_ref[:, kcols] * (GLA_HEAD_K ** -0.5)
            k = qk_ref[:, GLA_DK + h * GLA_HEAD_K:GLA_DK + (h + 1) * GLA_HEAD_K]
            vb = v_ref[:, vcols].astype(BF16)
            attn = jnp.zeros((n, n), F32)
            for l in range(GLA_LEVELS):
                fl = f[l * n:(l + 1) * n]
                p = _dot_nt((q * fl).astype(BF16), (k * fl).astype(BF16))
                attn = jnp.where(lvl == l, p, attn)
            if not reverse:
                p = _dot_nt(q.astype(BF16), k.astype(BF16))
                attn = jnp.where(lvl == GLA_LEVELS, p, attn)
            o_intra = _dot(attn.astype(BF16), vb)
            qh = (q * f[GLA_LEVELS * n:(GLA_LEVELS + 1) * n]).astype(BF16)
            kh = (k * f[(GLA_LEVELS + 1) * n:(GLA_LEVELS + 2) * n]).astype(BF16)
            ftot = f[(GLA_LEVELS + 2) * n:(GLA_LEVELS + 3) * n]
            chains.append(dict(d=d, h=h, vcols=vcols, o_ref=o_ref, reverse=reverse, vb=vb, o_intra=o_intra, qh=qh,
                               kh=kh, ftot=ftot, st=s_ref[d, h]))

    nchunks = n // c
    for step in range(nchunks):
        for ch in chains:
            j = nchunks - 1 - step if ch["reverse"] else step
            rows = slice(j * c, (j + 1) * c)
            st = ch["st"]
            ch["o_ref"][rows, ch["vcols"]] = ch["o_intra"][rows] + _dot_nt(ch["qh"][rows], st.astype(BF16))
            ch["st"] = st * ch["ftot"][j * c:j * c + 1, :] + _dot_tn(ch["vb"][rows], ch["kh"][rows])
    for ch in chains:
        s_ref[ch["d"], ch["h"]] = ch["st"]


def _gla_scan(qk, v, gd, w_up_f, b_f, w_up_b, b_b):
    T = qk.shape[0]
    n = GLA_TILE
    nt = T // n
    tables = []
    for reverse in (False, True):
        W, L = _gla_tables(reverse)
        tables += [jnp.asarray(W, BF16), jnp.asarray(L)]
    fwd = lambda width: pl.BlockSpec((n, width), lambda i: (i, 0))
    bwd = lambda width: pl.BlockSpec((n, width), lambda i: (nt - 1 - i, 0))
    full = lambda a: pl.BlockSpec(a.shape, lambda i: (0, 0))
    consts = [w_up_f, b_f.reshape(1, GLA_DK), w_up_b, b_b.reshape(1, GLA_DK)] + tables
    ngd = 2 * GLA_GATE_RANK
    out = jax.ShapeDtypeStruct((T, GLA_DV), F32)
    return pl.pallas_call(
        _gla_scan_kernel,
        grid=(nt,),
        in_specs=[fwd(2 * GLA_DK), fwd(GLA_DV), fwd(ngd), bwd(2 * GLA_DK), bwd(GLA_DV), bwd(ngd)]
                 + [full(a) for a in consts],
        out_specs=[fwd(GLA_DV), bwd(GLA_DV)],
        out_shape=[out, out],
        scratch_shapes=[pltpu.VMEM((2, GLA_HEADS, GLA_HEAD_V, GLA_HEAD_K), F32)],
        compiler_params=_cparams(("arbitrary",)),
        name="gla_scan",
    )(qk, v, gd, qk, v, gd, *consts)


def _gla_out_kernel(of_ref, ob_ref, r_ref, x_ref, hn_ref, w_ref, o_ref):
    acc = x_ref[...]
    for h in range(GLA_HEADS):
        cols = slice(h * GLA_HEAD_V, (h + 1) * GLA_HEAD_V)
        o = _rms(of_ref[:, cols] + ob_ref[:, cols], hn_ref[...])
        r = r_ref[:, cols]
        gated = o * (r * (1.0 / (1.0 + jnp.exp(-r))))
        acc = acc + _dot(gated.astype(BF16), w_ref[cols, :])
    o_ref[...] = acc


def _gla_out(of, ob, r, x, head_norm, w_out, tm=512):
    T = x.shape[0]
    w = w_out.astype(BF16)
    hn = head_norm.reshape(1, GLA_HEAD_V)
    row = lambda n: pl.BlockSpec((tm, n), lambda i: (i, 0))
    full = lambda a: pl.BlockSpec(a.shape, lambda i: (0, 0))
    return pl.pallas_call(
        _gla_out_kernel,
        grid=(T // tm,),
        in_specs=[row(GLA_DV), row(GLA_DV), row(GLA_DV), row(D_MODEL), full(hn), full(w)],
        out_specs=row(D_MODEL),
        out_shape=jax.ShapeDtypeStruct((T, D_MODEL), F32),
        compiler_params=_cparams(("arbitrary",)),
        name="gla_out",
    )(of, ob, r, x, hn, w)


def _gla_mixer(x, gain, w_in, w_up_f, b_f, w_up_b, b_b, head_norm, w_out):
    qk, v, r, gd = _gla_in(x, gain, w_in)
    of, ob = _gla_scan(qk, v, gd, w_up_f, b_f, w_up_b, b_b)
    return _gla_out(of, ob, r, x, head_norm, w_out)


def _router_kernel(h_ref, g_ref, w_ref, aff_ref, hn_ref):
    hn = _rms(h_ref[...], g_ref[...])
    hn_ref[...] = hn.astype(BF16)
    logits = _dot_split(hn, w_ref[...])
    e = jnp.exp(logits - jnp.max(logits, axis=-1, keepdims=True))
    aff_ref[...] = e / jnp.sum(e, axis=-1, keepdims=True)


def _router(h, gain, w_router, tm=512):
    T = h.shape[0]
    gain2 = gain.reshape(1, D_MODEL)
    row = lambda n: pl.BlockSpec((tm, n), lambda i: (i, 0))
    full = lambda a: pl.BlockSpec(a.shape, lambda i: (0, 0))
    return pl.pallas_call(
        _router_kernel,
        grid=(T // tm,),
        in_specs=[row(D_MODEL), full(gain2), full(w_router)],
        out_specs=[row(N_EXPERTS), row(D_MODEL)],
        out_shape=[jax.ShapeDtypeStruct((T, N_EXPERTS), F32), jax.ShapeDtypeStruct((T, D_MODEL), BF16)],
        compiler_params=_cparams(("arbitrary",)),
        name="moe_router",
    )(h, gain2, w_router)


def _select_kernel(aff_ref, tri_ref, posm_ref, before_ref, *, cap):
    T = aff_ref.shape[1]
    bits = pltpu.bitcast(aff_ref[...], jnp.int32)

    def search(it, thr):
        cand = thr | jnp.left_shift(jnp.int32(1), 30 - it)
        cnt = jnp.sum(jnp.where(bits >= cand, 1.0, 0.0), axis=1, keepdims=True)
        return jnp.where(cnt >= cap, cand, thr)

    thr = lax.fori_loop(0, 31, search, jnp.zeros((N_EXPERTS, 1), jnp.int32))
    n_gt = jnp.sum(jnp.where(bits > thr, 1.0, 0.0), axis=1, keepdims=True)
    need = cap - n_gt
    tri = tri_ref[...]

    def scan(j, carry):
        c_eq, c_sel = carry
        cols = pl.ds(pl.multiple_of(j * LANES, LANES), LANES)
        blk = pltpu.bitcast(aff_ref[:, cols], jnp.int32)
        eq = jnp.where(blk == thr, 1.0, 0.0)
        rank = _dot(eq.astype(BF16), tri) + c_eq - eq
        sel = jnp.where((blk > thr) | ((eq > 0.0) & (rank < need)), 1.0, 0.0)
        before = _dot(sel.astype(BF16), tri) + c_sel - sel
        before_ref[:, cols] = before.astype(jnp.int32)
        posm_ref[:, cols] = jnp.where(sel > 0.0, before, -1.0).astype(jnp.int32)
        return (c_eq + jnp.sum(eq, axis=1, keepdims=True), c_sel + jnp.sum(sel, axis=1, keepdims=True))

    zero = jnp.zeros((N_EXPERTS, 1), F32)
    lax.fori_loop(0, T // LANES, scan, (zero, zero))


def _select(aff_t, cap):
    T = aff_t.shape[1]
    tri = jnp.asarray(np.triu(np.ones((LANES, LANES), np.float32)), BF16)
    full = lambda a: pl.BlockSpec(a.shape, lambda: (0,) * a.ndim)
    out = jax.ShapeDtypeStruct((N_EXPERTS, T), jnp.int32)
    return pl.pallas_call(
        functools.partial(_select_kernel, cap=cap),
        in_specs=[full(aff_t), full(tri)],
        out_specs=[pl.BlockSpec((N_EXPERTS, T), lambda: (0, 0))] * 2,
        out_shape=[out, out],
        compiler_params=pltpu.CompilerParams(vmem_limit_bytes=VMEM_LIMIT),
        name="moe_select",
    )(aff_t, tri)


def _dispatch_window():
    return MOE_TILE + F32_SUBLANES


def _dispatch_kernel(start_ref, hn_ref, posm_ref, x_ref, acc_ref, *, cap, nt, sub):
    e, t = pl.program_id(0), pl.program_id(1)
    win = _dispatch_window()

    @pl.when(t == 0)
    def _():
        acc_ref[...] = jnp.zeros_like(acc_ref)

    for s in range(sub):
        tok = slice(s * MOE_TILE, (s + 1) * MOE_TILE)
        base = pl.multiple_of((start_ref[e * nt + t * sub + s] // F32_SUBLANES) * F32_SUBLANES, F32_SUBLANES)
        slot = base + lax.broadcasted_iota(jnp.int32, (win, MOE_TILE), 0)
        onehot = jnp.where(posm_ref[0, :, tok] == slot, 1.0, 0.0).astype(BF16)
        acc_ref[pl.ds(base, win), :] += _dot(onehot, hn_ref[tok, :])

    @pl.when(t == pl.num_programs(1) - 1)
    def _():
        x_ref[0] = acc_ref[:cap, :].astype(BF16)


def _dispatch(hn, posm, start, cap, sub=16):
    T = hn.shape[0]
    nt = T // MOE_TILE
    sub = math.gcd(sub, nt)
    blk = sub * MOE_TILE
    posm3 = posm.reshape(N_EXPERTS, 1, T)
    grid_spec = pltpu.PrefetchScalarGridSpec(
        num_scalar_prefetch=1,
        grid=(N_EXPERTS, nt // sub),
        in_specs=[
            pl.BlockSpec((blk, D_MODEL), lambda e, t, s: (t, 0)),
            pl.BlockSpec((1, 1, blk), lambda e, t, s: (e, 0, t)),
        ],
        out_specs=pl.BlockSpec((1, cap, D_MODEL), lambda e, t, s: (e, 0, 0)),
        scratch_shapes=[pltpu.VMEM((cap + _dispatch_window(), D_MODEL), F32)],
    )
    return pl.pallas_call(
        functools.partial(_dispatch_kernel, cap=cap, nt=nt, sub=sub),
        grid_spec=grid_spec,
        out_shape=jax.ShapeDtypeStruct((N_EXPERTS, cap, D_MODEL), BF16),
        compiler_params=_cparams(("arbitrary", "arbitrary")),
        name="moe_dispatch",
    )(start.reshape(-1), hn, posm3)


def _ffn_kernel(x_ref, wg_ref, wu_ref, wd_ref, y_ref, acc_ref):
    f = pl.program_id(1)

    @pl.when(f == 0)
    def _():
        acc_ref[...] = jnp.zeros_like(acc_ref)

    wg = wg_ref[0, 0].astype(BF16)
    wu = wu_ref[0, 0].astype(BF16)
    wd = wd_ref[0, 0].astype(BF16)
    cap = x_ref.shape[1]
    blk = cap // math.gcd(cap // F32_SUBLANES, 4)
    for r0 in range(0, cap, blk):
        rows = slice(r0, r0 + blk)
        x = x_ref[0, rows, :]
        a = _dot(x, wg)
        u = _dot(x, wu)
        mid = (a * (1.0 / (1.0 + jnp.exp(-a))) * u).astype(BF16)
        acc_ref[rows, :] += _dot(mid, wd)

    @pl.when(f == pl.num_programs(1) - 1)
    def _():
        y_ref[0] = acc_ref[...].astype(BF16)


def _ffn(x, w_gate, w_up, w_down, layer, tf=512):
    cap = x.shape[1]
    return pl.pallas_call(
        _ffn_kernel,
        grid=(N_EXPERTS, EXPERT_FF // tf),
        in_specs=[
            pl.BlockSpec((1, cap, D_MODEL), lambda e, f: (e, 0, 0)),
            pl.BlockSpec((1, 1, D_MODEL, tf), lambda e, f: (layer, e, 0, f)),
            pl.BlockSpec((1, 1, D_MODEL, tf), lambda e, f: (layer, e, 0, f)),
            pl.BlockSpec((1, 1, tf, D_MODEL), lambda e, f: (layer, e, f, 0)),
        ],
        out_specs=pl.BlockSpec((1, cap, D_MODEL), lambda e, f: (e, 0, 0)),
        out_shape=jax.ShapeDtypeStruct((N_EXPERTS, cap, D_MODEL), BF16),
        scratch_shapes=[pltpu.VMEM((cap, D_MODEL), F32)],
        compiler_params=_cparams(("arbitrary", "arbitrary")),
        name="moe_ffn",
    )(x, w_gate, w_up, w_down)


def _combine_window():
    return MOE_TILE + BF16_SUBLANES


def _combine_kernel(start_ref, h_ref, aff_ref, posm_ref, g_ref, y_hbm, o_ref, hbuf_ref, buf_ref, hsem_ref, rsem_ref,
                    acc_ref, *, cap, nt, final_norm):
    t = pl.program_id(0)
    win = _combine_window()
    head = COMBINE_HEAD

    def base_of(tile, e):
        b = (start_ref[e * (nt + 1) + tile] // BF16_SUBLANES) * BF16_SUBLANES
        return pl.multiple_of(jnp.minimum(b, cap - win), BF16_SUBLANES)

    def needs_rest(tile, e):
        return start_ref[e * (nt + 1) + tile + 1] > base_of(tile, e) + head

    def head_copy(tile, e, slot):
        return pltpu.make_async_copy(y_hbm.at[e, pl.ds(base_of(tile, e), head), :],
                                     hbuf_ref.at[slot, pl.ds(e * head, head), :], hsem_ref.at[slot])

    def rest_copy(tile, e, slot):
        return pltpu.make_async_copy(y_hbm.at[e, pl.ds(base_of(tile, e) + head, win - head), :],
                                     buf_ref.at[slot, e], rsem_ref.at[slot, e])

    def fetch(tile, slot):
        for e in range(N_EXPERTS):
            head_copy(tile, e, slot).start()

            @pl.when(needs_rest(tile, e))
            def _():
                rest_copy(tile, e, slot).start()

    slot = t % 2

    @pl.when(t == 0)
    def _():
        fetch(0, 0)

    @pl.when(t + 1 < nt)
    def _():
        fetch(t + 1, 1 - slot)

    def expand(e, lo, hi):
        slots = base_of(t, e) + lo + lax.broadcasted_iota(jnp.int32, (MOE_TILE, hi - lo), 1)
        onehot = jnp.where(posm_ref[:, e:e + 1] == slots, 1.0, 0.0).astype(BF16)
        return aff_ref[:, e:e + 1] * _dot(onehot, buf_ref[slot, e, lo - head:hi - head, :])

    for e in range(N_EXPERTS):
        head_copy(t, e, slot).wait()

    per_group = LANES // head
    a_hi, a_lo = _split_bf16(aff_ref[...])
    a_hi, a_lo = a_hi.astype(F32), a_lo.astype(F32)
    lane = lax.broadcasted_iota(jnp.int32, (MOE_TILE, LANES), 1)
    w_hi, w_lo = [], []
    for grp in range(N_EXPERTS // per_group):
        target = jnp.zeros((MOE_TILE, LANES), jnp.int32)
        pos = jnp.zeros((MOE_TILE, LANES), jnp.int32) - 1
        hi, lo = jnp.zeros((MOE_TILE, LANES), F32), jnp.zeros((MOE_TILE, LANES), F32)
        for k in range(per_group):
            e = grp * per_group + k
            mine = (lane >= k * head) & (lane < (k + 1) * head)
            target = jnp.where(mine, base_of(t, e) + lane - k * head, target)
            pos = jnp.where(mine, posm_ref[:, e:e + 1], pos)
            hi = jnp.where(mine, a_hi[:, e:e + 1], hi)
            lo = jnp.where(mine, a_lo[:, e:e + 1], lo)
        hit = pos == target
        w_hi.append(jnp.where(hit, hi, 0.0).astype(BF16))
        w_lo.append(jnp.where(hit, lo, 0.0).astype(BF16))
    y_heads = hbuf_ref[slot]
    acc_ref[...] = (h_ref[...] + _dot(jnp.concatenate(w_hi, axis=1), y_heads)
                    + _dot(jnp.concatenate(w_lo, axis=1), y_heads))

    any_rest = needs_rest(t, 0)
    for e in range(1, N_EXPERTS):
        any_rest = any_rest | needs_rest(t, e)

    @pl.when(any_rest)
    def _():
        for e in range(N_EXPERTS):
            @pl.when(needs_rest(t, e))
            def _():
                rest_copy(t, e, slot).wait()
                acc_ref[...] += expand(e, head, MOE_TILE) + expand(e, MOE_TILE, win)

    acc = acc_ref[...]
    if final_norm:
        acc = _rms(acc, g_ref[...])
    o_ref[...] = acc


def _combine(h, aff, posm_t, start, y, cap, final_gain):
    T = h.shape[0]
    nt = T // MOE_TILE
    win = _combine_window()
    assert cap >= win and (cap - win) % BF16_SUBLANES == 0
    final_norm = final_gain is not None
    gain = (final_gain if final_norm else jnp.ones((D_MODEL,), F32)).reshape(1, D_MODEL)
    grid_spec = pltpu.PrefetchScalarGridSpec(
        num_scalar_prefetch=1,
        grid=(nt,),
        in_specs=[
            pl.BlockSpec((MOE_TILE, D_MODEL), lambda t, s: (t, 0)),
            pl.BlockSpec((MOE_TILE, N_EXPERTS), lambda t, s: (t, 0)),
            pl.BlockSpec((MOE_TILE, N_EXPERTS), lambda t, s: (t, 0)),
            pl.BlockSpec((1, D_MODEL), lambda t, s: (0, 0)),
            pl.BlockSpec(memory_space=pl.ANY),
        ],
        out_specs=pl.BlockSpec((MOE_TILE, D_MODEL), lambda t, s: (t, 0)),
        scratch_shapes=[pltpu.VMEM((2, N_EXPERTS * COMBINE_HEAD, D_MODEL), BF16),
                        pltpu.VMEM((2, N_EXPERTS, win - COMBINE_HEAD, D_MODEL), BF16),
                        pltpu.SemaphoreType.DMA((2,)), pltpu.SemaphoreType.DMA((2, N_EXPERTS)),
                        pltpu.VMEM((MOE_TILE, D_MODEL), F32)],
    )
    return pl.pallas_call(
        functools.partial(_combine_kernel, cap=cap, nt=nt, final_norm=final_norm),
        grid_spec=grid_spec,
        out_shape=jax.ShapeDtypeStruct((T, D_MODEL), F32),
        compiler_params=_cparams(("arbitrary",)),
        name="moe_combine",
    )(start.reshape(-1), h, aff, posm_t, gain, y)


def _ec_moe(h, gain, w_router, w_gate, w_up, w_down, layer, final_gain=None):
    T = h.shape[0]
    cap = max(1, EC_CAPACITY_FACTOR * T // N_EXPERTS)
    aff, hn = _router(h, gain, w_router)
    posm, before = _select(aff.T, cap)
    start = before[:, ::MOE_TILE]
    x = _dispatch(hn, posm, start, cap)
    y = _ffn(x, w_gate, w_up, w_down, layer)
    start_end = jnp.concatenate([start, jnp.full((N_EXPERTS, 1), cap, jnp.int32)], axis=1)
    return _combine(h, aff, posm.T, start_end, y, cap, final_gain)


def _rms_cols(x, g):
    return x * lax.rsqrt(jnp.mean(x * x, axis=0, keepdims=True) + RMS_EPS) * g


MLA_QCOLS = MLA_NOPE + 2 * LANES


def _mla_in_kernel(h_ref, posc_ref, posr_ref, g_ref, win_ref, wintkv_ref, qn_ref, wq_ref, wukt_ref, kvn_ref,
                   kvnc_ref, freqr_ref, sign_ref, freqc_ref, q_out, kt_out, va_out):
    half = MLA_ROPE // 2
    hn = _rms(h_ref[...], g_ref[...]).astype(BF16)
    c = _dot(hn, win_ref[...])
    ckv_t = _dot_nt(wintkv_ref[...], hn)

    cq = _rms(c[:, :MLA_Q_RANK], qn_ref[...]).astype(BF16)
    qa = _dot(cq, wq_ref[...])
    ang = posc_ref[...].astype(F32) * freqr_ref[...]
    cos, sin_signed = jnp.cos(ang), jnp.sin(ang) * sign_ref[...]
    qscale = MLA_QK ** -0.5 * math.log2(math.e)
    for hd in range(MLA_HEADS):
        c0 = hd * MLA_QCOLS
        q_lat = _dot(qa[:, c0:c0 + MLA_NOPE].astype(BF16), wukt_ref[hd])
        rot = qa[:, c0 + MLA_NOPE:c0 + MLA_NOPE + LANES] * cos + qa[:, c0 + MLA_NOPE + LANES:c0 + MLA_QCOLS] * sin_signed
        q_out[hd, :, :MLA_NOPE] = (q_lat * qscale).astype(BF16)
        q_out[hd, :, MLA_NOPE:] = (rot[:, :MLA_ROPE] * qscale).astype(BF16)

    kt_out[:MLA_KV_RANK, :] = _rms_cols(ckv_t[:MLA_KV_RANK], kvnc_ref[...]).astype(BF16)
    ang_t = freqc_ref[...] * posr_ref[...].astype(F32)
    cos_t, sin_t = jnp.cos(ang_t), jnp.sin(ang_t)
    k1, k2 = ckv_t[MLA_KV_RANK:MLA_KV_RANK + half], ckv_t[MLA_KV_RANK + half:]
    kt_out[MLA_KV_RANK:MLA_KV_RANK + half, :] = (k1 * cos_t - k2 * sin_t).astype(BF16)
    kt_out[MLA_KV_RANK + half:, :] = (k1 * sin_t + k2 * cos_t).astype(BF16)

    ckv = _rms(c[:, MLA_Q_RANK:MLA_Q_RANK + MLA_KV_RANK], kvn_ref[...])
    va_out[:, :MLA_KV_RANK] = ckv.astype(BF16)
    lane = lax.broadcasted_iota(jnp.int32, (ckv.shape[0], LANES), 1)
    va_out[:, MLA_KV_RANK:] = jnp.where(lane == 0, 1.0, 0.0).astype(BF16)


def _mla_in(h, positions, gain, w_in, q_norm, w_uq, kv_norm, w_ukv, tm=256):
    T = h.shape[0]
    H, half = MLA_HEADS, MLA_ROPE // 2
    wq = w_uq.reshape(MLA_Q_RANK, H, MLA_QK)
    x1, x2 = wq[:, :, MLA_NOPE:MLA_NOPE + half], wq[:, :, MLA_NOPE + half:]
    pad = jnp.zeros((MLA_Q_RANK, H, LANES - MLA_ROPE), F32)
    wq_wide = jnp.concatenate([wq[:, :, :MLA_NOPE], x1, x2, pad, x2, x1, pad], axis=2)
    wq_wide = wq_wide.reshape(MLA_Q_RANK, H * MLA_QCOLS).astype(BF16)
    wuk_t = w_ukv.reshape(MLA_KV_RANK, H, MLA_NOPE + MLA_V)[:, :, :MLA_NOPE].transpose(1, 2, 0).astype(BF16)
    inv_freq = ROPE_BASE ** (-jnp.arange(half, dtype=F32) / half)
    zeros = jnp.zeros((LANES - MLA_ROPE,), F32)
    freq_row = jnp.concatenate([inv_freq, inv_freq, zeros]).reshape(1, LANES)
    sign_row = jnp.concatenate([-jnp.ones((half,), F32), jnp.ones((half,), F32), zeros]).reshape(1, LANES)
    w_in_b = w_in.astype(BF16)
    args = [h, positions.reshape(T, 1), positions.reshape(1, T), gain.reshape(1, -1), w_in_b,
            w_in_b[:, MLA_Q_RANK:].T, q_norm.reshape(1, -1), wq_wide, wuk_t, kv_norm.reshape(1, -1),
            kv_norm.reshape(-1, 1), freq_row, sign_row, inv_freq.reshape(half, 1)]
    full = lambda a: pl.BlockSpec(a.shape, lambda i: (0,) * a.ndim)
    in_specs = [pl.BlockSpec((tm, D_MODEL), lambda i: (i, 0)), pl.BlockSpec((tm, 1), lambda i: (i, 0)),
                pl.BlockSpec((1, tm), lambda i: (0, i))] + [full(a) for a in args[3:]]
    return pl.pallas_call(
        _mla_in_kernel,
        grid=(T // tm,),
        in_specs=in_specs,
        out_specs=[pl.BlockSpec((H, tm, MLA_QK), lambda i: (0, i, 0)),
                   pl.BlockSpec((MLA_QK, tm), lambda i: (0, i)),
                   pl.BlockSpec((tm, 2 * LANES), lambda i: (i, 0))],
        out_shape=[jax.ShapeDtypeStruct((H, T, MLA_QK), BF16), jax.ShapeDtypeStruct((MLA_QK, T), BF16),
                   jax.ShapeDtypeStruct((T, 2 * LANES), BF16)],
        compiler_params=_cparams(("arbitrary",)),
        name="mla_in",
    )(*args)


def _flash_kernel(q_ref, qn_ref, kt_ref, va_ref, o_ref, sa_ref, sb_ref, pa_ref, pb_ref, acc_ref, m_ref, alpha_ref,
                  *, tk, nblk):
    H, tq, _ = qn_ref.shape
    T = kt_ref.shape[1]
    n = T // tk
    tiles = q_ref.shape[1] // tq

    def q_rows(u):
        blk = qn_ref[...] if u == tiles else q_ref[:, u * tq:(u + 1) * tq, :]
        return blk.reshape(H * tq, MLA_QK)

    def k_tile(j):
        return kt_ref[:, j * tk:(j + 1) * tk]

    def v_tile(j):
        return va_ref[j * tk:(j + 1) * tk, :]

    def lanes_rep(x, width):
        return jnp.concatenate([x] * (width // LANES), axis=1)

    blk = H * tq // nblk
    row_blocks = [slice(rb * blk, (rb + 1) * blk) for rb in range(nblk)]

    def add_pv(rows, p_ref, j):
        acc_ref[rows, :] = (lanes_rep(alpha_ref[rows, :], 2 * LANES) * acc_ref[rows, :]
                            + _dot(p_ref[rows, :], v_tile(j)))

    def step(q, q_next, j, s_cur, s_nxt, p_cur, p_prv):
        q_sel, k_nxt = (q, k_tile(j + 1)) if j + 1 < n else (q_next, k_tile(0))
        for rows in row_blocks:
            if j > 0:
                add_pv(rows, p_prv, j - 1)
            s_nxt[rows, :] = _dot(q_sel[rows], k_nxt)
            s = s_cur[rows, :]
            m_old = m_ref[rows, :]
            m_new = jnp.maximum(m_old, jnp.max(s, axis=1, keepdims=True))
            alpha_ref[rows, :] = jnp.exp2(m_old - m_new)
            m_ref[rows, :] = m_new
            p_cur[rows, :] = jnp.exp2(s - lanes_rep(m_new, tk)).astype(BF16)

    @pl.when(pl.program_id(0) == 0)
    def _():
        sa_ref[...] = _dot(q_rows(0), k_tile(0))

    for u in range(tiles):
        q, q_next = q_rows(u), q_rows(u + 1)
        acc_ref[...] = jnp.zeros_like(acc_ref)
        m_ref[...] = jnp.full_like(m_ref, -jnp.inf)
        for j in range(n):
            if j % 2 == 0:
                step(q, q_next, j, sa_ref, sb_ref, pa_ref, pb_ref)
            else:
                step(q, q_next, j, sb_ref, sa_ref, pb_ref, pa_ref)
        p_last = pb_ref if n % 2 == 0 else pa_ref
        for rows in row_blocks:
            add_pv(rows, p_last, n - 1)
        acc = acc_ref[...]
        o_lat = (acc[:, :MLA_KV_RANK] / acc[:, MLA_KV_RANK:MLA_KV_RANK + 1]).astype(BF16)
        for hd in range(H):
            o_ref[u * tq:(u + 1) * tq, hd * MLA_KV_RANK:(hd + 1) * MLA_KV_RANK] = o_lat[hd * tq:(hd + 1) * tq]


def _flash(q, kt, va, tq=64, tk=1024, tiles=1):
    H, T, _ = q.shape
    tq, tk = min(tq, T), min(tk, T)
    assert T % (2 * tk) == 0 and tq % BF16_SUBLANES == 0
    assert T % (tiles * tq) == 0
    rows = H * tq
    steps = T // (tiles * tq)
    last_tile = T // tq - 1
    return pl.pallas_call(
        functools.partial(_flash_kernel, tk=tk, nblk=4),
        grid=(steps,),
        scratch_shapes=[pltpu.VMEM((rows, tk), F32), pltpu.VMEM((rows, tk), F32), pltpu.VMEM((rows, tk), BF16),
                        pltpu.VMEM((rows, tk), BF16), pltpu.VMEM((rows, 2 * LANES), F32),
                        pltpu.VMEM((rows, LANES), F32), pltpu.VMEM((rows, LANES), F32)],
        in_specs=[
            pl.BlockSpec((H, tiles * tq, MLA_QK), lambda i: (0, i, 0)),
            pl.BlockSpec((H, tq, MLA_QK), lambda i: (0, jnp.minimum((i + 1) * tiles, last_tile), 0)),
            pl.BlockSpec((MLA_QK, T), lambda i: (0, 0)),
            pl.BlockSpec((T, 2 * LANES), lambda i: (0, 0)),
        ],
        out_specs=pl.BlockSpec((tiles * tq, H * MLA_KV_RANK), lambda i: (i, 0)),
        out_shape=jax.ShapeDtypeStruct((T, H * MLA_KV_RANK), BF16),
        compiler_params=_cparams(("arbitrary",)),
        name="mla_flash",
    )(q, q, kt, va)


def _mla_out_kernel(o_ref, wuv_ref, w_ref, h_ref, out_ref):
    v = [_dot(o_ref[:, hd * MLA_KV_RANK:(hd + 1) * MLA_KV_RANK], wuv_ref[hd]).astype(BF16)
         for hd in range(MLA_HEADS)]
    out_ref[...] = h_ref[...] + _dot(jnp.concatenate(v, axis=1), w_ref[...])


def _mla_out(o_lat, w_ukv, w_out, h, tm=512):
    T = h.shape[0]
    H = MLA_HEADS
    wuv = w_ukv.reshape(MLA_KV_RANK, H, MLA_NOPE + MLA_V)[:, :, MLA_NOPE:].transpose(1, 0, 2).astype(BF16)
    wb = w_out.astype(BF16)
    return pl.pallas_call(
        _mla_out_kernel,
        grid=(T // tm,),
        in_specs=[pl.BlockSpec((tm, H * MLA_KV_RANK), lambda i: (i, 0)), pl.BlockSpec(wuv.shape, lambda i: (0, 0, 0)),
                  pl.BlockSpec(wb.shape, lambda i: (0, 0)), pl.BlockSpec((tm, D_MODEL), lambda i: (i, 0))],
        out_specs=pl.BlockSpec((tm, D_MODEL), lambda i: (i, 0)),
        out_shape=jax.ShapeDtypeStruct((T, D_MODEL), F32),
        compiler_params=_cparams(("arbitrary",)),
        name="mla_out",
    )(o_lat, wuv, wb, h)


def _mla_mixer(h, positions, gain, w_in, q_norm, w_uq, kv_norm, w_ukv, w_out):
    q, kt, va = _mla_in(h, positions, gain, w_in, q_norm, w_uq, kv_norm, w_ukv)
    o_lat = _flash(q, kt, va)
    return _mla_out(o_lat, w_ukv, w_out, h)


def kernel(x, positions, mix_norm, ffn_norm, final_norm, gla_w_in, gla_w_gate_up_f, gla_b_gate_f, gla_w_gate_up_b,
           gla_b_gate_b, gla_head_norm, gla_w_out, mla_w_in, mla_q_norm, mla_w_uq, mla_kv_norm, mla_w_ukv,
           mla_w_out, moe_w_router, moe_w_gate, moe_w_up, moe_w_down):
    B, T, D = x.shape
    outs = []
    for b in range(B):
        h = x[b]
        h = _gla_mixer(h, mix_norm[0], gla_w_in[0], gla_w_gate_up_f[0], gla_b_gate_f[0], gla_w_gate_up_b[0],
                       gla_b_gate_b[0], gla_head_norm[0], gla_w_out[0])
        h = _ec_moe(h, ffn_norm[0], moe_w_router[0], moe_w_gate, moe_w_up, moe_w_down, 0)
        h = _mla_mixer(h, positions[b], mix_norm[1], mla_w_in[0], mla_q_norm[0], mla_w_uq[0], mla_kv_norm[0],
                       mla_w_ukv[0], mla_w_out[0])
        h = _ec_moe(h, ffn_norm[1], moe_w_router[1], moe_w_gate, moe_w_up, moe_w_down, 1, final_gain=final_norm)
        outs.append(h)
    return jnp.stack(outs)
```

```python
import functools
import math

import numpy as np
import jax
import jax.numpy as jnp
from jax import lax
from jax.experimental import pallas as pl
from jax.experimental.pallas import tpu as pltpu

F32 = jnp.float32
BF16 = jnp.bfloat16

D_MODEL = 1024
RMS_EPS = 1e-6

GLA_HEADS = 4
GLA_DK = 512
GLA_DV = 1024
GLA_HEAD_K = GLA_DK // GLA_HEADS
GLA_HEAD_V = GLA_DV // GLA_HEADS
GLA_GATE_RANK = 16
GLA_TAU = 16.0
GLA_CHUNK = 64
GLA_TILE = 256
GLA_LEVELS = 6

MLA_HEADS = 16
MLA_Q_RANK = 256
MLA_KV_RANK = 128
MLA_NOPE = 128
MLA_ROPE = 64
MLA_V = 128
MLA_QK = MLA_NOPE + MLA_ROPE
ROPE_BASE = 10000.0

N_EXPERTS = 16
EXPERT_FF = 2048
EC_CAPACITY_FACTOR = 2
MOE_TILE = 256
COMBINE_HEAD = 64
BF16_SUBLANES = 16
F32_SUBLANES = 8
LANES = 128

VMEM_LIMIT = 56 * 1024 * 1024


def _cparams(sem):
    return pltpu.CompilerParams(dimension_semantics=sem, vmem_limit_bytes=VMEM_LIMIT)


def _rms(x, g):
    return x * lax.rsqrt(jnp.mean(x * x, axis=-1, keepdims=True) + RMS_EPS) * g


def _split_bf16(x):
    hi = x.astype(BF16)
    lo = (x - hi.astype(F32)).astype(BF16)
    return hi, lo


def _dot(a, b):
    return jnp.dot(a, b, preferred_element_type=F32)


def _dot_nt(a, b):
    return lax.dot_general(a, b, (((1,), (1,)), ((), ())), preferred_element_type=F32)


def _dot_tn(a, b):
    return lax.dot_general(a, b, (((0,), (0,)), ((), ())), preferred_element_type=F32)


def _dot_split(a, b):
    ah, al = _split_bf16(a)
    bh, bl = _split_bf16(b)
    return _dot(ah, bh) + _dot(ah, bl) + _dot(al, bh)


def _gla_in_kernel(x_ref, g_ref, wqk_ref, wv_ref, wr_ref, wgd_ref, qk_ref, v_ref, r_ref, gd_ref):
    hn = _rms(x_ref[...], g_ref[...]).astype(BF16)
    qk_ref[...] = _dot(hn, wqk_ref[...])
    v_ref[...] = _dot(hn, wv_ref[...])
    r_ref[...] = _dot(hn, wr_ref[...])
    gd_ref[...] = _dot(hn, wgd_ref[...])


def _gla_in(x, gain, w_in, tm=512):
    T = x.shape[0]
    wqk = w_in[:, :2 * GLA_DK].astype(BF16)
    wv = w_in[:, 2 * GLA_DK:2 * GLA_DK + GLA_DV].astype(BF16)
    wr = w_in[:, 2 * GLA_DK + GLA_DV:2 * GLA_DK + 2 * GLA_DV].astype(BF16)
    wgd = w_in[:, 2 * GLA_DK + 2 * GLA_DV:].astype(BF16)
    ngd = 2 * GLA_GATE_RANK
    row = lambda n: pl.BlockSpec((tm, n), lambda i: (i, 0))
    full = lambda a: pl.BlockSpec(a.shape, lambda i: (0, 0))
    gain2 = gain.reshape(1, D_MODEL)
    return pl.pallas_call(
        _gla_in_kernel,
        grid=(T // tm,),
        in_specs=[row(D_MODEL), full(gain2), full(wqk), full(wv), full(wr), full(wgd)],
        out_specs=[row(2 * GLA_DK), row(GLA_DV), row(GLA_DV), row(ngd)],
        out_shape=[jax.ShapeDtypeStruct((T, 2 * GLA_DK), F32), jax.ShapeDtypeStruct((T, GLA_DV), F32),
                   jax.ShapeDtypeStruct((T, GLA_DV), F32), jax.ShapeDtypeStruct((T, ngd), F32)],
        compiler_params=_cparams(("arbitrary",)),
        name="gla_in",
    )(x, gain2, wqk, wv, wr, wgd)


def _gla_tables(reverse):
    n, c = GLA_TILE, GLA_CHUNK
    W = np.zeros((GLA_LEVELS + 3, n, n), np.float32)
    L = np.full((n, n), -1, np.int32)
    for t in range(n):
        c0 = (t // c) * c
        tt = t - c0
        for l in range(GLA_LEVELS):
            b = (c // 2) >> l
            p0 = c0 + (tt // (2 * b)) * 2 * b
            mid = p0 + b
            second = t >= mid
            if not reverse:
                if second:
                    W[l, t, mid:t + 1] = 1
                    L[t, p0:mid] = l
                else:
                    W[l, t, t + 1:mid] = 1
            else:
                if second:
                    W[l, t, mid:t] = 1
                else:
                    W[l, t, t:mid] = 1
                    L[t, mid:p0 + 2 * b] = l
        if not reverse:
            W[GLA_LEVELS, t, c0:t + 1] = 1
            W[GLA_LEVELS + 1, t, t + 1:c0 + c] = 1
            L[t, t] = GLA_LEVELS
        else:
            W[GLA_LEVELS, t, t:c0 + c] = 1
            W[GLA_LEVELS + 1, t, c0:t] = 1
        W[GLA_LEVELS + 2, t, c0:c0 + c] = 1
    return W.reshape(-1, n), L


def _gla_scan_kernel(qkf_ref, vf_ref, gdf_ref, qkb_ref, vb_ref, gdb_ref, wupf_ref, bf_ref, wupb_ref, bb_ref,
                     wf_ref, lf_ref, wb_ref, lb_ref, of_ref, ob_ref, s_ref):
    n, c, r = GLA_TILE, GLA_CHUNK, GLA_GATE_RANK

    @pl.when(pl.program_id(0) == 0)
    def _():
        s_ref[...] = jnp.zeros_like(s_ref)

    dirs = [(qkf_ref, vf_ref, gdf_ref[:, :r], wupf_ref, bf_ref, wf_ref, lf_ref, of_ref, False),
            (qkb_ref, vb_ref, gdb_ref[:, r:], wupb_ref, bb_ref, wb_ref, lb_ref, ob_ref, True)]

    factors = []
    for qk_ref, v_ref, gd, wup_ref, b_ref, w_ref, l_ref, o_ref, reverse in dirs:
        z = _dot_split(gd, wup_ref[...]) + b_ref[...]
        g = (jnp.minimum(z, 0.0) - jnp.log1p(jnp.exp(-jnp.abs(z)))) * (1.0 / GLA_TAU)
        ghi, glo = _split_bf16(g)
        w = w_ref[...]
        factors.append(jnp.exp(_dot(w, ghi) + _dot(w, glo)))

    chains = []
    for d, (qk_ref, v_ref, gd, wup_ref, b_ref, w_ref, l_ref, o_ref, reverse) in enumerate(dirs):
        lvl = l_ref[...]
        for h in range(GLA_HEADS):
            kcols = slice(h * GLA_HEAD_K, (h + 1) * GLA_HEAD_K)
            vcols = slice(h * GLA_HEAD_V, (h + 1) * GLA_HEAD_V)
            f = factors[d][:, kcols]
            q = qk_ref[:, kcols] * (GLA_HEAD_K ** -0.5)
            k = qk_ref[:, GLA_DK + h * GLA_HEAD_K:GLA_DK + (h + 1) * GLA_HEAD_K]
            vb = v_ref[:, vcols].astype(BF16)
            attn = jnp.zeros((n, n), F32)
            for l in range(GLA_LEVELS):
                fl = f[l * n:(l + 1) * n]
                p = _dot_nt((q * fl).astype(BF16), (k * fl).astype(BF16))
                attn = jnp.where(lvl == l, p, attn)
            if not reverse:
                p = _dot_nt(q.astype(BF16), k.astype(BF16))
                attn = jnp.where(lvl == GLA_LEVELS, p, attn)
            o_intra = _dot(attn.astype(BF16), vb)
            qh = (q * f[GLA_LEVELS * n:(GLA_LEVELS + 1) * n]).astype(BF16)
            kh = (k * f[(GLA_LEVELS + 1) * n:(GLA_LEVELS + 2) * n]).astype(BF16)
            ftot = f[(GLA_LEVELS + 2) * n:(GLA_LEVELS + 3) * n]
            chains.append(dict(d=d, h=h, vcols=vcols, o_ref=o_ref, reverse=reverse, vb=vb, o_intra=o_intra, qh=qh,
                               kh=kh, ftot=ftot, st=s_ref[d, h]))

    nchunks = n // c
    for step in range(nchunks):
        for ch in chains:
            j = nchunks - 1 - step if ch["reverse"] else step
            rows = slice(j * c, (j + 1) * c)
            st = ch["st"]
            ch["o_ref"][rows, ch["vcols"]] = ch["o_intra"][rows] + _dot_nt(ch["qh"][rows], st.astype(BF16))
            ch["st"] = st * ch["ftot"][j * c:j * c + 1, :] + _dot_tn(ch["vb"][rows], ch["kh"][rows])
    for ch in chains:
        s_ref[ch["d"], ch["h"]] = ch["st"]


def _gla_scan(qk, v, gd, w_up_f, b_f, w_up_b, b_b):
    T = qk.shape[0]
    n = GLA_TILE
    nt = T // n
    tables = []
    for reverse in (False, True):
        W, L = _gla_tables(reverse)
        tables += [jnp.asarray(W, BF16), jnp.asarray(L)]
    fwd = lambda width: pl.BlockSpec((n, width), lambda i: (i, 0))
    bwd = lambda width: pl.BlockSpec((n, width), lambda i: (nt - 1 - i, 0))
    full = lambda a: pl.BlockSpec(a.shape, lambda i: (0, 0))
    consts = [w_up_f, b_f.reshape(1, GLA_DK), w_up_b, b_b.reshape(1, GLA_DK)] + tables
    ngd = 2 * GLA_GATE_RANK
    out = jax.ShapeDtypeStruct((T, GLA_DV), F32)
    return pl.pallas_call(
        _gla_scan_kernel,
        grid=(nt,),
        in_specs=[fwd(2 * GLA_DK), fwd(GLA_DV), fwd(ngd), bwd(2 * GLA_DK), bwd(GLA_DV), bwd(ngd)]
                 + [full(a) for a in consts],
        out_specs=[fwd(GLA_DV), bwd(GLA_DV)],
        out_shape=[out, out],
        scratch_shapes=[pltpu.VMEM((2, GLA_HEADS, GLA_HEAD_V, GLA_HEAD_K), F32)],
        compiler_params=_cparams(("arbitrary",)),
        name="gla_scan",
    )(qk, v, gd, qk, v, gd, *consts)


def _gla_out_kernel(of_ref, ob_ref, r_ref, x_ref, hn_ref, w_ref, o_ref):
    acc = x_ref[...]
    for h in range(GLA_HEADS):
        cols = slice(h * GLA_HEAD_V, (h + 1) * GLA_HEAD_V)
        o = _rms(of_ref[:, cols] + ob_ref[:, cols], hn_ref[...])
        r = r_ref[:, cols]
        gated = o * (r * (1.0 / (1.0 + jnp.exp(-r))))
        acc = acc + _dot(gated.astype(BF16), w_ref[cols, :])
    o_ref[...] = acc


def _gla_out(of, ob, r, x, head_norm, w_out, tm=512):
    T = x.shape[0]
    w = w_out.astype(BF16)
    hn = head_norm.reshape(1, GLA_HEAD_V)
    row = lambda n: pl.BlockSpec((tm, n), lambda i: (i, 0))
    full = lambda a: pl.BlockSpec(a.shape, lambda i: (0, 0))
    return pl.pallas_call(
        _gla_out_kernel,
        grid=(T // tm,),
        in_specs=[row(GLA_DV), row(GLA_DV), row(GLA_DV), row(D_MODEL), full(hn), full(w)],
        out_specs=row(D_MODEL),
        out_shape=jax.ShapeDtypeStruct((T, D_MODEL), F32),
        compiler_params=_cparams(("arbitrary",)),
        name="gla_out",
    )(of, ob, r, x, hn, w)


def _gla_mixer(x, gain, w_in, w_up_f, b_f, w_up_b, b_b, head_norm, w_out):
    qk, v, r, gd = _gla_in(x, gain, w_in)
    of, ob = _gla_scan(qk, v, gd, w_up_f, b_f, w_up_b, b_b)
    return _gla_out(of, ob, r, x, head_norm, w_out)


def _router_kernel(h_ref, g_ref, w_ref, aff_ref, hn_ref):
    hn = _rms(h_ref[...], g_ref[...])
    hn_ref[...] = hn.astype(BF16)
    logits = _dot_split(hn, w_ref[...])
    e = jnp.exp(logits - jnp.max(logits, axis=-1, keepdims=True))
    aff_ref[...] = e / jnp.sum(e, axis=-1, keepdims=True)


def _router(h, gain, w_router, tm=512):
    T = h.shape[0]
    gain2 = gain.reshape(1, D_MODEL)
    row = lambda n: pl.BlockSpec((tm, n), lambda i: (i, 0))
    full = lambda a: pl.BlockSpec(a.shape, lambda i: (0, 0))
    return pl.pallas_call(
        _router_kernel,
        grid=(T // tm,),
        in_specs=[row(D_MODEL), full(gain2), full(w_router)],
        out_specs=[row(N_EXPERTS), row(D_MODEL)],
        out_shape=[jax.ShapeDtypeStruct((T, N_EXPERTS), F32), jax.ShapeDtypeStruct((T, D_MODEL), BF16)],
        compiler_params=_cparams(("arbitrary",)),
        name="moe_router",
    )(h, gain2, w_router)


def _select_kernel(aff_ref, tri_ref, posm_ref, before_ref, *, cap):
    T = aff_ref.shape[1]
    bits = pltpu.bitcast(aff_ref[...], jnp.int32)

    def search(it, thr):
        cand = thr | jnp.left_shift(jnp.int32(1), 30 - it)
        cnt = jnp.sum(jnp.where(bits >= cand, 1.0, 0.0), axis=1, keepdims=True)
        return jnp.where(cnt >= cap, cand, thr)

    thr = lax.fori_loop(0, 31, search, jnp.zeros((N_EXPERTS, 1), jnp.int32))
    n_gt = jnp.sum(jnp.where(bits > thr, 1.0, 0.0), axis=1, keepdims=True)
    need = cap - n_gt
    tri = tri_ref[...]

    def scan(j, carry):
        c_eq, c_sel = carry
        cols = pl.ds(pl.multiple_of(j * LANES, LANES), LANES)
        blk = pltpu.bitcast(aff_ref[:, cols], jnp.int32)
        eq = jnp.where(blk == thr, 1.0, 0.0)
        rank = _dot(eq.astype(BF16), tri) + c_eq - eq
        sel = jnp.where((blk > thr) | ((eq > 0.0) & (rank < need)), 1.0, 0.0)
        before = _dot(sel.astype(BF16), tri) + c_sel - sel
        before_ref[:, cols] = before.astype(jnp.int32)
        posm_ref[:, cols] = jnp.where(sel > 0.0, before, -1.0).astype(jnp.int32)
        return (c_eq + jnp.sum(eq, axis=1, keepdims=True), c_sel + jnp.sum(sel, axis=1, keepdims=True))

    zero = jnp.zeros((N_EXPERTS, 1), F32)
    lax.fori_loop(0, T // LANES, scan, (zero, zero))


def _select(aff_t, cap):
    T = aff_t.shape[1]
    tri = jnp.asarray(np.triu(np.ones((LANES, LANES), np.float32)), BF16)
    full = lambda a: pl.BlockSpec(a.shape, lambda: (0,) * a.ndim)
    out = jax.ShapeDtypeStruct((N_EXPERTS, T), jnp.int32)
    return pl.pallas_call(
        functools.partial(_select_kernel, cap=cap),
        in_specs=[full(aff_t), full(tri)],
        out_specs=[pl.BlockSpec((N_EXPERTS, T), lambda: (0, 0))] * 2,
        out_shape=[out, out],
        compiler_params=pltpu.CompilerParams(vmem_limit_bytes=VMEM_LIMIT),
        name="moe_select",
    )(aff_t, tri)


def _dispatch_window():
    return MOE_TILE + F32_SUBLANES


def _dispatch_kernel(start_ref, hn_ref, posm_ref, x_ref, acc_ref, *, cap, nt, sub):
    e, t = pl.program_id(0), pl.program_id(1)
    win = _dispatch_window()

    @pl.when(t == 0)
    def _():
        acc_ref[...] = jnp.zeros_like(acc_ref)

    for s in range(sub):
        tok = slice(s * MOE_TILE, (s + 1) * MOE_TILE)
        base = pl.multiple_of((start_ref[e * nt + t * sub + s] // F32_SUBLANES) * F32_SUBLANES, F32_SUBLANES)
        slot = base + lax.broadcasted_iota(jnp.int32, (win, MOE_TILE), 0)
        onehot = jnp.where(posm_ref[0, :, tok] == slot, 1.0, 0.0).astype(BF16)
        acc_ref[pl.ds(base, win), :] += _dot(onehot, hn_ref[tok, :])

    @pl.when(t == pl.num_programs(1) - 1)
    def _():
        x_ref[0] = acc_ref[:cap, :].astype(BF16)


def _dispatch(hn, posm, start, cap, sub=16):
    T = hn.shape[0]
    nt = T // MOE_TILE
    sub = math.gcd(sub, nt)
    blk = sub * MOE_TILE
    posm3 = posm.reshape(N_EXPERTS, 1, T)
    grid_spec = pltpu.PrefetchScalarGridSpec(
        num_scalar_prefetch=1,
        grid=(N_EXPERTS, nt // sub),
        in_specs=[
            pl.BlockSpec((blk, D_MODEL), lambda e, t, s: (t, 0)),
            pl.BlockSpec((1, 1, blk), lambda e, t, s: (e, 0, t)),
        ],
        out_specs=pl.BlockSpec((1, cap, D_MODEL), lambda e, t, s: (e, 0, 0)),
        scratch_shapes=[pltpu.VMEM((cap + _dispatch_window(), D_MODEL), F32)],
    )
    return pl.pallas_call(
        functools.partial(_dispatch_kernel, cap=cap, nt=nt, sub=sub),
        grid_spec=grid_spec,
        out_shape=jax.ShapeDtypeStruct((N_EXPERTS, cap, D_MODEL), BF16),
        compiler_params=_cparams(("arbitrary", "arbitrary")),
        name="moe_dispatch",
    )(start.reshape(-1), hn, posm3)


def _ffn_kernel(x_ref, wg_ref, wu_ref, wd_ref, y_ref, acc_ref):
    f = pl.program_id(1)

    @pl.when(f == 0)
    def _():
        acc_ref[...] = jnp.zeros_like(acc_ref)

    wg = wg_ref[0, 0].astype(BF16)
    wu = wu_ref[0, 0].astype(BF16)
    wd = wd_ref[0, 0].astype(BF16)
    cap = x_ref.shape[1]
    blk = cap // math.gcd(cap // F32_SUBLANES, 4)
    for r0 in range(0, cap, blk):
        rows = slice(r0, r0 + blk)
        x = x_ref[0, rows, :]
        a = _dot(x, wg)
        u = _dot(x, wu)
        mid = (a * (1.0 / (1.0 + jnp.exp(-a))) * u).astype(BF16)
        acc_ref[rows, :] += _dot(mid, wd)

    @pl.when(f == pl.num_programs(1) - 1)
    def _():
        y_ref[0] = acc_ref[...].astype(BF16)


def _ffn(x, w_gate, w_up, w_down, layer, tf=512):
    cap = x.shape[1]
    return pl.pallas_call(
        _ffn_kernel,
        grid=(N_EXPERTS, EXPERT_FF // tf),
        in_specs=[
            pl.BlockSpec((1, cap, D_MODEL), lambda e, f: (e, 0, 0)),
            pl.BlockSpec((1, 1, D_MODEL, tf), lambda e, f: (layer, e, 0, f)),
            pl.BlockSpec((1, 1, D_MODEL, tf), lambda e, f: (layer, e, 0, f)),
            pl.BlockSpec((1, 1, tf, D_MODEL), lambda e, f: (layer, e, f, 0)),
        ],
        out_specs=pl.BlockSpec((1, cap, D_MODEL), lambda e, f: (e, 0, 0)),
        out_shape=jax.ShapeDtypeStruct((N_EXPERTS, cap, D_MODEL), BF16),
        scratch_shapes=[pltpu.VMEM((cap, D_MODEL), F32)],
        compiler_params=_cparams(("arbitrary", "arbitrary")),
        name="moe_ffn",
    )(x, w_gate, w_up, w_down)


def _combine_window():
    return MOE_TILE + BF16_SUBLANES


def _combine_kernel(start_ref, h_ref, aff_ref, posm_ref, g_ref, y_hbm, o_ref, hbuf_ref, buf_ref, hsem_ref, rsem_ref,
                    acc_ref, *, cap, nt, final_norm):
    t = pl.program_id(0)
    win = _combine_window()
    head = COMBINE_HEAD

    def base_of(tile, e):
        b = (start_ref[e * (nt + 1) + tile] // BF16_SUBLANES) * BF16_SUBLANES
        return pl.multiple_of(jnp.minimum(b, cap - win), BF16_SUBLANES)

    def needs_rest(tile, e):
        return start_ref[e * (nt + 1) + tile + 1] > base_of(tile, e) + head

    def head_copy(tile, e, slot):
        return pltpu.make_async_copy(y_hbm.at[e, pl.ds(base_of(tile, e), head), :],
                                     hbuf_ref.at[slot, pl.ds(e * head, head), :], hsem_ref.at[slot])

    def rest_copy(tile, e, slot):
        return pltpu.make_async_copy(y_hbm.at[e, pl.ds(base_of(tile, e) + head, win - head), :],
                                     buf_ref.at[slot, e], rsem_ref.at[slot, e])

    def fetch(tile, slot):
        for e in range(N_EXPERTS):
            head_copy(tile, e, slot).start()

            @pl.when(needs_rest(tile, e))
            def _():
                rest_copy(tile, e, slot).start()

    slot = t % 2

    @pl.when(t == 0)
    def _():
        fetch(0, 0)

    @pl.when(t + 1 < nt)
    def _():
        fetch(t + 1, 1 - slot)

    def expand(e, lo, hi):
        slots = base_of(t, e) + lo + lax.broadcasted_iota(jnp.int32, (MOE_TILE, hi - lo), 1)
        onehot = jnp.where(posm_ref[:, e:e + 1] == slots, 1.0, 0.0).astype(BF16)
        return aff_ref[:, e:e + 1] * _dot(onehot, buf_ref[slot, e, lo - head:hi - head, :])

    for e in range(N_EXPERTS):
        head_copy(t, e, slot).wait()

    per_group = LANES // head
    a_hi, a_lo = _split_bf16(aff_ref[...])
    a_hi, a_lo = a_hi.astype(F32), a_lo.astype(F32)
    lane = lax.broadcasted_iota(jnp.int32, (MOE_TILE, LANES), 1)
    w_hi, w_lo = [], []
    for grp in range(N_EXPERTS // per_group):
        target = jnp.zeros((MOE_TILE, LANES), jnp.int32)
        pos = jnp.zeros((MOE_TILE, LANES), jnp.int32) - 1
        hi, lo = jnp.zeros((MOE_TILE, LANES), F32), jnp.zeros((MOE_TILE, LANES), F32)
        for k in range(per_group):
            e = grp * per_group + k
            mine = (lane >= k * head) & (lane < (k + 1) * head)
            target = jnp.where(mine, base_of(t, e) + lane - k * head, target)
            pos = jnp.where(mine, posm_ref[:, e:e + 1], pos)
            hi = jnp.where(mine, a_hi[:, e:e + 1], hi)
            lo = jnp.where(mine, a_lo[:, e:e + 1], lo)
        hit = pos == target
        w_hi.append(jnp.where(hit, hi, 0.0).astype(BF16))
        w_lo.append(jnp.where(hit, lo, 0.0).astype(BF16))
    y_heads = hbuf_ref[slot]
    acc_ref[...] = (h_ref[...] + _dot(jnp.concatenate(w_hi, axis=1), y_heads)
                    + _dot(jnp.concatenate(w_lo, axis=1), y_heads))

    any_rest = needs_rest(t, 0)
    for e in range(1, N_EXPERTS):
        any_rest = any_rest | needs_rest(t, e)

    @pl.when(any_rest)
    def _():
        for e in range(N_EXPERTS):
            @pl.when(needs_rest(t, e))
            def _():
                rest_copy(t, e, slot).wait()
                acc_ref[...] += expand(e, head, MOE_TILE) + expand(e, MOE_TILE, win)

    acc = acc_ref[...]
    if final_norm:
        acc = _rms(acc, g_ref[...])
    o_ref[...] = acc


def _combine(h, aff, posm_t, start, y, cap, final_gain):
    T = h.shape[0]
    nt = T // MOE_TILE
    win = _combine_window()
    assert cap >= win and (cap - win) % BF16_SUBLANES == 0
    final_norm = final_gain is not None
    gain = (final_gain if final_norm else jnp.ones((D_MODEL,), F32)).reshape(1, D_MODEL)
    grid_spec = pltpu.PrefetchScalarGridSpec(
        num_scalar_prefetch=1,
        grid=(nt,),
        in_specs=[
            pl.BlockSpec((MOE_TILE, D_MODEL), lambda t, s: (t, 0)),
            pl.BlockSpec((MOE_TILE, N_EXPERTS), lambda t, s: (t, 0)),
            pl.BlockSpec((MOE_TILE, N_EXPERTS), lambda t, s: (t, 0)),
            pl.BlockSpec((1, D_MODEL), lambda t, s: (0, 0)),
            pl.BlockSpec(memory_space=pl.ANY),
        ],
        out_specs=pl.BlockSpec((MOE_TILE, D_MODEL), lambda t, s: (t, 0)),
        scratch_shapes=[pltpu.VMEM((2, N_EXPERTS * COMBINE_HEAD, D_MODEL), BF16),
                        pltpu.VMEM((2, N_EXPERTS, win - COMBINE_HEAD, D_MODEL), BF16),
                        pltpu.SemaphoreType.DMA((2,)), pltpu.SemaphoreType.DMA((2, N_EXPERTS)),
                        pltpu.VMEM((MOE_TILE, D_MODEL), F32)],
    )
    return pl.pallas_call(
        functools.partial(_combine_kernel, cap=cap, nt=nt, final_norm=final_norm),
        grid_spec=grid_spec,
        out_shape=jax.ShapeDtypeStruct((T, D_MODEL), F32),
        compiler_params=_cparams(("arbitrary",)),
        name="moe_combine",
    )(start.reshape(-1), h, aff, posm_t, gain, y)


def _ec_moe(h, gain, w_router, w_gate, w_up, w_down, layer, final_gain=None):
    T = h.shape[0]
    cap = max(1, EC_CAPACITY_FACTOR * T // N_EXPERTS)
    aff, hn = _router(h, gain, w_router)
    posm, before = _select(aff.T, cap)
    start = before[:, ::MOE_TILE]
    x = _dispatch(hn, posm, start, cap)
    y = _ffn(x, w_gate, w_up, w_down, layer)
    start_end = jnp.concatenate([start, jnp.full((N_EXPERTS, 1), cap, jnp.int32)], axis=1)
    return _combine(h, aff, posm.T, start_end, y, cap, final_gain)


def _rms_cols(x, g):
    return x * lax.rsqrt(jnp.mean(x * x, axis=0, keepdims=True) + RMS_EPS) * g


MLA_QCOLS = MLA_NOPE + 2 * LANES


def _mla_in_kernel(h_ref, posc_ref, posr_ref, g_ref, win_ref, wintkv_ref, qn_ref, wq_ref, wukt_ref, kvn_ref,
                   kvnc_ref, freqr_ref, sign_ref, freqc_ref, q_out, kt_out, va_out):
    half = MLA_ROPE // 2
    hn = _rms(h_ref[...], g_ref[...]).astype(BF16)
    c = _dot(hn, win_ref[...])
    ckv_t = _dot_nt(wintkv_ref[...], hn)

    cq = _rms(c[:, :MLA_Q_RANK], qn_ref[...]).astype(BF16)
    qa = _dot(cq, wq_ref[...])
    ang = posc_ref[...].astype(F32) * freqr_ref[...]
    cos, sin_signed = jnp.cos(ang), jnp.sin(ang) * sign_ref[...]
    qscale = MLA_QK ** -0.5 * math.log2(math.e)
    for hd in range(MLA_HEADS):
        c0 = hd * MLA_QCOLS
        q_lat = _dot(qa[:, c0:c0 + MLA_NOPE].astype(BF16), wukt_ref[hd])
        rot = qa[:, c0 + MLA_NOPE:c0 + MLA_NOPE + LANES] * cos + qa[:, c0 + MLA_NOPE + LANES:c0 + MLA_QCOLS] * sin_signed
        q_out[hd, :, :MLA_NOPE] = (q_lat * qscale).astype(BF16)
        q_out[hd, :, MLA_NOPE:] = (rot[:, :MLA_ROPE] * qscale).astype(BF16)

    kt_out[:MLA_KV_RANK, :] = _rms_cols(ckv_t[:MLA_KV_RANK], kvnc_ref[...]).astype(BF16)
    ang_t = freqc_ref[...] * posr_ref[...].astype(F32)
    cos_t, sin_t = jnp.cos(ang_t), jnp.sin(ang_t)
    k1, k2 = ckv_t[MLA_KV_RANK:MLA_KV_RANK + half], ckv_t[MLA_KV_RANK + half:]
    kt_out[MLA_KV_RANK:MLA_KV_RANK + half, :] = (k1 * cos_t - k2 * sin_t).astype(BF16)
    kt_out[MLA_KV_RANK + half:, :] = (k1 * sin_t + k2 * cos_t).astype(BF16)

    ckv = _rms(c[:, MLA_Q_RANK:MLA_Q_RANK + MLA_KV_RANK], kvn_ref[...])
    va_out[:, :MLA_KV_RANK] = ckv.astype(BF16)
    lane = lax.broadcasted_iota(jnp.int32, (ckv.shape[0], LANES), 1)
    va_out[:, MLA_KV_RANK:] = jnp.where(lane == 0, 1.0, 0.0).astype(BF16)


def _mla_in(h, positions, gain, w_in, q_norm, w_uq, kv_norm, w_ukv, tm=256):
    T = h.shape[0]
    H, half = MLA_HEADS, MLA_ROPE // 2
    wq = w_uq.reshape(MLA_Q_RANK, H, MLA_QK)
    x1, x2 = wq[:, :, MLA_NOPE:MLA_NOPE + half], wq[:, :, MLA_NOPE + half:]
    pad = jnp.zeros((MLA_Q_RANK, H, LANES - MLA_ROPE), F32)
    wq_wide = jnp.concatenate([wq[:, :, :MLA_NOPE], x1, x2, pad, x2, x1, pad], axis=2)
    wq_wide = wq_wide.reshape(MLA_Q_RANK, H * MLA_QCOLS).astype(BF16)
    wuk_t = w_ukv.reshape(MLA_KV_RANK, H, MLA_NOPE + MLA_V)[:, :, :MLA_NOPE].transpose(1, 2, 0).astype(BF16)
    inv_freq = ROPE_BASE ** (-jnp.arange(half, dtype=F32) / half)
    zeros = jnp.zeros((LANES - MLA_ROPE,), F32)
    freq_row = jnp.concatenate([inv_freq, inv_freq, zeros]).reshape(1, LANES)
    sign_row = jnp.concatenate([-jnp.ones((half,), F32), jnp.ones((half,), F32), zeros]).reshape(1, LANES)
    w_in_b = w_in.astype(BF16)
    args = [h, positions.reshape(T, 1), positions.reshape(1, T), gain.reshape(1, -1), w_in_b,
            w_in_b[:, MLA_Q_RANK:].T, q_norm.reshape(1, -1), wq_wide, wuk_t, kv_norm.reshape(1, -1),
            kv_norm.reshape(-1, 1), freq_row, sign_row, inv_freq.reshape(half, 1)]
    full = lambda a: pl.BlockSpec(a.shape, lambda i: (0,) * a.ndim)
    in_specs = [pl.BlockSpec((tm, D_MODEL), lambda i: (i, 0)), pl.BlockSpec((tm, 1), lambda i: (i, 0)),
                pl.BlockSpec((1, tm), lambda i: (0, i))] + [full(a) for a in args[3:]]
    return pl.pallas_call(
        _mla_in_kernel,
        grid=(T // tm,),
        in_specs=in_specs,
        out_specs=[pl.BlockSpec((H, tm, MLA_QK), lambda i: (0, i, 0)),
                   pl.BlockSpec((MLA_QK, tm), lambda i: (0, i)),
                   pl.BlockSpec((tm, 2 * LANES), lambda i: (i, 0))],
        out_shape=[jax.ShapeDtypeStruct((H, T, MLA_QK), BF16), jax.ShapeDtypeStruct((MLA_QK, T), BF16),
                   jax.ShapeDtypeStruct((T, 2 * LANES), BF16)],
        compiler_params=_cparams(("arbitrary",)),
        name="mla_in",
    )(*args)


def _flash_kernel(q_ref, qn_ref, kt_ref, va_ref, o_ref, sa_ref, sb_ref, pa_ref, pb_ref, acc_ref, m_ref, alpha_ref,
                  *, tk, nblk):
    H, tq, _ = qn_ref.shape
    T = kt_ref.shape[1]
    n = T // tk
    tiles = q_ref.shape[1] // tq

    def q_rows(u):
        if isinstance(u, int):
            blk = qn_ref[...] if u == tiles else q_ref[:, u * tq:(u + 1) * tq, :]
        else:
            blk = q_ref[:, pl.ds(pl.multiple_of(u * tq, tq), tq), :]
        return blk.reshape(H * tq, MLA_QK)

    def k_tile(j):
        return kt_ref[:, j * tk:(j + 1) * tk]

    def v_tile(j):
        return va_ref[j * tk:(j + 1) * tk, :]

    def lanes_rep(x, width):
        return jnp.concatenate([x] * (width // LANES), axis=1)

    blk = H * tq // nblk
    row_blocks = [slice(rb * blk, (rb + 1) * blk) for rb in range(nblk)]

    def add_pv(rows, p_ref, j):
        acc_ref[rows, :] = (lanes_rep(alpha_ref[rows, :], 2 * LANES) * acc_ref[rows, :]
                            + _dot(p_ref[rows, :], v_tile(j)))

    def step(q, q_next, j, s_cur, s_nxt, p_cur, p_prv):
        q_sel, k_nxt = (q, k_tile(j + 1)) if j + 1 < n else (q_next, k_tile(0))
        for rows in row_blocks:
            if j > 0:
                add_pv(rows, p_prv, j - 1)
            s_nxt[rows, :] = _dot(q_sel[rows], k_nxt)
            s = s_cur[rows, :]
            m_old = m_ref[rows, :]
            m_new = jnp.maximum(m_old, jnp.max(s, axis=1, keepdims=True))
            alpha_ref[rows, :] = jnp.exp2(m_old - m_new)
            m_ref[rows, :] = m_new
            p_cur[rows, :] = jnp.exp2(s - lanes_rep(m_new, tk)).astype(BF16)

    @pl.when(pl.program_id(0) == 0)
    def _():
        sa_ref[...] = _dot(q_rows(0), k_tile(0))

    def token_tile(u, carry):
        q = q_rows(u)
        q_next = jnp.where(u == tiles - 1, q_rows(tiles), q_rows(jnp.minimum(u + 1, tiles - 1)))
        acc_ref[...] = jnp.zeros_like(acc_ref)
        m_ref[...] = jnp.full_like(m_ref, -jnp.inf)
        for j in range(n):
            if j % 2 == 0:
                step(q, q_next, j, sa_ref, sb_ref, pa_ref, pb_ref)
            else:
                step(q, q_next, j, sb_ref, sa_ref, pb_ref, pa_ref)
        p_last = pb_ref if n % 2 == 0 else pa_ref
        for rows in row_blocks:
            add_pv(rows, p_last, n - 1)
        acc = acc_ref[...]
        o_lat = (acc[:, :MLA_KV_RANK] / acc[:, MLA_KV_RANK:MLA_KV_RANK + 1]).astype(BF16)
        out_rows = pl.ds(pl.multiple_of(u * tq, tq), tq)
        for hd in range(H):
            o_ref[out_rows, hd * MLA_KV_RANK:(hd + 1) * MLA_KV_RANK] = o_lat[hd * tq:(hd + 1) * tq]
        return carry

    lax.fori_loop(0, tiles, token_tile, 0)


def _flash(q, kt, va, tq=64, tk=1024, tiles=4):
    H, T, _ = q.shape
    tq, tk = min(tq, T), min(tk, T)
    assert T % (2 * tk) == 0 and tq % BF16_SUBLANES == 0
    assert T % (tiles * tq) == 0
    rows = H * tq
    steps = T // (tiles * tq)
    last_tile = T // tq - 1
    return pl.pallas_call(
        functools.partial(_flash_kernel, tk=tk, nblk=4),
        grid=(steps,),
        scratch_shapes=[pltpu.VMEM((rows, tk), F32), pltpu.VMEM((rows, tk), F32), pltpu.VMEM((rows, tk), BF16),
                        pltpu.VMEM((rows, tk), BF16), pltpu.VMEM((rows, 2 * LANES), F32),
                        pltpu.VMEM((rows, LANES), F32), pltpu.VMEM((rows, LANES), F32)],
        in_specs=[
            pl.BlockSpec((H, tiles * tq, MLA_QK), lambda i: (0, i, 0)),
            pl.BlockSpec((H, tq, MLA_QK), lambda i: (0, jnp.minimum((i + 1) * tiles, last_tile), 0)),
            pl.BlockSpec((MLA_QK, T), lambda i: (0, 0)),
            pl.BlockSpec((T, 2 * LANES), lambda i: (0, 0)),
        ],
        out_specs=pl.BlockSpec((tiles * tq, H * MLA_KV_RANK), lambda i: (i, 0)),
        out_shape=jax.ShapeDtypeStruct((T, H * MLA_KV_RANK), BF16),
        compiler_params=_cparams(("arbitrary",)),
        name="mla_flash",
    )(q, q, kt, va)


def _mla_out_kernel(o_ref, wuv_ref, w_ref, h_ref, out_ref):
    v = [_dot(o_ref[:, hd * MLA_KV_RANK:(hd + 1) * MLA_KV_RANK], wuv_ref[hd]).astype(BF16)
         for hd in range(MLA_HEADS)]
    out_ref[...] = h_ref[...] + _dot(jnp.concatenate(v, axis=1), w_ref[...])


def _mla_out(o_lat, w_ukv, w_out, h, tm=512):
    T = h.shape[0]
    H = MLA_HEADS
    wuv = w_ukv.reshape(MLA_KV_RANK, H, MLA_NOPE + MLA_V)[:, :, MLA_NOPE:].transpose(1, 0, 2).astype(BF16)
    wb = w_out.astype(BF16)
    return pl.pallas_call(
        _mla_out_kernel,
        grid=(T // tm,),
        in_specs=[pl.BlockSpec((tm, H * MLA_KV_RANK), lambda i: (i, 0)), pl.BlockSpec(wuv.shape, lambda i: (0, 0, 0)),
                  pl.BlockSpec(wb.shape, lambda i: (0, 0)), pl.BlockSpec((tm, D_MODEL), lambda i: (i, 0))],
        out_specs=pl.BlockSpec((tm, D_MODEL), lambda i: (i, 0)),
        out_shape=jax.ShapeDtypeStruct((T, D_MODEL), F32),
        compiler_params=_cparams(("arbitrary",)),
        name="mla_out",
    )(o_lat, wuv, wb, h)


def _mla_mixer(h, positions, gain, w_in, q_norm, w_uq, kv_norm, w_ukv, w_out):
    q, kt, va = _mla_in(h, positions, gain, w_in, q_norm, w_uq, kv_norm, w_ukv)
    o_lat = _flash(q, kt, va)
    return _mla_out(o_lat, w_ukv, w_out, h)


def kernel(x, positions, mix_norm, ffn_norm, final_norm, gla_w_in, gla_w_gate_up_f, gla_b_gate_f, gla_w_gate_up_b,
           gla_b_gate_b, gla_head_norm, gla_w_out, mla_w_in, mla_q_norm, mla_w_uq, mla_kv_norm, mla_w_ukv,
           mla_w_out, moe_w_router, moe_w_gate, moe_w_up, moe_w_down):
    B, T, D = x.shape
    outs = []
    for b in range(B):
        h = x[b]
        h = _gla_mixer(h, mix_norm[0], gla_w_in[0], gla_w_gate_up_f[0], gla_b_gate_f[0], gla_w_gate_up_b[0],
                       gla_b_gate_b[0], gla_head_norm[0], gla_w_out[0])
        h = _ec_moe(h, ffn_norm[0], moe_w_router[0], moe_w_gate, moe_w_up, moe_w_down, 0)
        h = _mla_mixer(h, positions[b], mix_norm[1], mla_w_in[0], mla_q_norm[0], mla_w_uq[0], mla_kv_norm[0],
                       mla_w_ukv[0], mla_w_out[0])
        h = _ec_moe(h, ffn_norm[1], moe_w_router[1], moe_w_gate, moe_w_up, moe_w_down, 1, final_gain=final_norm)
        outs.append(h)
    return jnp.stack(outs)
```

```python
import functools
import math

import numpy as np
import jax
import jax.numpy as jnp
from jax import lax
from jax.experimental import pallas as pl
from jax.experimental.pallas import tpu as pltpu

F32 = jnp.float32
BF16 = jnp.bfloat16

D_MODEL = 1024
RMS_EPS = 1e-6

GLA_HEADS = 4
GLA_DK = 512
GLA_DV = 1024
GLA_HEAD_K = GLA_DK // GLA_HEADS
GLA_HEAD_V = GLA_DV // GLA_HEADS
GLA_GATE_RANK = 16
GLA_TAU = 16.0
GLA_CHUNK = 64
GLA_TILE = 256
GLA_LEVELS = 6

MLA_HEADS = 16
MLA_Q_RANK = 256
MLA_KV_RANK = 128
MLA_NOPE = 128
MLA_ROPE = 64
MLA_V = 128
MLA_QK = MLA_NOPE + MLA_ROPE
ROPE_BASE = 10000.0

N_EXPERTS = 16
EXPERT_FF = 2048
EC_CAPACITY_FACTOR = 2
MOE_TILE = 256
COMBINE_HEAD = 64
BF16_SUBLANES = 16
F32_SUBLANES = 8
LANES = 128

VMEM_LIMIT = 56 * 1024 * 1024


def _cparams(sem):
    return pltpu.CompilerParams(dimension_semantics=sem, vmem_limit_bytes=VMEM_LIMIT)


def _rms(x, g):
    return x * lax.rsqrt(jnp.mean(x * x, axis=-1, keepdims=True) + RMS_EPS) * g


def _split_bf16(x):
    hi = x.astype(BF16)
    lo = (x - hi.astype(F32)).astype(BF16)
    return hi, lo


def _dot(a, b):
    return jnp.dot(a, b, preferred_element_type=F32)


def _dot_nt(a, b):
    return lax.dot_general(a, b, (((1,), (1,)), ((), ())), preferred_element_type=F32)


def _dot_tn(a, b):
    return lax.dot_general(a, b, (((0,), (0,)), ((), ())), preferred_element_type=F32)


def _dot_split(a, b):
    ah, al = _split_bf16(a)
    bh, bl = _split_bf16(b)
    return _dot(ah, bh) + _dot(ah, bl) + _dot(al, bh)


def _gla_in_kernel(x_ref, g_ref, wqk_ref, wv_ref, wr_ref, wgd_ref, qk_ref, v_ref, r_ref, gd_ref):
    hn = _rms(x_ref[...], g_ref[...]).astype(BF16)
    qk_ref[...] = _dot(hn, wqk_ref[...])
    v_ref[...] = _dot(hn, wv_ref[...])
    r_ref[...] = _dot(hn, wr_ref[...])
    gd_ref[...] = _dot(hn, wgd_ref[...])


def _gla_in(x, gain, w_in, tm=512):
    T = x.shape[0]
    wqk = w_in[:, :2 * GLA_DK].astype(BF16)
    wv = w_in[:, 2 * GLA_DK:2 * GLA_DK + GLA_DV].astype(BF16)
    wr = w_in[:, 2 * GLA_DK + GLA_DV:2 * GLA_DK + 2 * GLA_DV].astype(BF16)
    wgd = w_in[:, 2 * GLA_DK + 2 * GLA_DV:].astype(BF16)
    ngd = 2 * GLA_GATE_RANK
    row = lambda n: pl.BlockSpec((tm, n), lambda i: (i, 0))
    full = lambda a: pl.BlockSpec(a.shape, lambda i: (0, 0))
    gain2 = gain.reshape(1, D_MODEL)
    return pl.pallas_call(
        _gla_in_kernel,
        grid=(T // tm,),
        in_specs=[row(D_MODEL), full(gain2), full(wqk), full(wv), full(wr), full(wgd)],
        out_specs=[row(2 * GLA_DK), row(GLA_DV), row(GLA_DV), row(ngd)],
        out_shape=[jax.ShapeDtypeStruct((T, 2 * GLA_DK), F32), jax.ShapeDtypeStruct((T, GLA_DV), F32),
                   jax.ShapeDtypeStruct((T, GLA_DV), F32), jax.ShapeDtypeStruct((T, ngd), F32)],
        compiler_params=_cparams(("arbitrary",)),
        name="gla_in",
    )(x, gain2, wqk, wv, wr, wgd)


def _gla_tables(reverse):
    n, c = GLA_TILE, GLA_CHUNK
    W = np.zeros((GLA_LEVELS + 3, n, n), np.float32)
    L = np.full((n, n), -1, np.int32)
    for t in range(n):
        c0 = (t // c) * c
        tt = t - c0
        for l in range(GLA_LEVELS):
            b = (c // 2) >> l
            p0 = c0 + (tt // (2 * b)) * 2 * b
            mid = p0 + b
            second = t >= mid
            if not reverse:
                if second:
                    W[l, t, mid:t + 1] = 1
                    L[t, p0:mid] = l
                else:
                    W[l, t, t + 1:mid] = 1
            else:
                if second:
                    W[l, t, mid:t] = 1
                else:
                    W[l, t, t:mid] = 1
                    L[t, mid:p0 + 2 * b] = l
        if not reverse:
            W[GLA_LEVELS, t, c0:t + 1] = 1
            W[GLA_LEVELS + 1, t, t + 1:c0 + c] = 1
            L[t, t] = GLA_LEVELS
        else:
            W[GLA_LEVELS, t, t:c0 + c] = 1
            W[GLA_LEVELS + 1, t, c0:t] = 1
        W[GLA_LEVELS + 2, t, c0:c0 + c] = 1
    return W.reshape(-1, n), L


def _gla_scan_kernel(qkf_ref, vf_ref, gdf_ref, qkb_ref, vb_ref, gdb_ref, wupf_ref, bf_ref, wupb_ref, bb_ref,
                     wf_ref, lf_ref, wb_ref, lb_ref, of_ref, ob_ref, s_ref):
    n, c, r = GLA_TILE, GLA_CHUNK, GLA_GATE_RANK

    @pl.when(pl.program_id(0) == 0)
    def _():
        s_ref[...] = jnp.zeros_like(s_ref)

    dirs = [(qkf_ref, vf_ref, gdf_ref[:, :r], wupf_ref, bf_ref, wf_ref, lf_ref, of_ref, False),
            (qkb_ref, vb_ref, gdb_ref[:, r:], wupb_ref, bb_ref, wb_ref, lb_ref, ob_ref, True)]

    factors = []
    for qk_ref, v_ref, gd, wup_ref, b_ref, w_ref, l_ref, o_ref, reverse in dirs:
        z = _dot_split(gd, wup_ref[...]) + b_ref[...]
        g = (jnp.minimum(z, 0.0) - jnp.log1p(jnp.exp(-jnp.abs(z)))) * (1.0 / GLA_TAU)
        ghi, glo = _split_bf16(g)
        w = w_ref[...]
        factors.append(jnp.exp(_dot(w, ghi) + _dot(w, glo)))

    chains = []
    for d, (qk_ref, v_ref, gd, wup_ref, b_ref, w_ref, l_ref, o_ref, reverse) in enumerate(dirs):
        lvl = l_ref[...]
        for h in range(GLA_HEADS):
            kcols = slice(h * GLA_HEAD_K, (h + 1) * GLA_HEAD_K)
            vcols = slice(h * GLA_HEAD_V, (h + 1) * GLA_HEAD_V)
            f = factors[d][:, kcols]
            q = qk_ref[:, kcols] * (GLA_HEAD_K ** -0.5)
            k = qk_ref[:, GLA_DK + h * GLA_HEAD_K:GLA_DK + (h + 1) * GLA_HEAD_K]
            vb = v_ref[:, vcols].astype(BF16)
            attn = jnp.zeros((n, n), F32)
            for l in range(GLA_LEVELS):
                fl = f[l * n:(l + 1) * n]
                p = _dot_nt((q * fl).astype(BF16), (k * fl).astype(BF16))
                attn = jnp.where(lvl == l, p, attn)
            if not reverse:
                p = _dot_nt(q.astype(BF16), k.astype(BF16))
                attn = jnp.where(lvl == GLA_LEVELS, p, attn)
            o_intra = _dot(attn.astype(BF16), vb)
            qh = (q * f[GLA_LEVELS * n:(GLA_LEVELS + 1) * n]).astype(BF16)
            kh = (k * f[(GLA_LEVELS + 1) * n:(GLA_LEVELS + 2) * n]).astype(BF16)
            ftot = f[(GLA_LEVELS + 2) * n:(GLA_LEVELS + 3) * n]
            chains.append(dict(d=d, h=h, vcols=vcols, o_ref=o_ref, reverse=reverse, vb=vb, o_intra=o_intra, qh=qh,
                               kh=kh, ftot=ftot, st=s_ref[d, h]))

    nchunks = n // c
    for step in range(nchunks):
        for ch in chains:
            j = nchunks - 1 - step if ch["reverse"] else step
            rows = slice(j * c, (j + 1) * c)
            st = ch["st"]
            ch["o_ref"][rows, ch["vcols"]] = ch["o_intra"][rows] + _dot_nt(ch["qh"][rows], st.astype(BF16))
            ch["st"] = st * ch["ftot"][j * c:j * c + 1, :] + _dot_tn(ch["vb"][rows], ch["kh"][rows])
    for ch in chains:
        s_ref[ch["d"], ch["h"]] = ch["st"]


def _gla_scan(qk, v, gd, w_up_f, b_f, w_up_b, b_b):
    T = qk.shape[0]
    n = GLA_TILE
    nt = T // n
    tables = []
    for reverse in (False, True):
        W, L = _gla_tables(reverse)
        tables += [jnp.asarray(W, BF16), jnp.asarray(L)]
    fwd = lambda width: pl.BlockSpec((n, width), lambda i: (i, 0))
    bwd = lambda width: pl.BlockSpec((n, width), lambda i: (nt - 1 - i, 0))
    full = lambda a: pl.BlockSpec(a.shape, lambda i: (0, 0))
    consts = [w_up_f, b_f.reshape(1, GLA_DK), w_up_b, b_b.reshape(1, GLA_DK)] + tables
    ngd = 2 * GLA_GATE_RANK
    out = jax.ShapeDtypeStruct((T, GLA_DV), F32)
    return pl.pallas_call(
        _gla_scan_kernel,
        grid=(nt,),
        in_specs=[fwd(2 * GLA_DK), fwd(GLA_DV), fwd(ngd), bwd(2 * GLA_DK), bwd(GLA_DV), bwd(ngd)]
                 + [full(a) for a in consts],
        out_specs=[fwd(GLA_DV), bwd(GLA_DV)],
        out_shape=[out, out],
        scratch_shapes=[pltpu.VMEM((2, GLA_HEADS, GLA_HEAD_V, GLA_HEAD_K), F32)],
        compiler_params=_cparams(("arbitrary",)),
        name="gla_scan",
    )(qk, v, gd, qk, v, gd, *consts)


def _gla_out_kernel(of_ref, ob_ref, r_ref, x_ref, hn_ref, w_ref, o_ref):
    acc = x_ref[...]
    for h in range(GLA_HEADS):
        cols = slice(h * GLA_HEAD_V, (h + 1) * GLA_HEAD_V)
        o = _rms(of_ref[:, cols] + ob_ref[:, cols], hn_ref[...])
        r = r_ref[:, cols]
        gated = o * (r * (1.0 / (1.0 + jnp.exp(-r))))
        acc = acc + _dot(gated.astype(BF16), w_ref[cols, :])
    o_ref[...] = acc


def _gla_out(of, ob, r, x, head_norm, w_out, tm=512):
    T = x.shape[0]
    w = w_out.astype(BF16)
    hn = head_norm.reshape(1, GLA_HEAD_V)
    row = lambda n: pl.BlockSpec((tm, n), lambda i: (i, 0))
    full = lambda a: pl.BlockSpec(a.shape, lambda i: (0, 0))
    return pl.pallas_call(
        _gla_out_kernel,
        grid=(T // tm,),
        in_specs=[row(GLA_DV), row(GLA_DV), row(GLA_DV), row(D_MODEL), full(hn), full(w)],
        out_specs=row(D_MODEL),
        out_shape=jax.ShapeDtypeStruct((T, D_MODEL), F32),
        compiler_params=_cparams(("arbitrary",)),
        name="gla_out",
    )(of, ob, r, x, hn, w)


def _gla_mixer(x, gain, w_in, w_up_f, b_f, w_up_b, b_b, head_norm, w_out):
    qk, v, r, gd = _gla_in(x, gain, w_in)
    of, ob = _gla_scan(qk, v, gd, w_up_f, b_f, w_up_b, b_b)
    return _gla_out(of, ob, r, x, head_norm, w_out)


def _router_kernel(h_ref, g_ref, w_ref, aff_ref, hn_ref):
    hn = _rms(h_ref[...], g_ref[...])
    hn_ref[...] = hn.astype(BF16)
    logits = _dot_split(hn, w_ref[...])
    e = jnp.exp(logits - jnp.max(logits, axis=-1, keepdims=True))
    aff_ref[...] = e / jnp.sum(e, axis=-1, keepdims=True)


def _router(h, gain, w_router, tm=512):
    T = h.shape[0]
    gain2 = gain.reshape(1, D_MODEL)
    row = lambda n: pl.BlockSpec((tm, n), lambda i: (i, 0))
    full = lambda a: pl.BlockSpec(a.shape, lambda i: (0, 0))
    return pl.pallas_call(
        _router_kernel,
        grid=(T // tm,),
        in_specs=[row(D_MODEL), full(gain2), full(w_router)],
        out_specs=[row(N_EXPERTS), row(D_MODEL)],
        out_shape=[jax.ShapeDtypeStruct((T, N_EXPERTS), F32), jax.ShapeDtypeStruct((T, D_MODEL), BF16)],
        compiler_params=_cparams(("arbitrary",)),
        name="moe_router",
    )(h, gain2, w_router)


def _select_kernel(aff_ref, tri_ref, posm_ref, before_ref, *, cap):
    T = aff_ref.shape[1]
    bits = pltpu.bitcast(aff_ref[...], jnp.int32)

    def search(it, thr):
        cand = thr | jnp.left_shift(jnp.int32(1), 30 - it)
        cnt = jnp.sum(jnp.where(bits >= cand, 1.0, 0.0), axis=1, keepdims=True)
        return jnp.where(cnt >= cap, cand, thr)

    thr = lax.fori_loop(0, 31, search, jnp.zeros((N_EXPERTS, 1), jnp.int32))
    n_gt = jnp.sum(jnp.where(bits > thr, 1.0, 0.0), axis=1, keepdims=True)
    need = cap - n_gt
    tri = tri_ref[...]

    def scan(j, carry):
        c_eq, c_sel = carry
        cols = pl.ds(pl.multiple_of(j * LANES, LANES), LANES)
        blk = pltpu.bitcast(aff_ref[:, cols], jnp.int32)
        eq = jnp.where(blk == thr, 1.0, 0.0)
        rank = _dot(eq.astype(BF16), tri) + c_eq - eq
        sel = jnp.where((blk > thr) | ((eq > 0.0) & (rank < need)), 1.0, 0.0)
        before = _dot(sel.astype(BF16), tri) + c_sel - sel
        before_ref[:, cols] = before.astype(jnp.int32)
        posm_ref[:, cols] = jnp.where(sel > 0.0, before, -1.0).astype(jnp.int32)
        return (c_eq + jnp.sum(eq, axis=1, keepdims=True), c_sel + jnp.sum(sel, axis=1, keepdims=True))

    zero = jnp.zeros((N_EXPERTS, 1), F32)
    lax.fori_loop(0, T // LANES, scan, (zero, zero))


def _select(aff_t, cap):
    T = aff_t.shape[1]
    tri = jnp.asarray(np.triu(np.ones((LANES, LANES), np.float32)), BF16)
    full = lambda a: pl.BlockSpec(a.shape, lambda: (0,) * a.ndim)
    out = jax.ShapeDtypeStruct((N_EXPERTS, T), jnp.int32)
    return pl.pallas_call(
        functools.partial(_select_kernel, cap=cap),
        in_specs=[full(aff_t), full(tri)],
        out_specs=[pl.BlockSpec((N_EXPERTS, T), lambda: (0, 0))] * 2,
        out_shape=[out, out],
        compiler_params=pltpu.CompilerParams(vmem_limit_bytes=VMEM_LIMIT),
        name="moe_select",
    )(aff_t, tri)


def _dispatch_window():
    return MOE_TILE + F32_SUBLANES


def _dispatch_kernel(start_ref, hn_ref, posm_ref, x_ref, acc_ref, *, cap, nt, sub):
    e, t = pl.program_id(0), pl.program_id(1)
    win = _dispatch_window()

    @pl.when(t == 0)
    def _():
        acc_ref[...] = jnp.zeros_like(acc_ref)

    for s in range(sub):
        tok = slice(s * MOE_TILE, (s + 1) * MOE_TILE)
        base = pl.multiple_of((start_ref[e * nt + t * sub + s] // F32_SUBLANES) * F32_SUBLANES, F32_SUBLANES)
        slot = base + lax.broadcasted_iota(jnp.int32, (win, MOE_TILE), 0)
        onehot = jnp.where(posm_ref[0, :, tok] == slot, 1.0, 0.0).astype(BF16)
        acc_ref[pl.ds(base, win), :] += _dot(onehot, hn_ref[tok, :])

    @pl.when(t == pl.num_programs(1) - 1)
    def _():
        x_ref[0] = acc_ref[:cap, :].astype(BF16)


def _dispatch(hn, posm, start, cap, sub=16):
    T = hn.shape[0]
    nt = T // MOE_TILE
    sub = math.gcd(sub, nt)
    blk = sub * MOE_TILE
    posm3 = posm.reshape(N_EXPERTS, 1, T)
    grid_spec = pltpu.PrefetchScalarGridSpec(
        num_scalar_prefetch=1,
        grid=(N_EXPERTS, nt // sub),
        in_specs=[
            pl.BlockSpec((blk, D_MODEL), lambda e, t, s: (t, 0)),
            pl.BlockSpec((1, 1, blk), lambda e, t, s: (e, 0, t)),
        ],
        out_specs=pl.BlockSpec((1, cap, D_MODEL), lambda e, t, s: (e, 0, 0)),
        scratch_shapes=[pltpu.VMEM((cap + _dispatch_window(), D_MODEL), F32)],
    )
    return pl.pallas_call(
        functools.partial(_dispatch_kernel, cap=cap, nt=nt, sub=sub),
        grid_spec=grid_spec,
        out_shape=jax.ShapeDtypeStruct((N_EXPERTS, cap, D_MODEL), BF16),
        compiler_params=_cparams(("arbitrary", "arbitrary")),
        name="moe_dispatch",
    )(start.reshape(-1), hn, posm3)


def _ffn_kernel(x_ref, wg_ref, wu_ref, wd_ref, y_ref, acc_ref):
    f = pl.program_id(1)

    @pl.when(f == 0)
    def _():
        acc_ref[...] = jnp.zeros_like(acc_ref)

    wg = wg_ref[0, 0].astype(BF16)
    wu = wu_ref[0, 0].astype(BF16)
    wd = wd_ref[0, 0].astype(BF16)
    cap = x_ref.shape[1]
    blk = cap // math.gcd(cap // F32_SUBLANES, 4)
    for r0 in range(0, cap, blk):
        rows = slice(r0, r0 + blk)
        x = x_ref[0, rows, :]
        a = _dot(x, wg)
        u = _dot(x, wu)
        mid = (a * (1.0 / (1.0 + jnp.exp(-a))) * u).astype(BF16)
        acc_ref[rows, :] += _dot(mid, wd)

    @pl.when(f == pl.num_programs(1) - 1)
    def _():
        y_ref[0] = acc_ref[...].astype(BF16)


def _ffn(x, w_gate, w_up, w_down, layer, tf=512):
    cap = x.shape[1]
    return pl.pallas_call(
        _ffn_kernel,
        grid=(N_EXPERTS, EXPERT_FF // tf),
        in_specs=[
            pl.BlockSpec((1, cap, D_MODEL), lambda e, f: (e, 0, 0)),
            pl.BlockSpec((1, 1, D_MODEL, tf), lambda e, f: (layer, e, 0, f)),
            pl.BlockSpec((1, 1, D_MODEL, tf), lambda e, f: (layer, e, 0, f)),
            pl.BlockSpec((1, 1, tf, D_MODEL), lambda e, f: (layer, e, f, 0)),
        ],
        out_specs=pl.BlockSpec((1, cap, D_MODEL), lambda e, f: (e, 0, 0)),
        out_shape=jax.ShapeDtypeStruct((N_EXPERTS, cap, D_MODEL), BF16),
        scratch_shapes=[pltpu.VMEM((cap, D_MODEL), F32)],
        compiler_params=_cparams(("arbitrary", "arbitrary")),
        name="moe_ffn",
    )(x, w_gate, w_up, w_down)


def _combine_window():
    return MOE_TILE + BF16_SUBLANES


def _combine_kernel(start_ref, h_ref, aff_ref, posm_ref, g_ref, y_hbm, o_ref, hbuf_ref, buf_ref, hsem_ref, rsem_ref,
                    acc_ref, *, cap, nt, final_norm):
    t = pl.program_id(0)
    win = _combine_window()
    head = COMBINE_HEAD

    def base_of(tile, e):
        b = (start_ref[e * (nt + 1) + tile] // BF16_SUBLANES) * BF16_SUBLANES
        return pl.multiple_of(jnp.minimum(b, cap - win), BF16_SUBLANES)

    def needs_rest(tile, e):
        return start_ref[e * (nt + 1) + tile + 1] > base_of(tile, e) + head

    def head_copy(tile, e, slot):
        return pltpu.make_async_copy(y_hbm.at[e, pl.ds(base_of(tile, e), head), :],
                                     hbuf_ref.at[slot, pl.ds(e * head, head), :], hsem_ref.at[slot])

    def rest_copy(tile, e, slot):
        return pltpu.make_async_copy(y_hbm.at[e, pl.ds(base_of(tile, e) + head, win - head), :],
                                     buf_ref.at[slot, e], rsem_ref.at[slot, e])

    def fetch(tile, slot):
        for e in range(N_EXPERTS):
            head_copy(tile, e, slot).start()

            @pl.when(needs_rest(tile, e))
            def _():
                rest_copy(tile, e, slot).start()

    slot = t % 2

    @pl.when(t == 0)
    def _():
        fetch(0, 0)

    @pl.when(t + 1 < nt)
    def _():
        fetch(t + 1, 1 - slot)

    def weights_t(e, lo, hi):
        row = base_of(t, e) + lo + lax.broadcasted_iota(jnp.int32, (hi - lo, MOE_TILE), 0)
        return jnp.where(posm_ref[e:e + 1, :] == row, aff_ref[e:e + 1, :], 0.0)

    def expand(w_t, y_rows):
        w_hi, w_lo = _split_bf16(w_t.T)
        return _dot(w_hi, y_rows) + _dot(w_lo, y_rows)

    for e in range(N_EXPERTS):
        head_copy(t, e, slot).wait()

    w_heads = jnp.concatenate([weights_t(e, 0, head) for e in range(N_EXPERTS)], axis=0)
    acc_ref[...] = h_ref[...] + expand(w_heads, hbuf_ref[slot])

    any_rest = needs_rest(t, 0)
    for e in range(1, N_EXPERTS):
        any_rest = any_rest | needs_rest(t, e)

    @pl.when(any_rest)
    def _():
        for e in range(N_EXPERTS):
            @pl.when(needs_rest(t, e))
            def _():
                rest_copy(t, e, slot).wait()
                for lo, hi in ((head, MOE_TILE), (MOE_TILE, win)):
                    acc_ref[...] += expand(weights_t(e, lo, hi), buf_ref[slot, e, lo - head:hi - head, :])

    acc = acc_ref[...]
    if final_norm:
        acc = _rms(acc, g_ref[...])
    o_ref[...] = acc


def _combine(h, aff_t, posm, start, y, cap, final_gain):
    T = h.shape[0]
    nt = T // MOE_TILE
    win = _combine_window()
    assert cap >= win and (cap - win) % BF16_SUBLANES == 0
    final_norm = final_gain is not None
    gain = (final_gain if final_norm else jnp.ones((D_MODEL,), F32)).reshape(1, D_MODEL)
    grid_spec = pltpu.PrefetchScalarGridSpec(
        num_scalar_prefetch=1,
        grid=(nt,),
        in_specs=[
            pl.BlockSpec((MOE_TILE, D_MODEL), lambda t, s: (t, 0)),
            pl.BlockSpec((N_EXPERTS, MOE_TILE), lambda t, s: (0, t)),
            pl.BlockSpec((N_EXPERTS, MOE_TILE), lambda t, s: (0, t)),
            pl.BlockSpec((1, D_MODEL), lambda t, s: (0, 0)),
            pl.BlockSpec(memory_space=pl.ANY),
        ],
        out_specs=pl.BlockSpec((MOE_TILE, D_MODEL), lambda t, s: (t, 0)),
        scratch_shapes=[pltpu.VMEM((2, N_EXPERTS * COMBINE_HEAD, D_MODEL), BF16),
                        pltpu.VMEM((2, N_EXPERTS, win - COMBINE_HEAD, D_MODEL), BF16),
                        pltpu.SemaphoreType.DMA((2,)), pltpu.SemaphoreType.DMA((2, N_EXPERTS)),
                        pltpu.VMEM((MOE_TILE, D_MODEL), F32)],
    )
    return pl.pallas_call(
        functools.partial(_combine_kernel, cap=cap, nt=nt, final_norm=final_norm),
        grid_spec=grid_spec,
        out_shape=jax.ShapeDtypeStruct((T, D_MODEL), F32),
        compiler_params=_cparams(("arbitrary",)),
        name="moe_combine",
    )(start.reshape(-1), h, aff_t, posm, gain, y)


def _ec_moe(h, gain, w_router, w_gate, w_up, w_down, layer, final_gain=None):
    T = h.shape[0]
    cap = max(1, EC_CAPACITY_FACTOR * T // N_EXPERTS)
    aff, hn = _router(h, gain, w_router)
    aff_t = aff.T
    posm, before = _select(aff_t, cap)
    start = before[:, ::MOE_TILE]
    x = _dispatch(hn, posm, start, cap)
    y = _ffn(x, w_gate, w_up, w_down, layer)
    start_end = jnp.concatenate([start, jnp.full((N_EXPERTS, 1), cap, jnp.int32)], axis=1)
    return _combine(h, aff_t, posm, start_end, y, cap, final_gain)


def _rms_cols(x, g):
    return x * lax.rsqrt(jnp.mean(x * x, axis=0, keepdims=True) + RMS_EPS) * g


MLA_QCOLS = MLA_NOPE + 2 * LANES


def _mla_in_kernel(h_ref, posc_ref, posr_ref, g_ref, win_ref, wintkv_ref, qn_ref, wq_ref, wukt_ref, kvn_ref,
                   kvnc_ref, freqr_ref, sign_ref, freqc_ref, q_out, kt_out, va_out):
    half = MLA_ROPE // 2
    hn = _rms(h_ref[...], g_ref[...]).astype(BF16)
    c = _dot(hn, win_ref[...])
    ckv_t = _dot_nt(wintkv_ref[...], hn)

    cq = _rms(c[:, :MLA_Q_RANK], qn_ref[...]).astype(BF16)
    qa = _dot(cq, wq_ref[...])
    ang = posc_ref[...].astype(F32) * freqr_ref[...]
    cos, sin_signed = jnp.cos(ang), jnp.sin(ang) * sign_ref[...]
    qscale = MLA_QK ** -0.5 * math.log2(math.e)
    for hd in range(MLA_HEADS):
        c0 = hd * MLA_QCOLS
        q_lat = _dot(qa[:, c0:c0 + MLA_NOPE].astype(BF16), wukt_ref[hd])
        rot = qa[:, c0 + MLA_NOPE:c0 + MLA_NOPE + LANES] * cos + qa[:, c0 + MLA_NOPE + LANES:c0 + MLA_QCOLS] * sin_signed
        q_out[hd, :, :MLA_NOPE] = (q_lat * qscale).astype(BF16)
        q_out[hd, :, MLA_NOPE:] = (rot[:, :MLA_ROPE] * qscale).astype(BF16)

    kt_out[:MLA_KV_RANK, :] = _rms_cols(ckv_t[:MLA_KV_RANK], kvnc_ref[...]).astype(BF16)
    ang_t = freqc_ref[...] * posr_ref[...].astype(F32)
    cos_t, sin_t = jnp.cos(ang_t), jnp.sin(ang_t)
    k1, k2 = ckv_t[MLA_KV_RANK:MLA_KV_RANK + half], ckv_t[MLA_KV_RANK + half:]
    kt_out[MLA_KV_RANK:MLA_KV_RANK + half, :] = (k1 * cos_t - k2 * sin_t).astype(BF16)
    kt_out[MLA_KV_RANK + half:, :] = (k1 * sin_t + k2 * cos_t).astype(BF16)

    ckv = _rms(c[:, MLA_Q_RANK:MLA_Q_RANK + MLA_KV_RANK], kvn_ref[...])
    va_out[:, :MLA_KV_RANK] = ckv.astype(BF16)
    lane = lax.broadcasted_iota(jnp.int32, (ckv.shape[0], LANES), 1)
    va_out[:, MLA_KV_RANK:] = jnp.where(lane == 0, 1.0, 0.0).astype(BF16)


def _mla_in(h, positions, gain, w_in, q_norm, w_uq, kv_norm, w_ukv, tm=256):
    T = h.shape[0]
    H, half = MLA_HEADS, MLA_ROPE // 2
    wq = w_uq.reshape(MLA_Q_RANK, H, MLA_QK)
    x1, x2 = wq[:, :, MLA_NOPE:MLA_NOPE + half], wq[:, :, MLA_NOPE + half:]
    pad = jnp.zeros((MLA_Q_RANK, H, LANES - MLA_ROPE), F32)
    wq_wide = jnp.concatenate([wq[:, :, :MLA_NOPE], x1, x2, pad, x2, x1, pad], axis=2)
    wq_wide = wq_wide.reshape(MLA_Q_RANK, H * MLA_QCOLS).astype(BF16)
    wuk_t = w_ukv.reshape(MLA_KV_RANK, H, MLA_NOPE + MLA_V)[:, :, :MLA_NOPE].transpose(1, 2, 0).astype(BF16)
    inv_freq = ROPE_BASE ** (-jnp.arange(half, dtype=F32) / half)
    zeros = jnp.zeros((LANES - MLA_ROPE,), F32)
    freq_row = jnp.concatenate([inv_freq, inv_freq, zeros]).reshape(1, LANES)
    sign_row = jnp.concatenate([-jnp.ones((half,), F32), jnp.ones((half,), F32), zeros]).reshape(1, LANES)
    w_in_b = w_in.astype(BF16)
    args = [h, positions.reshape(T, 1), positions.reshape(1, T), gain.reshape(1, -1), w_in_b,
            w_in_b[:, MLA_Q_RANK:].T, q_norm.reshape(1, -1), wq_wide, wuk_t, kv_norm.reshape(1, -1),
            kv_norm.reshape(-1, 1), freq_row, sign_row, inv_freq.reshape(half, 1)]
    full = lambda a: pl.BlockSpec(a.shape, lambda i: (0,) * a.ndim)
    in_specs = [pl.BlockSpec((tm, D_MODEL), lambda i: (i, 0)), pl.BlockSpec((tm, 1), lambda i: (i, 0)),
                pl.BlockSpec((1, tm), lambda i: (0, i))] + [full(a) for a in args[3:]]
    return pl.pallas_call(
        _mla_in_kernel,
        grid=(T // tm,),
        in_specs=in_specs,
        out_specs=[pl.BlockSpec((H, tm, MLA_QK), lambda i: (0, i, 0)),
                   pl.BlockSpec((MLA_QK, tm), lambda i: (0, i)),
                   pl.BlockSpec((tm, 2 * LANES), lambda i: (i, 0))],
        out_shape=[jax.ShapeDtypeStruct((H, T, MLA_QK), BF16), jax.ShapeDtypeStruct((MLA_QK, T), BF16),
                   jax.ShapeDtypeStruct((T, 2 * LANES), BF16)],
        compiler_params=_cparams(("arbitrary",)),
        name="mla_in",
    )(*args)


def _flash_kernel(q_ref, qn_ref, kt_ref, va_ref, o_ref, sa_ref, sb_ref, pa_ref, pb_ref, acc_ref, m_ref, alpha_ref,
                  *, tk, nblk):
    H, tq, _ = qn_ref.shape
    T = kt_ref.shape[1]
    n = T // tk
    tiles = q_ref.shape[1] // tq

    def q_rows(u):
        if isinstance(u, int):
            blk = qn_ref[...] if u == tiles else q_ref[:, u * tq:(u + 1) * tq, :]
        else:
            blk = q_ref[:, pl.ds(pl.multiple_of(u * tq, tq), tq), :]
        return blk.reshape(H * tq, MLA_QK)

    def k_tile(j):
        return kt_ref[:, j * tk:(j + 1) * tk]

    def v_tile(j):
        return va_ref[j * tk:(j + 1) * tk, :]

    def lanes_rep(x, width):
        return jnp.concatenate([x] * (width // LANES), axis=1)

    blk = H * tq // nblk
    row_blocks = [slice(rb * blk, (rb + 1) * blk) for rb in range(nblk)]

    def add_pv(rows, p_ref, j):
        acc_ref[rows, :] = (lanes_rep(alpha_ref[rows, :], 2 * LANES) * acc_ref[rows, :]
                            + _dot(p_ref[rows, :], v_tile(j)))

    def step(q, q_next, j, s_cur, s_nxt, p_cur, p_prv):
        q_sel, k_nxt = (q, k_tile(j + 1)) if j + 1 < n else (q_next, k_tile(0))
        for rows in row_blocks:
            if j > 0:
                add_pv(rows, p_prv, j - 1)
            s_nxt[rows, :] = _dot(q_sel[rows], k_nxt)
            s = s_cur[rows, :]
            m_old = m_ref[rows, :]
            m_new = jnp.maximum(m_old, jnp.max(s, axis=1, keepdims=True))
            alpha_ref[rows, :] = jnp.exp2(m_old - m_new)
            m_ref[rows, :] = m_new
            p_cur[rows, :] = jnp.exp2(s - lanes_rep(m_new, tk)).astype(BF16)

    @pl.when(pl.program_id(0) == 0)
    def _():
        sa_ref[...] = _dot(q_rows(0), k_tile(0))

    def token_tile(u, carry):
        q = q_rows(u)
        q_next = jnp.where(u == tiles - 1, q_rows(tiles), q_rows(jnp.minimum(u + 1, tiles - 1)))
        acc_ref[...] = jnp.zeros_like(acc_ref)
        m_ref[...] = jnp.full_like(m_ref, -jnp.inf)
        for j in range(n):
            if j % 2 == 0:
                step(q, q_next, j, sa_ref, sb_ref, pa_ref, pb_ref)
            else:
                step(q, q_next, j, sb_ref, sa_ref, pb_ref, pa_ref)
        p_last = pb_ref if n % 2 == 0 else pa_ref
        for rows in row_blocks:
            add_pv(rows, p_last, n - 1)
        acc = acc_ref[...]
        o_lat = (acc[:, :MLA_KV_RANK] / acc[:, MLA_KV_RANK:MLA_KV_RANK + 1]).astype(BF16)
        out_rows = pl.ds(pl.multiple_of(u * tq, tq), tq)
        for hd in range(H):
            o_ref[out_rows, hd * MLA_KV_RANK:(hd + 1) * MLA_KV_RANK] = o_lat[hd * tq:(hd + 1) * tq]
        return carry

    lax.fori_loop(0, tiles, token_tile, 0)


def _flash(q, kt, va, tq=64, tk=1024, tiles=4):
    H, T, _ = q.shape
    tq, tk = min(tq, T), min(tk, T)
    assert T % (2 * tk) == 0 and tq % BF16_SUBLANES == 0
    assert T % (tiles * tq) == 0
    rows = H * tq
    steps = T // (tiles * tq)
    last_tile = T // tq - 1
    return pl.pallas_call(
        functools.partial(_flash_kernel, tk=tk, nblk=4),
        grid=(steps,),
        scratch_shapes=[pltpu.VMEM((rows, tk), F32), pltpu.VMEM((rows, tk), F32), pltpu.VMEM((rows, tk), BF16),
                        pltpu.VMEM((rows, tk), BF16), pltpu.VMEM((rows, 2 * LANES), F32),
                        pltpu.VMEM((rows, LANES), F32), pltpu.VMEM((rows, LANES), F32)],
        in_specs=[
            pl.BlockSpec((H, tiles * tq, MLA_QK), lambda i: (0, i, 0)),
            pl.BlockSpec((H, tq, MLA_QK), lambda i: (0, jnp.minimum((i + 1) * tiles, last_tile), 0)),
            pl.BlockSpec((MLA_QK, T), lambda i: (0, 0)),
            pl.BlockSpec((T, 2 * LANES), lambda i: (0, 0)),
        ],
        out_specs=pl.BlockSpec((tiles * tq, H * MLA_KV_RANK), lambda i: (i, 0)),
        out_shape=jax.ShapeDtypeStruct((T, H * MLA_KV_RANK), BF16),
        compiler_params=_cparams(("arbitrary",)),
        name="mla_flash",
    )(q, q, kt, va)


def _mla_out_kernel(o_ref, wuv_ref, w_ref, h_ref, out_ref):
    v = [_dot(o_ref[:, hd * MLA_KV_RANK:(hd + 1) * MLA_KV_RANK], wuv_ref[hd]).astype(BF16)
         for hd in range(MLA_HEADS)]
    out_ref[...] = h_ref[...] + _dot(jnp.concatenate(v, axis=1), w_ref[...])


def _mla_out(o_lat, w_ukv, w_out, h, tm=512):
    T = h.shape[0]
    H = MLA_HEADS
    wuv = w_ukv.reshape(MLA_KV_RANK, H, MLA_NOPE + MLA_V)[:, :, MLA_NOPE:].transpose(1, 0, 2).astype(BF16)
    wb = w_out.astype(BF16)
    return pl.pallas_call(
        _mla_out_kernel,
        grid=(T // tm,),
        in_specs=[pl.BlockSpec((tm, H * MLA_KV_RANK), lambda i: (i, 0)), pl.BlockSpec(wuv.shape, lambda i: (0, 0, 0)),
                  pl.BlockSpec(wb.shape, lambda i: (0, 0)), pl.BlockSpec((tm, D_MODEL), lambda i: (i, 0))],
        out_specs=pl.BlockSpec((tm, D_MODEL), lambda i: (i, 0)),
        out_shape=jax.ShapeDtypeStruct((T, D_MODEL), F32),
        compiler_params=_cparams(("arbitrary",)),
        name="mla_out",
    )(o_lat, wuv, wb, h)


def _mla_mixer(h, positions, gain, w_in, q_norm, w_uq, kv_norm, w_ukv, w_out):
    q, kt, va = _mla_in(h, positions, gain, w_in, q_norm, w_uq, kv_norm, w_ukv)
    o_lat = _flash(q, kt, va)
    return _mla_out(o_lat, w_ukv, w_out, h)


def kernel(x, positions, mix_norm, ffn_norm, final_norm, gla_w_in, gla_w_gate_up_f, gla_b_gate_f, gla_w_gate_up_b,
           gla_b_gate_b, gla_head_norm, gla_w_out, mla_w_in, mla_q_norm, mla_w_uq, mla_kv_norm, mla_w_ukv,
           mla_w_out, moe_w_router, moe_w_gate, moe_w_up, moe_w_down):
    B, T, D = x.shape
    outs = []
    for b in range(B):
        h = x[b]
        h = _gla_mixer(h, mix_norm[0], gla_w_in[0], gla_w_gate_up_f[0], gla_b_gate_f[0], gla_w_gate_up_b[0],
                       gla_b_gate_b[0], gla_head_norm[0], gla_w_out[0])
        h = _ec_moe(h, ffn_norm[0], moe_w_router[0], moe_w_gate, moe_w_up, moe_w_down, 0)
        h = _mla_mixer(h, positions[b], mix_norm[1], mla_w_in[0], mla_q_norm[0], mla_w_uq[0], mla_kv_norm[0],
                       mla_w_ukv[0], mla_w_out[0])
        h = _ec_moe(h, ffn_norm[1], moe_w_router[1], moe_w_gate, moe_w_up, moe_w_down, 1, final_gain=final_norm)
        outs.append(h)
    return jnp.stack(outs)
```

```python
import functools
import math

import numpy as np
import jax
import jax.numpy as jnp
from jax import lax
from jax.experimental import pallas as pl
from jax.experimental.pallas import tpu as pltpu

F32 = jnp.float32
BF16 = jnp.bfloat16

D_MODEL = 1024
RMS_EPS = 1e-6

GLA_HEADS = 4
GLA_DK = 512
GLA_DV = 1024
GLA_HEAD_K = GLA_DK // GLA_HEADS
GLA_HEAD_V = GLA_DV // GLA_HEADS
GLA_GATE_RANK = 16
GLA_TAU = 16.0
GLA_CHUNK = 64
GLA_TILE = 256
GLA_LEVELS = 6

MLA_HEADS = 16
MLA_Q_RANK = 256
MLA_KV_RANK = 128
MLA_NOPE = 128
MLA_ROPE = 64
MLA_V = 128
MLA_QK = MLA_NOPE + MLA_ROPE
ROPE_BASE = 10000.0

N_EXPERTS = 16
EXPERT_FF = 2048
EC_CAPACITY_FACTOR = 2
MOE_TILE = 256
COMBINE_HEAD = 64
BF16_SUBLANES = 16
F32_SUBLANES = 8
LANES = 128

VMEM_LIMIT = 56 * 1024 * 1024


def _cparams(sem):
    return pltpu.CompilerParams(dimension_semantics=sem, vmem_limit_bytes=VMEM_LIMIT)


def _rms(x, g):
    return x * lax.rsqrt(jnp.mean(x * x, axis=-1, keepdims=True) + RMS_EPS) * g


def _split_bf16(x):
    hi = x.astype(BF16)
    lo = (x - hi.astype(F32)).astype(BF16)
    return hi, lo


def _dot(a, b):
    return jnp.dot(a, b, preferred_element_type=F32)


def _dot_nt(a, b):
    return lax.dot_general(a, b, (((1,), (1,)), ((), ())), preferred_element_type=F32)


def _dot_tn(a, b):
    return lax.dot_general(a, b, (((0,), (0,)), ((), ())), preferred_element_type=F32)


def _dot_split(a, b):
    ah, al = _split_bf16(a)
    bh, bl = _split_bf16(b)
    return _dot(ah, bh) + _dot(ah, bl) + _dot(al, bh)


def _gla_in_kernel(x_ref, g_ref, wqk_ref, wv_ref, wr_ref, wgd_ref, qk_ref, v_ref, r_ref, gd_ref):
    hn = _rms(x_ref[...], g_ref[...]).astype(BF16)
    qk_ref[...] = _dot(hn, wqk_ref[...])
    v_ref[...] = _dot(hn, wv_ref[...])
    r_ref[...] = _dot(hn, wr_ref[...])
    gd_ref[...] = _dot(hn, wgd_ref[...])


def _gla_in(x, gain, w_in, tm=512):
    T = x.shape[0]
    wqk = w_in[:, :2 * GLA_DK].astype(BF16)
    wv = w_in[:, 2 * GLA_DK:2 * GLA_DK + GLA_DV].astype(BF16)
    wr = w_in[:, 2 * GLA_DK + GLA_DV:2 * GLA_DK + 2 * GLA_DV].astype(BF16)
    wgd = w_in[:, 2 * GLA_DK + 2 * GLA_DV:].astype(BF16)
    ngd = 2 * GLA_GATE_RANK
    row = lambda n: pl.BlockSpec((tm, n), lambda i: (i, 0))
    full = lambda a: pl.BlockSpec(a.shape, lambda i: (0, 0))
    gain2 = gain.reshape(1, D_MODEL)
    return pl.pallas_call(
        _gla_in_kernel,
        grid=(T // tm,),
        in_specs=[row(D_MODEL), full(gain2), full(wqk), full(wv), full(wr), full(wgd)],
        out_specs=[row(2 * GLA_DK), row(GLA_DV), row(GLA_DV), row(ngd)],
        out_shape=[jax.ShapeDtypeStruct((T, 2 * GLA_DK), F32), jax.ShapeDtypeStruct((T, GLA_DV), F32),
                   jax.ShapeDtypeStruct((T, GLA_DV), F32), jax.ShapeDtypeStruct((T, ngd), F32)],
        compiler_params=_cparams(("arbitrary",)),
        name="gla_in",
    )(x, gain2, wqk, wv, wr, wgd)


def _gla_tables(reverse):
    n, c = GLA_TILE, GLA_CHUNK
    W = np.zeros((GLA_LEVELS + 3, n, n), np.float32)
    L = np.full((n, n), -1, np.int32)
    for t in range(n):
        c0 = (t // c) * c
        tt = t - c0
        for l in range(GLA_LEVELS):
            b = (c // 2) >> l
            p0 = c0 + (tt // (2 * b)) * 2 * b
            mid = p0 + b
            second = t >= mid
            if not reverse:
                if second:
                    W[l, t, mid:t + 1] = 1
                    L[t, p0:mid] = l
                else:
                    W[l, t, t + 1:mid] = 1
            else:
                if second:
                    W[l, t, mid:t] = 1
                else:
                    W[l, t, t:mid] = 1
                    L[t, mid:p0 + 2 * b] = l
        if not reverse:
            W[GLA_LEVELS, t, c0:t + 1] = 1
            W[GLA_LEVELS + 1, t, t + 1:c0 + c] = 1
            L[t, t] = GLA_LEVELS
        else:
            W[GLA_LEVELS, t, t:c0 + c] = 1
            W[GLA_LEVELS + 1, t, c0:t] = 1
        W[GLA_LEVELS + 2, t, c0:c0 + c] = 1
    return W.reshape(-1, n), L


def _gla_scan_kernel(qkf_ref, vf_ref, gdf_ref, qkb_ref, vb_ref, gdb_ref, wupf_ref, bf_ref, wupb_ref, bb_ref,
                     wf_ref, lf_ref, wb_ref, lb_ref, of_ref, ob_ref, s_ref):
    n, c, r = GLA_TILE, GLA_CHUNK, GLA_GATE_RANK

    @pl.when(pl.program_id(0) == 0)
    def _():
        s_ref[...] = jnp.zeros_like(s_ref)

    dirs = [(qkf_ref, vf_ref, gdf_ref[:, :r], wupf_ref, bf_ref, wf_ref, lf_ref, of_ref, False),
            (qkb_ref, vb_ref, gdb_ref[:, r:], wupb_ref, bb_ref, wb_ref, lb_ref, ob_ref, True)]

    factors = []
    for qk_ref, v_ref, gd, wup_ref, b_ref, w_ref, l_ref, o_ref, reverse in dirs:
        z = _dot_split(gd, wup_ref[...]) + b_ref[...]
        g = (jnp.minimum(z, 0.0) - jnp.log1p(jnp.exp(-jnp.abs(z)))) * (1.0 / GLA_TAU)
        ghi, glo = _split_bf16(g)
        w = w_ref[...]
        factors.append(jnp.exp(_dot(w, ghi) + _dot(w, glo)))

    chains = []
    for d, (qk_ref, v_ref, gd, wup_ref, b_ref, w_ref, l_ref, o_ref, reverse) in enumerate(dirs):
        lvl = l_ref[...]
        for h in range(GLA_HEADS):
            kcols = slice(h * GLA_HEAD_K, (h + 1) * GLA_HEAD_K)
            vcols = slice(h * GLA_HEAD_V, (h + 1) * GLA_HEAD_V)
            f = factors[d][:, kcols]
            q = qk_ref[:, kcols] * (GLA_HEAD_K ** -0.5)
            k = qk_ref[:, GLA_DK + h * GLA_HEAD_K:GLA_DK + (h + 1) * GLA_HEAD_K]
            vb = v_ref[:, vcols].astype(BF16)
            attn = jnp.zeros((n, n), F32)
            for l in range(GLA_LEVELS):
                fl = f[l * n:(l + 1) * n]
                p = _dot_nt((q * fl).astype(BF16), (k * fl).astype(BF16))
                attn = jnp.where(lvl == l, p, attn)
            if not reverse:
                p = _dot_nt(q.astype(BF16), k.astype(BF16))
                attn = jnp.where(lvl == GLA_LEVELS, p, attn)
            o_intra = _dot(attn.astype(BF16), vb)
            qh = (q * f[GLA_LEVELS * n:(GLA_LEVELS + 1) * n]).astype(BF16)
            kh = (k * f[(GLA_LEVELS + 1) * n:(GLA_LEVELS + 2) * n]).astype(BF16)
            ftot = f[(GLA_LEVELS + 2) * n:(GLA_LEVELS + 3) * n]
            chains.append(dict(d=d, h=h, vcols=vcols, o_ref=o_ref, reverse=reverse, vb=vb, o_intra=o_intra, qh=qh,
                               kh=kh, ftot=ftot, st=s_ref[d, h]))

    nchunks = n // c
    for step in range(nchunks):
        for ch in chains:
            j = nchunks - 1 - step if ch["reverse"] else step
            rows = slice(j * c, (j + 1) * c)
            st = ch["st"]
            ch["o_ref"][rows, ch["vcols"]] = ch["o_intra"][rows] + _dot_nt(ch["qh"][rows], st.astype(BF16))
            ch["st"] = st * ch["ftot"][j * c:j * c + 1, :] + _dot_tn(ch["vb"][rows], ch["kh"][rows])
    for ch in chains:
        s_ref[ch["d"], ch["h"]] = ch["st"]


def _gla_scan(qk, v, gd, w_up_f, b_f, w_up_b, b_b):
    T = qk.shape[0]
    n = GLA_TILE
    nt = T // n
    tables = []
    for reverse in (False, True):
        W, L = _gla_tables(reverse)
        tables += [jnp.asarray(W, BF16), jnp.asarray(L)]
    fwd = lambda width: pl.BlockSpec((n, width), lambda i: (i, 0))
    bwd = lambda width: pl.BlockSpec((n, width), lambda i: (nt - 1 - i, 0))
    full = lambda a: pl.BlockSpec(a.shape, lambda i: (0, 0))
    consts = [w_up_f, b_f.reshape(1, GLA_DK), w_up_b, b_b.reshape(1, GLA_DK)] + tables
    ngd = 2 * GLA_GATE_RANK
    out = jax.ShapeDtypeStruct((T, GLA_DV), F32)
    return pl.pallas_call(
        _gla_scan_kernel,
        grid=(nt,),
        in_specs=[fwd(2 * GLA_DK), fwd(GLA_DV), fwd(ngd), bwd(2 * GLA_DK), bwd(GLA_DV), bwd(ngd)]
                 + [full(a) for a in consts],
        out_specs=[fwd(GLA_DV), bwd(GLA_DV)],
        out_shape=[out, out],
        scratch_shapes=[pltpu.VMEM((2, GLA_HEADS, GLA_HEAD_V, GLA_HEAD_K), F32)],
        compiler_params=_cparams(("arbitrary",)),
        name="gla_scan",
    )(qk, v, gd, qk, v, gd, *consts)


def _gla_out_kernel(of_ref, ob_ref, r_ref, x_ref, hn_ref, w_ref, o_ref):
    acc = x_ref[...]
    for h in range(GLA_HEADS):
        cols = slice(h * GLA_HEAD_V, (h + 1) * GLA_HEAD_V)
        o = _rms(of_ref[:, cols] + ob_ref[:, cols], hn_ref[...])
        r = r_ref[:, cols]
        gated = o * (r * (1.0 / (1.0 + jnp.exp(-r))))
        acc = acc + _dot(gated.astype(BF16), w_ref[cols, :])
    o_ref[...] = acc


def _gla_out(of, ob, r, x, head_norm, w_out, tm=512):
    T = x.shape[0]
    w = w_out.astype(BF16)
    hn = head_norm.reshape(1, GLA_HEAD_V)
    row = lambda n: pl.BlockSpec((tm, n), lambda i: (i, 0))
    full = lambda a: pl.BlockSpec(a.shape, lambda i: (0, 0))
    return pl.pallas_call(
        _gla_out_kernel,
        grid=(T // tm,),
        in_specs=[row(GLA_DV), row(GLA_DV), row(GLA_DV), row(D_MODEL), full(hn), full(w)],
        out_specs=row(D_MODEL),
        out_shape=jax.ShapeDtypeStruct((T, D_MODEL), F32),
        compiler_params=_cparams(("arbitrary",)),
        name="gla_out",
    )(of, ob, r, x, hn, w)


def _gla_mixer(x, gain, w_in, w_up_f, b_f, w_up_b, b_b, head_norm, w_out):
    qk, v, r, gd = _gla_in(x, gain, w_in)
    of, ob = _gla_scan(qk, v, gd, w_up_f, b_f, w_up_b, b_b)
    return _gla_out(of, ob, r, x, head_norm, w_out)


def _router_kernel(h_ref, g_ref, w_ref, aff_ref, hn_ref):
    hn = _rms(h_ref[...], g_ref[...])
    hn_ref[...] = hn.astype(BF16)
    logits = _dot_split(hn, w_ref[...])
    e = jnp.exp(logits - jnp.max(logits, axis=-1, keepdims=True))
    aff_ref[...] = e / jnp.sum(e, axis=-1, keepdims=True)


def _router(h, gain, w_router, tm=512):
    T = h.shape[0]
    gain2 = gain.reshape(1, D_MODEL)
    row = lambda n: pl.BlockSpec((tm, n), lambda i: (i, 0))
    full = lambda a: pl.BlockSpec(a.shape, lambda i: (0, 0))
    return pl.pallas_call(
        _router_kernel,
        grid=(T // tm,),
        in_specs=[row(D_MODEL), full(gain2), full(w_router)],
        out_specs=[row(N_EXPERTS), row(D_MODEL)],
        out_shape=[jax.ShapeDtypeStruct((T, N_EXPERTS), F32), jax.ShapeDtypeStruct((T, D_MODEL), BF16)],
        compiler_params=_cparams(("arbitrary",)),
        name="moe_router",
    )(h, gain2, w_router)


def _select_kernel(aff_ref, tri_ref, posm_ref, before_ref, *, cap):
    T = aff_ref.shape[1]
    bits = pltpu.bitcast(aff_ref[...], jnp.int32)

    def search(it, thr):
        cand = thr | jnp.left_shift(jnp.int32(1), 30 - it)
        cnt = jnp.sum(jnp.where(bits >= cand, 1.0, 0.0), axis=1, keepdims=True)
        return jnp.where(cnt >= cap, cand, thr)

    thr = lax.fori_loop(0, 31, search, jnp.zeros((N_EXPERTS, 1), jnp.int32))
    n_gt = jnp.sum(jnp.where(bits > thr, 1.0, 0.0), axis=1, keepdims=True)
    need = cap - n_gt
    tri = tri_ref[...]

    def scan(j, carry):
        c_eq, c_sel = carry
        cols = pl.ds(pl.multiple_of(j * LANES, LANES), LANES)
        blk = pltpu.bitcast(aff_ref[:, cols], jnp.int32)
        eq = jnp.where(blk == thr, 1.0, 0.0)
        rank = _dot(eq.astype(BF16), tri) + c_eq - eq
        sel = jnp.where((blk > thr) | ((eq > 0.0) & (rank < need)), 1.0, 0.0)
        before = _dot(sel.astype(BF16), tri) + c_sel - sel
        before_ref[:, cols] = before.astype(jnp.int32)
        posm_ref[:, cols] = jnp.where(sel > 0.0, before, -1.0).astype(jnp.int32)
        return (c_eq + jnp.sum(eq, axis=1, keepdims=True), c_sel + jnp.sum(sel, axis=1, keepdims=True))

    zero = jnp.zeros((N_EXPERTS, 1), F32)
    lax.fori_loop(0, T // LANES, scan, (zero, zero))


def _select(aff_t, cap):
    T = aff_t.shape[1]
    tri = jnp.asarray(np.triu(np.ones((LANES, LANES), np.float32)), BF16)
    full = lambda a: pl.BlockSpec(a.shape, lambda: (0,) * a.ndim)
    out = jax.ShapeDtypeStruct((N_EXPERTS, T), jnp.int32)
    return pl.pallas_call(
        functools.partial(_select_kernel, cap=cap),
        in_specs=[full(aff_t), full(tri)],
        out_specs=[pl.BlockSpec((N_EXPERTS, T), lambda: (0, 0))] * 2,
        out_shape=[out, out],
        compiler_params=pltpu.CompilerParams(vmem_limit_bytes=VMEM_LIMIT),
        name="moe_select",
    )(aff_t, tri)


def _dispatch_window():
    return MOE_TILE + F32_SUBLANES


def _dispatch_kernel(start_ref, hn_ref, posm_ref, x_ref, acc_ref, *, cap, nt, sub):
    e, t = pl.program_id(0), pl.program_id(1)
    win = _dispatch_window()

    @pl.when(t == 0)
    def _():
        acc_ref[...] = jnp.zeros_like(acc_ref)

    for s in range(sub):
        tok = slice(s * MOE_TILE, (s + 1) * MOE_TILE)
        base = pl.multiple_of(start_ref[e * nt + t * sub + s] & -F32_SUBLANES, F32_SUBLANES)
        slot = base + lax.broadcasted_iota(jnp.int32, (win, MOE_TILE), 0)
        onehot = jnp.where(posm_ref[0, :, tok] == slot, 1.0, 0.0).astype(BF16)
        acc_ref[pl.ds(base, win), :] += _dot(onehot, hn_ref[tok, :])

    @pl.when(t == pl.num_programs(1) - 1)
    def _():
        x_ref[0] = acc_ref[:cap, :].astype(BF16)


def _dispatch(hn, posm, start, cap, sub=16):
    T = hn.shape[0]
    nt = T // MOE_TILE
    sub = math.gcd(sub, nt)
    blk = sub * MOE_TILE
    posm3 = posm.reshape(N_EXPERTS, 1, T)
    grid_spec = pltpu.PrefetchScalarGridSpec(
        num_scalar_prefetch=1,
        grid=(N_EXPERTS, nt // sub),
        in_specs=[
            pl.BlockSpec((blk, D_MODEL), lambda e, t, s: (t, 0)),
            pl.BlockSpec((1, 1, blk), lambda e, t, s: (e, 0, t)),
        ],
        out_specs=pl.BlockSpec((1, cap, D_MODEL), lambda e, t, s: (e, 0, 0)),
        scratch_shapes=[pltpu.VMEM((cap + _dispatch_window(), D_MODEL), F32)],
    )
    return pl.pallas_call(
        functools.partial(_dispatch_kernel, cap=cap, nt=nt, sub=sub),
        grid_spec=grid_spec,
        out_shape=jax.ShapeDtypeStruct((N_EXPERTS, cap, D_MODEL), BF16),
        compiler_params=_cparams(("arbitrary", "arbitrary")),
        name="moe_dispatch",
    )(start.reshape(-1), hn, posm3)


def _ffn_kernel(x_ref, wg_ref, wu_ref, wd_ref, y_ref, acc_ref):
    f = pl.program_id(1)

    @pl.when(f == 0)
    def _():
        acc_ref[...] = jnp.zeros_like(acc_ref)

    wg = wg_ref[0, 0].astype(BF16)
    wu = wu_ref[0, 0].astype(BF16)
    wd = wd_ref[0, 0].astype(BF16)
    cap = x_ref.shape[1]
    blk = cap // math.gcd(cap // F32_SUBLANES, 4)
    for r0 in range(0, cap, blk):
        rows = slice(r0, r0 + blk)
        x = x_ref[0, rows, :]
        a = _dot(x, wg)
        u = _dot(x, wu)
        mid = (a * (1.0 / (1.0 + jnp.exp(-a))) * u).astype(BF16)
        acc_ref[rows, :] += _dot(mid, wd)

    @pl.when(f == pl.num_programs(1) - 1)
    def _():
        y_ref[0] = acc_ref[...].astype(BF16)


def _ffn(x, w_gate, w_up, w_down, layer, tf=512):
    cap = x.shape[1]
    return pl.pallas_call(
        _ffn_kernel,
        grid=(N_EXPERTS, EXPERT_FF // tf),
        in_specs=[
            pl.BlockSpec((1, cap, D_MODEL), lambda e, f: (e, 0, 0)),
            pl.BlockSpec((1, 1, D_MODEL, tf), lambda e, f: (layer, e, 0, f)),
            pl.BlockSpec((1, 1, D_MODEL, tf), lambda e, f: (layer, e, 0, f)),
            pl.BlockSpec((1, 1, tf, D_MODEL), lambda e, f: (layer, e, f, 0)),
        ],
        out_specs=pl.BlockSpec((1, cap, D_MODEL), lambda e, f: (e, 0, 0)),
        out_shape=jax.ShapeDtypeStruct((N_EXPERTS, cap, D_MODEL), BF16),
        scratch_shapes=[pltpu.VMEM((cap, D_MODEL), F32)],
        compiler_params=_cparams(("arbitrary", "arbitrary")),
        name="moe_ffn",
    )(x, w_gate, w_up, w_down)


def _combine_window():
    return MOE_TILE + BF16_SUBLANES


def _combine_kernel(start_ref, h_ref, aff_ref, posm_ref, g_ref, y_hbm, o_ref, hbuf_ref, buf_ref, hsem_ref, rsem_ref,
                    acc_ref, *, cap, nt, final_norm):
    t = pl.program_id(0)
    win = _combine_window()
    head = COMBINE_HEAD

    def base_of(tile, e):
        b = start_ref[e * (nt + 1) + tile] & -BF16_SUBLANES
        return pl.multiple_of(jnp.minimum(b, cap - win), BF16_SUBLANES)

    def needs_rest(tile, e):
        return start_ref[e * (nt + 1) + tile + 1] > base_of(tile, e) + head

    def head_copy(tile, e, slot):
        return pltpu.make_async_copy(y_hbm.at[e, pl.ds(base_of(tile, e), head), :],
                                     hbuf_ref.at[slot, pl.ds(e * head, head), :], hsem_ref.at[slot])

    def rest_copy(tile, e, slot):
        return pltpu.make_async_copy(y_hbm.at[e, pl.ds(base_of(tile, e) + head, win - head), :],
                                     buf_ref.at[slot, e], rsem_ref.at[slot, e])

    def fetch(tile, slot):
        for e in range(N_EXPERTS):
            head_copy(tile, e, slot).start()

            @pl.when(needs_rest(tile, e))
            def _():
                rest_copy(tile, e, slot).start()

    slot = t % 2

    @pl.when(t == 0)
    def _():
        fetch(0, 0)

    @pl.when(t + 1 < nt)
    def _():
        fetch(t + 1, 1 - slot)

    def weights_t(e, lo, hi):
        row = base_of(t, e) + lo + lax.broadcasted_iota(jnp.int32, (hi - lo, MOE_TILE), 0)
        return jnp.where(posm_ref[e:e + 1, :] == row, aff_ref[e:e + 1, :], 0.0)

    def expand(w_t, y_rows):
        w_hi, w_lo = _split_bf16(w_t.T)
        return _dot(w_hi, y_rows) + _dot(w_lo, y_rows)

    for e in range(N_EXPERTS):
        head_copy(t, e, slot).wait()

    w_heads = jnp.concatenate([weights_t(e, 0, head) for e in range(N_EXPERTS)], axis=0)
    acc_ref[...] = h_ref[...] + expand(w_heads, hbuf_ref[slot])

    any_rest = needs_rest(t, 0)
    for e in range(1, N_EXPERTS):
        any_rest = any_rest | needs_rest(t, e)

    @pl.when(any_rest)
    def _():
        for e in range(N_EXPERTS):
            @pl.when(needs_rest(t, e))
            def _():
                rest_copy(t, e, slot).wait()
                for lo, hi in ((head, MOE_TILE), (MOE_TILE, win)):
                    acc_ref[...] += expand(weights_t(e, lo, hi), buf_ref[slot, e, lo - head:hi - head, :])

    acc = acc_ref[...]
    if final_norm:
        acc = _rms(acc, g_ref[...])
    o_ref[...] = acc


def _combine(h, aff_t, posm, start, y, cap, final_gain):
    T = h.shape[0]
    nt = T // MOE_TILE
    win = _combine_window()
    assert cap >= win and (cap - win) % BF16_SUBLANES == 0
    final_norm = final_gain is not None
    gain = (final_gain if final_norm else jnp.ones((D_MODEL,), F32)).reshape(1, D_MODEL)
    grid_spec = pltpu.PrefetchScalarGridSpec(
        num_scalar_prefetch=1,
        grid=(nt,),
        in_specs=[
            pl.BlockSpec((MOE_TILE, D_MODEL), lambda t, s: (t, 0)),
            pl.BlockSpec((N_EXPERTS, MOE_TILE), lambda t, s: (0, t)),
            pl.BlockSpec((N_EXPERTS, MOE_TILE), lambda t, s: (0, t)),
            pl.BlockSpec((1, D_MODEL), lambda t, s: (0, 0)),
            pl.BlockSpec(memory_space=pl.ANY),
        ],
        out_specs=pl.BlockSpec((MOE_TILE, D_MODEL), lambda t, s: (t, 0)),
        scratch_shapes=[pltpu.VMEM((2, N_EXPERTS * COMBINE_HEAD, D_MODEL), BF16),
                        pltpu.VMEM((2, N_EXPERTS, win - COMBINE_HEAD, D_MODEL), BF16),
                        pltpu.SemaphoreType.DMA((2,)), pltpu.SemaphoreType.DMA((2, N_EXPERTS)),
                        pltpu.VMEM((MOE_TILE, D_MODEL), F32)],
    )
    return pl.pallas_call(
        functools.partial(_combine_kernel, cap=cap, nt=nt, final_norm=final_norm),
        grid_spec=grid_spec,
        out_shape=jax.ShapeDtypeStruct((T, D_MODEL), F32),
        compiler_params=_cparams(("arbitrary",)),
        name="moe_combine",
    )(start.reshape(-1), h, aff_t, posm, gain, y)


def _ec_moe(h, gain, w_router, w_gate, w_up, w_down, layer, final_gain=None):
    T = h.shape[0]
    cap = max(1, EC_CAPACITY_FACTOR * T // N_EXPERTS)
    aff, hn = _router(h, gain, w_router)
    aff_t = aff.T
    posm, before = _select(aff_t, cap)
    start = before[:, ::MOE_TILE]
    x = _dispatch(hn, posm, start, cap)
    y = _ffn(x, w_gate, w_up, w_down, layer)
    start_end = jnp.concatenate([start, jnp.full((N_EXPERTS, 1), cap, jnp.int32)], axis=1)
    return _combine(h, aff_t, posm, start_end, y, cap, final_gain)


def _rms_cols(x, g):
    return x * lax.rsqrt(jnp.mean(x * x, axis=0, keepdims=True) + RMS_EPS) * g


MLA_QCOLS = MLA_NOPE + 2 * LANES


def _mla_in_kernel(h_ref, posc_ref, posr_ref, g_ref, win_ref, wintkv_ref, qn_ref, wq_ref, wukt_ref, kvn_ref,
                   kvnc_ref, freqr_ref, sign_ref, freqc_ref, q_out, kt_out, va_out):
    half = MLA_ROPE // 2
    hn = _rms(h_ref[...], g_ref[...]).astype(BF16)
    c = _dot(hn, win_ref[...])
    ckv_t = _dot_nt(wintkv_ref[...], hn)

    cq = _rms(c[:, :MLA_Q_RANK], qn_ref[...]).astype(BF16)
    qa = _dot(cq, wq_ref[...])
    ang = posc_ref[...].astype(F32) * freqr_ref[...]
    cos, sin_signed = jnp.cos(ang), jnp.sin(ang) * sign_ref[...]
    qscale = MLA_QK ** -0.5 * math.log2(math.e)
    for hd in range(MLA_HEADS):
        c0 = hd * MLA_QCOLS
        q_lat = _dot(qa[:, c0:c0 + MLA_NOPE].astype(BF16), wukt_ref[hd])
        rot = qa[:, c0 + MLA_NOPE:c0 + MLA_NOPE + LANES] * cos + qa[:, c0 + MLA_NOPE + LANES:c0 + MLA_QCOLS] * sin_signed
        q_out[hd, :, :MLA_NOPE] = (q_lat * qscale).astype(BF16)
        q_out[hd, :, MLA_NOPE:] = (rot[:, :MLA_ROPE] * qscale).astype(BF16)

    kt_out[:MLA_KV_RANK, :] = _rms_cols(ckv_t[:MLA_KV_RANK], kvnc_ref[...]).astype(BF16)
    ang_t = freqc_ref[...] * posr_ref[...].astype(F32)
    cos_t, sin_t = jnp.cos(ang_t), jnp.sin(ang_t)
    k1, k2 = ckv_t[MLA_KV_RANK:MLA_KV_RANK + half], ckv_t[MLA_KV_RANK + half:]
    kt_out[MLA_KV_RANK:MLA_KV_RANK + half, :] = (k1 * cos_t - k2 * sin_t).astype(BF16)
    kt_out[MLA_KV_RANK + half:, :] = (k1 * sin_t + k2 * cos_t).astype(BF16)

    ckv = _rms(c[:, MLA_Q_RANK:MLA_Q_RANK + MLA_KV_RANK], kvn_ref[...])
    va_out[:, :MLA_KV_RANK] = ckv.astype(BF16)
    lane = lax.broadcasted_iota(jnp.int32, (ckv.shape[0], LANES), 1)
    va_out[:, MLA_KV_RANK:] = jnp.where(lane == 0, 1.0, 0.0).astype(BF16)


def _mla_in(h, positions, gain, w_in, q_norm, w_uq, kv_norm, w_ukv, tm=256):
    T = h.shape[0]
    H, half = MLA_HEADS, MLA_ROPE // 2
    wq = w_uq.reshape(MLA_Q_RANK, H, MLA_QK)
    x1, x2 = wq[:, :, MLA_NOPE:MLA_NOPE + half], wq[:, :, MLA_NOPE + half:]
    pad = jnp.zeros((MLA_Q_RANK, H, LANES - MLA_ROPE), F32)
    wq_wide = jnp.concatenate([wq[:, :, :MLA_NOPE], x1, x2, pad, x2, x1, pad], axis=2)
    wq_wide = wq_wide.reshape(MLA_Q_RANK, H * MLA_QCOLS).astype(BF16)
    wuk_t = w_ukv.reshape(MLA_KV_RANK, H, MLA_NOPE + MLA_V)[:, :, :MLA_NOPE].transpose(1, 2, 0).astype(BF16)
    inv_freq = ROPE_BASE ** (-jnp.arange(half, dtype=F32) / half)
    zeros = jnp.zeros((LANES - MLA_ROPE,), F32)
    freq_row = jnp.concatenate([inv_freq, inv_freq, zeros]).reshape(1, LANES)
    sign_row = jnp.concatenate([-jnp.ones((half,), F32), jnp.ones((half,), F32), zeros]).reshape(1, LANES)
    w_in_b = w_in.astype(BF16)
    args = [h, positions.reshape(T, 1), positions.reshape(1, T), gain.reshape(1, -1), w_in_b,
            w_in_b[:, MLA_Q_RANK:].T, q_norm.reshape(1, -1), wq_wide, wuk_t, kv_norm.reshape(1, -1),
            kv_norm.reshape(-1, 1), freq_row, sign_row, inv_freq.reshape(half, 1)]
    full = lambda a: pl.BlockSpec(a.shape, lambda i: (0,) * a.ndim)
    in_specs = [pl.BlockSpec((tm, D_MODEL), lambda i: (i, 0)), pl.BlockSpec((tm, 1), lambda i: (i, 0)),
                pl.BlockSpec((1, tm), lambda i: (0, i))] + [full(a) for a in args[3:]]
    return pl.pallas_call(
        _mla_in_kernel,
        grid=(T // tm,),
        in_specs=in_specs,
        out_specs=[pl.BlockSpec((H, tm, MLA_QK), lambda i: (0, i, 0)),
                   pl.BlockSpec((MLA_QK, tm), lambda i: (0, i)),
                   pl.BlockSpec((tm, 2 * LANES), lambda i: (i, 0))],
        out_shape=[jax.ShapeDtypeStruct((H, T, MLA_QK), BF16), jax.ShapeDtypeStruct((MLA_QK, T), BF16),
                   jax.ShapeDtypeStruct((T, 2 * LANES), BF16)],
        compiler_params=_cparams(("arbitrary",)),
        name="mla_in",
    )(*args)


def _flash_kernel(q_ref, qn_ref, kt_ref, va_ref, o_ref, sa_ref, sb_ref, pa_ref, pb_ref, acc_ref, m_ref, alpha_ref,
                  *, tk, nblk):
    H, tq, _ = qn_ref.shape
    T = kt_ref.shape[1]
    n = T // tk
    tiles = q_ref.shape[1] // tq

    def q_rows(u):
        if isinstance(u, int):
            blk = qn_ref[...] if u == tiles else q_ref[:, u * tq:(u + 1) * tq, :]
        else:
            blk = q_ref[:, pl.ds(pl.multiple_of(u * tq, tq), tq), :]
        return blk.reshape(H * tq, MLA_QK)

    def k_tile(j):
        return kt_ref[:, j * tk:(j + 1) * tk]

    def v_tile(j):
        return va_ref[j * tk:(j + 1) * tk, :]

    def lanes_rep(x, width):
        return jnp.concatenate([x] * (width // LANES), axis=1)

    blk = H * tq // nblk
    row_blocks = [slice(rb * blk, (rb + 1) * blk) for rb in range(nblk)]

    def add_pv(rows, p_ref, j):
        acc_ref[rows, :] = (lanes_rep(alpha_ref[rows, :], 2 * LANES) * acc_ref[rows, :]
                            + _dot(p_ref[rows, :], v_tile(j)))

    def step(q, q_next, j, s_cur, s_nxt, p_cur, p_prv):
        q_sel, k_nxt = (q, k_tile(j + 1)) if j + 1 < n else (q_next, k_tile(0))
        for rows in row_blocks:
            if j > 0:
                add_pv(rows, p_prv, j - 1)
            s_nxt[rows, :] = _dot(q_sel[rows], k_nxt)
            s = s_cur[rows, :]
            m_old = m_ref[rows, :]
            m_new = jnp.maximum(m_old, jnp.max(s, axis=1, keepdims=True))
            alpha_ref[rows, :] = jnp.exp2(m_old - m_new)
            m_ref[rows, :] = m_new
            p_cur[rows, :] = jnp.exp2(s - lanes_rep(m_new, tk)).astype(BF16)

    @pl.when(pl.program_id(0) == 0)
    def _():
        sa_ref[...] = _dot(q_rows(0), k_tile(0))

    def token_tile(u, carry):
        q = q_rows(u)
        q_next = jnp.where(u == tiles - 1, q_rows(tiles), q_rows(jnp.minimum(u + 1, tiles - 1)))
        acc_ref[...] = jnp.zeros_like(acc_ref)
        m_ref[...] = jnp.full_like(m_ref, -jnp.inf)
        for j in range(n):
            if j % 2 == 0:
                step(q, q_next, j, sa_ref, sb_ref, pa_ref, pb_ref)
            else:
                step(q, q_next, j, sb_ref, sa_ref, pb_ref, pa_ref)
        p_last = pb_ref if n % 2 == 0 else pa_ref
        for rows in row_blocks:
            add_pv(rows, p_last, n - 1)
        acc = acc_ref[...]
        o_lat = (acc[:, :MLA_KV_RANK] / acc[:, MLA_KV_RANK:MLA_KV_RANK + 1]).astype(BF16)
        out_rows = pl.ds(pl.multiple_of(u * tq, tq), tq)
        for hd in range(H):
            o_ref[out_rows, hd * MLA_KV_RANK:(hd + 1) * MLA_KV_RANK] = o_lat[hd * tq:(hd + 1) * tq]
        return carry

    lax.fori_loop(0, tiles, token_tile, 0)


def _flash(q, kt, va, tq=64, tk=1024, tiles=4):
    H, T, _ = q.shape
    tq, tk = min(tq, T), min(tk, T)
    assert T % (2 * tk) == 0 and tq % BF16_SUBLANES == 0
    assert T % (tiles * tq) == 0
    rows = H * tq
    steps = T // (tiles * tq)
    last_tile = T // tq - 1
    return pl.pallas_call(
        functools.partial(_flash_kernel, tk=tk, nblk=4),
        grid=(steps,),
        scratch_shapes=[pltpu.VMEM((rows, tk), F32), pltpu.VMEM((rows, tk), F32), pltpu.VMEM((rows, tk), BF16),
                        pltpu.VMEM((rows, tk), BF16), pltpu.VMEM((rows, 2 * LANES), F32),
                        pltpu.VMEM((rows, LANES), F32), pltpu.VMEM((rows, LANES), F32)],
        in_specs=[
            pl.BlockSpec((H, tiles * tq, MLA_QK), lambda i: (0, i, 0)),
            pl.BlockSpec((H, tq, MLA_QK), lambda i: (0, jnp.minimum((i + 1) * tiles, last_tile), 0)),
            pl.BlockSpec((MLA_QK, T), lambda i: (0, 0)),
            pl.BlockSpec((T, 2 * LANES), lambda i: (0, 0)),
        ],
        out_specs=pl.BlockSpec((tiles * tq, H * MLA_KV_RANK), lambda i: (i, 0)),
        out_shape=jax.ShapeDtypeStruct((T, H * MLA_KV_RANK), BF16),
        compiler_params=_cparams(("arbitrary",)),
        name="mla_flash",
    )(q, q, kt, va)


def _mla_out_kernel(o_ref, wuv_ref, w_ref, h_ref, out_ref):
    v = [_dot(o_ref[:, hd * MLA_KV_RANK:(hd + 1) * MLA_KV_RANK], wuv_ref[hd]).astype(BF16)
         for hd in range(MLA_HEADS)]
    out_ref[...] = h_ref[...] + _dot(jnp.concatenate(v, axis=1), w_ref[...])


def _mla_out(o_lat, w_ukv, w_out, h, tm=512):
    T = h.shape[0]
    H = MLA_HEADS
    wuv = w_ukv.reshape(MLA_KV_RANK, H, MLA_NOPE + MLA_V)[:, :, MLA_NOPE:].transpose(1, 0, 2).astype(BF16)
    wb = w_out.astype(BF16)
    return pl.pallas_call(
        _mla_out_kernel,
        grid=(T // tm,),
        in_specs=[pl.BlockSpec((tm, H * MLA_KV_RANK), lambda i: (i, 0)), pl.BlockSpec(wuv.shape, lambda i: (0, 0, 0)),
                  pl.BlockSpec(wb.shape, lambda i: (0, 0)), pl.BlockSpec((tm, D_MODEL), lambda i: (i, 0))],
        out_specs=pl.BlockSpec((tm, D_MODEL), lambda i: (i, 0)),
        out_shape=jax.ShapeDtypeStruct((T, D_MODEL), F32),
        compiler_params=_cparams(("arbitrary",)),
        name="mla_out",
    )(o_lat, wuv, wb, h)


def _mla_mixer(h, positions, gain, w_in, q_norm, w_uq, kv_norm, w_ukv, w_out):
    q, kt, va = _mla_in(h, positions, gain, w_in, q_norm, w_uq, kv_norm, w_ukv)
    o_lat = _flash(q, kt, va)
    return _mla_out(o_lat, w_ukv, w_out, h)


def kernel(x, positions, mix_norm, ffn_norm, final_norm, gla_w_in, gla_w_gate_up_f, gla_b_gate_f, gla_w_gate_up_b,
           gla_b_gate_b, gla_head_norm, gla_w_out, mla_w_in, mla_q_norm, mla_w_uq, mla_kv_norm, mla_w_ukv,
           mla_w_out, moe_w_router, moe_w_gate, moe_w_up, moe_w_down):
    B, T, D = x.shape
    outs = []
    for b in range(B):
        h = x[b]
        h = _gla_mixer(h, mix_norm[0], gla_w_in[0], gla_w_gate_up_f[0], gla_b_gate_f[0], gla_w_gate_up_b[0],
                       gla_b_gate_b[0], gla_head_norm[0], gla_w_out[0])
        h = _ec_moe(h, ffn_norm[0], moe_w_router[0], moe_w_gate, moe_w_up, moe_w_down, 0)
        h = _mla_mixer(h, positions[b], mix_norm[1], mla_w_in[0], mla_q_norm[0], mla_w_uq[0], mla_kv_norm[0],
                       mla_w_ukv[0], mla_w_out[0])
        h = _ec_moe(h, ffn_norm[1], moe_w_router[1], moe_w_gate, moe_w_up, moe_w_down, 1, final_gain=final_norm)
        outs.append(h)
    return jnp.stack(outs)
```

```python
import functools
import math

import numpy as np
import jax
import jax.numpy as jnp
from jax import lax
from jax.experimental import pallas as pl
from jax.experimental.pallas import tpu as pltpu

F32 = jnp.float32
BF16 = jnp.bfloat16

D_MODEL = 1024
RMS_EPS = 1e-6

GLA_HEADS = 4
GLA_DK = 512
GLA_DV = 1024
GLA_HEAD_K = GLA_DK // GLA_HEADS
GLA_HEAD_V = GLA_DV // GLA_HEADS
GLA_GATE_RANK = 16
GLA_TAU = 16.0
GLA_CHUNK = 64
GLA_TILE = 256
GLA_LEVELS = 6

MLA_HEADS = 16
MLA_Q_RANK = 256
MLA_KV_RANK = 128
MLA_NOPE = 128
MLA_ROPE = 64
MLA_V = 128
MLA_QK = MLA_NOPE + MLA_ROPE
ROPE_BASE = 10000.0

N_EXPERTS = 16
EXPERT_FF = 2048
EC_CAPACITY_FACTOR = 2
MOE_TILE = 256
COMBINE_HEAD = 64
BF16_SUBLANES = 16
F32_SUBLANES = 8
LANES = 128

VMEM_LIMIT = 56 * 1024 * 1024


def _cparams(sem):
    return pltpu.CompilerParams(dimension_semantics=sem, vmem_limit_bytes=VMEM_LIMIT)


def _rms(x, g):
    return x * lax.rsqrt(jnp.mean(x * x, axis=-1, keepdims=True) + RMS_EPS) * g


def _split_bf16(x):
    hi = x.astype(BF16)
    lo = (x - hi.astype(F32)).astype(BF16)
    return hi, lo


def _dot(a, b):
    return jnp.dot(a, b, preferred_element_type=F32)


def _dot_nt(a, b):
    return lax.dot_general(a, b, (((1,), (1,)), ((), ())), preferred_element_type=F32)


def _dot_tn(a, b):
    return lax.dot_general(a, b, (((0,), (0,)), ((), ())), preferred_element_type=F32)


def _dot_split(a, b):
    ah, al = _split_bf16(a)
    bh, bl = _split_bf16(b)
    return _dot(ah, bh) + _dot(ah, bl) + _dot(al, bh)


def _gla_in_kernel(x_ref, g_ref, wqk_ref, wv_ref, wr_ref, wgd_ref, qk_ref, v_ref, r_ref, gd_ref):
    hn = _rms(x_ref[...], g_ref[...]).astype(BF16)
    qk_ref[...] = _dot(hn, wqk_ref[...])
    v_ref[...] = _dot(hn, wv_ref[...])
    r_ref[...] = _dot(hn, wr_ref[...])
    gd_ref[...] = _dot(hn, wgd_ref[...])


def _gla_in(x, gain, w_in, tm=512):
    T = x.shape[0]
    wqk = w_in[:, :2 * GLA_DK].astype(BF16)
    wv = w_in[:, 2 * GLA_DK:2 * GLA_DK + GLA_DV].astype(BF16)
    wr = w_in[:, 2 * GLA_DK + GLA_DV:2 * GLA_DK + 2 * GLA_DV].astype(BF16)
    wgd = w_in[:, 2 * GLA_DK + 2 * GLA_DV:].astype(BF16)
    ngd = 2 * GLA_GATE_RANK
    row = lambda n: pl.BlockSpec((tm, n), lambda i: (i, 0))
    full = lambda a: pl.BlockSpec(a.shape, lambda i: (0, 0))
    gain2 = gain.reshape(1, D_MODEL)
    return pl.pallas_call(
        _gla_in_kernel,
        grid=(T // tm,),
        in_specs=[row(D_MODEL), full(gain2), full(wqk), full(wv), full(wr), full(wgd)],
        out_specs=[row(2 * GLA_DK), row(GLA_DV), row(GLA_DV), row(ngd)],
        out_shape=[jax.ShapeDtypeStruct((T, 2 * GLA_DK), F32), jax.ShapeDtypeStruct((T, GLA_DV), F32),
                   jax.ShapeDtypeStruct((T, GLA_DV), F32), jax.ShapeDtypeStruct((T, ngd), F32)],
        compiler_params=_cparams(("arbitrary",)),
        name="gla_in",
    )(x, gain2, wqk, wv, wr, wgd)


def _gla_tables(reverse):
    n, c = GLA_TILE, GLA_CHUNK
    W = np.zeros((GLA_LEVELS + 3, n, n), np.float32)
    L = np.full((n, n), -1, np.int32)
    for t in range(n):
        c0 = (t // c) * c
        tt = t - c0
        for l in range(GLA_LEVELS):
            b = (c // 2) >> l
            p0 = c0 + (tt // (2 * b)) * 2 * b
            mid = p0 + b
            second = t >= mid
            if not reverse:
                if second:
                    W[l, t, mid:t + 1] = 1
                    L[t, p0:mid] = l
                else:
                    W[l, t, t + 1:mid] = 1
            else:
                if second:
                    W[l, t, mid:t] = 1
                else:
                    W[l, t, t:mid] = 1
                    L[t, mid:p0 + 2 * b] = l
        if not reverse:
            W[GLA_LEVELS, t, c0:t + 1] = 1
            W[GLA_LEVELS + 1, t, t + 1:c0 + c] = 1
            L[t, t] = GLA_LEVELS
        else:
            W[GLA_LEVELS, t, t:c0 + c] = 1
            W[GLA_LEVELS + 1, t, c0:t] = 1
        W[GLA_LEVELS + 2, t, c0:c0 + c] = 1
    return W.reshape(-1, n), L


def _gla_scan_kernel(qkf_ref, vf_ref, gdf_ref, qkb_ref, vb_ref, gdb_ref, wupf_ref, bf_ref, wupb_ref, bb_ref,
                     wf_ref, lf_ref, wb_ref, lb_ref, of_ref, ob_ref, s_ref):
    n, c, r = GLA_TILE, GLA_CHUNK, GLA_GATE_RANK

    @pl.when(pl.program_id(0) == 0)
    def _():
        s_ref[...] = jnp.zeros_like(s_ref)

    dirs = [(qkf_ref, vf_ref, gdf_ref[:, :r], wupf_ref, bf_ref, wf_ref, lf_ref, of_ref, False),
            (qkb_ref, vb_ref, gdb_ref[:, r:], wupb_ref, bb_ref, wb_ref, lb_ref, ob_ref, True)]

    factors = []
    for qk_ref, v_ref, gd, wup_ref, b_ref, w_ref, l_ref, o_ref, reverse in dirs:
        z = _dot_split(gd, wup_ref[...]) + b_ref[...]
        g = (jnp.minimum(z, 0.0) - jnp.log1p(jnp.exp(-jnp.abs(z)))) * (1.0 / GLA_TAU)
        ghi, glo = _split_bf16(g)
        w = w_ref[...]
        factors.append(jnp.exp(_dot(w, ghi) + _dot(w, glo)))

    chains = []
    for d, (qk_ref, v_ref, gd, wup_ref, b_ref, w_ref, l_ref, o_ref, reverse) in enumerate(dirs):
        lvl = l_ref[...]
        for h in range(GLA_HEADS):
            kcols = slice(h * GLA_HEAD_K, (h + 1) * GLA_HEAD_K)
            vcols = slice(h * GLA_HEAD_V, (h + 1) * GLA_HEAD_V)
            f = factors[d][:, kcols]
            q = qk_ref[:, kcols] * (GLA_HEAD_K ** -0.5)
            k = qk_ref[:, GLA_DK + h * GLA_HEAD_K:GLA_DK + (h + 1) * GLA_HEAD_K]
            vb = v_ref[:, vcols].astype(BF16)
            attn = jnp.zeros((n, n), F32)
            for l in range(GLA_LEVELS):
                fl = f[l * n:(l + 1) * n]
                p = _dot_nt((q * fl).astype(BF16), (k * fl).astype(BF16))
                attn = jnp.where(lvl == l, p, attn)
            if not reverse:
                p = _dot_nt(q.astype(BF16), k.astype(BF16))
                attn = jnp.where(lvl == GLA_LEVELS, p, attn)
            o_intra = _dot(attn.astype(BF16), vb)
            qh = (q * f[GLA_LEVELS * n:(GLA_LEVELS + 1) * n]).astype(BF16)
            kh = (k * f[(GLA_LEVELS + 1) * n:(GLA_LEVELS + 2) * n]).astype(BF16)
            ftot = f[(GLA_LEVELS + 2) * n:(GLA_LEVELS + 3) * n]
            chains.append(dict(d=d, h=h, vcols=vcols, o_ref=o_ref, reverse=reverse, vb=vb, o_intra=o_intra, qh=qh,
                               kh=kh, ftot=ftot, st=s_ref[d, h]))

    nchunks = n // c
    for step in range(nchunks):
        for ch in chains:
            j = nchunks - 1 - step if ch["reverse"] else step
            rows = slice(j * c, (j + 1) * c)
            st = ch["st"]
            ch["o_ref"][rows, ch["vcols"]] = ch["o_intra"][rows] + _dot_nt(ch["qh"][rows], st.astype(BF16))
            ch["st"] = st * ch["ftot"][j * c:j * c + 1, :] + _dot_tn(ch["vb"][rows], ch["kh"][rows])
    for ch in chains:
        s_ref[ch["d"], ch["h"]] = ch["st"]


def _gla_scan(qk, v, gd, w_up_f, b_f, w_up_b, b_b):
    T = qk.shape[0]
    n = GLA_TILE
    nt = T // n
    tables = []
    for reverse in (False, True):
        W, L = _gla_tables(reverse)
        tables += [jnp.asarray(W, BF16), jnp.asarray(L)]
    fwd = lambda width: pl.BlockSpec((n, width), lambda i: (i, 0))
    bwd = lambda width: pl.BlockSpec((n, width), lambda i: (nt - 1 - i, 0))
    full = lambda a: pl.BlockSpec(a.shape, lambda i: (0, 0))
    consts = [w_up_f, b_f.reshape(1, GLA_DK), w_up_b, b_b.reshape(1, GLA_DK)] + tables
    ngd = 2 * GLA_GATE_RANK
    out = jax.ShapeDtypeStruct((T, GLA_DV), F32)
    return pl.pallas_call(
        _gla_scan_kernel,
        grid=(nt,),
        in_specs=[fwd(2 * GLA_DK), fwd(GLA_DV), fwd(ngd), bwd(2 * GLA_DK), bwd(GLA_DV), bwd(ngd)]
                 + [full(a) for a in consts],
        out_specs=[fwd(GLA_DV), bwd(GLA_DV)],
        out_shape=[out, out],
        scratch_shapes=[pltpu.VMEM((2, GLA_HEADS, GLA_HEAD_V, GLA_HEAD_K), F32)],
        compiler_params=_cparams(("arbitrary",)),
        name="gla_scan",
    )(qk, v, gd, qk, v, gd, *consts)


def _gla_out_kernel(of_ref, ob_ref, r_ref, x_ref, hn_ref, w_ref, o_ref):
    acc = x_ref[...]
    for h in range(GLA_HEADS):
        cols = slice(h * GLA_HEAD_V, (h + 1) * GLA_HEAD_V)
        o = _rms(of_ref[:, cols] + ob_ref[:, cols], hn_ref[...])
        r = r_ref[:, cols]
        gated = o * (r * (1.0 / (1.0 + jnp.exp(-r))))
        acc = acc + _dot(gated.astype(BF16), w_ref[cols, :])
    o_ref[...] = acc


def _gla_out(of, ob, r, x, head_norm, w_out, tm=512):
    T = x.shape[0]
    w = w_out.astype(BF16)
    hn = head_norm.reshape(1, GLA_HEAD_V)
    row = lambda n: pl.BlockSpec((tm, n), lambda i: (i, 0))
    full = lambda a: pl.BlockSpec(a.shape, lambda i: (0, 0))
    return pl.pallas_call(
        _gla_out_kernel,
        grid=(T // tm,),
        in_specs=[row(GLA_DV), row(GLA_DV), row(GLA_DV), row(D_MODEL), full(hn), full(w)],
        out_specs=row(D_MODEL),
        out_shape=jax.ShapeDtypeStruct((T, D_MODEL), F32),
        compiler_params=_cparams(("arbitrary",)),
        name="gla_out",
    )(of, ob, r, x, hn, w)


def _gla_mixer(x, gain, w_in, w_up_f, b_f, w_up_b, b_b, head_norm, w_out):
    qk, v, r, gd = _gla_in(x, gain, w_in)
    of, ob = _gla_scan(qk, v, gd, w_up_f, b_f, w_up_b, b_b)
    return _gla_out(of, ob, r, x, head_norm, w_out)


def _router_kernel(h_ref, g_ref, w_ref, aff_ref, hn_ref):
    hn = _rms(h_ref[...], g_ref[...])
    hn_ref[...] = hn.astype(BF16)
    logits = _dot_split(hn, w_ref[...])
    e = jnp.exp(logits - jnp.max(logits, axis=-1, keepdims=True))
    aff_ref[...] = e / jnp.sum(e, axis=-1, keepdims=True)


def _router(h, gain, w_router, tm=512):
    T = h.shape[0]
    gain2 = gain.reshape(1, D_MODEL)
    row = lambda n: pl.BlockSpec((tm, n), lambda i: (i, 0))
    full = lambda a: pl.BlockSpec(a.shape, lambda i: (0, 0))
    return pl.pallas_call(
        _router_kernel,
        grid=(T // tm,),
        in_specs=[row(D_MODEL), full(gain2), full(w_router)],
        out_specs=[row(N_EXPERTS), row(D_MODEL)],
        out_shape=[jax.ShapeDtypeStruct((T, N_EXPERTS), F32), jax.ShapeDtypeStruct((T, D_MODEL), BF16)],
        compiler_params=_cparams(("arbitrary",)),
        name="moe_router",
    )(h, gain2, w_router)


def _select_kernel(aff_ref, tri_ref, posm_ref, before_ref, *, cap):
    T = aff_ref.shape[1]
    bits = pltpu.bitcast(aff_ref[...], jnp.int32)

    def search(it, thr):
        cand = thr | jnp.left_shift(jnp.int32(1), 30 - it)
        cnt = jnp.sum(jnp.where(bits >= cand, 1.0, 0.0), axis=1, keepdims=True)
        return jnp.where(cnt >= cap, cand, thr)

    thr = lax.fori_loop(0, 31, search, jnp.zeros((N_EXPERTS, 1), jnp.int32))
    n_gt = jnp.sum(jnp.where(bits > thr, 1.0, 0.0), axis=1, keepdims=True)
    need = cap - n_gt
    tri = tri_ref[...]

    def scan(j, carry):
        c_eq, c_sel = carry
        cols = pl.ds(pl.multiple_of(j * LANES, LANES), LANES)
        blk = pltpu.bitcast(aff_ref[:, cols], jnp.int32)
        eq = jnp.where(blk == thr, 1.0, 0.0)
        rank = _dot(eq.astype(BF16), tri) + c_eq - eq
        sel = jnp.where((blk > thr) | ((eq > 0.0) & (rank < need)), 1.0, 0.0)
        before = _dot(sel.astype(BF16), tri) + c_sel - sel
        before_ref[:, cols] = before.astype(jnp.int32)
        posm_ref[:, cols] = jnp.where(sel > 0.0, before, -1.0).astype(jnp.int32)
        return (c_eq + jnp.sum(eq, axis=1, keepdims=True), c_sel + jnp.sum(sel, axis=1, keepdims=True))

    zero = jnp.zeros((N_EXPERTS, 1), F32)
    lax.fori_loop(0, T // LANES, scan, (zero, zero))


def _select(aff_t, cap):
    T = aff_t.shape[1]
    tri = jnp.asarray(np.triu(np.ones((LANES, LANES), np.float32)), BF16)
    full = lambda a: pl.BlockSpec(a.shape, lambda: (0,) * a.ndim)
    out = jax.ShapeDtypeStruct((N_EXPERTS, T), jnp.int32)
    return pl.pallas_call(
        functools.partial(_select_kernel, cap=cap),
        in_specs=[full(aff_t), full(tri)],
        out_specs=[pl.BlockSpec((N_EXPERTS, T), lambda: (0, 0))] * 2,
        out_shape=[out, out],
        compiler_params=pltpu.CompilerParams(vmem_limit_bytes=VMEM_LIMIT),
        name="moe_select",
    )(aff_t, tri)


def _dispatch_window():
    return MOE_TILE + F32_SUBLANES


def _dispatch_kernel(start_ref, hn_ref, posm_ref, x_ref, acc_ref, *, cap, nt, sub):
    e, t = pl.program_id(0), pl.program_id(1)
    win = _dispatch_window()

    @pl.when(t == 0)
    def _():
        acc_ref[...] = jnp.zeros_like(acc_ref)

    for s in range(sub):
        tok = slice(s * MOE_TILE, (s + 1) * MOE_TILE)
        base = pl.multiple_of(start_ref[e * nt + t * sub + s] & -F32_SUBLANES, F32_SUBLANES)
        slot = base + lax.broadcasted_iota(jnp.int32, (win, MOE_TILE), 0)
        onehot = jnp.where(posm_ref[0, :, tok] == slot, 1.0, 0.0).astype(BF16)
        acc_ref[pl.ds(base, win), :] += _dot(onehot, hn_ref[tok, :])

    @pl.when(t == pl.num_programs(1) - 1)
    def _():
        x_ref[0] = acc_ref[:cap, :].astype(BF16)


def _dispatch(hn, posm, start, cap, sub=16):
    T = hn.shape[0]
    nt = T // MOE_TILE
    sub = math.gcd(sub, nt)
    blk = sub * MOE_TILE
    posm3 = posm.reshape(N_EXPERTS, 1, T)
    grid_spec = pltpu.PrefetchScalarGridSpec(
        num_scalar_prefetch=1,
        grid=(N_EXPERTS, nt // sub),
        in_specs=[
            pl.BlockSpec((blk, D_MODEL), lambda e, t, s: (t, 0)),
            pl.BlockSpec((1, 1, blk), lambda e, t, s: (e, 0, t)),
        ],
        out_specs=pl.BlockSpec((1, cap, D_MODEL), lambda e, t, s: (e, 0, 0)),
        scratch_shapes=[pltpu.VMEM((cap + _dispatch_window(), D_MODEL), F32)],
    )
    return pl.pallas_call(
        functools.partial(_dispatch_kernel, cap=cap, nt=nt, sub=sub),
        grid_spec=grid_spec,
        out_shape=jax.ShapeDtypeStruct((N_EXPERTS, cap, D_MODEL), BF16),
        compiler_params=_cparams(("arbitrary", "arbitrary")),
        name="moe_dispatch",
    )(start.reshape(-1), hn, posm3)


def _ffn_kernel(x_ref, wg_ref, wu_ref, wd_ref, y_ref, acc_ref):
    f = pl.program_id(1)

    @pl.when(f == 0)
    def _():
        acc_ref[...] = jnp.zeros_like(acc_ref)

    wg = wg_ref[0, 0].astype(BF16)
    wu = wu_ref[0, 0].astype(BF16)
    wd = wd_ref[0, 0].astype(BF16)
    cap = x_ref.shape[1]
    blk = cap // math.gcd(cap // F32_SUBLANES, 4)
    for r0 in range(0, cap, blk):
        rows = slice(r0, r0 + blk)
        x = x_ref[0, rows, :]
        a = _dot(x, wg)
        u = _dot(x, wu)
        mid = (a * (1.0 / (1.0 + jnp.exp(-a))) * u).astype(BF16)
        acc_ref[rows, :] += _dot(mid, wd)

    @pl.when(f == pl.num_programs(1) - 1)
    def _():
        y_ref[0] = acc_ref[...].astype(BF16)


def _ffn(x, w_gate, w_up, w_down, layer, tf=512):
    cap = x.shape[1]
    return pl.pallas_call(
        _ffn_kernel,
        grid=(N_EXPERTS, EXPERT_FF // tf),
        in_specs=[
            pl.BlockSpec((1, cap, D_MODEL), lambda e, f: (e, 0, 0)),
            pl.BlockSpec((1, 1, D_MODEL, tf), lambda e, f: (layer, e, 0, f)),
            pl.BlockSpec((1, 1, D_MODEL, tf), lambda e, f: (layer, e, 0, f)),
            pl.BlockSpec((1, 1, tf, D_MODEL), lambda e, f: (layer, e, f, 0)),
        ],
        out_specs=pl.BlockSpec((1, cap, D_MODEL), lambda e, f: (e, 0, 0)),
        out_shape=jax.ShapeDtypeStruct((N_EXPERTS, cap, D_MODEL), BF16),
        scratch_shapes=[pltpu.VMEM((cap, D_MODEL), F32)],
        compiler_params=_cparams(("arbitrary", "arbitrary")),
        name="moe_ffn",
    )(x, w_gate, w_up, w_down)


def _combine_window():
    return MOE_TILE + BF16_SUBLANES


def _combine_kernel(start_ref, h_ref, aff_ref, posm_ref, g_ref, y_hbm, o_ref, hbuf_ref, buf_ref, hsem_ref, rsem_ref,
                    acc_ref, *, cap, nt, final_norm):
    t = pl.program_id(0)
    win = _combine_window()
    head = COMBINE_HEAD

    def base_of(tile, e):
        b = start_ref[e * (nt + 1) + tile] & -BF16_SUBLANES
        return pl.multiple_of(jnp.minimum(b, cap - win), BF16_SUBLANES)

    def needs_rest(tile, e):
        return start_ref[e * (nt + 1) + tile + 1] > base_of(tile, e) + head

    def head_copy(tile, e, slot):
        return pltpu.make_async_copy(y_hbm.at[e, pl.ds(base_of(tile, e), head), :],
                                     hbuf_ref.at[slot, pl.ds(e * head, head), :], hsem_ref.at[slot])

    def rest_copy(tile, e, slot):
        return pltpu.make_async_copy(y_hbm.at[e, pl.ds(base_of(tile, e) + head, win - head), :],
                                     buf_ref.at[slot, e], rsem_ref.at[slot, e])

    def fetch(tile, slot):
        for e in range(N_EXPERTS):
            head_copy(tile, e, slot).start()

            @pl.when(needs_rest(tile, e))
            def _():
                rest_copy(tile, e, slot).start()

    slot = t % 2

    @pl.when(t == 0)
    def _():
        fetch(0, 0)

    @pl.when(t + 1 < nt)
    def _():
        fetch(t + 1, 1 - slot)

    def weights_t(e, lo, hi):
        row = base_of(t, e) + lo + lax.broadcasted_iota(jnp.int32, (hi - lo, MOE_TILE), 0)
        return jnp.where(posm_ref[e:e + 1, :] == row, aff_ref[e:e + 1, :], 0.0)

    def expand(w_t, y_rows):
        w_hi, w_lo = _split_bf16(w_t.T)
        return _dot(w_hi, y_rows) + _dot(w_lo, y_rows)

    for e in range(N_EXPERTS):
        head_copy(t, e, slot).wait()

    w_heads = jnp.concatenate([weights_t(e, 0, head) for e in range(N_EXPERTS)], axis=0)
    acc_ref[...] = h_ref[...] + expand(w_heads, hbuf_ref[slot])

    any_rest = needs_rest(t, 0)
    for e in range(1, N_EXPERTS):
        any_rest = any_rest | needs_rest(t, e)

    @pl.when(any_rest)
    def _():
        for e in range(N_EXPERTS):
            @pl.when(needs_rest(t, e))
            def _():
                rest_copy(t, e, slot).wait()
                for lo, hi in ((head, MOE_TILE), (MOE_TILE, win)):
                    acc_ref[...] += expand(weights_t(e, lo, hi), buf_ref[slot, e, lo - head:hi - head, :])

    acc = acc_ref[...]
    if final_norm:
        acc = _rms(acc, g_ref[...])
    o_ref[...] = acc


def _combine(h, aff_t, posm, start, y, cap, final_gain):
    T = h.shape[0]
    nt = T // MOE_TILE
    win = _combine_window()
    assert cap >= win and (cap - win) % BF16_SUBLANES == 0
    final_norm = final_gain is not None
    gain = (final_gain if final_norm else jnp.ones((D_MODEL,), F32)).reshape(1, D_MODEL)
    grid_spec = pltpu.PrefetchScalarGridSpec(
        num_scalar_prefetch=1,
        grid=(nt,),
        in_specs=[
            pl.BlockSpec((MOE_TILE, D_MODEL), lambda t, s: (t, 0)),
            pl.BlockSpec((N_EXPERTS, MOE_TILE), lambda t, s: (0, t)),
            pl.BlockSpec((N_EXPERTS, MOE_TILE), lambda t, s: (0, t)),
            pl.BlockSpec((1, D_MODEL), lambda t, s: (0, 0)),
            pl.BlockSpec(memory_space=pl.ANY),
        ],
        out_specs=pl.BlockSpec((MOE_TILE, D_MODEL), lambda t, s: (t, 0)),
        scratch_shapes=[pltpu.VMEM((2, N_EXPERTS * COMBINE_HEAD, D_MODEL), BF16),
                        pltpu.VMEM((2, N_EXPERTS, win - COMBINE_HEAD, D_MODEL), BF16),
                        pltpu.SemaphoreType.DMA((2,)), pltpu.SemaphoreType.DMA((2, N_EXPERTS)),
                        pltpu.VMEM((MOE_TILE, D_MODEL), F32)],
    )
    return pl.pallas_call(
        functools.partial(_combine_kernel, cap=cap, nt=nt, final_norm=final_norm),
        grid_spec=grid_spec,
        out_shape=jax.ShapeDtypeStruct((T, D_MODEL), F32),
        compiler_params=_cparams(("arbitrary",)),
        name="moe_combine",
    )(start.reshape(-1), h, aff_t, posm, gain, y)


def _ec_moe(h, gain, w_router, w_gate, w_up, w_down, layer, final_gain=None):
    T = h.shape[0]
    cap = max(1, EC_CAPACITY_FACTOR * T // N_EXPERTS)
    aff, hn = _router(h, gain, w_router)
    aff_t = aff.T
    posm, before = _select(aff_t, cap)
    start = before[:, ::MOE_TILE]
    x = _dispatch(hn, posm, start, cap)
    y = _ffn(x, w_gate, w_up, w_down, layer)
    start_end = jnp.concatenate([start, jnp.full((N_EXPERTS, 1), cap, jnp.int32)], axis=1)
    return _combine(h, aff_t, posm, start_end, y, cap, final_gain)


def _rms_cols(x, g):
    return x * lax.rsqrt(jnp.mean(x * x, axis=0, keepdims=True) + RMS_EPS) * g


MLA_QCOLS = MLA_NOPE + 2 * LANES


def _mla_in_kernel(h_ref, posc_ref, posr_ref, g_ref, win_ref, wintkv_ref, qn_ref, wq_ref, wukt_ref, kvn_ref,
                   kvnc_ref, freqr_ref, sign_ref, freqc_ref, q_out, kt_out, va_out):
    half = MLA_ROPE // 2
    hn = _rms(h_ref[...], g_ref[...]).astype(BF16)
    c = _dot(hn, win_ref[...])
    ckv_t = _dot_nt(wintkv_ref[...], hn)

    cq = _rms(c[:, :MLA_Q_RANK], qn_ref[...]).astype(BF16)
    qa = _dot(cq, wq_ref[...])
    ang = posc_ref[...].astype(F32) * freqr_ref[...]
    cos, sin_signed = jnp.cos(ang), jnp.sin(ang) * sign_ref[...]
    qscale = MLA_QK ** -0.5 * math.log2(math.e)
    for hd in range(MLA_HEADS):
        c0 = hd * MLA_QCOLS
        q_lat = _dot(qa[:, c0:c0 + MLA_NOPE].astype(BF16), wukt_ref[hd])
        rot = qa[:, c0 + MLA_NOPE:c0 + MLA_NOPE + LANES] * cos + qa[:, c0 + MLA_NOPE + LANES:c0 + MLA_QCOLS] * sin_signed
        q_out[hd, :, :MLA_NOPE] = (q_lat * qscale).astype(BF16)
        q_out[hd, :, MLA_NOPE:] = (rot[:, :MLA_ROPE] * qscale).astype(BF16)

    kt_out[:MLA_KV_RANK, :] = _rms_cols(ckv_t[:MLA_KV_RANK], kvnc_ref[...]).astype(BF16)
    ang_t = freqc_ref[...] * posr_ref[...].astype(F32)
    cos_t, sin_t = jnp.cos(ang_t), jnp.sin(ang_t)
    k1, k2 = ckv_t[MLA_KV_RANK:MLA_KV_RANK + half], ckv_t[MLA_KV_RANK + half:]
    kt_out[MLA_KV_RANK:MLA_KV_RANK + half, :] = (k1 * cos_t - k2 * sin_t).astype(BF16)
    kt_out[MLA_KV_RANK + half:, :] = (k1 * sin_t + k2 * cos_t).astype(BF16)

    ckv = _rms(c[:, MLA_Q_RANK:MLA_Q_RANK + MLA_KV_RANK], kvn_ref[...])
    va_out[:, :MLA_KV_RANK] = ckv.astype(BF16)
    lane = lax.broadcasted_iota(jnp.int32, (ckv.shape[0], LANES), 1)
    va_out[:, MLA_KV_RANK:] = jnp.where(lane == 0, 1.0, 0.0).astype(BF16)


def _mla_in(h, positions, gain, w_in, q_norm, w_uq, kv_norm, w_ukv, tm=256):
    T = h.shape[0]
    H, half = MLA_HEADS, MLA_ROPE // 2
    wq = w_uq.reshape(MLA_Q_RANK, H, MLA_QK)
    x1, x2 = wq[:, :, MLA_NOPE:MLA_NOPE + half], wq[:, :, MLA_NOPE + half:]
    pad = jnp.zeros((MLA_Q_RANK, H, LANES - MLA_ROPE), F32)
    wq_wide = jnp.concatenate([wq[:, :, :MLA_NOPE], x1, x2, pad, x2, x1, pad], axis=2)
    wq_wide = wq_wide.reshape(MLA_Q_RANK, H * MLA_QCOLS).astype(BF16)
    wuk_t = w_ukv.reshape(MLA_KV_RANK, H, MLA_NOPE + MLA_V)[:, :, :MLA_NOPE].transpose(1, 2, 0).astype(BF16)
    inv_freq = ROPE_BASE ** (-jnp.arange(half, dtype=F32) / half)
    zeros = jnp.zeros((LANES - MLA_ROPE,), F32)
    freq_row = jnp.concatenate([inv_freq, inv_freq, zeros]).reshape(1, LANES)
    sign_row = jnp.concatenate([-jnp.ones((half,), F32), jnp.ones((half,), F32), zeros]).reshape(1, LANES)
    w_in_b = w_in.astype(BF16)
    args = [h, positions.reshape(T, 1), positions.reshape(1, T), gain.reshape(1, -1), w_in_b,
            w_in_b[:, MLA_Q_RANK:].T, q_norm.reshape(1, -1), wq_wide, wuk_t, kv_norm.reshape(1, -1),
            kv_norm.reshape(-1, 1), freq_row, sign_row, inv_freq.reshape(half, 1)]
    full = lambda a: pl.BlockSpec(a.shape, lambda i: (0,) * a.ndim)
    in_specs = [pl.BlockSpec((tm, D_MODEL), lambda i: (i, 0)), pl.BlockSpec((tm, 1), lambda i: (i, 0)),
                pl.BlockSpec((1, tm), lambda i: (0, i))] + [full(a) for a in args[3:]]
    return pl.pallas_call(
        _mla_in_kernel,
        grid=(T // tm,),
        in_specs=in_specs,
        out_specs=[pl.BlockSpec((H, tm, MLA_QK), lambda i: (0, i, 0)),
                   pl.BlockSpec((MLA_QK, tm), lambda i: (0, i)),
                   pl.BlockSpec((tm, 2 * LANES), lambda i: (i, 0))],
        out_shape=[jax.ShapeDtypeStruct((H, T, MLA_QK), BF16), jax.ShapeDtypeStruct((MLA_QK, T), BF16),
                   jax.ShapeDtypeStruct((T, 2 * LANES), BF16)],
        compiler_params=_cparams(("arbitrary",)),
        name="mla_in",
    )(*args)


def _flash_kernel(q_ref, qn_ref, kt_ref, va_ref, o_ref, sa_ref, sb_ref, pa_ref, pb_ref, acc_ref, m_ref, alpha_ref,
                  *, tk, nblk):
    H, tq, _ = qn_ref.shape
    T = kt_ref.shape[1]
    n = T // tk
    tiles = q_ref.shape[1] // tq

    def q_rows(u):
        if isinstance(u, int):
            blk = qn_ref[...] if u == tiles else q_ref[:, u * tq:(u + 1) * tq, :]
        else:
            blk = q_ref[:, pl.ds(pl.multiple_of(u * tq, tq), tq), :]
        return blk.reshape(H * tq, MLA_QK)

    def k_tile(j):
        return kt_ref[:, j * tk:(j + 1) * tk]

    def v_tile(j):
        return va_ref[j * tk:(j + 1) * tk, :]

    def lanes_rep(x, width):
        return jnp.concatenate([x] * (width // LANES), axis=1)

    blk = H * tq // nblk
    row_blocks = [slice(rb * blk, (rb + 1) * blk) for rb in range(nblk)]

    def add_pv(rows, p_ref, j):
        acc_ref[rows, :] = (lanes_rep(alpha_ref[rows, :], 2 * LANES) * acc_ref[rows, :]
                            + _dot(p_ref[rows, :], v_tile(j)))

    def step(q, q_next, j, s_cur, s_nxt, p_cur, p_prv):
        q_sel, k_nxt = (q, k_tile(j + 1)) if j + 1 < n else (q_next, k_tile(0))
        for rows in row_blocks:
            if j > 0:
                add_pv(rows, p_prv, j - 1)
            s_nxt[rows, :] = _dot(q_sel[rows], k_nxt)
            m_old = m_ref[rows, :]
            m_new = jnp.maximum(m_old, jnp.max(s_cur[rows, :], axis=1, keepdims=True))
            alpha_ref[rows, :] = jnp.exp2(m_old - m_new)
            m_ref[rows, :] = m_new
            p_cur[rows, :] = jnp.exp2(s_cur[rows, :] - lanes_rep(m_new, tk)).astype(BF16)

    @pl.when(pl.program_id(0) == 0)
    def _():
        sa_ref[...] = _dot(q_rows(0), k_tile(0))

    def token_tile(u, carry):
        q = q_rows(u)
        q_next = jnp.where(u == tiles - 1, q_rows(tiles), q_rows(jnp.minimum(u + 1, tiles - 1)))
        acc_ref[...] = jnp.zeros_like(acc_ref)
        m_ref[...] = jnp.full_like(m_ref, -jnp.inf)
        for j in range(n):
            if j % 2 == 0:
                step(q, q_next, j, sa_ref, sb_ref, pa_ref, pb_ref)
            else:
                step(q, q_next, j, sb_ref, sa_ref, pb_ref, pa_ref)
        p_last = pb_ref if n % 2 == 0 else pa_ref
        for rows in row_blocks:
            add_pv(rows, p_last, n - 1)
        acc = acc_ref[...]
        o_lat = (acc[:, :MLA_KV_RANK] / acc[:, MLA_KV_RANK:MLA_KV_RANK + 1]).astype(BF16)
        out_rows = pl.ds(pl.multiple_of(u * tq, tq), tq)
        for hd in range(H):
            o_ref[out_rows, hd * MLA_KV_RANK:(hd + 1) * MLA_KV_RANK] = o_lat[hd * tq:(hd + 1) * tq]
        return carry

    lax.fori_loop(0, tiles, token_tile, 0)


def _flash(q, kt, va, tq=64, tk=1024, tiles=4):
    H, T, _ = q.shape
    tq, tk = min(tq, T), min(tk, T)
    assert T % (2 * tk) == 0 and tq % BF16_SUBLANES == 0
    assert T % (tiles * tq) == 0
    rows = H * tq
    steps = T // (tiles * tq)
    last_tile = T // tq - 1
    return pl.pallas_call(
        functools.partial(_flash_kernel, tk=tk, nblk=4),
        grid=(steps,),
        scratch_shapes=[pltpu.VMEM((rows, tk), F32), pltpu.VMEM((rows, tk), F32), pltpu.VMEM((rows, tk), BF16),
                        pltpu.VMEM((rows, tk), BF16), pltpu.VMEM((rows, 2 * LANES), F32),
                        pltpu.VMEM((rows, LANES), F32), pltpu.VMEM((rows, LANES), F32)],
        in_specs=[
            pl.BlockSpec((H, tiles * tq, MLA_QK), lambda i: (0, i, 0)),
            pl.BlockSpec((H, tq, MLA_QK), lambda i: (0, jnp.minimum((i + 1) * tiles, last_tile), 0)),
            pl.BlockSpec((MLA_QK, T), lambda i: (0, 0)),
            pl.BlockSpec((T, 2 * LANES), lambda i: (0, 0)),
        ],
        out_specs=pl.BlockSpec((tiles * tq, H * MLA_KV_RANK), lambda i: (i, 0)),
        out_shape=jax.ShapeDtypeStruct((T, H * MLA_KV_RANK), BF16),
        compiler_params=_cparams(("arbitrary",)),
        name="mla_flash",
    )(q, q, kt, va)


def _mla_out_kernel(o_ref, wuv_ref, w_ref, h_ref, out_ref):
    v = [_dot(o_ref[:, hd * MLA_KV_RANK:(hd + 1) * MLA_KV_RANK], wuv_ref[hd]).astype(BF16)
         for hd in range(MLA_HEADS)]
    out_ref[...] = h_ref[...] + _dot(jnp.concatenate(v, axis=1), w_ref[...])


def _mla_out(o_lat, w_ukv, w_out, h, tm=512):
    T = h.shape[0]
    H = MLA_HEADS
    wuv = w_ukv.reshape(MLA_KV_RANK, H, MLA_NOPE + MLA_V)[:, :, MLA_NOPE:].transpose(1, 0, 2).astype(BF16)
    wb = w_out.astype(BF16)
    return pl.pallas_call(
        _mla_out_kernel,
        grid=(T // tm,),
        in_specs=[pl.BlockSpec((tm, H * MLA_KV_RANK), lambda i: (i, 0)), pl.BlockSpec(wuv.shape, lambda i: (0, 0, 0)),
                  pl.BlockSpec(wb.shape, lambda i: (0, 0)), pl.BlockSpec((tm, D_MODEL), lambda i: (i, 0))],
        out_specs=pl.BlockSpec((tm, D_MODEL), lambda i: (i, 0)),
        out_shape=jax.ShapeDtypeStruct((T, D_MODEL), F32),
        compiler_params=_cparams(("arbitrary",)),
        name="mla_out",
    )(o_lat, wuv, wb, h)


def _mla_mixer(h, positions, gain, w_in, q_norm, w_uq, kv_norm, w_ukv, w_out):
    q, kt, va = _mla_in(h, positions, gain, w_in, q_norm, w_uq, kv_norm, w_ukv)
    o_lat = _flash(q, kt, va)
    return _mla_out(o_lat, w_ukv, w_out, h)


def kernel(x, positions, mix_norm, ffn_norm, final_norm, gla_w_in, gla_w_gate_up_f, gla_b_gate_f, gla_w_gate_up_b,
           gla_b_gate_b, gla_head_norm, gla_w_out, mla_w_in, mla_q_norm, mla_w_uq, mla_kv_norm, mla_w_ukv,
           mla_w_out, moe_w_router, moe_w_gate, moe_w_up, moe_w_down):
    B, T, D = x.shape
    outs = []
    for b in range(B):
        h = x[b]
        h = _gla_mixer(h, mix_norm[0], gla_w_in[0], gla_w_gate_up_f[0], gla_b_gate_f[0], gla_w_gate_up_b[0],
                       gla_b_gate_b[0], gla_head_norm[0], gla_w_out[0])
        h = _ec_moe(h, ffn_norm[0], moe_w_router[0], moe_w_gate, moe_w_up, moe_w_down, 0)
        h = _mla_mixer(h, positions[b], mix_norm[1], mla_w_in[0], mla_q_norm[0], mla_w_uq[0], mla_kv_norm[0],
                       mla_w_ukv[0], mla_w_out[0])
        h = _ec_moe(h, ffn_norm[1], moe_w_router[1], moe_w_gate, moe_w_up, moe_w_down, 1, final_gain=final_norm)
        outs.append(h)
    return jnp.stack(outs)
```

```python
import functools
import math

import numpy as np
import jax
import jax.numpy as jnp
from jax import lax
from jax.experimental import pallas as pl
from jax.experimental.pallas import tpu as pltpu

F32 = jnp.float32
BF16 = jnp.bfloat16

D_MODEL = 1024
RMS_EPS = 1e-6

GLA_HEADS = 4
GLA_DK = 512
GLA_DV = 1024
GLA_HEAD_K = GLA_DK // GLA_HEADS
GLA_HEAD_V = GLA_DV // GLA_HEADS
GLA_GATE_RANK = 16
GLA_TAU = 16.0
GLA_CHUNK = 64
GLA_TILE = 256
GLA_LEVELS = 6

MLA_HEADS = 16
MLA_Q_RANK = 256
MLA_KV_RANK = 128
MLA_NOPE = 128
MLA_ROPE = 64
MLA_V = 128
MLA_QK = MLA_NOPE + MLA_ROPE
ROPE_BASE = 10000.0

N_EXPERTS = 16
EXPERT_FF = 2048
EC_CAPACITY_FACTOR = 2
MOE_TILE = 256
COMBINE_HEAD = 64
BF16_SUBLANES = 16
F32_SUBLANES = 8
LANES = 128

VMEM_LIMIT = 56 * 1024 * 1024


def _cparams(sem):
    return pltpu.CompilerParams(dimension_semantics=sem, vmem_limit_bytes=VMEM_LIMIT)


def _rms(x, g):
    return x * lax.rsqrt(jnp.mean(x * x, axis=-1, keepdims=True) + RMS_EPS) * g


def _split_bf16(x):
    hi = x.astype(BF16)
    lo = (x - hi.astype(F32)).astype(BF16)
    return hi, lo


def _dot(a, b):
    return jnp.dot(a, b, preferred_element_type=F32)


def _dot_nt(a, b):
    return lax.dot_general(a, b, (((1,), (1,)), ((), ())), preferred_element_type=F32)


def _dot_tn(a, b):
    return lax.dot_general(a, b, (((0,), (0,)), ((), ())), preferred_element_type=F32)


def _dot_split(a, b):
    ah, al = _split_bf16(a)
    bh, bl = _split_bf16(b)
    return _dot(ah, bh) + _dot(ah, bl) + _dot(al, bh)


def _gla_in_kernel(x_ref, g_ref, wqk_ref, wv_ref, wr_ref, wgd_ref, qk_ref, v_ref, r_ref, gd_ref):
    hn = _rms(x_ref[...], g_ref[...]).astype(BF16)
    qk_ref[...] = _dot(hn, wqk_ref[...])
    v_ref[...] = _dot(hn, wv_ref[...])
    r_ref[...] = _dot(hn, wr_ref[...])
    gd_ref[...] = _dot(hn, wgd_ref[...])


def _gla_in(x, gain, w_in, tm=512):
    T = x.shape[0]
    wqk = w_in[:, :2 * GLA_DK].astype(BF16)
    wv = w_in[:, 2 * GLA_DK:2 * GLA_DK + GLA_DV].astype(BF16)
    wr = w_in[:, 2 * GLA_DK + GLA_DV:2 * GLA_DK + 2 * GLA_DV].astype(BF16)
    wgd = w_in[:, 2 * GLA_DK + 2 * GLA_DV:].astype(BF16)
    ngd = 2 * GLA_GATE_RANK
    row = lambda n: pl.BlockSpec((tm, n), lambda i: (i, 0))
    full = lambda a: pl.BlockSpec(a.shape, lambda i: (0, 0))
    gain2 = gain.reshape(1, D_MODEL)
    return pl.pallas_call(
        _gla_in_kernel,
        grid=(T // tm,),
        in_specs=[row(D_MODEL), full(gain2), full(wqk), full(wv), full(wr), full(wgd)],
        out_specs=[row(2 * GLA_DK), row(GLA_DV), row(GLA_DV), row(ngd)],
        out_shape=[jax.ShapeDtypeStruct((T, 2 * GLA_DK), F32), jax.ShapeDtypeStruct((T, GLA_DV), F32),
                   jax.ShapeDtypeStruct((T, GLA_DV), F32), jax.ShapeDtypeStruct((T, ngd), F32)],
        compiler_params=_cparams(("arbitrary",)),
        name="gla_in",
    )(x, gain2, wqk, wv, wr, wgd)


def _gla_tables(reverse):
    n, c = GLA_TILE, GLA_CHUNK
    W = np.zeros((GLA_LEVELS + 3, n, n), np.float32)
    L = np.full((n, n), -1, np.int32)
    for t in range(n):
        c0 = (t // c) * c
        tt = t - c0
        for l in range(GLA_LEVELS):
            b = (c // 2) >> l
            p0 = c0 + (tt // (2 * b)) * 2 * b
            mid = p0 + b
            second = t >= mid
            if not reverse:
                if second:
                    W[l, t, mid:t + 1] = 1
                    L[t, p0:mid] = l
                else:
                    W[l, t, t + 1:mid] = 1
            else:
                if second:
                    W[l, t, mid:t] = 1
                else:
                    W[l, t, t:mid] = 1
                    L[t, mid:p0 + 2 * b] = l
        if not reverse:
            W[GLA_LEVELS, t, c0:t + 1] = 1
            W[GLA_LEVELS + 1, t, t + 1:c0 + c] = 1
            L[t, t] = GLA_LEVELS
        else:
            W[GLA_LEVELS, t, t:c0 + c] = 1
            W[GLA_LEVELS + 1, t, c0:t] = 1
        W[GLA_LEVELS + 2, t, c0:c0 + c] = 1
    return W.reshape(-1, n), L


def _gla_scan_kernel(qkf_ref, vf_ref, gdf_ref, qkb_ref, vb_ref, gdb_ref, wupf_ref, bf_ref, wupb_ref, bb_ref,
                     wf_ref, lf_ref, wb_ref, lb_ref, of_ref, ob_ref, s_ref):
    n, c, r = GLA_TILE, GLA_CHUNK, GLA_GATE_RANK

    @pl.when(pl.program_id(0) == 0)
    def _():
        s_ref[...] = jnp.zeros_like(s_ref)

    dirs = [(qkf_ref, vf_ref, gdf_ref[:, :r], wupf_ref, bf_ref, wf_ref, lf_ref, of_ref, False),
            (qkb_ref, vb_ref, gdb_ref[:, r:], wupb_ref, bb_ref, wb_ref, lb_ref, ob_ref, True)]

    factors = []
    for qk_ref, v_ref, gd, wup_ref, b_ref, w_ref, l_ref, o_ref, reverse in dirs:
        z = _dot_split(gd, wup_ref[...]) + b_ref[...]
        g = (jnp.minimum(z, 0.0) - jnp.log1p(jnp.exp(-jnp.abs(z)))) * (1.0 / GLA_TAU)
        ghi, glo = _split_bf16(g)
        w = w_ref[...]
        factors.append(jnp.exp(_dot(w, ghi) + _dot(w, glo)))

    chains = []
    for d, (qk_ref, v_ref, gd, wup_ref, b_ref, w_ref, l_ref, o_ref, reverse) in enumerate(dirs):
        lvl = l_ref[...]
        for h in range(GLA_HEADS):
            kcols = slice(h * GLA_HEAD_K, (h + 1) * GLA_HEAD_K)
            vcols = slice(h * GLA_HEAD_V, (h + 1) * GLA_HEAD_V)
            f = factors[d][:, kcols]
            q = qk_ref[:, kcols] * (GLA_HEAD_K ** -0.5)
            k = qk_ref[:, GLA_DK + h * GLA_HEAD_K:GLA_DK + (h + 1) * GLA_HEAD_K]
            vb = v_ref[:, vcols].astype(BF16)
            attn = jnp.zeros((n, n), F32)
            for l in range(GLA_LEVELS):
                fl = f[l * n:(l + 1) * n]
                p = _dot_nt((q * fl).astype(BF16), (k * fl).astype(BF16))
                attn = jnp.where(lvl == l, p, attn)
            if not reverse:
                p = _dot_nt(q.astype(BF16), k.astype(BF16))
                attn = jnp.where(lvl == GLA_LEVELS, p, attn)
            o_intra = _dot(attn.astype(BF16), vb)
            qh = (q * f[GLA_LEVELS * n:(GLA_LEVELS + 1) * n]).astype(BF16)
            kh = (k * f[(GLA_LEVELS + 1) * n:(GLA_LEVELS + 2) * n]).astype(BF16)
            ftot = f[(GLA_LEVELS + 2) * n:(GLA_LEVELS + 3) * n]
            chains.append(dict(d=d, h=h, vcols=vcols, o_ref=o_ref, reverse=reverse, vb=vb, o_intra=o_intra, qh=qh,
                               kh=kh, ftot=ftot, st=s_ref[d, h]))

    nchunks = n // c
    for step in range(nchunks):
        for ch in chains:
            j = nchunks - 1 - step if ch["reverse"] else step
            rows = slice(j * c, (j + 1) * c)
            st = ch["st"]
            ch["o_ref"][rows, ch["vcols"]] = ch["o_intra"][rows] + _dot_nt(ch["qh"][rows], st.astype(BF16))
            ch["st"] = st * ch["ftot"][j * c:j * c + 1, :] + _dot_tn(ch["vb"][rows], ch["kh"][rows])
    for ch in chains:
        s_ref[ch["d"], ch["h"]] = ch["st"]


def _gla_scan(qk, v, gd, w_up_f, b_f, w_up_b, b_b):
    T = qk.shape[0]
    n = GLA_TILE
    nt = T // n
    tables = []
    for reverse in (False, True):
        W, L = _gla_tables(reverse)
        tables += [jnp.asarray(W, BF16), jnp.asarray(L)]
    fwd = lambda width: pl.BlockSpec((n, width), lambda i: (i, 0))
    bwd = lambda width: pl.BlockSpec((n, width), lambda i: (nt - 1 - i, 0))
    full = lambda a: pl.BlockSpec(a.shape, lambda i: (0, 0))
    consts = [w_up_f, b_f.reshape(1, GLA_DK), w_up_b, b_b.reshape(1, GLA_DK)] + tables
    ngd = 2 * GLA_GATE_RANK
    out = jax.ShapeDtypeStruct((T, GLA_DV), F32)
    return pl.pallas_call(
        _gla_scan_kernel,
        grid=(nt,),
        in_specs=[fwd(2 * GLA_DK), fwd(GLA_DV), fwd(ngd), bwd(2 * GLA_DK), bwd(GLA_DV), bwd(ngd)]
                 + [full(a) for a in consts],
        out_specs=[fwd(GLA_DV), bwd(GLA_DV)],
        out_shape=[out, out],
        scratch_shapes=[pltpu.VMEM((2, GLA_HEADS, GLA_HEAD_V, GLA_HEAD_K), F32)],
        compiler_params=_cparams(("arbitrary",)),
        name="gla_scan",
    )(qk, v, gd, qk, v, gd, *consts)


def _gla_out_kernel(of_ref, ob_ref, r_ref, x_ref, hn_ref, w_ref, o_ref):
    acc = x_ref[...]
    for h in range(GLA_HEADS):
        cols = slice(h * GLA_HEAD_V, (h + 1) * GLA_HEAD_V)
        o = _rms(of_ref[:, cols] + ob_ref[:, cols], hn_ref[...])
        r = r_ref[:, cols]
        gated = o * (r * (1.0 / (1.0 + jnp.exp(-r))))
        acc = acc + _dot(gated.astype(BF16), w_ref[cols, :])
    o_ref[...] = acc


def _gla_out(of, ob, r, x, head_norm, w_out, tm=512):
    T = x.shape[0]
    w = w_out.astype(BF16)
    hn = head_norm.reshape(1, GLA_HEAD_V)
    row = lambda n: pl.BlockSpec((tm, n), lambda i: (i, 0))
    full = lambda a: pl.BlockSpec(a.shape, lambda i: (0, 0))
    return pl.pallas_call(
        _gla_out_kernel,
        grid=(T // tm,),
        in_specs=[row(GLA_DV), row(GLA_DV), row(GLA_DV), row(D_MODEL), full(hn), full(w)],
        out_specs=row(D_MODEL),
        out_shape=jax.ShapeDtypeStruct((T, D_MODEL), F32),
        compiler_params=_cparams(("arbitrary",)),
        name="gla_out",
    )(of, ob, r, x, hn, w)


def _gla_mixer(x, gain, w_in, w_up_f, b_f, w_up_b, b_b, head_norm, w_out):
    qk, v, r, gd = _gla_in(x, gain, w_in)
    of, ob = _gla_scan(qk, v, gd, w_up_f, b_f, w_up_b, b_b)
    return _gla_out(of, ob, r, x, head_norm, w_out)


def _router_kernel(h_ref, g_ref, w_ref, aff_ref, hn_ref):
    hn = _rms(h_ref[...], g_ref[...])
    hn_ref[...] = hn.astype(BF16)
    logits = _dot_split(hn, w_ref[...])
    e = jnp.exp(logits - jnp.max(logits, axis=-1, keepdims=True))
    aff_ref[...] = e / jnp.sum(e, axis=-1, keepdims=True)


def _router(h, gain, w_router, tm=512):
    T = h.shape[0]
    gain2 = gain.reshape(1, D_MODEL)
    row = lambda n: pl.BlockSpec((tm, n), lambda i: (i, 0))
    full = lambda a: pl.BlockSpec(a.shape, lambda i: (0, 0))
    return pl.pallas_call(
        _router_kernel,
        grid=(T // tm,),
        in_specs=[row(D_MODEL), full(gain2), full(w_router)],
        out_specs=[row(N_EXPERTS), row(D_MODEL)],
        out_shape=[jax.ShapeDtypeStruct((T, N_EXPERTS), F32), jax.ShapeDtypeStruct((T, D_MODEL), BF16)],
        compiler_params=_cparams(("arbitrary",)),
        name="moe_router",
    )(h, gain2, w_router)


def _select_kernel(aff_ref, tri_ref, posm_ref, before_ref, *, cap):
    T = aff_ref.shape[1]
    bits = pltpu.bitcast(aff_ref[...], jnp.int32)

    def search(it, thr):
        cand = thr | jnp.left_shift(jnp.int32(1), 30 - it)
        cnt = jnp.sum(jnp.where(bits >= cand, 1.0, 0.0), axis=1, keepdims=True)
        return jnp.where(cnt >= cap, cand, thr)

    thr = lax.fori_loop(0, 31, search, jnp.zeros((N_EXPERTS, 1), jnp.int32))
    n_gt = jnp.sum(jnp.where(bits > thr, 1.0, 0.0), axis=1, keepdims=True)
    need = cap - n_gt
    tri = tri_ref[...]

    def scan(j, carry):
        c_eq, c_sel = carry
        cols = pl.ds(pl.multiple_of(j * LANES, LANES), LANES)
        blk = pltpu.bitcast(aff_ref[:, cols], jnp.int32)
        eq = jnp.where(blk == thr, 1.0, 0.0)
        rank = _dot(eq.astype(BF16), tri) + c_eq - eq
        sel = jnp.where((blk > thr) | ((eq > 0.0) & (rank < need)), 1.0, 0.0)
        before = _dot(sel.astype(BF16), tri) + c_sel - sel
        before_ref[:, cols] = before.astype(jnp.int32)
        posm_ref[:, cols] = jnp.where(sel > 0.0, before, -1.0).astype(jnp.int32)
        return (c_eq + jnp.sum(eq, axis=1, keepdims=True), c_sel + jnp.sum(sel, axis=1, keepdims=True))

    zero = jnp.zeros((N_EXPERTS, 1), F32)
    lax.fori_loop(0, T // LANES, scan, (zero, zero))


def _select(aff_t, cap):
    T = aff_t.shape[1]
    tri = jnp.asarray(np.triu(np.ones((LANES, LANES), np.float32)), BF16)
    full = lambda a: pl.BlockSpec(a.shape, lambda: (0,) * a.ndim)
    out = jax.ShapeDtypeStruct((N_EXPERTS, T), jnp.int32)
    return pl.pallas_call(
        functools.partial(_select_kernel, cap=cap),
        in_specs=[full(aff_t), full(tri)],
        out_specs=[pl.BlockSpec((N_EXPERTS, T), lambda: (0, 0))] * 2,
        out_shape=[out, out],
        compiler_params=pltpu.CompilerParams(vmem_limit_bytes=VMEM_LIMIT),
        name="moe_select",
    )(aff_t, tri)


def _dispatch_window():
    return MOE_TILE + F32_SUBLANES


def _dispatch_kernel(start_ref, hn_ref, posm_ref, x_ref, acc_ref, *, cap, nt, sub):
    e, t = pl.program_id(0), pl.program_id(1)
    win = _dispatch_window()

    @pl.when(t == 0)
    def _():
        acc_ref[...] = jnp.zeros_like(acc_ref)

    for s in range(sub):
        tok = slice(s * MOE_TILE, (s + 1) * MOE_TILE)
        base = pl.multiple_of((start_ref[e * nt + t * sub + s] // F32_SUBLANES) * F32_SUBLANES, F32_SUBLANES)
        slot = base + lax.broadcasted_iota(jnp.int32, (win, MOE_TILE), 0)
        onehot = jnp.where(posm_ref[0, :, tok] == slot, 1.0, 0.0).astype(BF16)
        acc_ref[pl.ds(base, win), :] += _dot(onehot, hn_ref[tok, :])

    @pl.when(t == pl.num_programs(1) - 1)
    def _():
        x_ref[0] = acc_ref[:cap, :].astype(BF16)


def _dispatch(hn, posm, start, cap, sub=16):
    T = hn.shape[0]
    nt = T // MOE_TILE
    sub = math.gcd(sub, nt)
    blk = sub * MOE_TILE
    posm3 = posm.reshape(N_EXPERTS, 1, T)
    grid_spec = pltpu.PrefetchScalarGridSpec(
        num_scalar_prefetch=1,
        grid=(N_EXPERTS, nt // sub),
        in_specs=[
            pl.BlockSpec((blk, D_MODEL), lambda e, t, s: (t, 0)),
            pl.BlockSpec((1, 1, blk), lambda e, t, s: (e, 0, t)),
        ],
        out_specs=pl.BlockSpec((1, cap, D_MODEL), lambda e, t, s: (e, 0, 0)),
        scratch_shapes=[pltpu.VMEM((cap + _dispatch_window(), D_MODEL), F32)],
    )
    return pl.pallas_call(
        functools.partial(_dispatch_kernel, cap=cap, nt=nt, sub=sub),
        grid_spec=grid_spec,
        out_shape=jax.ShapeDtypeStruct((N_EXPERTS, cap, D_MODEL), BF16),
        compiler_params=_cparams(("arbitrary", "arbitrary")),
        name="moe_dispatch",
    )(start.reshape(-1), hn, posm3)


def _ffn_kernel(x_ref, wg_ref, wu_ref, wd_ref, y_ref, acc_ref):
    f = pl.program_id(1)

    @pl.when(f == 0)
    def _():
        acc_ref[...] = jnp.zeros_like(acc_ref)

    wg = wg_ref[0, 0].astype(BF16)
    wu = wu_ref[0, 0].astype(BF16)
    wd = wd_ref[0, 0].astype(BF16)
    cap = x_ref.shape[1]
    blk = cap // math.gcd(cap // F32_SUBLANES, 8)
    for r0 in range(0, cap, blk):
        rows = slice(r0, r0 + blk)
        x = x_ref[0, rows, :]
        a = _dot(x, wg)
        u = _dot(x, wu)
        mid = (a * (1.0 / (1.0 + jnp.exp(-a))) * u).astype(BF16)
        acc_ref[rows, :] += _dot(mid, wd)

    @pl.when(f == pl.num_programs(1) - 1)
    def _():
        y_ref[0] = acc_ref[...].astype(BF16)


def _ffn(x, w_gate, w_up, w_down, layer, tf=512):
    cap = x.shape[1]
    return pl.pallas_call(
        _ffn_kernel,
        grid=(N_EXPERTS, EXPERT_FF // tf),
        in_specs=[
            pl.BlockSpec((1, cap, D_MODEL), lambda e, f: (e, 0, 0)),
            pl.BlockSpec((1, 1, D_MODEL, tf), lambda e, f: (layer, e, 0, f)),
            pl.BlockSpec((1, 1, D_MODEL, tf), lambda e, f: (layer, e, 0, f)),
            pl.BlockSpec((1, 1, tf, D_MODEL), lambda e, f: (layer, e, f, 0)),
        ],
        out_specs=pl.BlockSpec((1, cap, D_MODEL), lambda e, f: (e, 0, 0)),
        out_shape=jax.ShapeDtypeStruct((N_EXPERTS, cap, D_MODEL), BF16),
        scratch_shapes=[pltpu.VMEM((cap, D_MODEL), F32)],
        compiler_params=_cparams(("arbitrary", "arbitrary")),
        name="moe_ffn",
    )(x, w_gate, w_up, w_down)


def _combine_window():
    return MOE_TILE + BF16_SUBLANES


def _combine_kernel(start_ref, h_ref, aff_ref, posm_ref, g_ref, y_hbm, o_ref, hbuf_ref, buf_ref, hsem_ref, rsem_ref,
                    acc_ref, *, cap, nt, final_norm):
    t = pl.program_id(0)
    win = _combine_window()
    head = COMBINE_HEAD

    def base_of(tile, e):
        b = (start_ref[e * (nt + 1) + tile] // BF16_SUBLANES) * BF16_SUBLANES
        return pl.multiple_of(jnp.minimum(b, cap - win), BF16_SUBLANES)

    def needs_rest(tile, e):
        return start_ref[e * (nt + 1) + tile + 1] > base_of(tile, e) + head

    def head_copy(tile, e, slot):
        return pltpu.make_async_copy(y_hbm.at[e, pl.ds(base_of(tile, e), head), :],
                                     hbuf_ref.at[slot, pl.ds(e * head, head), :], hsem_ref.at[slot])

    def rest_copy(tile, e, slot):
        return pltpu.make_async_copy(y_hbm.at[e, pl.ds(base_of(tile, e) + head, win - head), :],
                                     buf_ref.at[slot, e], rsem_ref.at[slot, e])

    def fetch(tile, slot):
        for e in range(N_EXPERTS):
            head_copy(tile, e, slot).start()

            @pl.when(needs_rest(tile, e))
            def _():
                rest_copy(tile, e, slot).start()

    slot = t % 2

    @pl.when(t == 0)
    def _():
        fetch(0, 0)

    @pl.when(t + 1 < nt)
    def _():
        fetch(t + 1, 1 - slot)

    def weights_t(e, lo, hi):
        row = base_of(t, e) + lo + lax.broadcasted_iota(jnp.int32, (hi - lo, MOE_TILE), 0)
        return jnp.where(posm_ref[e:e + 1, :] == row, aff_ref[e:e + 1, :], 0.0)

    def expand(w_t, y_rows):
        w_hi, w_lo = _split_bf16(w_t.T)
        return _dot(w_hi, y_rows) + _dot(w_lo, y_rows)

    for e in range(N_EXPERTS):
        head_copy(t, e, slot).wait()

    w_heads = jnp.concatenate([weights_t(e, 0, head) for e in range(N_EXPERTS)], axis=0)
    acc_ref[...] = h_ref[...] + expand(w_heads, hbuf_ref[slot])

    any_rest = needs_rest(t, 0)
    for e in range(1, N_EXPERTS):
        any_rest = any_rest | needs_rest(t, e)

    @pl.when(any_rest)
    def _():
        for e in range(N_EXPERTS):
            @pl.when(needs_rest(t, e))
            def _():
                rest_copy(t, e, slot).wait()
                for lo, hi in ((head, MOE_TILE), (MOE_TILE, win)):
                    acc_ref[...] += expand(weights_t(e, lo, hi), buf_ref[slot, e, lo - head:hi - head, :])

    acc = acc_ref[...]
    if final_norm:
        acc = _rms(acc, g_ref[...])
    o_ref[...] = acc


def _combine(h, aff_t, posm, start, y, cap, final_gain):
    T = h.shape[0]
    nt = T // MOE_TILE
    win = _combine_window()
    assert cap >= win and (cap - win) % BF16_SUBLANES == 0
    final_norm = final_gain is not None
    gain = (final_gain if final_norm else jnp.ones((D_MODEL,), F32)).reshape(1, D_MODEL)
    grid_spec = pltpu.PrefetchScalarGridSpec(
        num_scalar_prefetch=1,
        grid=(nt,),
        in_specs=[
            pl.BlockSpec((MOE_TILE, D_MODEL), lambda t, s: (t, 0)),
            pl.BlockSpec((N_EXPERTS, MOE_TILE), lambda t, s: (0, t)),
            pl.BlockSpec((N_EXPERTS, MOE_TILE), lambda t, s: (0, t)),
            pl.BlockSpec((1, D_MODEL), lambda t, s: (0, 0)),
            pl.BlockSpec(memory_space=pl.ANY),
        ],
        out_specs=pl.BlockSpec((MOE_TILE, D_MODEL), lambda t, s: (t, 0)),
        scratch_shapes=[pltpu.VMEM((2, N_EXPERTS * COMBINE_HEAD, D_MODEL), BF16),
                        pltpu.VMEM((2, N_EXPERTS, win - COMBINE_HEAD, D_MODEL), BF16),
                        pltpu.SemaphoreType.DMA((2,)), pltpu.SemaphoreType.DMA((2, N_EXPERTS)),
                        pltpu.VMEM((MOE_TILE, D_MODEL), F32)],
    )
    return pl.pallas_call(
        functools.partial(_combine_kernel, cap=cap, nt=nt, final_norm=final_norm),
        grid_spec=grid_spec,
        out_shape=jax.ShapeDtypeStruct((T, D_MODEL), F32),
        compiler_params=_cparams(("arbitrary",)),
        name="moe_combine",
    )(start.reshape(-1), h, aff_t, posm, gain, y)


def _ec_moe(h, gain, w_router, w_gate, w_up, w_down, layer, final_gain=None):
    T = h.shape[0]
    cap = max(1, EC_CAPACITY_FACTOR * T // N_EXPERTS)
    aff, hn = _router(h, gain, w_router)
    aff_t = aff.T
    posm, before = _select(aff_t, cap)
    start = before[:, ::MOE_TILE]
    x = _dispatch(hn, posm, start, cap)
    y = _ffn(x, w_gate, w_up, w_down, layer)
    start_end = jnp.concatenate([start, jnp.full((N_EXPERTS, 1), cap, jnp.int32)], axis=1)
    return _combine(h, aff_t, posm, start_end, y, cap, final_gain)


def _rms_cols(x, g):
    return x * lax.rsqrt(jnp.mean(x * x, axis=0, keepdims=True) + RMS_EPS) * g


MLA_QCOLS = MLA_NOPE + 2 * LANES


def _mla_in_kernel(h_ref, posc_ref, posr_ref, g_ref, win_ref, wintkv_ref, qn_ref, wq_ref, wukt_ref, kvn_ref,
                   kvnc_ref, freqr_ref, sign_ref, freqc_ref, q_out, kt_out, va_out):
    half = MLA_ROPE // 2
    hn = _rms(h_ref[...], g_ref[...]).astype(BF16)
    c = _dot(hn, win_ref[...])
    ckv_t = _dot_nt(wintkv_ref[...], hn)

    cq = _rms(c[:, :MLA_Q_RANK], qn_ref[...]).astype(BF16)
    qa = _dot(cq, wq_ref[...])
    ang = posc_ref[...].astype(F32) * freqr_ref[...]
    cos, sin_signed = jnp.cos(ang), jnp.sin(ang) * sign_ref[...]
    qscale = MLA_QK ** -0.5 * math.log2(math.e)
    for hd in range(MLA_HEADS):
        c0 = hd * MLA_QCOLS
        q_lat = _dot(qa[:, c0:c0 + MLA_NOPE].astype(BF16), wukt_ref[hd])
        rot = qa[:, c0 + MLA_NOPE:c0 + MLA_NOPE + LANES] * cos + qa[:, c0 + MLA_NOPE + LANES:c0 + MLA_QCOLS] * sin_signed
        q_out[hd, :, :MLA_NOPE] = (q_lat * qscale).astype(BF16)
        q_out[hd, :, MLA_NOPE:] = (rot[:, :MLA_ROPE] * qscale).astype(BF16)

    kt_out[:MLA_KV_RANK, :] = _rms_cols(ckv_t[:MLA_KV_RANK], kvnc_ref[...]).astype(BF16)
    ang_t = freqc_ref[...] * posr_ref[...].astype(F32)
    cos_t, sin_t = jnp.cos(ang_t), jnp.sin(ang_t)
    k1, k2 = ckv_t[MLA_KV_RANK:MLA_KV_RANK + half], ckv_t[MLA_KV_RANK + half:]
    kt_out[MLA_KV_RANK:MLA_KV_RANK + half, :] = (k1 * cos_t - k2 * sin_t).astype(BF16)
    kt_out[MLA_KV_RANK + half:, :] = (k1 * sin_t + k2 * cos_t).astype(BF16)

    ckv = _rms(c[:, MLA_Q_RANK:MLA_Q_RANK + MLA_KV_RANK], kvn_ref[...])
    va_out[:, :MLA_KV_RANK] = ckv.astype(BF16)
    lane = lax.broadcasted_iota(jnp.int32, (ckv.shape[0], LANES), 1)
    va_out[:, MLA_KV_RANK:] = jnp.where(lane == 0, 1.0, 0.0).astype(BF16)


def _mla_in(h, positions, gain, w_in, q_norm, w_uq, kv_norm, w_ukv, tm=256):
    T = h.shape[0]
    H, half = MLA_HEADS, MLA_ROPE // 2
    wq = w_uq.reshape(MLA_Q_RANK, H, MLA_QK)
    x1, x2 = wq[:, :, MLA_NOPE:MLA_NOPE + half], wq[:, :, MLA_NOPE + half:]
    pad = jnp.zeros((MLA_Q_RANK, H, LANES - MLA_ROPE), F32)
    wq_wide = jnp.concatenate([wq[:, :, :MLA_NOPE], x1, x2, pad, x2, x1, pad], axis=2)
    wq_wide = wq_wide.reshape(MLA_Q_RANK, H * MLA_QCOLS).astype(BF16)
    wuk_t = w_ukv.reshape(MLA_KV_RANK, H, MLA_NOPE + MLA_V)[:, :, :MLA_NOPE].transpose(1, 2, 0).astype(BF16)
    inv_freq = ROPE_BASE ** (-jnp.arange(half, dtype=F32) / half)
    zeros = jnp.zeros((LANES - MLA_ROPE,), F32)
    freq_row = jnp.concatenate([inv_freq, inv_freq, zeros]).reshape(1, LANES)
    sign_row = jnp.concatenate([-jnp.ones((half,), F32), jnp.ones((half,), F32), zeros]).reshape(1, LANES)
    w_in_b = w_in.astype(BF16)
    args = [h, positions.reshape(T, 1), positions.reshape(1, T), gain.reshape(1, -1), w_in_b,
            w_in_b[:, MLA_Q_RANK:].T, q_norm.reshape(1, -1), wq_wide, wuk_t, kv_norm.reshape(1, -1),
            kv_norm.reshape(-1, 1), freq_row, sign_row, inv_freq.reshape(half, 1)]
    full = lambda a: pl.BlockSpec(a.shape, lambda i: (0,) * a.ndim)
    in_specs = [pl.BlockSpec((tm, D_MODEL), lambda i: (i, 0)), pl.BlockSpec((tm, 1), lambda i: (i, 0)),
                pl.BlockSpec((1, tm), lambda i: (0, i))] + [full(a) for a in args[3:]]
    return pl.pallas_call(
        _mla_in_kernel,
        grid=(T // tm,),
        in_specs=in_specs,
        out_specs=[pl.BlockSpec((H, tm, MLA_QK), lambda i: (0, i, 0)),
                   pl.BlockSpec((MLA_QK, tm), lambda i: (0, i)),
                   pl.BlockSpec((tm, 2 * LANES), lambda i: (i, 0))],
        out_shape=[jax.ShapeDtypeStruct((H, T, MLA_QK), BF16), jax.ShapeDtypeStruct((MLA_QK, T), BF16),
                   jax.ShapeDtypeStruct((T, 2 * LANES), BF16)],
        compiler_params=_cparams(("arbitrary",)),
        name="mla_in",
    )(*args)


def _flash_kernel(q_ref, qn_ref, kt_ref, va_ref, o_ref, sa_ref, sb_ref, pa_ref, pb_ref, acc_ref, m_ref, alpha_ref,
                  *, tk, nblk):
    H, tq, _ = qn_ref.shape
    T = kt_ref.shape[1]
    n = T // tk
    tiles = q_ref.shape[1] // tq

    def q_rows(u):
        if isinstance(u, int):
            blk = qn_ref[...] if u == tiles else q_ref[:, u * tq:(u + 1) * tq, :]
        else:
            blk = q_ref[:, pl.ds(pl.multiple_of(u * tq, tq), tq), :]
        return blk.reshape(H * tq, MLA_QK)

    def k_tile(j):
        return kt_ref[:, j * tk:(j + 1) * tk]

    def v_tile(j):
        return va_ref[j * tk:(j + 1) * tk, :]

    def lanes_rep(x, width):
        return jnp.concatenate([x] * (width // LANES), axis=1)

    blk = H * tq // nblk
    row_blocks = [slice(rb * blk, (rb + 1) * blk) for rb in range(nblk)]

    def add_pv(rows, p_ref, j):
        acc_ref[rows, :] = (lanes_rep(alpha_ref[rows, :], 2 * LANES) * acc_ref[rows, :]
                            + _dot(p_ref[rows, :], v_tile(j)))

    def step(q, q_next, j, s_cur, s_nxt, p_cur, p_prv):
        q_sel, k_nxt = (q, k_tile(j + 1)) if j + 1 < n else (q_next, k_tile(0))
        for rows in row_blocks:
            if j > 0:
                add_pv(rows, p_prv, j - 1)
            s_nxt[rows, :] = _dot(q_sel[rows], k_nxt)
            s = s_cur[rows, :]
            m_old = m_ref[rows, :]
            m_new = jnp.maximum(m_old, jnp.max(s, axis=1, keepdims=True))
            alpha_ref[rows, :] = jnp.exp2(m_old - m_new)
            m_ref[rows, :] = m_new
            p_cur[rows, :] = jnp.exp2(s - lanes_rep(m_new, tk)).astype(BF16)

    @pl.when(pl.program_id(0) == 0)
    def _():
        sa_ref[...] = _dot(q_rows(0), k_tile(0))

    def token_tile(u, carry):
        q = q_rows(u)
        q_next = jnp.where(u == tiles - 1, q_rows(tiles), q_rows(jnp.minimum(u + 1, tiles - 1)))
        acc_ref[...] = jnp.zeros_like(acc_ref)
        m_ref[...] = jnp.full_like(m_ref, -jnp.inf)
        for j in range(n):
            if j % 2 == 0:
                step(q, q_next, j, sa_ref, sb_ref, pa_ref, pb_ref)
            else:
                step(q, q_next, j, sb_ref, sa_ref, pb_ref, pa_ref)
        p_last = pb_ref if n % 2 == 0 else pa_ref
        for rows in row_blocks:
            add_pv(rows, p_last, n - 1)
        acc = acc_ref[...]
        o_lat = (acc[:, :MLA_KV_RANK] / acc[:, MLA_KV_RANK:MLA_KV_RANK + 1]).astype(BF16)
        out_rows = pl.ds(pl.multiple_of(u * tq, tq), tq)
        for hd in range(H):
            o_ref[out_rows, hd * MLA_KV_RANK:(hd + 1) * MLA_KV_RANK] = o_lat[hd * tq:(hd + 1) * tq]
        return carry

    lax.fori_loop(0, tiles, token_tile, 0)


def _flash(q, kt, va, tq=64, tk=1024, tiles=4):
    H, T, _ = q.shape
    tq, tk = min(tq, T), min(tk, T)
    assert T % (2 * tk) == 0 and tq % BF16_SUBLANES == 0
    assert T % (tiles * tq) == 0
    rows = H * tq
    steps = T // (tiles * tq)
    last_tile = T // tq - 1
    return pl.pallas_call(
        functools.partial(_flash_kernel, tk=tk, nblk=2),
        grid=(steps,),
        scratch_shapes=[pltpu.VMEM((rows, tk), F32), pltpu.VMEM((rows, tk), F32), pltpu.VMEM((rows, tk), BF16),
                        pltpu.VMEM((rows, tk), BF16), pltpu.VMEM((rows, 2 * LANES), F32),
                        pltpu.VMEM((rows, LANES), F32), pltpu.VMEM((rows, LANES), F32)],
        in_specs=[
            pl.BlockSpec((H, tiles * tq, MLA_QK), lambda i: (0, i, 0)),
            pl.BlockSpec((H, tq, MLA_QK), lambda i: (0, jnp.minimum((i + 1) * tiles, last_tile), 0)),
            pl.BlockSpec((MLA_QK, T), lambda i: (0, 0)),
            pl.BlockSpec((T, 2 * LANES), lambda i: (0, 0)),
        ],
        out_specs=pl.BlockSpec((tiles * tq, H * MLA_KV_RANK), lambda i: (i, 0)),
        out_shape=jax.ShapeDtypeStruct((T, H * MLA_KV_RANK), BF16),
        compiler_params=_cparams(("arbitrary",)),
        name="mla_flash",
    )(q, q, kt, va)


def _mla_out_kernel(o_ref, wuv_ref, w_ref, h_ref, out_ref):
    v = [_dot(o_ref[:, hd * MLA_KV_RANK:(hd + 1) * MLA_KV_RANK], wuv_ref[hd]).astype(BF16)
         for hd in range(MLA_HEADS)]
    out_ref[...] = h_ref[...] + _dot(jnp.concatenate(v, axis=1), w_ref[...])


def _mla_out(o_lat, w_ukv, w_out, h, tm=512):
    T = h.shape[0]
    H = MLA_HEADS
    wuv = w_ukv.reshape(MLA_KV_RANK, H, MLA_NOPE + MLA_V)[:, :, MLA_NOPE:].transpose(1, 0, 2).astype(BF16)
    wb = w_out.astype(BF16)
    return pl.pallas_call(
        _mla_out_kernel,
        grid=(T // tm,),
        in_specs=[pl.BlockSpec((tm, H * MLA_KV_RANK), lambda i: (i, 0)), pl.BlockSpec(wuv.shape, lambda i: (0, 0, 0)),
                  pl.BlockSpec(wb.shape, lambda i: (0, 0)), pl.BlockSpec((tm, D_MODEL), lambda i: (i, 0))],
        out_specs=pl.BlockSpec((tm, D_MODEL), lambda i: (i, 0)),
        out_shape=jax.ShapeDtypeStruct((T, D_MODEL), F32),
        compiler_params=_cparams(("arbitrary",)),
        name="mla_out",
    )(o_lat, wuv, wb, h)


def _mla_mixer(h, positions, gain, w_in, q_norm, w_uq, kv_norm, w_ukv, w_out):
    q, kt, va = _mla_in(h, positions, gain, w_in, q_norm, w_uq, kv_norm, w_ukv)
    o_lat = _flash(q, kt, va)
    return _mla_out(o_lat, w_ukv, w_out, h)


def kernel(x, positions, mix_norm, ffn_norm, final_norm, gla_w_in, gla_w_gate_up_f, gla_b_gate_f, gla_w_gate_up_b,
           gla_b_gate_b, gla_head_norm, gla_w_out, mla_w_in, mla_q_norm, mla_w_uq, mla_kv_norm, mla_w_ukv,
           mla_w_out, moe_w_router, moe_w_gate, moe_w_up, moe_w_down):
    B, T, D = x.shape
    outs = []
    for b in range(B):
        h = x[b]
        h = _gla_mixer(h, mix_norm[0], gla_w_in[0], gla_w_gate_up_f[0], gla_b_gate_f[0], gla_w_gate_up_b[0],
                       gla_b_gate_b[0], gla_head_norm[0], gla_w_out[0])
        h = _ec_moe(h, ffn_norm[0], moe_w_router[0], moe_w_gate, moe_w_up, moe_w_down, 0)
        h = _mla_mixer(h, positions[b], mix_norm[1], mla_w_in[0], mla_q_norm[0], mla_w_uq[0], mla_kv_norm[0],
                       mla_w_ukv[0], mla_w_out[0])
        h = _ec_moe(h, ffn_norm[1], moe_w_router[1], moe_w_gate, moe_w_up, moe_w_down, 1, final_gain=final_norm)
        outs.append(h)
    return jnp.stack(outs)
```

```python
import functools
import math

import numpy as np
import jax
import jax.numpy as jnp
from jax import lax
from jax.experimental import pallas as pl
from jax.experimental.pallas import tpu as pltpu

F32 = jnp.float32
BF16 = jnp.bfloat16

D_MODEL = 1024
RMS_EPS = 1e-6

GLA_HEADS = 4
GLA_DK = 512
GLA_DV = 1024
GLA_HEAD_K = GLA_DK // GLA_HEADS
GLA_HEAD_V = GLA_DV // GLA_HEADS
GLA_GATE_RANK = 16
GLA_TAU = 16.0
GLA_CHUNK = 64
GLA_TILE = 256
GLA_LEVELS = 6

MLA_HEADS = 16
MLA_Q_RANK = 256
MLA_KV_RANK = 128
MLA_NOPE = 128
MLA_ROPE = 64
MLA_V = 128
MLA_QK = MLA_NOPE + MLA_ROPE
ROPE_BASE = 10000.0

N_EXPERTS = 16
EXPERT_FF = 2048
EC_CAPACITY_FACTOR = 2
MOE_TILE = 256
COMBINE_HEAD = 64
BF16_SUBLANES = 16
F32_SUBLANES = 8
LANES = 128

VMEM_LIMIT = 56 * 1024 * 1024


def _cparams(sem):
    return pltpu.CompilerParams(dimension_semantics=sem, vmem_limit_bytes=VMEM_LIMIT)


def _rms(x, g):
    return x * lax.rsqrt(jnp.mean(x * x, axis=-1, keepdims=True) + RMS_EPS) * g


def _split_bf16(x):
    hi = x.astype(BF16)
    lo = (x - hi.astype(F32)).astype(BF16)
    return hi, lo


def _dot(a, b):
    return jnp.dot(a, b, preferred_element_type=F32)


def _dot_nt(a, b):
    return lax.dot_general(a, b, (((1,), (1,)), ((), ())), preferred_element_type=F32)


def _dot_tn(a, b):
    return lax.dot_general(a, b, (((0,), (0,)), ((), ())), preferred_element_type=F32)


def _dot_split(a, b):
    ah, al = _split_bf16(a)
    bh, bl = _split_bf16(b)
    return _dot(ah, bh) + _dot(ah, bl) + _dot(al, bh)


def _gla_in_kernel(x_ref, g_ref, wqk_ref, wv_ref, wr_ref, wgd_ref, qk_ref, v_ref, r_ref, gd_ref):
    hn = _rms(x_ref[...], g_ref[...]).astype(BF16)
    qk_ref[...] = _dot(hn, wqk_ref[...])
    v_ref[...] = _dot(hn, wv_ref[...])
    r_ref[...] = _dot(hn, wr_ref[...])
    gd_ref[...] = _dot(hn, wgd_ref[...])


def _gla_in(x, gain, w_in, tm=512):
    T = x.shape[0]
    wqk = w_in[:, :2 * GLA_DK].astype(BF16)
    wv = w_in[:, 2 * GLA_DK:2 * GLA_DK + GLA_DV].astype(BF16)
    wr = w_in[:, 2 * GLA_DK + GLA_DV:2 * GLA_DK + 2 * GLA_DV].astype(BF16)
    wgd = w_in[:, 2 * GLA_DK + 2 * GLA_DV:].astype(BF16)
    ngd = 2 * GLA_GATE_RANK
    row = lambda n: pl.BlockSpec((tm, n), lambda i: (i, 0))
    full = lambda a: pl.BlockSpec(a.shape, lambda i: (0, 0))
    gain2 = gain.reshape(1, D_MODEL)
    return pl.pallas_call(
        _gla_in_kernel,
        grid=(T // tm,),
        in_specs=[row(D_MODEL), full(gain2), full(wqk), full(wv), full(wr), full(wgd)],
        out_specs=[row(2 * GLA_DK), row(GLA_DV), row(GLA_DV), row(ngd)],
        out_shape=[jax.ShapeDtypeStruct((T, 2 * GLA_DK), F32), jax.ShapeDtypeStruct((T, GLA_DV), F32),
                   jax.ShapeDtypeStruct((T, GLA_DV), F32), jax.ShapeDtypeStruct((T, ngd), F32)],
        compiler_params=_cparams(("arbitrary",)),
        name="gla_in",
    )(x, gain2, wqk, wv, wr, wgd)


def _gla_tables(reverse):
    n, c = GLA_TILE, GLA_CHUNK
    W = np.zeros((GLA_LEVELS + 3, n, n), np.float32)
    L = np.full((n, n), -1, np.int32)
    for t in range(n):
        c0 = (t // c) * c
        tt = t - c0
        for l in range(GLA_LEVELS):
            b = (c // 2) >> l
            p0 = c0 + (tt // (2 * b)) * 2 * b
            mid = p0 + b
            second = t >= mid
            if not reverse:
                if second:
                    W[l, t, mid:t + 1] = 1
                    L[t, p0:mid] = l
                else:
                    W[l, t, t + 1:mid] = 1
            else:
                if second:
                    W[l, t, mid:t] = 1
                else:
                    W[l, t, t:mid] = 1
                    L[t, mid:p0 + 2 * b] = l
        if not reverse:
            W[GLA_LEVELS, t, c0:t + 1] = 1
            W[GLA_LEVELS + 1, t, t + 1:c0 + c] = 1
            L[t, t] = GLA_LEVELS
        else:
            W[GLA_LEVELS, t, t:c0 + c] = 1
            W[GLA_LEVELS + 1, t, c0:t] = 1
        W[GLA_LEVELS + 2, t, c0:c0 + c] = 1
    return W.reshape(-1, n), L


def _gla_scan_kernel(qkf_ref, vf_ref, gdf_ref, qkb_ref, vb_ref, gdb_ref, wupf_ref, bf_ref, wupb_ref, bb_ref,
                     wf_ref, lf_ref, wb_ref, lb_ref, of_ref, ob_ref, s_ref):
    n, c, r = GLA_TILE, GLA_CHUNK, GLA_GATE_RANK

    @pl.when(pl.program_id(0) == 0)
    def _():
        s_ref[...] = jnp.zeros_like(s_ref)

    dirs = [(qkf_ref, vf_ref, gdf_ref[:, :r], wupf_ref, bf_ref, wf_ref, lf_ref, of_ref, False),
            (qkb_ref, vb_ref, gdb_ref[:, r:], wupb_ref, bb_ref, wb_ref, lb_ref, ob_ref, True)]

    factors = []
    for qk_ref, v_ref, gd, wup_ref, b_ref, w_ref, l_ref, o_ref, reverse in dirs:
        z = _dot_split(gd, wup_ref[...]) + b_ref[...]
        g = (jnp.minimum(z, 0.0) - jnp.log1p(jnp.exp(-jnp.abs(z)))) * (1.0 / GLA_TAU)
        ghi, glo = _split_bf16(g)
        w = w_ref[...]
        factors.append(jnp.exp(_dot(w, ghi) + _dot(w, glo)))

    chains = []
    for d, (qk_ref, v_ref, gd, wup_ref, b_ref, w_ref, l_ref, o_ref, reverse) in enumerate(dirs):
        lvl = l_ref[...]
        for h in range(GLA_HEADS):
            kcols = slice(h * GLA_HEAD_K, (h + 1) * GLA_HEAD_K)
            vcols = slice(h * GLA_HEAD_V, (h + 1) * GLA_HEAD_V)
            f = factors[d][:, kcols]
            q = qk_ref[:, kcols] * (GLA_HEAD_K ** -0.5)
            k = qk_ref[:, GLA_DK + h * GLA_HEAD_K:GLA_DK + (h + 1) * GLA_HEAD_K]
            vb = v_ref[:, vcols].astype(BF16)
            attn = jnp.zeros((n, n), F32)
            for l in range(GLA_LEVELS):
                fl = f[l * n:(l + 1) * n]
                p = _dot_nt((q * fl).astype(BF16), (k * fl).astype(BF16))
                attn = jnp.where(lvl == l, p, attn)
            if not reverse:
                p = _dot_nt(q.astype(BF16), k.astype(BF16))
                attn = jnp.where(lvl == GLA_LEVELS, p, attn)
            o_intra = _dot(attn.astype(BF16), vb)
            qh = (q * f[GLA_LEVELS * n:(GLA_LEVELS + 1) * n]).astype(BF16)
            kh = (k * f[(GLA_LEVELS + 1) * n:(GLA_LEVELS + 2) * n]).astype(BF16)
            ftot = f[(GLA_LEVELS + 2) * n:(GLA_LEVELS + 3) * n]
            chains.append(dict(d=d, h=h, vcols=vcols, o_ref=o_ref, reverse=reverse, vb=vb, o_intra=o_intra, qh=qh,
                               kh=kh, ftot=ftot, st=s_ref[d, h]))

    nchunks = n // c
    for step in range(nchunks):
        for ch in chains:
            j = nchunks - 1 - step if ch["reverse"] else step
            rows = slice(j * c, (j + 1) * c)
            st = ch["st"]
            ch["o_ref"][rows, ch["vcols"]] = ch["o_intra"][rows] + _dot_nt(ch["qh"][rows], st.astype(BF16))
            ch["st"] = st * ch["ftot"][j * c:j * c + 1, :] + _dot_tn(ch["vb"][rows], ch["kh"][rows])
    for ch in chains:
        s_ref[ch["d"], ch["h"]] = ch["st"]


def _gla_scan(qk, v, gd, w_up_f, b_f, w_up_b, b_b):
    T = qk.shape[0]
    n = GLA_TILE
    nt = T // n
    tables = []
    for reverse in (False, True):
        W, L = _gla_tables(reverse)
        tables += [jnp.asarray(W, BF16), jnp.asarray(L)]
    fwd = lambda width: pl.BlockSpec((n, width), lambda i: (i, 0))
    bwd = lambda width: pl.BlockSpec((n, width), lambda i: (nt - 1 - i, 0))
    full = lambda a: pl.BlockSpec(a.shape, lambda i: (0, 0))
    consts = [w_up_f, b_f.reshape(1, GLA_DK), w_up_b, b_b.reshape(1, GLA_DK)] + tables
    ngd = 2 * GLA_GATE_RANK
    out = jax.ShapeDtypeStruct((T, GLA_DV), F32)
    return pl.pallas_call(
        _gla_scan_kernel,
        grid=(nt,),
        in_specs=[fwd(2 * GLA_DK), fwd(GLA_DV), fwd(ngd), bwd(2 * GLA_DK), bwd(GLA_DV), bwd(ngd)]
                 + [full(a) for a in consts],
        out_specs=[fwd(GLA_DV), bwd(GLA_DV)],
        out_shape=[out, out],
        scratch_shapes=[pltpu.VMEM((2, GLA_HEADS, GLA_HEAD_V, GLA_HEAD_K), F32)],
        compiler_params=_cparams(("arbitrary",)),
        name="gla_scan",
    )(qk, v, gd, qk, v, gd, *consts)


def _gla_out_kernel(of_ref, ob_ref, r_ref, x_ref, hn_ref, w_ref, o_ref):
    acc = x_ref[...]
    for h in range(GLA_HEADS):
        cols = slice(h * GLA_HEAD_V, (h + 1) * GLA_HEAD_V)
        o = _rms(of_ref[:, cols] + ob_ref[:, cols], hn_ref[...])
        r = r_ref[:, cols]
        gated = o * (r * (1.0 / (1.0 + jnp.exp(-r))))
        acc = acc + _dot(gated.astype(BF16), w_ref[cols, :])
    o_ref[...] = acc


def _gla_out(of, ob, r, x, head_norm, w_out, tm=512):
    T = x.shape[0]
    w = w_out.astype(BF16)
    hn = head_norm.reshape(1, GLA_HEAD_V)
    row = lambda n: pl.BlockSpec((tm, n), lambda i: (i, 0))
    full = lambda a: pl.BlockSpec(a.shape, lambda i: (0, 0))
    return pl.pallas_call(
        _gla_out_kernel,
        grid=(T // tm,),
        in_specs=[row(GLA_DV), row(GLA_DV), row(GLA_DV), row(D_MODEL), full(hn), full(w)],
        out_specs=row(D_MODEL),
        out_shape=jax.ShapeDtypeStruct((T, D_MODEL), F32),
        compiler_params=_cparams(("arbitrary",)),
        name="gla_out",
    )(of, ob, r, x, hn, w)


def _gla_mixer(x, gain, w_in, w_up_f, b_f, w_up_b, b_b, head_norm, w_out):
    qk, v, r, gd = _gla_in(x, gain, w_in)
    of, ob = _gla_scan(qk, v, gd, w_up_f, b_f, w_up_b, b_b)
    return _gla_out(of, ob, r, x, head_norm, w_out)


def _router_kernel(h_ref, g_ref, w_ref, aff_ref, hn_ref):
    hn = _rms(h_ref[...], g_ref[...])
    hn_ref[...] = hn.astype(BF16)
    logits = _dot_split(hn, w_ref[...])
    e = jnp.exp(logits - jnp.max(logits, axis=-1, keepdims=True))
    aff_ref[...] = e / jnp.sum(e, axis=-1, keepdims=True)


def _router(h, gain, w_router, tm=512):
    T = h.shape[0]
    gain2 = gain.reshape(1, D_MODEL)
    row = lambda n: pl.BlockSpec((tm, n), lambda i: (i, 0))
    full = lambda a: pl.BlockSpec(a.shape, lambda i: (0, 0))
    return pl.pallas_call(
        _router_kernel,
        grid=(T // tm,),
        in_specs=[row(D_MODEL), full(gain2), full(w_router)],
        out_specs=[row(N_EXPERTS), row(D_MODEL)],
        out_shape=[jax.ShapeDtypeStruct((T, N_EXPERTS), F32), jax.ShapeDtypeStruct((T, D_MODEL), BF16)],
        compiler_params=_cparams(("arbitrary",)),
        name="moe_router",
    )(h, gain2, w_router)


def _select_kernel(aff_ref, tri_ref, posm_ref, before_ref, *, cap):
    T = aff_ref.shape[1]
    bits = pltpu.bitcast(aff_ref[...], jnp.int32)

    def search(it, thr):
        cand = thr | jnp.left_shift(jnp.int32(1), 30 - it)
        cnt = jnp.sum(jnp.where(bits >= cand, 1.0, 0.0), axis=1, keepdims=True)
        return jnp.where(cnt >= cap, cand, thr)

    thr = lax.fori_loop(0, 31, search, jnp.zeros((N_EXPERTS, 1), jnp.int32))
    n_gt = jnp.sum(jnp.where(bits > thr, 1.0, 0.0), axis=1, keepdims=True)
    need = cap - n_gt
    tri = tri_ref[...]

    def scan(j, carry):
        c_eq, c_sel = carry
        cols = pl.ds(pl.multiple_of(j * LANES, LANES), LANES)
        blk = pltpu.bitcast(aff_ref[:, cols], jnp.int32)
        eq = jnp.where(blk == thr, 1.0, 0.0)
        rank = _dot(eq.astype(BF16), tri) + c_eq - eq
        sel = jnp.where((blk > thr) | ((eq > 0.0) & (rank < need)), 1.0, 0.0)
        before = _dot(sel.astype(BF16), tri) + c_sel - sel
        before_ref[:, cols] = before.astype(jnp.int32)
        posm_ref[:, cols] = jnp.where(sel > 0.0, before, -1.0).astype(jnp.int32)
        return (c_eq + jnp.sum(eq, axis=1, keepdims=True), c_sel + jnp.sum(sel, axis=1, keepdims=True))

    zero = jnp.zeros((N_EXPERTS, 1), F32)
    lax.fori_loop(0, T // LANES, scan, (zero, zero))


def _select(aff_t, cap):
    T = aff_t.shape[1]
    tri = jnp.asarray(np.triu(np.ones((LANES, LANES), np.float32)), BF16)
    full = lambda a: pl.BlockSpec(a.shape, lambda: (0,) * a.ndim)
    out = jax.ShapeDtypeStruct((N_EXPERTS, T), jnp.int32)
    return pl.pallas_call(
        functools.partial(_select_kernel, cap=cap),
        in_specs=[full(aff_t), full(tri)],
        out_specs=[pl.BlockSpec((N_EXPERTS, T), lambda: (0, 0))] * 2,
        out_shape=[out, out],
        compiler_params=pltpu.CompilerParams(vmem_limit_bytes=VMEM_LIMIT),
        name="moe_select",
    )(aff_t, tri)


def _dispatch_window():
    return MOE_TILE + F32_SUBLANES


def _dispatch_kernel(start_ref, hn_ref, posm_ref, x_ref, acc_ref, *, cap, nt, sub):
    e, t = pl.program_id(0), pl.program_id(1)
    win = _dispatch_window()

    @pl.when(t == 0)
    def _():
        acc_ref[...] = jnp.zeros_like(acc_ref)

    for s in range(sub):
        tok = slice(s * MOE_TILE, (s + 1) * MOE_TILE)
        base = pl.multiple_of((start_ref[e * nt + t * sub + s] // F32_SUBLANES) * F32_SUBLANES, F32_SUBLANES)
        slot = base + lax.broadcasted_iota(jnp.int32, (win, MOE_TILE), 0)
        onehot = jnp.where(posm_ref[0, :, tok] == slot, 1.0, 0.0).astype(BF16)
        acc_ref[pl.ds(base, win), :] += _dot(onehot, hn_ref[tok, :])

    @pl.when(t == pl.num_programs(1) - 1)
    def _():
        x_ref[0] = acc_ref[:cap, :].astype(BF16)


def _dispatch(hn, posm, start, cap, sub=16):
    T = hn.shape[0]
    nt = T // MOE_TILE
    sub = math.gcd(sub, nt)
    blk = sub * MOE_TILE
    posm3 = posm.reshape(N_EXPERTS, 1, T)
    grid_spec = pltpu.PrefetchScalarGridSpec(
        num_scalar_prefetch=1,
        grid=(N_EXPERTS, nt // sub),
        in_specs=[
            pl.BlockSpec((blk, D_MODEL), lambda e, t, s: (t, 0)),
            pl.BlockSpec((1, 1, blk), lambda e, t, s: (e, 0, t)),
        ],
        out_specs=pl.BlockSpec((1, cap, D_MODEL), lambda e, t, s: (e, 0, 0)),
        scratch_shapes=[pltpu.VMEM((cap + _dispatch_window(), D_MODEL), F32)],
    )
    return pl.pallas_call(
        functools.partial(_dispatch_kernel, cap=cap, nt=nt, sub=sub),
        grid_spec=grid_spec,
        out_shape=jax.ShapeDtypeStruct((N_EXPERTS, cap, D_MODEL), BF16),
        compiler_params=_cparams(("arbitrary", "arbitrary")),
        name="moe_dispatch",
    )(start.reshape(-1), hn, posm3)


def _ffn_kernel(x_ref, wg_ref, wu_ref, wd_ref, y_ref, acc_ref):
    f = pl.program_id(1)

    @pl.when(f == 0)
    def _():
        acc_ref[...] = jnp.zeros_like(acc_ref)

    wg = wg_ref[0, 0].astype(BF16)
    wu = wu_ref[0, 0].astype(BF16)
    wd = wd_ref[0, 0].astype(BF16)
    cap = x_ref.shape[1]
    blk = cap // math.gcd(cap // F32_SUBLANES, 4)
    for r0 in range(0, cap, blk):
        rows = slice(r0, r0 + blk)
        x = x_ref[0, rows, :]
        a = _dot(x, wg)
        u = _dot(x, wu)
        mid = (a * (1.0 / (1.0 + jnp.exp(-a))) * u).astype(BF16)
        acc_ref[rows, :] += _dot(mid, wd)

    @pl.when(f == pl.num_programs(1) - 1)
    def _():
        y_ref[0] = acc_ref[...].astype(BF16)


def _ffn(x, w_gate, w_up, w_down, layer, tf=512):
    cap = x.shape[1]
    return pl.pallas_call(
        _ffn_kernel,
        grid=(N_EXPERTS, EXPERT_FF // tf),
        in_specs=[
            pl.BlockSpec((1, cap, D_MODEL), lambda e, f: (e, 0, 0)),
            pl.BlockSpec((1, 1, D_MODEL, tf), lambda e, f: (layer, e, 0, f)),
            pl.BlockSpec((1, 1, D_MODEL, tf), lambda e, f: (layer, e, 0, f)),
            pl.BlockSpec((1, 1, tf, D_MODEL), lambda e, f: (layer, e, f, 0)),
        ],
        out_specs=pl.BlockSpec((1, cap, D_MODEL), lambda e, f: (e, 0, 0)),
        out_shape=jax.ShapeDtypeStruct((N_EXPERTS, cap, D_MODEL), BF16),
        scratch_shapes=[pltpu.VMEM((cap, D_MODEL), F32)],
        compiler_params=_cparams(("arbitrary", "arbitrary")),
        name="moe_ffn",
    )(x, w_gate, w_up, w_down)


def _combine_window():
    return MOE_TILE + BF16_SUBLANES


def _combine_kernel(start_ref, h_ref, aff_ref, posm_ref, g_ref, y_hbm, o_ref, hbuf_ref, buf_ref, hsem_ref, rsem_ref,
                    acc_ref, *, cap, nt, final_norm):
    t = pl.program_id(0)
    win = _combine_window()
    head = COMBINE_HEAD

    def base_of(tile, e):
        b = (start_ref[e * (nt + 1) + tile] // BF16_SUBLANES) * BF16_SUBLANES
        return pl.multiple_of(jnp.minimum(b, cap - win), BF16_SUBLANES)

    def needs_rest(tile, e):
        return start_ref[e * (nt + 1) + tile + 1] > base_of(tile, e) + head

    def head_copy(tile, e, slot):
        return pltpu.make_async_copy(y_hbm.at[e, pl.ds(base_of(tile, e), head), :],
                                     hbuf_ref.at[slot, pl.ds(e * head, head), :], hsem_ref.at[slot])

    def rest_copy(tile, e, slot):
        return pltpu.make_async_copy(y_hbm.at[e, pl.ds(base_of(tile, e) + head, win - head), :],
                                     buf_ref.at[slot, e], rsem_ref.at[slot, e])

    def fetch(tile, slot):
        for e in range(N_EXPERTS):
            head_copy(tile, e, slot).start()

            @pl.when(needs_rest(tile, e))
            def _():
                rest_copy(tile, e, slot).start()

    slot = t % 2

    @pl.when(t == 0)
    def _():
        fetch(0, 0)

    @pl.when(t + 1 < nt)
    def _():
        fetch(t + 1, 1 - slot)

    def weights_t(e, lo, hi):
        row = base_of(t, e) + lo + lax.broadcasted_iota(jnp.int32, (hi - lo, MOE_TILE), 0)
        return jnp.where(posm_ref[e:e + 1, :] == row, aff_ref[e:e + 1, :], 0.0)

    def expand(w_t, y_rows):
        w_hi, w_lo = _split_bf16(w_t.T)
        return _dot(w_hi, y_rows) + _dot(w_lo, y_rows)

    for e in range(N_EXPERTS):
        head_copy(t, e, slot).wait()

    w_heads = jnp.concatenate([weights_t(e, 0, head) for e in range(N_EXPERTS)], axis=0)
    acc_ref[...] = h_ref[...] + expand(w_heads, hbuf_ref[slot])

    any_rest = needs_rest(t, 0)
    for e in range(1, N_EXPERTS):
        any_rest = any_rest | needs_rest(t, e)

    @pl.when(any_rest)
    def _():
        for e in range(N_EXPERTS):
            @pl.when(needs_rest(t, e))
            def _():
                rest_copy(t, e, slot).wait()
                for lo, hi in ((head, MOE_TILE), (MOE_TILE, win)):
                    acc_ref[...] += expand(weights_t(e, lo, hi), buf_ref[slot, e, lo - head:hi - head, :])

    acc = acc_ref[...]
    if final_norm:
        acc = _rms(acc, g_ref[...])
    o_ref[...] = acc


def _combine(h, aff_t, posm, start, y, cap, final_gain):
    T = h.shape[0]
    nt = T // MOE_TILE
    win = _combine_window()
    assert cap >= win and (cap - win) % BF16_SUBLANES == 0
    final_norm = final_gain is not None
    gain = (final_gain if final_norm else jnp.ones((D_MODEL,), F32)).reshape(1, D_MODEL)
    grid_spec = pltpu.PrefetchScalarGridSpec(
        num_scalar_prefetch=1,
        grid=(nt,),
        in_specs=[
            pl.BlockSpec((MOE_TILE, D_MODEL), lambda t, s: (t, 0)),
            pl.BlockSpec((N_EXPERTS, MOE_TILE), lambda t, s: (0, t)),
            pl.BlockSpec((N_EXPERTS, MOE_TILE), lambda t, s: (0, t)),
            pl.BlockSpec((1, D_MODEL), lambda t, s: (0, 0)),
            pl.BlockSpec(memory_space=pl.ANY),
        ],
        out_specs=pl.BlockSpec((MOE_TILE, D_MODEL), lambda t, s: (t, 0)),
        scratch_shapes=[pltpu.VMEM((2, N_EXPERTS * COMBINE_HEAD, D_MODEL), BF16),
                        pltpu.VMEM((2, N_EXPERTS, win - COMBINE_HEAD, D_MODEL), BF16),
                        pltpu.SemaphoreType.DMA((2,)), pltpu.SemaphoreType.DMA((2, N_EXPERTS)),
                        pltpu.VMEM((MOE_TILE, D_MODEL), F32)],
    )
    return pl.pallas_call(
        functools.partial(_combine_kernel, cap=cap, nt=nt, final_norm=final_norm),
        grid_spec=grid_spec,
        out_shape=jax.ShapeDtypeStruct((T, D_MODEL), F32),
        compiler_params=_cparams(("arbitrary",)),
        name="moe_combine",
    )(start.reshape(-1), h, aff_t, posm, gain, y)


def _ec_moe(h, gain, w_router, w_gate, w_up, w_down, layer, final_gain=None):
    T = h.shape[0]
    cap = max(1, EC_CAPACITY_FACTOR * T // N_EXPERTS)
    aff, hn = _router(h, gain, w_router)
    aff_t = aff.T
    posm, before = _select(aff_t, cap)
    start = before[:, ::MOE_TILE]
    x = _dispatch(hn, posm, start, cap)
    y = _ffn(x, w_gate, w_up, w_down, layer)
    start_end = jnp.concatenate([start, jnp.full((N_EXPERTS, 1), cap, jnp.int32)], axis=1)
    return _combine(h, aff_t, posm, start_end, y, cap, final_gain)


def _rms_cols(x, g):
    return x * lax.rsqrt(jnp.mean(x * x, axis=0, keepdims=True) + RMS_EPS) * g


MLA_QCOLS = MLA_NOPE + 2 * LANES


def _mla_in_kernel(h_ref, posc_ref, posr_ref, g_ref, win_ref, wintkv_ref, qn_ref, wq_ref, wukt_ref, kvn_ref,
                   kvnc_ref, freqr_ref, sign_ref, freqc_ref, q_out, kt_out, va_out):
    half = MLA_ROPE // 2
    hn = _rms(h_ref[...], g_ref[...]).astype(BF16)
    c = _dot(hn, win_ref[...])
    ckv_t = _dot_nt(wintkv_ref[...], hn)

    cq = _rms(c[:, :MLA_Q_RANK], qn_ref[...]).astype(BF16)
    qa = _dot(cq, wq_ref[...])
    ang = posc_ref[...].astype(F32) * freqr_ref[...]
    cos, sin_signed = jnp.cos(ang), jnp.sin(ang) * sign_ref[...]
    qscale = MLA_QK ** -0.5 * math.log2(math.e)
    for hd in range(MLA_HEADS):
        c0 = hd * MLA_QCOLS
        q_lat = _dot(qa[:, c0:c0 + MLA_NOPE].astype(BF16), wukt_ref[hd])
        rot = qa[:, c0 + MLA_NOPE:c0 + MLA_NOPE + LANES] * cos + qa[:, c0 + MLA_NOPE + LANES:c0 + MLA_QCOLS] * sin_signed
        q_out[hd, :, :MLA_NOPE] = (q_lat * qscale).astype(BF16)
        q_out[hd, :, MLA_NOPE:] = (rot[:, :MLA_ROPE] * qscale).astype(BF16)

    kt_out[:MLA_KV_RANK, :] = _rms_cols(ckv_t[:MLA_KV_RANK], kvnc_ref[...]).astype(BF16)
    ang_t = freqc_ref[...] * posr_ref[...].astype(F32)
    cos_t, sin_t = jnp.cos(ang_t), jnp.sin(ang_t)
    k1, k2 = ckv_t[MLA_KV_RANK:MLA_KV_RANK + half], ckv_t[MLA_KV_RANK + half:]
    kt_out[MLA_KV_RANK:MLA_KV_RANK + half, :] = (k1 * cos_t - k2 * sin_t).astype(BF16)
    kt_out[MLA_KV_RANK + half:, :] = (k1 * sin_t + k2 * cos_t).astype(BF16)

    ckv = _rms(c[:, MLA_Q_RANK:MLA_Q_RANK + MLA_KV_RANK], kvn_ref[...])
    va_out[:, :MLA_KV_RANK] = ckv.astype(BF16)
    lane = lax.broadcasted_iota(jnp.int32, (ckv.shape[0], LANES), 1)
    va_out[:, MLA_KV_RANK:] = jnp.where(lane == 0, 1.0, 0.0).astype(BF16)


def _mla_in(h, positions, gain, w_in, q_norm, w_uq, kv_norm, w_ukv, tm=256):
    T = h.shape[0]
    H, half = MLA_HEADS, MLA_ROPE // 2
    wq = w_uq.reshape(MLA_Q_RANK, H, MLA_QK)
    x1, x2 = wq[:, :, MLA_NOPE:MLA_NOPE + half], wq[:, :, MLA_NOPE + half:]
    pad = jnp.zeros((MLA_Q_RANK, H, LANES - MLA_ROPE), F32)
    wq_wide = jnp.concatenate([wq[:, :, :MLA_NOPE], x1, x2, pad, x2, x1, pad], axis=2)
    wq_wide = wq_wide.reshape(MLA_Q_RANK, H * MLA_QCOLS).astype(BF16)
    wuk_t = w_ukv.reshape(MLA_KV_RANK, H, MLA_NOPE + MLA_V)[:, :, :MLA_NOPE].transpose(1, 2, 0).astype(BF16)
    inv_freq = ROPE_BASE ** (-jnp.arange(half, dtype=F32) / half)
    zeros = jnp.zeros((LANES - MLA_ROPE,), F32)
    freq_row = jnp.concatenate([inv_freq, inv_freq, zeros]).reshape(1, LANES)
    sign_row = jnp.concatenate([-jnp.ones((half,), F32), jnp.ones((half,), F32), zeros]).reshape(1, LANES)
    w_in_b = w_in.astype(BF16)
    args = [h, positions.reshape(T, 1), positions.reshape(1, T), gain.reshape(1, -1), w_in_b,
            w_in_b[:, MLA_Q_RANK:].T, q_norm.reshape(1, -1), wq_wide, wuk_t, kv_norm.reshape(1, -1),
            kv_norm.reshape(-1, 1), freq_row, sign_row, inv_freq.reshape(half, 1)]
    full = lambda a: pl.BlockSpec(a.shape, lambda i: (0,) * a.ndim)
    in_specs = [pl.BlockSpec((tm, D_MODEL), lambda i: (i, 0)), pl.BlockSpec((tm, 1), lambda i: (i, 0)),
                pl.BlockSpec((1, tm), lambda i: (0, i))] + [full(a) for a in args[3:]]
    return pl.pallas_call(
        _mla_in_kernel,
        grid=(T // tm,),
        in_specs=in_specs,
        out_specs=[pl.BlockSpec((H, tm, MLA_QK), lambda i: (0, i, 0)),
                   pl.BlockSpec((MLA_QK, tm), lambda i: (0, i)),
                   pl.BlockSpec((tm, 2 * LANES), lambda i: (i, 0))],
        out_shape=[jax.ShapeDtypeStruct((H, T, MLA_QK), BF16), jax.ShapeDtypeStruct((MLA_QK, T), BF16),
                   jax.ShapeDtypeStruct((T, 2 * LANES), BF16)],
        compiler_params=_cparams(("arbitrary",)),
        name="mla_in",
    )(*args)


def _flash_kernel(q_ref, qn_ref, kt_ref, va_ref, o_ref, sa_ref, sb_ref, pa_ref, pb_ref, acc_ref, m_ref, alpha_ref,
                  *, tk, nblk):
    H, tq, _ = qn_ref.shape
    T = kt_ref.shape[1]
    n = T // tk
    tiles = q_ref.shape[1] // tq

    def q_rows(u):
        if isinstance(u, int):
            blk = qn_ref[...] if u == tiles else q_ref[:, u * tq:(u + 1) * tq, :]
        else:
            blk = q_ref[:, pl.ds(pl.multiple_of(u * tq, tq), tq), :]
        return blk.reshape(H * tq, MLA_QK)

    def k_tile(j):
        return kt_ref[:, j * tk:(j + 1) * tk]

    def v_tile(j):
        return va_ref[j * tk:(j + 1) * tk, :]

    def lanes_rep(x, width):
        return jnp.concatenate([x] * (width // LANES), axis=1)

    blk = H * tq // nblk
    row_blocks = [slice(rb * blk, (rb + 1) * blk) for rb in range(nblk)]

    def add_pv(rows, p_ref, j):
        acc_ref[rows, :] = (lanes_rep(alpha_ref[rows, :], 2 * LANES) * acc_ref[rows, :]
                            + _dot(p_ref[rows, :], v_tile(j)))

    def step(q, q_next, j, s_cur, s_nxt, p_cur, p_prv):
        q_sel, k_nxt = (q, k_tile(j + 1)) if j + 1 < n else (q_next, k_tile(0))
        for rows in row_blocks:
            if j > 0:
                add_pv(rows, p_prv, j - 1)
            s_nxt[rows, :] = _dot(q_sel[rows], k_nxt)
            s = s_cur[rows, :]
            m_old = m_ref[rows, :]
            m_new = jnp.maximum(m_old, jnp.max(s, axis=1, keepdims=True))
            alpha_ref[rows, :] = jnp.exp2(m_old - m_new)
            m_ref[rows, :] = m_new
            p_cur[rows, :] = jnp.exp2(s - lanes_rep(m_new, tk)).astype(BF16)

    @pl.when(pl.program_id(0) == 0)
    def _():
        sa_ref[...] = _dot(q_rows(0), k_tile(0))

    def token_tile(u, carry):
        q = q_rows(u)
        q_next = jnp.where(u == tiles - 1, q_rows(tiles), q_rows(jnp.minimum(u + 1, tiles - 1)))
        acc_ref[...] = jnp.zeros_like(acc_ref)
        m_ref[...] = jnp.full_like(m_ref, -jnp.inf)
        for j in range(n):
            if j % 2 == 0:
                step(q, q_next, j, sa_ref, sb_ref, pa_ref, pb_ref)
            else:
                step(q, q_next, j, sb_ref, sa_ref, pb_ref, pa_ref)
        p_last = pb_ref if n % 2 == 0 else pa_ref
        for rows in row_blocks:
            add_pv(rows, p_last, n - 1)
        acc = acc_ref[...]
        o_lat = (acc[:, :MLA_KV_RANK] / acc[:, MLA_KV_RANK:MLA_KV_RANK + 1]).astype(BF16)
        out_rows = pl.ds(pl.multiple_of(u * tq, tq), tq)
        for hd in range(H):
            o_ref[out_rows, hd * MLA_KV_RANK:(hd + 1) * MLA_KV_RANK] = o_lat[hd * tq:(hd + 1) * tq]
        return carry

    lax.fori_loop(0, tiles, token_tile, 0)


def _flash(q, kt, va, tq=64, tk=1024, tiles=4):
    H, T, _ = q.shape
    tq, tk = min(tq, T), min(tk, T)
    assert T % (2 * tk) == 0 and tq % BF16_SUBLANES == 0
    assert T % (tiles * tq) == 0
    rows = H * tq
    steps = T // (tiles * tq)
    last_tile = T // tq - 1
    return pl.pallas_call(
        functools.partial(_flash_kernel, tk=tk, nblk=2),
        grid=(steps,),
        scratch_shapes=[pltpu.VMEM((rows, tk), F32), pltpu.VMEM((rows, tk), F32), pltpu.VMEM((rows, tk), BF16),
                        pltpu.VMEM((rows, tk), BF16), pltpu.VMEM((rows, 2 * LANES), F32),
                        pltpu.VMEM((rows, LANES), F32), pltpu.VMEM((rows, LANES), F32)],
        in_specs=[
            pl.BlockSpec((H, tiles * tq, MLA_QK), lambda i: (0, i, 0)),
            pl.BlockSpec((H, tq, MLA_QK), lambda i: (0, jnp.minimum((i + 1) * tiles, last_tile), 0)),
            pl.BlockSpec((MLA_QK, T), lambda i: (0, 0)),
            pl.BlockSpec((T, 2 * LANES), lambda i: (0, 0)),
        ],
        out_specs=pl.BlockSpec((tiles * tq, H * MLA_KV_RANK), lambda i: (i, 0)),
        out_shape=jax.ShapeDtypeStruct((T, H * MLA_KV_RANK), BF16),
        compiler_params=_cparams(("arbitrary",)),
        name="mla_flash",
    )(q, q, kt, va)


def _mla_out_kernel(o_ref, wuv_ref, w_ref, h_ref, out_ref):
    v = [_dot(o_ref[:, hd * MLA_KV_RANK:(hd + 1) * MLA_KV_RANK], wuv_ref[hd]).astype(BF16)
         for hd in range(MLA_HEADS)]
    out_ref[...] = h_ref[...] + _dot(jnp.concatenate(v, axis=1), w_ref[...])


def _mla_out(o_lat, w_ukv, w_out, h, tm=512):
    T = h.shape[0]
    H = MLA_HEADS
    wuv = w_ukv.reshape(MLA_KV_RANK, H, MLA_NOPE + MLA_V)[:, :, MLA_NOPE:].transpose(1, 0, 2).astype(BF16)
    wb = w_out.astype(BF16)
    return pl.pallas_call(
        _mla_out_kernel,
        grid=(T // tm,),
        in_specs=[pl.BlockSpec((tm, H * MLA_KV_RANK), lambda i: (i, 0)), pl.BlockSpec(wuv.shape, lambda i: (0, 0, 0)),
                  pl.BlockSpec(wb.shape, lambda i: (0, 0)), pl.BlockSpec((tm, D_MODEL), lambda i: (i, 0))],
        out_specs=pl.BlockSpec((tm, D_MODEL), lambda i: (i, 0)),
        out_shape=jax.ShapeDtypeStruct((T, D_MODEL), F32),
        compiler_params=_cparams(("arbitrary",)),
        name="mla_out",
    )(o_lat, wuv, wb, h)


def _mla_mixer(h, positions, gain, w_in, q_norm, w_uq, kv_norm, w_ukv, w_out):
    q, kt, va = _mla_in(h, positions, gain, w_in, q_norm, w_uq, kv_norm, w_ukv)
    o_lat = _flash(q, kt, va)
    return _mla_out(o_lat, w_ukv, w_out, h)


def kernel(x, positions, mix_norm, ffn_norm, final_norm, gla_w_in, gla_w_gate_up_f, gla_b_gate_f, gla_w_gate_up_b,
           gla_b_gate_b, gla_head_norm, gla_w_out, mla_w_in, mla_q_norm, mla_w_uq, mla_kv_norm, mla_w_ukv,
           mla_w_out, moe_w_router, moe_w_gate, moe_w_up, moe_w_down):
    B, T, D = x.shape
    outs = []
    for b in range(B):
        h = x[b]
        h = _gla_mixer(h, mix_norm[0], gla_w_in[0], gla_w_gate_up_f[0], gla_b_gate_f[0], gla_w_gate_up_b[0],
                       gla_b_gate_b[0], gla_head_norm[0], gla_w_out[0])
        h = _ec_moe(h, ffn_norm[0], moe_w_router[0], moe_w_gate, moe_w_up, moe_w_down, 0)
        h = _mla_mixer(h, positions[b], mix_norm[1], mla_w_in[0], mla_q_norm[0], mla_w_uq[0], mla_kv_norm[0],
                       mla_w_ukv[0], mla_w_out[0])
        h = _ec_moe(h, ffn_norm[1], moe_w_router[1], moe_w_gate, moe_w_up, moe_w_down, 1, final_gain=final_norm)
        outs.append(h)
    return jnp.stack(outs)
```

```python
import functools
import math

import numpy as np
import jax
import jax.numpy as jnp
from jax import lax
from jax.experimental import pallas as pl
from jax.experimental.pallas import tpu as pltpu

F32 = jnp.float32
BF16 = jnp.bfloat16

D_MODEL = 1024
RMS_EPS = 1e-6

GLA_HEADS = 4
GLA_DK = 512
GLA_DV = 1024
GLA_HEAD_K = GLA_DK // GLA_HEADS
GLA_HEAD_V = GLA_DV // GLA_HEADS
GLA_GATE_RANK = 16
GLA_TAU = 16.0
GLA_CHUNK = 64
GLA_TILE = 256
GLA_LEVELS = 6

MLA_HEADS = 16
MLA_Q_RANK = 256
MLA_KV_RANK = 128
MLA_NOPE = 128
MLA_ROPE = 64
MLA_V = 128
MLA_QK = MLA_NOPE + MLA_ROPE
ROPE_BASE = 10000.0

N_EXPERTS = 16
EXPERT_FF = 2048
EC_CAPACITY_FACTOR = 2
MOE_TILE = 256
COMBINE_HEAD = 64
BF16_SUBLANES = 16
F32_SUBLANES = 8
LANES = 128

VMEM_LIMIT = 56 * 1024 * 1024


def _cparams(sem):
    return pltpu.CompilerParams(dimension_semantics=sem, vmem_limit_bytes=VMEM_LIMIT)


def _rms(x, g):
    return x * lax.rsqrt(jnp.mean(x * x, axis=-1, keepdims=True) + RMS_EPS) * g


def _split_bf16(x):
    hi = x.astype(BF16)
    lo = (x - hi.astype(F32)).astype(BF16)
    return hi, lo


def _dot(a, b):
    return jnp.dot(a, b, preferred_element_type=F32)


def _dot_nt(a, b):
    return lax.dot_general(a, b, (((1,), (1,)), ((), ())), preferred_element_type=F32)


def _dot_tn(a, b):
    return lax.dot_general(a, b, (((0,), (0,)), ((), ())), preferred_element_type=F32)


def _dot_split(a, b):
    ah, al = _split_bf16(a)
    bh, bl = _split_bf16(b)
    return _dot(ah, bh) + _dot(ah, bl) + _dot(al, bh)


def _gla_in_kernel(x_ref, g_ref, wqk_ref, wv_ref, wr_ref, wgd_ref, qk_ref, v_ref, r_ref, gd_ref):
    hn = _rms(x_ref[...], g_ref[...]).astype(BF16)
    qk_ref[...] = _dot(hn, wqk_ref[...])
    v_ref[...] = _dot(hn, wv_ref[...])
    r_ref[...] = _dot(hn, wr_ref[...])
    gd_ref[...] = _dot(hn, wgd_ref[...])


def _gla_in(x, gain, w_in, tm=512):
    T = x.shape[0]
    wqk = w_in[:, :2 * GLA_DK].astype(BF16)
    wv = w_in[:, 2 * GLA_DK:2 * GLA_DK + GLA_DV].astype(BF16)
    wr = w_in[:, 2 * GLA_DK + GLA_DV:2 * GLA_DK + 2 * GLA_DV].astype(BF16)
    wgd = w_in[:, 2 * GLA_DK + 2 * GLA_DV:].astype(BF16)
    ngd = 2 * GLA_GATE_RANK
    row = lambda n: pl.BlockSpec((tm, n), lambda i: (i, 0))
    full = lambda a: pl.BlockSpec(a.shape, lambda i: (0, 0))
    gain2 = gain.reshape(1, D_MODEL)
    return pl.pallas_call(
        _gla_in_kernel,
        grid=(T // tm,),
        in_specs=[row(D_MODEL), full(gain2), full(wqk), full(wv), full(wr), full(wgd)],
        out_specs=[row(2 * GLA_DK), row(GLA_DV), row(GLA_DV), row(ngd)],
        out_shape=[jax.ShapeDtypeStruct((T, 2 * GLA_DK), F32), jax.ShapeDtypeStruct((T, GLA_DV), F32),
                   jax.ShapeDtypeStruct((T, GLA_DV), F32), jax.ShapeDtypeStruct((T, ngd), F32)],
        compiler_params=_cparams(("arbitrary",)),
        name="gla_in",
    )(x, gain2, wqk, wv, wr, wgd)


def _gla_tables(reverse):
    n, c = GLA_TILE, GLA_CHUNK
    W = np.zeros((GLA_LEVELS + 3, n, n), np.float32)
    L = np.full((n, n), -1, np.int32)
    for t in range(n):
        c0 = (t // c) * c
        tt = t - c0
        for l in range(GLA_LEVELS):
            b = (c // 2) >> l
            p0 = c0 + (tt // (2 * b)) * 2 * b
            mid = p0 + b
            second = t >= mid
            if not reverse:
                if second:
                    W[l, t, mid:t + 1] = 1
                    L[t, p0:mid] = l
                else:
                    W[l, t, t + 1:mid] = 1
            else:
                if second:
                    W[l, t, mid:t] = 1
                else:
                    W[l, t, t:mid] = 1
                    L[t, mid:p0 + 2 * b] = l
        if not reverse:
            W[GLA_LEVELS, t, c0:t + 1] = 1
            W[GLA_LEVELS + 1, t, t + 1:c0 + c] = 1
            L[t, t] = GLA_LEVELS
        else:
            W[GLA_LEVELS, t, t:c0 + c] = 1
            W[GLA_LEVELS + 1, t, c0:t] = 1
        W[GLA_LEVELS + 2, t, c0:c0 + c] = 1
    return W.reshape(-1, n), L


def _gla_scan_kernel(qkf_ref, vf_ref, gdf_ref, qkb_ref, vb_ref, gdb_ref, wupf_ref, bf_ref, wupb_ref, bb_ref,
                     wf_ref, lf_ref, wb_ref, lb_ref, of_ref, ob_ref, s_ref):
    n, c, r = GLA_TILE, GLA_CHUNK, GLA_GATE_RANK

    @pl.when(pl.program_id(0) == 0)
    def _():
        s_ref[...] = jnp.zeros_like(s_ref)

    dirs = [(qkf_ref, vf_ref, gdf_ref[:, :r], wupf_ref, bf_ref, wf_ref, lf_ref, of_ref, False),
            (qkb_ref, vb_ref, gdb_ref[:, r:], wupb_ref, bb_ref, wb_ref, lb_ref, ob_ref, True)]

    factors = []
    for qk_ref, v_ref, gd, wup_ref, b_ref, w_ref, l_ref, o_ref, reverse in dirs:
        z = _dot_split(gd, wup_ref[...]) + b_ref[...]
        g = (jnp.minimum(z, 0.0) - jnp.log1p(jnp.exp(-jnp.abs(z)))) * (1.0 / GLA_TAU)
        ghi, glo = _split_bf16(g)
        w = w_ref[...]
        factors.append(jnp.exp(_dot(w, ghi) + _dot(w, glo)))

    chains = []
    for d, (qk_ref, v_ref, gd, wup_ref, b_ref, w_ref, l_ref, o_ref, reverse) in enumerate(dirs):
        lvl = l_ref[...]
        for h in range(GLA_HEADS):
            kcols = slice(h * GLA_HEAD_K, (h + 1) * GLA_HEAD_K)
            vcols = slice(h * GLA_HEAD_V, (h + 1) * GLA_HEAD_V)
            f = factors[d][:, kcols]
            q = qk_ref[:, kcols] * (GLA_HEAD_K ** -0.5)
            k = qk_ref[:, GLA_DK + h * GLA_HEAD_K:GLA_DK + (h + 1) * GLA_HEAD_K]
            vb = v_ref[:, vcols].astype(BF16)
            attn = jnp.zeros((n, n), F32)
            for l in range(GLA_LEVELS):
                fl = f[l * n:(l + 1) * n]
                p = _dot_nt((q * fl).astype(BF16), (k * fl).astype(BF16))
                attn = jnp.where(lvl == l, p, attn)
            if not reverse:
                p = _dot_nt(q.astype(BF16), k.astype(BF16))
                attn = jnp.where(lvl == GLA_LEVELS, p, attn)
            o_intra = _dot(attn.astype(BF16), vb)
            qh = (q * f[GLA_LEVELS * n:(GLA_LEVELS + 1) * n]).astype(BF16)
            kh = (k * f[(GLA_LEVELS + 1) * n:(GLA_LEVELS + 2) * n]).astype(BF16)
            ftot = f[(GLA_LEVELS + 2) * n:(GLA_LEVELS + 3) * n]
            chains.append(dict(d=d, h=h, vcols=vcols, o_ref=o_ref, reverse=reverse, vb=vb, o_intra=o_intra, qh=qh,
                               kh=kh, ftot=ftot, st=s_ref[d, h]))

    nchunks = n // c
    for step in range(nchunks):
        for ch in chains:
            j = nchunks - 1 - step if ch["reverse"] else step
            rows = slice(j * c, (j + 1) * c)
            st = ch["st"]
            ch["o_ref"][rows, ch["vcols"]] = ch["o_intra"][rows] + _dot_nt(ch["qh"][rows], st.astype(BF16))
            ch["st"] = st * ch["ftot"][j * c:j * c + 1, :] + _dot_tn(ch["vb"][rows], ch["kh"][rows])
    for ch in chains:
        s_ref[ch["d"], ch["h"]] = ch["st"]


def _gla_scan(qk, v, gd, w_up_f, b_f, w_up_b, b_b):
    T = qk.shape[0]
    n = GLA_TILE
    nt = T // n
    tables = []
    for reverse in (False, True):
        W, L = _gla_tables(reverse)
        tables += [jnp.asarray(W, BF16), jnp.asarray(L)]
    fwd = lambda width: pl.BlockSpec((n, width), lambda i: (i, 0))
    bwd = lambda width: pl.BlockSpec((n, width), lambda i: (nt - 1 - i, 0))
    full = lambda a: pl.BlockSpec(a.shape, lambda i: (0, 0))
    consts = [w_up_f, b_f.reshape(1, GLA_DK), w_up_b, b_b.reshape(1, GLA_DK)] + tables
    ngd = 2 * GLA_GATE_RANK
    out = jax.ShapeDtypeStruct((T, GLA_DV), F32)
    return pl.pallas_call(
        _gla_scan_kernel,
        grid=(nt,),
        in_specs=[fwd(2 * GLA_DK), fwd(GLA_DV), fwd(ngd), bwd(2 * GLA_DK), bwd(GLA_DV), bwd(ngd)]
                 + [full(a) for a in consts],
        out_specs=[fwd(GLA_DV), bwd(GLA_DV)],
        out_shape=[out, out],
        scratch_shapes=[pltpu.VMEM((2, GLA_HEADS, GLA_HEAD_V, GLA_HEAD_K), F32)],
        compiler_params=_cparams(("arbitrary",)),
        name="gla_scan",
    )(qk, v, gd, qk, v, gd, *consts)


def _gla_out_kernel(of_ref, ob_ref, r_ref, x_ref, hn_ref, w_ref, o_ref):
    acc = x_ref[...]
    for h in range(GLA_HEADS):
        cols = slice(h * GLA_HEAD_V, (h + 1) * GLA_HEAD_V)
        o = _rms(of_ref[:, cols] + ob_ref[:, cols], hn_ref[...])
        r = r_ref[:, cols]
        gated = o * (r * (1.0 / (1.0 + jnp.exp(-r))))
        acc = acc + _dot(gated.astype(BF16), w_ref[cols, :])
    o_ref[...] = acc


def _gla_out(of, ob, r, x, head_norm, w_out, tm=512):
    T = x.shape[0]
    w = w_out.astype(BF16)
    hn = head_norm.reshape(1, GLA_HEAD_V)
    row = lambda n: pl.BlockSpec((tm, n), lambda i: (i, 0))
    full = lambda a: pl.BlockSpec(a.shape, lambda i: (0, 0))
    return pl.pallas_call(
        _gla_out_kernel,
        grid=(T // tm,),
        in_specs=[row(GLA_DV), row(GLA_DV), row(GLA_DV), row(D_MODEL), full(hn), full(w)],
        out_specs=row(D_MODEL),
        out_shape=jax.ShapeDtypeStruct((T, D_MODEL), F32),
        compiler_params=_cparams(("arbitrary",)),
        name="gla_out",
    )(of, ob, r, x, hn, w)


def _gla_mixer(x, gain, w_in, w_up_f, b_f, w_up_b, b_b, head_norm, w_out):
    qk, v, r, gd = _gla_in(x, gain, w_in)
    of, ob = _gla_scan(qk, v, gd, w_up_f, b_f, w_up_b, b_b)
    return _gla_out(of, ob, r, x, head_norm, w_out)


def _router_kernel(h_ref, g_ref, w_ref, aff_ref, hn_ref):
    hn = _rms(h_ref[...], g_ref[...])
    hn_ref[...] = hn.astype(BF16)
    logits = _dot_split(hn, w_ref[...])
    e = jnp.exp(logits - jnp.max(logits, axis=-1, keepdims=True))
    aff_ref[...] = e / jnp.sum(e, axis=-1, keepdims=True)


def _router(h, gain, w_router, tm=512):
    T = h.shape[0]
    gain2 = gain.reshape(1, D_MODEL)
    row = lambda n: pl.BlockSpec((tm, n), lambda i: (i, 0))
    full = lambda a: pl.BlockSpec(a.shape, lambda i: (0, 0))
    return pl.pallas_call(
        _router_kernel,
        grid=(T // tm,),
        in_specs=[row(D_MODEL), full(gain2), full(w_router)],
        out_specs=[row(N_EXPERTS), row(D_MODEL)],
        out_shape=[jax.ShapeDtypeStruct((T, N_EXPERTS), F32), jax.ShapeDtypeStruct((T, D_MODEL), BF16)],
        compiler_params=_cparams(("arbitrary",)),
        name="moe_router",
    )(h, gain2, w_router)


def _select_kernel(aff_ref, tri_ref, posm_ref, before_ref, *, cap):
    T = aff_ref.shape[1]
    bits = pltpu.bitcast(aff_ref[...], jnp.int32)

    def search(it, thr):
        cand = thr | jnp.left_shift(jnp.int32(1), 30 - it)
        cnt = jnp.sum(jnp.where(bits >= cand, 1.0, 0.0), axis=1, keepdims=True)
        return jnp.where(cnt >= cap, cand, thr)

    thr = lax.fori_loop(0, 31, search, jnp.zeros((N_EXPERTS, 1), jnp.int32))
    n_gt = jnp.sum(jnp.where(bits > thr, 1.0, 0.0), axis=1, keepdims=True)
    need = cap - n_gt
    tri = tri_ref[...]

    def scan(j, carry):
        c_eq, c_sel = carry
        cols = pl.ds(pl.multiple_of(j * LANES, LANES), LANES)
        blk = pltpu.bitcast(aff_ref[:, cols], jnp.int32)
        eq = jnp.where(blk == thr, 1.0, 0.0)
        rank = _dot(eq.astype(BF16), tri) + c_eq - eq
        sel = jnp.where((blk > thr) | ((eq > 0.0) & (rank < need)), 1.0, 0.0)
        before = _dot(sel.astype(BF16), tri) + c_sel - sel
        before_ref[:, cols] = before.astype(jnp.int32)
        posm_ref[:, cols] = jnp.where(sel > 0.0, before, -1.0).astype(jnp.int32)
        return (c_eq + jnp.sum(eq, axis=1, keepdims=True), c_sel + jnp.sum(sel, axis=1, keepdims=True))

    zero = jnp.zeros((N_EXPERTS, 1), F32)
    lax.fori_loop(0, T // LANES, scan, (zero, zero))


def _select(aff_t, cap):
    T = aff_t.shape[1]
    tri = jnp.asarray(np.triu(np.ones((LANES, LANES), np.float32)), BF16)
    full = lambda a: pl.BlockSpec(a.shape, lambda: (0,) * a.ndim)
    out = jax.ShapeDtypeStruct((N_EXPERTS, T), jnp.int32)
    return pl.pallas_call(
        functools.partial(_select_kernel, cap=cap),
        in_specs=[full(aff_t), full(tri)],
        out_specs=[pl.BlockSpec((N_EXPERTS, T), lambda: (0, 0))] * 2,
        out_shape=[out, out],
        compiler_params=pltpu.CompilerParams(vmem_limit_bytes=VMEM_LIMIT),
        name="moe_select",
    )(aff_t, tri)


def _dispatch_window():
    return MOE_TILE + F32_SUBLANES


def _dispatch_kernel(start_ref, hn_ref, posm_ref, x_ref, acc_ref, *, cap, nt, sub):
    e, t = pl.program_id(0), pl.program_id(1)
    win = _dispatch_window()

    @pl.when(t == 0)
    def _():
        acc_ref[...] = jnp.zeros_like(acc_ref)

    for s in range(sub):
        tok = slice(s * MOE_TILE, (s + 1) * MOE_TILE)
        base = pl.multiple_of((start_ref[e * nt + t * sub + s] // F32_SUBLANES) * F32_SUBLANES, F32_SUBLANES)
        slot = base + lax.broadcasted_iota(jnp.int32, (win, MOE_TILE), 0)
        onehot = jnp.where(posm_ref[0, :, tok] == slot, 1.0, 0.0).astype(BF16)
        acc_ref[pl.ds(base, win), :] += _dot(onehot, hn_ref[tok, :])

    @pl.when(t == pl.num_programs(1) - 1)
    def _():
        x_ref[0] = acc_ref[:cap, :].astype(BF16)


def _dispatch(hn, posm, start, cap, sub=16):
    T = hn.shape[0]
    nt = T // MOE_TILE
    sub = math.gcd(sub, nt)
    blk = sub * MOE_TILE
    posm3 = posm.reshape(N_EXPERTS, 1, T)
    grid_spec = pltpu.PrefetchScalarGridSpec(
        num_scalar_prefetch=1,
        grid=(N_EXPERTS, nt // sub),
        in_specs=[
            pl.BlockSpec((blk, D_MODEL), lambda e, t, s: (t, 0)),
            pl.BlockSpec((1, 1, blk), lambda e, t, s: (e, 0, t)),
        ],
        out_specs=pl.BlockSpec((1, cap, D_MODEL), lambda e, t, s: (e, 0, 0)),
        scratch_shapes=[pltpu.VMEM((cap + _dispatch_window(), D_MODEL), F32)],
    )
    return pl.pallas_call(
        functools.partial(_dispatch_kernel, cap=cap, nt=nt, sub=sub),
        grid_spec=grid_spec,
        out_shape=jax.ShapeDtypeStruct((N_EXPERTS, cap, D_MODEL), BF16),
        compiler_params=_cparams(("arbitrary", "arbitrary")),
        name="moe_dispatch",
    )(start.reshape(-1), hn, posm3)


def _ffn_kernel(x_ref, wg_ref, wu_ref, wd_ref, y_ref, acc_ref):
    f = pl.program_id(1)

    @pl.when(f == 0)
    def _():
        acc_ref[...] = jnp.zeros_like(acc_ref)

    wg = wg_ref[0, 0].astype(BF16)
    wu = wu_ref[0, 0].astype(BF16)
    wd = wd_ref[0, 0].astype(BF16)
    cap = x_ref.shape[1]
    blk = cap // math.gcd(cap // F32_SUBLANES, 4)
    for r0 in range(0, cap, blk):
        rows = slice(r0, r0 + blk)
        x = x_ref[0, rows, :]
        a = _dot(x, wg)
        u = _dot(x, wu)
        mid = (a * (1.0 / (1.0 + jnp.exp(-a))) * u).astype(BF16)
        acc_ref[rows, :] += _dot(mid, wd)

    @pl.when(f == pl.num_programs(1) - 1)
    def _():
        y_ref[0] = acc_ref[...].astype(BF16)


def _ffn(x, w_gate, w_up, w_down, layer, tf=512):
    cap = x.shape[1]
    return pl.pallas_call(
        _ffn_kernel,
        grid=(N_EXPERTS, EXPERT_FF // tf),
        in_specs=[
            pl.BlockSpec((1, cap, D_MODEL), lambda e, f: (e, 0, 0)),
            pl.BlockSpec((1, 1, D_MODEL, tf), lambda e, f: (layer, e, 0, f)),
            pl.BlockSpec((1, 1, D_MODEL, tf), lambda e, f: (layer, e, 0, f)),
            pl.BlockSpec((1, 1, tf, D_MODEL), lambda e, f: (layer, e, f, 0)),
        ],
        out_specs=pl.BlockSpec((1, cap, D_MODEL), lambda e, f: (e, 0, 0)),
        out_shape=jax.ShapeDtypeStruct((N_EXPERTS, cap, D_MODEL), BF16),
        scratch_shapes=[pltpu.VMEM((cap, D_MODEL), F32)],
        compiler_params=_cparams(("arbitrary", "arbitrary")),
        name="moe_ffn",
    )(x, w_gate, w_up, w_down)


def _combine_window():
    return MOE_TILE + BF16_SUBLANES


def _combine_kernel(start_ref, h_ref, aff_ref, posm_ref, g_ref, y_hbm, o_ref, hbuf_ref, buf_ref, hsem_ref, rsem_ref,
                    acc_ref, *, cap, nt, final_norm):
    t = pl.program_id(0)
    win = _combine_window()
    head = COMBINE_HEAD

    def base_of(tile, e):
        b = (start_ref[e * (nt + 1) + tile] // BF16_SUBLANES) * BF16_SUBLANES
        return pl.multiple_of(jnp.minimum(b, cap - win), BF16_SUBLANES)

    def needs_rest(tile, e):
        return start_ref[e * (nt + 1) + tile + 1] > base_of(tile, e) + head

    def head_copy(tile, e, slot):
        return pltpu.make_async_copy(y_hbm.at[e, pl.ds(base_of(tile, e), head), :],
                                     hbuf_ref.at[slot, pl.ds(e * head, head), :], hsem_ref.at[slot])

    def rest_copy(tile, e, slot):
        return pltpu.make_async_copy(y_hbm.at[e, pl.ds(base_of(tile, e) + head, win - head), :],
                                     buf_ref.at[slot, e], rsem_ref.at[slot, e])

    def fetch(tile, slot):
        for e in range(N_EXPERTS):
            head_copy(tile, e, slot).start()

            @pl.when(needs_rest(tile, e))
            def _():
                rest_copy(tile, e, slot).start()

    slot = t % 2

    @pl.when(t == 0)
    def _():
        fetch(0, 0)

    @pl.when(t + 1 < nt)
    def _():
        fetch(t + 1, 1 - slot)

    def weights_t(e, lo, hi):
        row = base_of(t, e) + lo + lax.broadcasted_iota(jnp.int32, (hi - lo, MOE_TILE), 0)
        return jnp.where(posm_ref[e:e + 1, :] == row, aff_ref[e:e + 1, :], 0.0)

    def expand(w_t, y_rows):
        w_hi, w_lo = _split_bf16(w_t.T)
        return _dot(w_hi, y_rows) + _dot(w_lo, y_rows)

    for e in range(N_EXPERTS):
        head_copy(t, e, slot).wait()

    w_heads = jnp.concatenate([weights_t(e, 0, head) for e in range(N_EXPERTS)], axis=0)
    acc_ref[...] = h_ref[...] + expand(w_heads, hbuf_ref[slot])

    any_rest = needs_rest(t, 0)
    for e in range(1, N_EXPERTS):
        any_rest = any_rest | needs_rest(t, e)

    @pl.when(any_rest)
    def _():
        for e in range(N_EXPERTS):
            @pl.when(needs_rest(t, e))
            def _():
                rest_copy(t, e, slot).wait()
                for lo, hi in ((head, MOE_TILE), (MOE_TILE, win)):
                    acc_ref[...] += expand(weights_t(e, lo, hi), buf_ref[slot, e, lo - head:hi - head, :])

    acc = acc_ref[...]
    if final_norm:
        acc = _rms(acc, g_ref[...])
    o_ref[...] = acc


def _combine(h, aff_t, posm, start, y, cap, final_gain):
    T = h.shape[0]
    nt = T // MOE_TILE
    win = _combine_window()
    assert cap >= win and (cap - win) % BF16_SUBLANES == 0
    final_norm = final_gain is not None
    gain = (final_gain if final_norm else jnp.ones((D_MODEL,), F32)).reshape(1, D_MODEL)
    grid_spec = pltpu.PrefetchScalarGridSpec(
        num_scalar_prefetch=1,
        grid=(nt,),
        in_specs=[
            pl.BlockSpec((MOE_TILE, D_MODEL), lambda t, s: (t, 0)),
            pl.BlockSpec((N_EXPERTS, MOE_TILE), lambda t, s: (0, t)),
            pl.BlockSpec((N_EXPERTS, MOE_TILE), lambda t, s: (0, t)),
            pl.BlockSpec((1, D_MODEL), lambda t, s: (0, 0)),
            pl.BlockSpec(memory_space=pl.ANY),
        ],
        out_specs=pl.BlockSpec((MOE_TILE, D_MODEL), lambda t, s: (t, 0)),
        scratch_shapes=[pltpu.VMEM((2, N_EXPERTS * COMBINE_HEAD, D_MODEL), BF16),
                        pltpu.VMEM((2, N_EXPERTS, win - COMBINE_HEAD, D_MODEL), BF16),
                        pltpu.SemaphoreType.DMA((2,)), pltpu.SemaphoreType.DMA((2, N_EXPERTS)),
                        pltpu.VMEM((MOE_TILE, D_MODEL), F32)],
    )
    return pl.pallas_call(
        functools.partial(_combine_kernel, cap=cap, nt=nt, final_norm=final_norm),
        grid_spec=grid_spec,
        out_shape=jax.ShapeDtypeStruct((T, D_MODEL), F32),
        compiler_params=_cparams(("arbitrary",)),
        name="moe_combine",
    )(start.reshape(-1), h, aff_t, posm, gain, y)


def _ec_moe(h, gain, w_router, w_gate, w_up, w_down, layer, final_gain=None):
    T = h.shape[0]
    cap = max(1, EC_CAPACITY_FACTOR * T // N_EXPERTS)
    aff, hn = _router(h, gain, w_router)
    aff_t = aff.T
    posm, before = _select(aff_t, cap)
    start = before[:, ::MOE_TILE]
    x = _dispatch(hn, posm, start, cap)
    y = _ffn(x, w_gate, w_up, w_down, layer)
    start_end = jnp.concatenate([start, jnp.full((N_EXPERTS, 1), cap, jnp.int32)], axis=1)
    return _combine(h, aff_t, posm, start_end, y, cap, final_gain)


def _rms_cols(x, g):
    return x * lax.rsqrt(jnp.mean(x * x, axis=0, keepdims=True) + RMS_EPS) * g


MLA_QCOLS = MLA_NOPE + 2 * LANES


def _mla_in_kernel(h_ref, posc_ref, posr_ref, g_ref, win_ref, wintkv_ref, qn_ref, wq_ref, wukt_ref, kvn_ref,
                   kvnc_ref, freqr_ref, sign_ref, freqc_ref, q_out, kt_out, va_out):
    half = MLA_ROPE // 2
    hn = _rms(h_ref[...], g_ref[...]).astype(BF16)
    c = _dot(hn, win_ref[...])
    ckv_t = _dot_nt(wintkv_ref[...], hn)

    cq = _rms(c[:, :MLA_Q_RANK], qn_ref[...]).astype(BF16)
    qa = _dot(cq, wq_ref[...])
    ang = posc_ref[...].astype(F32) * freqr_ref[...]
    cos, sin_signed = jnp.cos(ang), jnp.sin(ang) * sign_ref[...]
    qscale = MLA_QK ** -0.5 * math.log2(math.e)
    for hd in range(MLA_HEADS):
        c0 = hd * MLA_QCOLS
        q_lat = _dot(qa[:, c0:c0 + MLA_NOPE].astype(BF16), wukt_ref[hd])
        rot = qa[:, c0 + MLA_NOPE:c0 + MLA_NOPE + LANES] * cos + qa[:, c0 + MLA_NOPE + LANES:c0 + MLA_QCOLS] * sin_signed
        q_out[hd, :, :MLA_NOPE] = (q_lat * qscale).astype(BF16)
        q_out[hd, :, MLA_NOPE:] = (rot[:, :MLA_ROPE] * qscale).astype(BF16)

    kt_out[:MLA_KV_RANK, :] = _rms_cols(ckv_t[:MLA_KV_RANK], kvnc_ref[...]).astype(BF16)
    ang_t = freqc_ref[...] * posr_ref[...].astype(F32)
    cos_t, sin_t = jnp.cos(ang_t), jnp.sin(ang_t)
    k1, k2 = ckv_t[MLA_KV_RANK:MLA_KV_RANK + half], ckv_t[MLA_KV_RANK + half:]
    kt_out[MLA_KV_RANK:MLA_KV_RANK + half, :] = (k1 * cos_t - k2 * sin_t).astype(BF16)
    kt_out[MLA_KV_RANK + half:, :] = (k1 * sin_t + k2 * cos_t).astype(BF16)

    ckv = _rms(c[:, MLA_Q_RANK:MLA_Q_RANK + MLA_KV_RANK], kvn_ref[...])
    va_out[:, :MLA_KV_RANK] = ckv.astype(BF16)
    lane = lax.broadcasted_iota(jnp.int32, (ckv.shape[0], LANES), 1)
    va_out[:, MLA_KV_RANK:] = jnp.where(lane == 0, 1.0, 0.0).astype(BF16)


def _mla_in(h, positions, gain, w_in, q_norm, w_uq, kv_norm, w_ukv, tm=256):
    T = h.shape[0]
    H, half = MLA_HEADS, MLA_ROPE // 2
    wq = w_uq.reshape(MLA_Q_RANK, H, MLA_QK)
    x1, x2 = wq[:, :, MLA_NOPE:MLA_NOPE + half], wq[:, :, MLA_NOPE + half:]
    pad = jnp.zeros((MLA_Q_RANK, H, LANES - MLA_ROPE), F32)
    wq_wide = jnp.concatenate([wq[:, :, :MLA_NOPE], x1, x2, pad, x2, x1, pad], axis=2)
    wq_wide = wq_wide.reshape(MLA_Q_RANK, H * MLA_QCOLS).astype(BF16)
    wuk_t = w_ukv.reshape(MLA_KV_RANK, H, MLA_NOPE + MLA_V)[:, :, :MLA_NOPE].transpose(1, 2, 0).astype(BF16)
    inv_freq = ROPE_BASE ** (-jnp.arange(half, dtype=F32) / half)
    zeros = jnp.zeros((LANES - MLA_ROPE,), F32)
    freq_row = jnp.concatenate([inv_freq, inv_freq, zeros]).reshape(1, LANES)
    sign_row = jnp.concatenate([-jnp.ones((half,), F32), jnp.ones((half,), F32), zeros]).reshape(1, LANES)
    w_in_b = w_in.astype(BF16)
    args = [h, positions.reshape(T, 1), positions.reshape(1, T), gain.reshape(1, -1), w_in_b,
            w_in_b[:, MLA_Q_RANK:].T, q_norm.reshape(1, -1), wq_wide, wuk_t, kv_norm.reshape(1, -1),
            kv_norm.reshape(-1, 1), freq_row, sign_row, inv_freq.reshape(half, 1)]
    full = lambda a: pl.BlockSpec(a.shape, lambda i: (0,) * a.ndim)
    in_specs = [pl.BlockSpec((tm, D_MODEL), lambda i: (i, 0)), pl.BlockSpec((tm, 1), lambda i: (i, 0)),
                pl.BlockSpec((1, tm), lambda i: (0, i))] + [full(a) for a in args[3:]]
    return pl.pallas_call(
        _mla_in_kernel,
        grid=(T // tm,),
        in_specs=in_specs,
        out_specs=[pl.BlockSpec((H, tm, MLA_QK), lambda i: (0, i, 0)),
                   pl.BlockSpec((MLA_QK, tm), lambda i: (0, i)),
                   pl.BlockSpec((tm, 2 * LANES), lambda i: (i, 0))],
        out_shape=[jax.ShapeDtypeStruct((H, T, MLA_QK), BF16), jax.ShapeDtypeStruct((MLA_QK, T), BF16),
                   jax.ShapeDtypeStruct((T, 2 * LANES), BF16)],
        compiler_params=_cparams(("arbitrary",)),
        name="mla_in",
    )(*args)


def _flash_kernel(q_ref, qn_ref, kt_ref, va_ref, o_ref, sa_ref, sb_ref, pa_ref, pb_ref, acc_ref, m_ref, alpha_ref,
                  *, tk, nblk):
    H, tq, _ = qn_ref.shape
    T = kt_ref.shape[1]
    n = T // tk
    tiles = q_ref.shape[1] // tq

    def q_rows(u):
        if isinstance(u, int):
            blk = qn_ref[...] if u == tiles else q_ref[:, u * tq:(u + 1) * tq, :]
        else:
            blk = q_ref[:, pl.ds(pl.multiple_of(u * tq, tq), tq), :]
        return blk.reshape(H * tq, MLA_QK)

    def k_tile(j):
        return kt_ref[:, j * tk:(j + 1) * tk]

    def v_tile(j):
        return va_ref[j * tk:(j + 1) * tk, :]

    def lanes_rep(x, width):
        return jnp.concatenate([x] * (width // LANES), axis=1)

    blk = H * tq // nblk
    row_blocks = [slice(rb * blk, (rb + 1) * blk) for rb in range(nblk)]

    def add_pv(rows, p_ref, j):
        acc_ref[rows, :] = (lanes_rep(alpha_ref[rows, :], 2 * LANES) * acc_ref[rows, :]
                            + _dot(p_ref[rows, :], v_tile(j)))

    def step(q, q_next, j, s_cur, s_nxt, p_cur, p_prv):
        q_sel, k_nxt = (q, k_tile(j + 1)) if j + 1 < n else (q_next, k_tile(0))
        for rows in row_blocks:
            s_nxt[rows, :] = _dot(q_sel[rows], k_nxt)
            if j > 0:
                add_pv(rows, p_prv, j - 1)
            s = s_cur[rows, :]
            m_old = m_ref[rows, :]
            m_new = jnp.maximum(m_old, jnp.max(s, axis=1, keepdims=True))
            alpha_ref[rows, :] = jnp.exp2(m_old - m_new)
            m_ref[rows, :] = m_new
            p_cur[rows, :] = jnp.exp2(s - lanes_rep(m_new, tk)).astype(BF16)

    @pl.when(pl.program_id(0) == 0)
    def _():
        sa_ref[...] = _dot(q_rows(0), k_tile(0))

    def token_tile(u, carry):
        q = q_rows(u)
        q_next = jnp.where(u == tiles - 1, q_rows(tiles), q_rows(jnp.minimum(u + 1, tiles - 1)))
        acc_ref[...] = jnp.zeros_like(acc_ref)
        m_ref[...] = jnp.full_like(m_ref, -jnp.inf)
        for j in range(n):
            if j % 2 == 0:
                step(q, q_next, j, sa_ref, sb_ref, pa_ref, pb_ref)
            else:
                step(q, q_next, j, sb_ref, sa_ref, pb_ref, pa_ref)
        p_last = pb_ref if n % 2 == 0 else pa_ref
        for rows in row_blocks:
            add_pv(rows, p_last, n - 1)
        acc = acc_ref[...]
        o_lat = (acc[:, :MLA_KV_RANK] / acc[:, MLA_KV_RANK:MLA_KV_RANK + 1]).astype(BF16)
        out_rows = pl.ds(pl.multiple_of(u * tq, tq), tq)
        for hd in range(H):
            o_ref[out_rows, hd * MLA_KV_RANK:(hd + 1) * MLA_KV_RANK] = o_lat[hd * tq:(hd + 1) * tq]
        return carry

    lax.fori_loop(0, tiles, token_tile, 0)


def _flash(q, kt, va, tq=64, tk=1024, tiles=4):
    H, T, _ = q.shape
    tq, tk = min(tq, T), min(tk, T)
    assert T % (2 * tk) == 0 and tq % BF16_SUBLANES == 0
    assert T % (tiles * tq) == 0
    rows = H * tq
    steps = T // (tiles * tq)
    last_tile = T // tq - 1
    return pl.pallas_call(
        functools.partial(_flash_kernel, tk=tk, nblk=4),
        grid=(steps,),
        scratch_shapes=[pltpu.VMEM((rows, tk), F32), pltpu.VMEM((rows, tk), F32), pltpu.VMEM((rows, tk), BF16),
                        pltpu.VMEM((rows, tk), BF16), pltpu.VMEM((rows, 2 * LANES), F32),
                        pltpu.VMEM((rows, LANES), F32), pltpu.VMEM((rows, LANES), F32)],
        in_specs=[
            pl.BlockSpec((H, tiles * tq, MLA_QK), lambda i: (0, i, 0)),
            pl.BlockSpec((H, tq, MLA_QK), lambda i: (0, jnp.minimum((i + 1) * tiles, last_tile), 0)),
            pl.BlockSpec((MLA_QK, T), lambda i: (0, 0)),
            pl.BlockSpec((T, 2 * LANES), lambda i: (0, 0)),
        ],
        out_specs=pl.BlockSpec((tiles * tq, H * MLA_KV_RANK), lambda i: (i, 0)),
        out_shape=jax.ShapeDtypeStruct((T, H * MLA_KV_RANK), BF16),
        compiler_params=_cparams(("arbitrary",)),
        name="mla_flash",
    )(q, q, kt, va)


def _mla_out_kernel(o_ref, wuv_ref, w_ref, h_ref, out_ref):
    v = [_dot(o_ref[:, hd * MLA_KV_RANK:(hd + 1) * MLA_KV_RANK], wuv_ref[hd]).astype(BF16)
         for hd in range(MLA_HEADS)]
    out_ref[...] = h_ref[...] + _dot(jnp.concatenate(v, axis=1), w_ref[...])


def _mla_out(o_lat, w_ukv, w_out, h, tm=512):
    T = h.shape[0]
    H = MLA_HEADS
    wuv = w_ukv.reshape(MLA_KV_RANK, H, MLA_NOPE + MLA_V)[:, :, MLA_NOPE:].transpose(1, 0, 2).astype(BF16)
    wb = w_out.astype(BF16)
    return pl.pallas_call(
        _mla_out_kernel,
        grid=(T // tm,),
        in_specs=[pl.BlockSpec((tm, H * MLA_KV_RANK), lambda i: (i, 0)), pl.BlockSpec(wuv.shape, lambda i: (0, 0, 0)),
                  pl.BlockSpec(wb.shape, lambda i: (0, 0)), pl.BlockSpec((tm, D_MODEL), lambda i: (i, 0))],
        out_specs=pl.BlockSpec((tm, D_MODEL), lambda i: (i, 0)),
        out_shape=jax.ShapeDtypeStruct((T, D_MODEL), F32),
        compiler_params=_cparams(("arbitrary",)),
        name="mla_out",
    )(o_lat, wuv, wb, h)


def _mla_mixer(h, positions, gain, w_in, q_norm, w_uq, kv_norm, w_ukv, w_out):
    q, kt, va = _mla_in(h, positions, gain, w_in, q_norm, w_uq, kv_norm, w_ukv)
    o_lat = _flash(q, kt, va)
    return _mla_out(o_lat, w_ukv, w_out, h)


def kernel(x, positions, mix_norm, ffn_norm, final_norm, gla_w_in, gla_w_gate_up_f, gla_b_gate_f, gla_w_gate_up_b,
           gla_b_gate_b, gla_head_norm, gla_w_out, mla_w_in, mla_q_norm, mla_w_uq, mla_kv_norm, mla_w_ukv,
           mla_w_out, moe_w_router, moe_w_gate, moe_w_up, moe_w_down):
    B, T, D = x.shape
    outs = []
    for b in range(B):
        h = x[b]
        h = _gla_mixer(h, mix_norm[0], gla_w_in[0], gla_w_gate_up_f[0], gla_b_gate_f[0], gla_w_gate_up_b[0],
                       gla_b_gate_b[0], gla_head_norm[0], gla_w_out[0])
        h = _ec_moe(h, ffn_norm[0], moe_w_router[0], moe_w_gate, moe_w_up, moe_w_down, 0)
        h = _mla_mixer(h, positions[b], mix_norm[1], mla_w_in[0], mla_q_norm[0], mla_w_uq[0], mla_kv_norm[0],
                       mla_w_ukv[0], mla_w_out[0])
        h = _ec_moe(h, ffn_norm[1], moe_w_router[1], moe_w_gate, moe_w_up, moe_w_down, 1, final_gain=final_norm)
        outs.append(h)
    return jnp.stack(outs)
```

```python
import functools
import math

import numpy as np
import jax
import jax.numpy as jnp
from jax import lax
from jax.experimental import pallas as pl
from jax.experimental.pallas import tpu as pltpu

F32 = jnp.float32
BF16 = jnp.bfloat16

D_MODEL = 1024
RMS_EPS = 1e-6

GLA_HEADS = 4
GLA_DK = 512
GLA_DV = 1024
GLA_HEAD_K = GLA_DK // GLA_HEADS
GLA_HEAD_V = GLA_DV // GLA_HEADS
GLA_GATE_RANK = 16
GLA_TAU = 16.0
GLA_CHUNK = 64
GLA_TILE = 256
GLA_LEVELS = 6

MLA_HEADS = 16
MLA_Q_RANK = 256
MLA_KV_RANK = 128
MLA_NOPE = 128
MLA_ROPE = 64
MLA_V = 128
MLA_QK = MLA_NOPE + MLA_ROPE
ROPE_BASE = 10000.0

N_EXPERTS = 16
EXPERT_FF = 2048
EC_CAPACITY_FACTOR = 2
MOE_TILE = 256
COMBINE_HEAD = 64
BF16_SUBLANES = 16
F32_SUBLANES = 8
LANES = 128

VMEM_LIMIT = 56 * 1024 * 1024


def _cparams(sem):
    return pltpu.CompilerParams(dimension_semantics=sem, vmem_limit_bytes=VMEM_LIMIT)


def _rms(x, g):
    return x * lax.rsqrt(jnp.mean(x * x, axis=-1, keepdims=True) + RMS_EPS) * g


def _split_bf16(x):
    hi = x.astype(BF16)
    lo = (x - hi.astype(F32)).astype(BF16)
    return hi, lo


def _dot(a, b):
    return jnp.dot(a, b, preferred_element_type=F32)


def _dot_nt(a, b):
    return lax.dot_general(a, b, (((1,), (1,)), ((), ())), preferred_element_type=F32)


def _dot_tn(a, b):
    return lax.dot_general(a, b, (((0,), (0,)), ((), ())), preferred_element_type=F32)


def _dot_split(a, b):
    ah, al = _split_bf16(a)
    bh, bl = _split_bf16(b)
    return _dot(ah, bh) + _dot(ah, bl) + _dot(al, bh)


def _gla_in_kernel(x_ref, g_ref, wqk_ref, wv_ref, wr_ref, wgd_ref, qk_ref, v_ref, r_ref, gd_ref):
    hn = _rms(x_ref[...], g_ref[...]).astype(BF16)
    qk_ref[...] = _dot(hn, wqk_ref[...])
    v_ref[...] = _dot(hn, wv_ref[...])
    r_ref[...] = _dot(hn, wr_ref[...])
    gd_ref[...] = _dot(hn, wgd_ref[...])


def _gla_in(x, gain, w_in, tm=512):
    T = x.shape[0]
    wqk = w_in[:, :2 * GLA_DK].astype(BF16)
    wv = w_in[:, 2 * GLA_DK:2 * GLA_DK + GLA_DV].astype(BF16)
    wr = w_in[:, 2 * GLA_DK + GLA_DV:2 * GLA_DK + 2 * GLA_DV].astype(BF16)
    wgd = w_in[:, 2 * GLA_DK + 2 * GLA_DV:].astype(BF16)
    ngd = 2 * GLA_GATE_RANK
    row = lambda n: pl.BlockSpec((tm, n), lambda i: (i, 0))
    full = lambda a: pl.BlockSpec(a.shape, lambda i: (0, 0))
    gain2 = gain.reshape(1, D_MODEL)
    return pl.pallas_call(
        _gla_in_kernel,
        grid=(T // tm,),
        in_specs=[row(D_MODEL), full(gain2), full(wqk), full(wv), full(wr), full(wgd)],
        out_specs=[row(2 * GLA_DK), row(GLA_DV), row(GLA_DV), row(ngd)],
        out_shape=[jax.ShapeDtypeStruct((T, 2 * GLA_DK), F32), jax.ShapeDtypeStruct((T, GLA_DV), F32),
                   jax.ShapeDtypeStruct((T, GLA_DV), F32), jax.ShapeDtypeStruct((T, ngd), F32)],
        compiler_params=_cparams(("arbitrary",)),
        name="gla_in",
    )(x, gain2, wqk, wv, wr, wgd)


def _gla_tables(reverse):
    n, c = GLA_TILE, GLA_CHUNK
    W = np.zeros((GLA_LEVELS + 3, n, n), np.float32)
    L = np.full((n, n), -1, np.int32)
    for t in range(n):
        c0 = (t // c) * c
        tt = t - c0
        for l in range(GLA_LEVELS):
            b = (c // 2) >> l
            p0 = c0 + (tt // (2 * b)) * 2 * b
            mid = p0 + b
            second = t >= mid
            if not reverse:
                if second:
                    W[l, t, mid:t + 1] = 1
                    L[t, p0:mid] = l
                else:
                    W[l, t, t + 1:mid] = 1
            else:
                if second:
                    W[l, t, mid:t] = 1
                else:
                    W[l, t, t:mid] = 1
                    L[t, mid:p0 + 2 * b] = l
        if not reverse:
            W[GLA_LEVELS, t, c0:t + 1] = 1
            W[GLA_LEVELS + 1, t, t + 1:c0 + c] = 1
            L[t, t] = GLA_LEVELS
        else:
            W[GLA_LEVELS, t, t:c0 + c] = 1
            W[GLA_LEVELS + 1, t, c0:t] = 1
        W[GLA_LEVELS + 2, t, c0:c0 + c] = 1
    return W.reshape(-1, n), L


def _gla_scan_kernel(qkf_ref, vf_ref, gdf_ref, qkb_ref, vb_ref, gdb_ref, wupf_ref, bf_ref, wupb_ref, bb_ref,
                     wf_ref, lf_ref, wb_ref, lb_ref, of_ref, ob_ref, s_ref):
    n, c, r = GLA_TILE, GLA_CHUNK, GLA_GATE_RANK

    @pl.when(pl.program_id(0) == 0)
    def _():
        s_ref[...] = jnp.zeros_like(s_ref)

    dirs = [(qkf_ref, vf_ref, gdf_ref[:, :r], wupf_ref, bf_ref, wf_ref, lf_ref, of_ref, False),
            (qkb_ref, vb_ref, gdb_ref[:, r:], wupb_ref, bb_ref, wb_ref, lb_ref, ob_ref, True)]

    factors = []
    for qk_ref, v_ref, gd, wup_ref, b_ref, w_ref, l_ref, o_ref, reverse in dirs:
        z = _dot_split(gd, wup_ref[...]) + b_ref[...]
        g = (jnp.minimum(z, 0.0) - jnp.log1p(jnp.exp(-jnp.abs(z)))) * (1.0 / GLA_TAU)
        ghi, glo = _split_bf16(g)
        w = w_ref[...]
        factors.append(jnp.exp(_dot(w, ghi) + _dot(w, glo)))

    chains = []
    for d, (qk_ref, v_ref, gd, wup_ref, b_ref, w_ref, l_ref, o_ref, reverse) in enumerate(dirs):
        lvl = l_ref[...]
        for h in range(GLA_HEADS):
            kcols = slice(h * GLA_HEAD_K, (h + 1) * GLA_HEAD_K)
            vcols = slice(h * GLA_HEAD_V, (h + 1) * GLA_HEAD_V)
            f = factors[d][:, kcols]
            q = qk_ref[:, kcols] * (GLA_HEAD_K ** -0.5)
            k = qk_ref[:, GLA_DK + h * GLA_HEAD_K:GLA_DK + (h + 1) * GLA_HEAD_K]
            vb = v_ref[:, vcols].astype(BF16)
            attn = jnp.zeros((n, n), F32)
            for l in range(GLA_LEVELS):
                fl = f[l * n:(l + 1) * n]
                p = _dot_nt((q * fl).astype(BF16), (k * fl).astype(BF16))
                attn = jnp.where(lvl == l, p, attn)
            if not reverse:
                p = _dot_nt(q.astype(BF16), k.astype(BF16))
                attn = jnp.where(lvl == GLA_LEVELS, p, attn)
            o_intra = _dot(attn.astype(BF16), vb)
            qh = (q * f[GLA_LEVELS * n:(GLA_LEVELS + 1) * n]).astype(BF16)
            kh = (k * f[(GLA_LEVELS + 1) * n:(GLA_LEVELS + 2) * n]).astype(BF16)
            ftot = f[(GLA_LEVELS + 2) * n:(GLA_LEVELS + 3) * n]
            chains.append(dict(d=d, h=h, vcols=vcols, o_ref=o_ref, reverse=reverse, vb=vb, o_intra=o_intra, qh=qh,
                               kh=kh, ftot=ftot, st=s_ref[d, h]))

    nchunks = n // c
    for step in range(nchunks):
        for ch in chains:
            j = nchunks - 1 - step if ch["reverse"] else step
            rows = slice(j * c, (j + 1) * c)
            st = ch["st"]
            ch["o_ref"][rows, ch["vcols"]] = ch["o_intra"][rows] + _dot_nt(ch["qh"][rows], st.astype(BF16))
            ch["st"] = st * ch["ftot"][j * c:j * c + 1, :] + _dot_tn(ch["vb"][rows], ch["kh"][rows])
    for ch in chains:
        s_ref[ch["d"], ch["h"]] = ch["st"]


def _gla_scan(qk, v, gd, w_up_f, b_f, w_up_b, b_b):
    T = qk.shape[0]
    n = GLA_TILE
    nt = T // n
    tables = []
    for reverse in (False, True):
        W, L = _gla_tables(reverse)
        tables += [jnp.asarray(W, BF16), jnp.asarray(L)]
    fwd = lambda width: pl.BlockSpec((n, width), lambda i: (i, 0))
    bwd = lambda width: pl.BlockSpec((n, width), lambda i: (nt - 1 - i, 0))
    full = lambda a: pl.BlockSpec(a.shape, lambda i: (0, 0))
    consts = [w_up_f, b_f.reshape(1, GLA_DK), w_up_b, b_b.reshape(1, GLA_DK)] + tables
    ngd = 2 * GLA_GATE_RANK
    out = jax.ShapeDtypeStruct((T, GLA_DV), F32)
    return pl.pallas_call(
        _gla_scan_kernel,
        grid=(nt,),
        in_specs=[fwd(2 * GLA_DK), fwd(GLA_DV), fwd(ngd), bwd(2 * GLA_DK), bwd(GLA_DV), bwd(ngd)]
                 + [full(a) for a in consts],
        out_specs=[fwd(GLA_DV), bwd(GLA_DV)],
        out_shape=[out, out],
        scratch_shapes=[pltpu.VMEM((2, GLA_HEADS, GLA_HEAD_V, GLA_HEAD_K), F32)],
        compiler_params=_cparams(("arbitrary",)),
        name="gla_scan",
    )(qk, v, gd, qk, v, gd, *consts)


def _gla_out_kernel(of_ref, ob_ref, r_ref, x_ref, hn_ref, w_ref, o_ref):
    acc = x_ref[...]
    for h in range(GLA_HEADS):
        cols = slice(h * GLA_HEAD_V, (h + 1) * GLA_HEAD_V)
        o = _rms(of_ref[:, cols] + ob_ref[:, cols], hn_ref[...])
        r = r_ref[:, cols]
        gated = o * (r * (1.0 / (1.0 + jnp.exp(-r))))
        acc = acc + _dot(gated.astype(BF16), w_ref[cols, :])
    o_ref[...] = acc


def _gla_out(of, ob, r, x, head_norm, w_out, tm=512):
    T = x.shape[0]
    w = w_out.astype(BF16)
    hn = head_norm.reshape(1, GLA_HEAD_V)
    row = lambda n: pl.BlockSpec((tm, n), lambda i: (i, 0))
    full = lambda a: pl.BlockSpec(a.shape, lambda i: (0, 0))
    return pl.pallas_call(
        _gla_out_kernel,
        grid=(T // tm,),
        in_specs=[row(GLA_DV), row(GLA_DV), row(GLA_DV), row(D_MODEL), full(hn), full(w)],
        out_specs=row(D_MODEL),
        out_shape=jax.ShapeDtypeStruct((T, D_MODEL), F32),
        compiler_params=_cparams(("arbitrary",)),
        name="gla_out",
    )(of, ob, r, x, hn, w)


def _gla_mixer(x, gain, w_in, w_up_f, b_f, w_up_b, b_b, head_norm, w_out):
    qk, v, r, gd = _gla_in(x, gain, w_in)
    of, ob = _gla_scan(qk, v, gd, w_up_f, b_f, w_up_b, b_b)
    return _gla_out(of, ob, r, x, head_norm, w_out)


def _router_kernel(h_ref, g_ref, w_ref, aff_ref, hn_ref):
    hn = _rms(h_ref[...], g_ref[...])
    hn_ref[...] = hn.astype(BF16)
    logits = _dot_split(hn, w_ref[...])
    e = jnp.exp(logits - jnp.max(logits, axis=-1, keepdims=True))
    aff_ref[...] = e / jnp.sum(e, axis=-1, keepdims=True)


def _router(h, gain, w_router, tm=512):
    T = h.shape[0]
    gain2 = gain.reshape(1, D_MODEL)
    row = lambda n: pl.BlockSpec((tm, n), lambda i: (i, 0))
    full = lambda a: pl.BlockSpec(a.shape, lambda i: (0, 0))
    return pl.pallas_call(
        _router_kernel,
        grid=(T // tm,),
        in_specs=[row(D_MODEL), full(gain2), full(w_router)],
        out_specs=[row(N_EXPERTS), row(D_MODEL)],
        out_shape=[jax.ShapeDtypeStruct((T, N_EXPERTS), F32), jax.ShapeDtypeStruct((T, D_MODEL), BF16)],
        compiler_params=_cparams(("arbitrary",)),
        name="moe_router",
    )(h, gain2, w_router)


def _select_kernel(aff_ref, tri_ref, posm_ref, before_ref, *, cap):
    T = aff_ref.shape[1]
    bits = pltpu.bitcast(aff_ref[...], jnp.int32)

    def search(it, thr):
        cand = thr | jnp.left_shift(jnp.int32(1), 30 - it)
        cnt = jnp.sum(jnp.where(bits >= cand, 1.0, 0.0), axis=1, keepdims=True)
        return jnp.where(cnt >= cap, cand, thr)

    thr = lax.fori_loop(0, 31, search, jnp.zeros((N_EXPERTS, 1), jnp.int32))
    n_gt = jnp.sum(jnp.where(bits > thr, 1.0, 0.0), axis=1, keepdims=True)
    need = cap - n_gt
    tri = tri_ref[...]

    def scan(j, carry):
        c_eq, c_sel = carry
        cols = pl.ds(pl.multiple_of(j * LANES, LANES), LANES)
        blk = pltpu.bitcast(aff_ref[:, cols], jnp.int32)
        eq = jnp.where(blk == thr, 1.0, 0.0)
        rank = _dot(eq.astype(BF16), tri) + c_eq - eq
        sel = jnp.where((blk > thr) | ((eq > 0.0) & (rank < need)), 1.0, 0.0)
        before = _dot(sel.astype(BF16), tri) + c_sel - sel
        before_ref[:, cols] = before.astype(jnp.int32)
        posm_ref[:, cols] = jnp.where(sel > 0.0, before, -1.0).astype(jnp.int32)
        return (c_eq + jnp.sum(eq, axis=1, keepdims=True), c_sel + jnp.sum(sel, axis=1, keepdims=True))

    zero = jnp.zeros((N_EXPERTS, 1), F32)
    lax.fori_loop(0, T // LANES, scan, (zero, zero))


def _select(aff_t, cap):
    T = aff_t.shape[1]
    tri = jnp.asarray(np.triu(np.ones((LANES, LANES), np.float32)), BF16)
    full = lambda a: pl.BlockSpec(a.shape, lambda: (0,) * a.ndim)
    out = jax.ShapeDtypeStruct((N_EXPERTS, T), jnp.int32)
    return pl.pallas_call(
        functools.partial(_select_kernel, cap=cap),
        in_specs=[full(aff_t), full(tri)],
        out_specs=[pl.BlockSpec((N_EXPERTS, T), lambda: (0, 0))] * 2,
        out_shape=[out, out],
        compiler_params=pltpu.CompilerParams(vmem_limit_bytes=VMEM_LIMIT),
        name="moe_select",
    )(aff_t, tri)


def _dispatch_window():
    return MOE_TILE + F32_SUBLANES


def _dispatch_kernel(start_ref, hn_ref, posm_ref, x_ref, acc_ref, *, cap, nt, sub):
    e, t = pl.program_id(0), pl.program_id(1)
    win = _dispatch_window()

    @pl.when(t == 0)
    def _():
        acc_ref[...] = jnp.zeros_like(acc_ref)

    for s in range(sub):
        tok = slice(s * MOE_TILE, (s + 1) * MOE_TILE)
        base = pl.multiple_of((start_ref[e * nt + t * sub + s] // F32_SUBLANES) * F32_SUBLANES, F32_SUBLANES)
        slot = base + lax.broadcasted_iota(jnp.int32, (win, MOE_TILE), 0)
        onehot = jnp.where(posm_ref[0, :, tok] == slot, 1.0, 0.0).astype(BF16)
        acc_ref[pl.ds(base, win), :] += _dot(onehot, hn_ref[tok, :])

    @pl.when(t == pl.num_programs(1) - 1)
    def _():
        x_ref[0] = acc_ref[:cap, :].astype(BF16)


def _dispatch(hn, posm, start, cap, sub=16):
    T = hn.shape[0]
    nt = T // MOE_TILE
    sub = math.gcd(sub, nt)
    blk = sub * MOE_TILE
    posm3 = posm.reshape(N_EXPERTS, 1, T)
    grid_spec = pltpu.PrefetchScalarGridSpec(
        num_scalar_prefetch=1,
        grid=(N_EXPERTS, nt // sub),
        in_specs=[
            pl.BlockSpec((blk, D_MODEL), lambda e, t, s: (t, 0)),
            pl.BlockSpec((1, 1, blk), lambda e, t, s: (e, 0, t)),
        ],
        out_specs=pl.BlockSpec((1, cap, D_MODEL), lambda e, t, s: (e, 0, 0)),
        scratch_shapes=[pltpu.VMEM((cap + _dispatch_window(), D_MODEL), F32)],
    )
    return pl.pallas_call(
        functools.partial(_dispatch_kernel, cap=cap, nt=nt, sub=sub),
        grid_spec=grid_spec,
        out_shape=jax.ShapeDtypeStruct((N_EXPERTS, cap, D_MODEL), BF16),
        compiler_params=_cparams(("arbitrary", "arbitrary")),
        name="moe_dispatch",
    )(start.reshape(-1), hn, posm3)


def _ffn_kernel(x_ref, wg_ref, wu_ref, wd_ref, y_ref, acc_ref):
    f = pl.program_id(1)

    @pl.when(f == 0)
    def _():
        acc_ref[...] = jnp.zeros_like(acc_ref)

    wg = wg_ref[0, 0].astype(BF16)
    wu = wu_ref[0, 0].astype(BF16)
    wd = wd_ref[0, 0].astype(BF16)
    cap = x_ref.shape[1]
    blk = cap // math.gcd(cap // F32_SUBLANES, 4)
    for r0 in range(0, cap, blk):
        rows = slice(r0, r0 + blk)
        x = x_ref[0, rows, :]
        a = _dot(x, wg)
        u = _dot(x, wu)
        mid = (a * (1.0 / (1.0 + jnp.exp(-a))) * u).astype(BF16)
        acc_ref[rows, :] += _dot(mid, wd)

    @pl.when(f == pl.num_programs(1) - 1)
    def _():
        y_ref[0] = acc_ref[...].astype(BF16)


def _ffn(x, w_gate, w_up, w_down, layer, tf=512):
    cap = x.shape[1]
    return pl.pallas_call(
        _ffn_kernel,
        grid=(N_EXPERTS, EXPERT_FF // tf),
        in_specs=[
            pl.BlockSpec((1, cap, D_MODEL), lambda e, f: (e, 0, 0)),
            pl.BlockSpec((1, 1, D_MODEL, tf), lambda e, f: (layer, e, 0, f)),
            pl.BlockSpec((1, 1, D_MODEL, tf), lambda e, f: (layer, e, 0, f)),
            pl.BlockSpec((1, 1, tf, D_MODEL), lambda e, f: (layer, e, f, 0)),
        ],
        out_specs=pl.BlockSpec((1, cap, D_MODEL), lambda e, f: (e, 0, 0)),
        out_shape=jax.ShapeDtypeStruct((N_EXPERTS, cap, D_MODEL), BF16),
        scratch_shapes=[pltpu.VMEM((cap, D_MODEL), F32)],
        compiler_params=_cparams(("arbitrary", "arbitrary")),
        name="moe_ffn",
    )(x, w_gate, w_up, w_down)


def _combine_window():
    return MOE_TILE + BF16_SUBLANES


def _combine_kernel(start_ref, h_ref, aff_ref, posm_ref, g_ref, y_hbm, o_ref, hbuf_ref, buf_ref, hsem_ref, rsem_ref,
                    acc_ref, *, cap, nt, final_norm):
    t = pl.program_id(0)
    win = _combine_window()
    head = COMBINE_HEAD

    def base_of(tile, e):
        b = (start_ref[e * (nt + 1) + tile] // BF16_SUBLANES) * BF16_SUBLANES
        return pl.multiple_of(jnp.minimum(b, cap - win), BF16_SUBLANES)

    def needs_rest(tile, e):
        return start_ref[e * (nt + 1) + tile + 1] > base_of(tile, e) + head

    def head_copy(tile, e, slot):
        return pltpu.make_async_copy(y_hbm.at[e, pl.ds(base_of(tile, e), head), :],
                                     hbuf_ref.at[slot, pl.ds(e * head, head), :], hsem_ref.at[slot])

    def rest_copy(tile, e, slot):
        return pltpu.make_async_copy(y_hbm.at[e, pl.ds(base_of(tile, e) + head, win - head), :],
                                     buf_ref.at[slot, e], rsem_ref.at[slot, e])

    def fetch(tile, slot):
        for e in range(N_EXPERTS):
            head_copy(tile, e, slot).start(priority=e % 2)

            @pl.when(needs_rest(tile, e))
            def _():
                rest_copy(tile, e, slot).start()

    slot = t % 2

    @pl.when(t == 0)
    def _():
        fetch(0, 0)

    @pl.when(t + 1 < nt)
    def _():
        fetch(t + 1, 1 - slot)

    def weights_t(e, lo, hi):
        row = base_of(t, e) + lo + lax.broadcasted_iota(jnp.int32, (hi - lo, MOE_TILE), 0)
        return jnp.where(posm_ref[e:e + 1, :] == row, aff_ref[e:e + 1, :], 0.0)

    def expand(w_t, y_rows):
        w_hi, w_lo = _split_bf16(w_t.T)
        return _dot(w_hi, y_rows) + _dot(w_lo, y_rows)

    for e in range(N_EXPERTS):
        head_copy(t, e, slot).wait()

    w_heads = jnp.concatenate([weights_t(e, 0, head) for e in range(N_EXPERTS)], axis=0)
    acc_ref[...] = h_ref[...] + expand(w_heads, hbuf_ref[slot])

    any_rest = needs_rest(t, 0)
    for e in range(1, N_EXPERTS):
        any_rest = any_rest | needs_rest(t, e)

    @pl.when(any_rest)
    def _():
        for e in range(N_EXPERTS):
            @pl.when(needs_rest(t, e))
            def _():
                rest_copy(t, e, slot).wait()
                for lo, hi in ((head, MOE_TILE), (MOE_TILE, win)):
                    acc_ref[...] += expand(weights_t(e, lo, hi), buf_ref[slot, e, lo - head:hi - head, :])

    acc = acc_ref[...]
    if final_norm:
        acc = _rms(acc, g_ref[...])
    o_ref[...] = acc


def _combine(h, aff_t, posm, start, y, cap, final_gain):
    T = h.shape[0]
    nt = T // MOE_TILE
    win = _combine_window()
    assert cap >= win and (cap - win) % BF16_SUBLANES == 0
    final_norm = final_gain is not None
    gain = (final_gain if final_norm else jnp.ones((D_MODEL,), F32)).reshape(1, D_MODEL)
    grid_spec = pltpu.PrefetchScalarGridSpec(
        num_scalar_prefetch=1,
        grid=(nt,),
        in_specs=[
            pl.BlockSpec((MOE_TILE, D_MODEL), lambda t, s: (t, 0)),
            pl.BlockSpec((N_EXPERTS, MOE_TILE), lambda t, s: (0, t)),
            pl.BlockSpec((N_EXPERTS, MOE_TILE), lambda t, s: (0, t)),
            pl.BlockSpec((1, D_MODEL), lambda t, s: (0, 0)),
            pl.BlockSpec(memory_space=pl.ANY),
        ],
        out_specs=pl.BlockSpec((MOE_TILE, D_MODEL), lambda t, s: (t, 0)),
        scratch_shapes=[pltpu.VMEM((2, N_EXPERTS * COMBINE_HEAD, D_MODEL), BF16),
                        pltpu.VMEM((2, N_EXPERTS, win - COMBINE_HEAD, D_MODEL), BF16),
                        pltpu.SemaphoreType.DMA((2,)), pltpu.SemaphoreType.DMA((2, N_EXPERTS)),
                        pltpu.VMEM((MOE_TILE, D_MODEL), F32)],
    )
    return pl.pallas_call(
        functools.partial(_combine_kernel, cap=cap, nt=nt, final_norm=final_norm),
        grid_spec=grid_spec,
        out_shape=jax.ShapeDtypeStruct((T, D_MODEL), F32),
        compiler_params=_cparams(("arbitrary",)),
        name="moe_combine",
    )(start.reshape(-1), h, aff_t, posm, gain, y)


def _ec_moe(h, gain, w_router, w_gate, w_up, w_down, layer, final_gain=None):
    T = h.shape[0]
    cap = max(1, EC_CAPACITY_FACTOR * T // N_EXPERTS)
    aff, hn = _router(h, gain, w_router)
    aff_t = aff.T
    posm, before = _select(aff_t, cap)
    start = before[:, ::MOE_TILE]
    x = _dispatch(hn, posm, start, cap)
    y = _ffn(x, w_gate, w_up, w_down, layer)
    start_end = jnp.concatenate([start, jnp.full((N_EXPERTS, 1), cap, jnp.int32)], axis=1)
    return _combine(h, aff_t, posm, start_end, y, cap, final_gain)


def _rms_cols(x, g):
    return x * lax.rsqrt(jnp.mean(x * x, axis=0, keepdims=True) + RMS_EPS) * g


MLA_QCOLS = MLA_NOPE + 2 * LANES


def _mla_in_kernel(h_ref, posc_ref, posr_ref, g_ref, win_ref, wintkv_ref, qn_ref, wq_ref, wukt_ref, kvn_ref,
                   kvnc_ref, freqr_ref, sign_ref, freqc_ref, q_out, kt_out, va_out):
    half = MLA_ROPE // 2
    hn = _rms(h_ref[...], g_ref[...]).astype(BF16)
    c = _dot(hn, win_ref[...])
    ckv_t = _dot_nt(wintkv_ref[...], hn)

    cq = _rms(c[:, :MLA_Q_RANK], qn_ref[...]).astype(BF16)
    qa = _dot(cq, wq_ref[...])
    ang = posc_ref[...].astype(F32) * freqr_ref[...]
    cos, sin_signed = jnp.cos(ang), jnp.sin(ang) * sign_ref[...]
    qscale = MLA_QK ** -0.5 * math.log2(math.e)
    for hd in range(MLA_HEADS):
        c0 = hd * MLA_QCOLS
        q_lat = _dot(qa[:, c0:c0 + MLA_NOPE].astype(BF16), wukt_ref[hd])
        rot = qa[:, c0 + MLA_NOPE:c0 + MLA_NOPE + LANES] * cos + qa[:, c0 + MLA_NOPE + LANES:c0 + MLA_QCOLS] * sin_signed
        q_out[hd, :, :MLA_NOPE] = (q_lat * qscale).astype(BF16)
        q_out[hd, :, MLA_NOPE:] = (rot[:, :MLA_ROPE] * qscale).astype(BF16)

    kt_out[:MLA_KV_RANK, :] = _rms_cols(ckv_t[:MLA_KV_RANK], kvnc_ref[...]).astype(BF16)
    ang_t = freqc_ref[...] * posr_ref[...].astype(F32)
    cos_t, sin_t = jnp.cos(ang_t), jnp.sin(ang_t)
    k1, k2 = ckv_t[MLA_KV_RANK:MLA_KV_RANK + half], ckv_t[MLA_KV_RANK + half:]
    kt_out[MLA_KV_RANK:MLA_KV_RANK + half, :] = (k1 * cos_t - k2 * sin_t).astype(BF16)
    kt_out[MLA_KV_RANK + half:, :] = (k1 * sin_t + k2 * cos_t).astype(BF16)

    ckv = _rms(c[:, MLA_Q_RANK:MLA_Q_RANK + MLA_KV_RANK], kvn_ref[...])
    va_out[:, :MLA_KV_RANK] = ckv.astype(BF16)
    lane = lax.broadcasted_iota(jnp.int32, (ckv.shape[0], LANES), 1)
    va_out[:, MLA_KV_RANK:] = jnp.where(lane == 0, 1.0, 0.0).astype(BF16)


def _mla_in(h, positions, gain, w_in, q_norm, w_uq, kv_norm, w_ukv, tm=256):
    T = h.shape[0]
    H, half = MLA_HEADS, MLA_ROPE // 2
    wq = w_uq.reshape(MLA_Q_RANK, H, MLA_QK)
    x1, x2 = wq[:, :, MLA_NOPE:MLA_NOPE + half], wq[:, :, MLA_NOPE + half:]
    pad = jnp.zeros((MLA_Q_RANK, H, LANES - MLA_ROPE), F32)
    wq_wide = jnp.concatenate([wq[:, :, :MLA_NOPE], x1, x2, pad, x2, x1, pad], axis=2)
    wq_wide = wq_wide.reshape(MLA_Q_RANK, H * MLA_QCOLS).astype(BF16)
    wuk_t = w_ukv.reshape(MLA_KV_RANK, H, MLA_NOPE + MLA_V)[:, :, :MLA_NOPE].transpose(1, 2, 0).astype(BF16)
    inv_freq = ROPE_BASE ** (-jnp.arange(half, dtype=F32) / half)
    zeros = jnp.zeros((LANES - MLA_ROPE,), F32)
    freq_row = jnp.concatenate([inv_freq, inv_freq, zeros]).reshape(1, LANES)
    sign_row = jnp.concatenate([-jnp.ones((half,), F32), jnp.ones((half,), F32), zeros]).reshape(1, LANES)
    w_in_b = w_in.astype(BF16)
    args = [h, positions.reshape(T, 1), positions.reshape(1, T), gain.reshape(1, -1), w_in_b,
            w_in_b[:, MLA_Q_RANK:].T, q_norm.reshape(1, -1), wq_wide, wuk_t, kv_norm.reshape(1, -1),
            kv_norm.reshape(-1, 1), freq_row, sign_row, inv_freq.reshape(half, 1)]
    full = lambda a: pl.BlockSpec(a.shape, lambda i: (0,) * a.ndim)
    in_specs = [pl.BlockSpec((tm, D_MODEL), lambda i: (i, 0)), pl.BlockSpec((tm, 1), lambda i: (i, 0)),
                pl.BlockSpec((1, tm), lambda i: (0, i))] + [full(a) for a in args[3:]]
    return pl.pallas_call(
        _mla_in_kernel,
        grid=(T // tm,),
        in_specs=in_specs,
        out_specs=[pl.BlockSpec((H, tm, MLA_QK), lambda i: (0, i, 0)),
                   pl.BlockSpec((MLA_QK, tm), lambda i: (0, i)),
                   pl.BlockSpec((tm, 2 * LANES), lambda i: (i, 0))],
        out_shape=[jax.ShapeDtypeStruct((H, T, MLA_QK), BF16), jax.ShapeDtypeStruct((MLA_QK, T), BF16),
                   jax.ShapeDtypeStruct((T, 2 * LANES), BF16)],
        compiler_params=_cparams(("arbitrary",)),
        name="mla_in",
    )(*args)


def _flash_kernel(q_ref, qn_ref, kt_ref, va_ref, o_ref, sa_ref, sb_ref, pa_ref, pb_ref, acc_ref, m_ref, alpha_ref,
                  *, tk, nblk):
    H, tq, _ = qn_ref.shape
    T = kt_ref.shape[1]
    n = T // tk
    tiles = q_ref.shape[1] // tq

    def q_rows(u):
        if isinstance(u, int):
            blk = qn_ref[...] if u == tiles else q_ref[:, u * tq:(u + 1) * tq, :]
        else:
            blk = q_ref[:, pl.ds(pl.multiple_of(u * tq, tq), tq), :]
        return blk.reshape(H * tq, MLA_QK)

    def k_tile(j):
        return kt_ref[:, j * tk:(j + 1) * tk]

    def v_tile(j):
        return va_ref[j * tk:(j + 1) * tk, :]

    def lanes_rep(x, width):
        return jnp.concatenate([x] * (width // LANES), axis=1)

    blk = H * tq // nblk
    row_blocks = [slice(rb * blk, (rb + 1) * blk) for rb in range(nblk)]

    def add_pv(rows, p_ref, j):
        acc_ref[rows, :] = (lanes_rep(alpha_ref[rows, :], 2 * LANES) * acc_ref[rows, :]
                            + _dot(p_ref[rows, :], v_tile(j)))

    def step(q, q_next, j, s_cur, s_nxt, p_cur, p_prv):
        q_sel, k_nxt = (q, k_tile(j + 1)) if j + 1 < n else (q_next, k_tile(0))
        for rows in row_blocks:
            if j > 0:
                add_pv(rows, p_prv, j - 1)
            s_nxt[rows, :] = _dot(q_sel[rows], k_nxt)
            s = s_cur[rows, :]
            m_old = m_ref[rows, :]
            m_new = jnp.maximum(m_old, jnp.max(s, axis=1, keepdims=True))
            alpha_ref[rows, :] = jnp.exp2(m_old - m_new)
            m_ref[rows, :] = m_new
            p_cur[rows, :] = jnp.exp2(s - lanes_rep(m_new, tk)).astype(BF16)

    @pl.when(pl.program_id(0) == 0)
    def _():
        sa_ref[...] = _dot(q_rows(0), k_tile(0))

    def token_tile(u, carry):
        q = q_rows(u)
        q_next = jnp.where(u == tiles - 1, q_rows(tiles), q_rows(jnp.minimum(u + 1, tiles - 1)))
        acc_ref[...] = jnp.zeros_like(acc_ref)
        m_ref[...] = jnp.full_like(m_ref, -jnp.inf)
        for j in range(n):
            if j % 2 == 0:
                step(q, q_next, j, sa_ref, sb_ref, pa_ref, pb_ref)
            else:
                step(q, q_next, j, sb_ref, sa_ref, pb_ref, pa_ref)
        p_last = pb_ref if n % 2 == 0 else pa_ref
        for rows in row_blocks:
            add_pv(rows, p_last, n - 1)
        acc = acc_ref[...]
        o_lat = (acc[:, :MLA_KV_RANK] / acc[:, MLA_KV_RANK:MLA_KV_RANK + 1]).astype(BF16)
        out_rows = pl.ds(pl.multiple_of(u * tq, tq), tq)
        for hd in range(H):
            o_ref[out_rows, hd * MLA_KV_RANK:(hd + 1) * MLA_KV_RANK] = o_lat[hd * tq:(hd + 1) * tq]
        return carry

    lax.fori_loop(0, tiles, token_tile, 0)


def _flash(q, kt, va, tq=64, tk=1024, tiles=4):
    H, T, _ = q.shape
    tq, tk = min(tq, T), min(tk, T)
    assert T % (2 * tk) == 0 and tq % BF16_SUBLANES == 0
    assert T % (tiles * tq) == 0
    rows = H * tq
    steps = T // (tiles * tq)
    last_tile = T // tq - 1
    return pl.pallas_call(
        functools.partial(_flash_kernel, tk=tk, nblk=4),
        grid=(steps,),
        scratch_shapes=[pltpu.VMEM((rows, tk), F32), pltpu.VMEM((rows, tk), F32), pltpu.VMEM((rows, tk), BF16),
                        pltpu.VMEM((rows, tk), BF16), pltpu.VMEM((rows, 2 * LANES), F32),
                        pltpu.VMEM((rows, LANES), F32), pltpu.VMEM((rows, LANES), F32)],
        in_specs=[
            pl.BlockSpec((H, tiles * tq, MLA_QK), lambda i: (0, i, 0)),
            pl.BlockSpec((H, tq, MLA_QK), lambda i: (0, jnp.minimum((i + 1) * tiles, last_tile), 0)),
            pl.BlockSpec((MLA_QK, T), lambda i: (0, 0)),
            pl.BlockSpec((T, 2 * LANES), lambda i: (0, 0)),
        ],
        out_specs=pl.BlockSpec((tiles * tq, H * MLA_KV_RANK), lambda i: (i, 0)),
        out_shape=jax.ShapeDtypeStruct((T, H * MLA_KV_RANK), BF16),
        compiler_params=_cparams(("arbitrary",)),
        name="mla_flash",
    )(q, q, kt, va)


def _mla_out_kernel(o_ref, wuv_ref, w_ref, h_ref, out_ref):
    v = [_dot(o_ref[:, hd * MLA_KV_RANK:(hd + 1) * MLA_KV_RANK], wuv_ref[hd]).astype(BF16)
         for hd in range(MLA_HEADS)]
    out_ref[...] = h_ref[...] + _dot(jnp.concatenate(v, axis=1), w_ref[...])


def _mla_out(o_lat, w_ukv, w_out, h, tm=512):
    T = h.shape[0]
    H = MLA_HEADS
    wuv = w_ukv.reshape(MLA_KV_RANK, H, MLA_NOPE + MLA_V)[:, :, MLA_NOPE:].transpose(1, 0, 2).astype(BF16)
    wb = w_out.astype(BF16)
    return pl.pallas_call(
        _mla_out_kernel,
        grid=(T // tm,),
        in_specs=[pl.BlockSpec((tm, H * MLA_KV_RANK), lambda i: (i, 0)), pl.BlockSpec(wuv.shape, lambda i: (0, 0, 0)),
                  pl.BlockSpec(wb.shape, lambda i: (0, 0)), pl.BlockSpec((tm, D_MODEL), lambda i: (i, 0))],
        out_specs=pl.BlockSpec((tm, D_MODEL), lambda i: (i, 0)),
        out_shape=jax.ShapeDtypeStruct((T, D_MODEL), F32),
        compiler_params=_cparams(("arbitrary",)),
        name="mla_out",
    )(o_lat, wuv, wb, h)


def _mla_mixer(h, positions, gain, w_in, q_norm, w_uq, kv_norm, w_ukv, w_out):
    q, kt, va = _mla_in(h, positions, gain, w_in, q_norm, w_uq, kv_norm, w_ukv)
    o_lat = _flash(q, kt, va)
    return _mla_out(o_lat, w_ukv, w_out, h)


def kernel(x, positions, mix_norm, ffn_norm, final_norm, gla_w_in, gla_w_gate_up_f, gla_b_gate_f, gla_w_gate_up_b,
           gla_b_gate_b, gla_head_norm, gla_w_out, mla_w_in, mla_q_norm, mla_w_uq, mla_kv_norm, mla_w_ukv,
           mla_w_out, moe_w_router, moe_w_gate, moe_w_up, moe_w_down):
    B, T, D = x.shape
    outs = []
    for b in range(B):
        h = x[b]
        h = _gla_mixer(h, mix_norm[0], gla_w_in[0], gla_w_gate_up_f[0], gla_b_gate_f[0], gla_w_gate_up_b[0],
                       gla_b_gate_b[0], gla_head_norm[0], gla_w_out[0])
        h = _ec_moe(h, ffn_norm[0], moe_w_router[0], moe_w_gate, moe_w_up, moe_w_down, 0)
        h = _mla_mixer(h, positions[b], mix_norm[1], mla_w_in[0], mla_q_norm[0], mla_w_uq[0], mla_kv_norm[0],
                       mla_w_ukv[0], mla_w_out[0])
        h = _ec_moe(h, ffn_norm[1], moe_w_router[1], moe_w_gate, moe_w_up, moe_w_down, 1, final_gain=final_norm)
        outs.append(h)
    return jnp.stack(outs)
```
